```python
import math
import jax, jax.numpy as jnp
from jax import lax
import numpy as np

D_MODEL = 1024
BATCH = 2
SEQ = 8192
DEPTH = 1
DEC_BATCH = 32
DEC_SEQ = 1
PAST_LEN = 8192
PAGE_SIZE = 128

N_HEADS = 8
HEAD_DIM = 64
D_ATTN = N_HEADS * HEAD_DIM
IDX_HEADS = 4
IDX_DIM = 64
TOPK_MAX = 256
Q_BLOCK = 128
SSM_WIDTH = 512
GROUP_CH = 16
N_GROUPS = SSM_WIDTH // GROUP_CH
STATE_DIM = 64
D_FF = 2816
NORM_EPS = 1e-6
NEG_BIG = -1e30
PROJ_SIZES = (D_ATTN, D_ATTN, D_ATTN, IDX_HEADS * IDX_DIM, IDX_DIM, IDX_HEADS, SSM_WIDTH, 2 * D_MODEL)
PROJ_DIM = sum(PROJ_SIZES)
PROJ_SPLITS = [int(s) for s in np.cumsum(PROJ_SIZES)[:-1]]

kernel_name = "dsa_s5_gated_macaron_step"


def rmsnorm(x, g):
    xf = x.astype(jnp.float32)
    y = xf * lax.rsqrt(jnp.mean(xf * xf, axis=-1, keepdims=True) + NORM_EPS)
    return (y * g.astype(jnp.float32)).astype(x.dtype)


def swiglu(x, w_gate, w_up, w_down):
    return (jax.nn.silu(x @ w_gate) * (x @ w_up)) @ w_down


def index_scores(q_idx, w_idx, k_idx):
    dots = jnp.einsum('bthd,bsd->bths', q_idx, k_idx).astype(jnp.float32)
    w = w_idx.astype(jnp.float32) * (IDX_HEADS * IDX_DIM) ** -0.5
    return jnp.einsum('bths,bth->bts', jax.nn.relu(dots), w)


def n_selected(n_keys):
    return min(TOPK_MAX, n_keys // 4)


def select_keys(scores, q_pos, n_keys, k_top):
    k_pos = jnp.arange(n_keys)
    causal = k_pos[None, None, :] <= q_pos[None, :, None]
    _, idx = lax.top_k(jnp.where(causal, scores, NEG_BIG), k_top)
    valid = idx <= q_pos[None, :, None]
    return idx, valid


def gather_rows(rows, idx):
    return jax.vmap(lambda r, i: r[i])(rows, idx)


def sparse_attend(q, kg, vg, valid):
    logits = jnp.einsum('bqhd,bqkhd->bqhk', q, kg).astype(jnp.float32) * HEAD_DIM ** -0.5
    logits = jnp.where(valid[:, :, None, :], logits, -jnp.inf)
    p = jax.nn.softmax(logits, axis=-1)
    return jnp.einsum('bqhk,bqkhd->bqhd', p.astype(vg.dtype), vg)


def s5_scan(u, x0_re, x0_im, a_re, a_im, log_dt, b_re, b_im, c_re, c_im, d_skip):
    f32 = jnp.float32
    bsz, t = u.shape[:2]
    uf = u.astype(f32)
    ug = uf.reshape(bsz, t, N_GROUPS, GROUP_CH)
    dt = jnp.exp(log_dt.astype(f32))[:, None]
    ar, ai = a_re.astype(f32), a_im.astype(f32)
    mag = jnp.exp(dt * ar)
    ab_re, ab_im = mag * jnp.cos(dt * ai), mag * jnp.sin(dt * ai)
    den = ar * ar + ai * ai
    nr, ni = ab_re - 1.0, ab_im
    f_re, f_im = (nr * ar + ni * ai) / den, (ni * ar - nr * ai) / den
    br, bi = b_re.astype(f32), b_im.astype(f32)
    bb_re = f_re[..., None] * br - f_im[..., None] * bi
    bb_im = f_re[..., None] * bi + f_im[..., None] * br
    bu_re = jnp.einsum('gpc,btgc->btgp', bb_re, ug)
    bu_im = jnp.einsum('gpc,btgc->btgp', bb_im, ug)
    a_t_re = jnp.broadcast_to(ab_re, bu_re.shape)
    a_t_im = jnp.broadcast_to(ab_im, bu_im.shape)

    def combine(e1, e2):
        a1r, a1i, b1r, b1i = e1
        a2r, a2i, b2r, b2i = e2
        return (a2r * a1r - a2i * a1i, a2r * a1i + a2i * a1r,
                a2r * b1r - a2i * b1i + b2r, a2r * b1i + a2i * b1r + b2i)

    cr, ci, sr, si = lax.associative_scan(combine, (a_t_re, a_t_im, bu_re, bu_im), axis=1)
    x0r = x0_re.astype(f32)[:, None]
    x0i = x0_im.astype(f32)[:, None]
    xr = cr * x0r - ci * x0i + sr
    xi = cr * x0i + ci * x0r + si
    y = (jnp.einsum('gcp,btgp->btgc', c_re.astype(f32), xr)
         - jnp.einsum('gcp,btgp->btgc', c_im.astype(f32), xi))
    y = y.reshape(bsz, t, SSM_WIDTH) + d_skip.astype(f32) * uf
    return y.astype(u.dtype), xr[:, -1].astype(x0_re.dtype), xi[:, -1].astype(x0_im.dtype)


def setup_inputs(seed: int = 0) -> dict:
    key = jax.random.key(seed)
    keys = iter(jax.random.split(key, 40))
    f32 = jnp.float32

    def nrm(shape, scale):
        return jax.random.normal(next(keys), shape, f32) * scale

    n_pages = PAST_LEN // PAGE_SIZE
    used = DEC_BATCH * n_pages
    n_pool = used + used // 4
    page_table = jax.random.permutation(next(keys), n_pool)[:used].reshape(DEC_BATCH, n_pages).astype(jnp.int32)
    a_im_base = math.pi * jnp.arange(STATE_DIM, dtype=f32)
    return {
        'x_prompt': nrm((BATCH, SEQ, D_MODEL), 1.0),
        'x_sample': nrm((DEC_BATCH, DEC_SEQ, D_MODEL), 1.0),
        'cache_k': nrm((DEPTH, n_pool, PAGE_SIZE, N_HEADS, HEAD_DIM), 1.0),
        'cache_v': nrm((DEPTH, n_pool, PAGE_SIZE, N_HEADS, HEAD_DIM), 1.0),
        'cache_kidx': nrm((DEPTH, n_pool, PAGE_SIZE, IDX_DIM), 1.0),
        'state_ssm_re': nrm((DEPTH, DEC_BATCH, N_GROUPS, STATE_DIM), 0.5),
        'state_ssm_im': nrm((DEPTH, DEC_BATCH, N_GROUPS, STATE_DIM), 0.5),
        'page_table': page_table,
        'g_ffn1': 1.0 + nrm((DEPTH, D_MODEL), 0.02),
        'w1_gate': nrm((DEPTH, D_MODEL, D_FF), D_MODEL ** -0.5),
        'w1_up': nrm((DEPTH, D_MODEL, D_FF), D_MODEL ** -0.5),
        'w1_down': nrm((DEPTH, D_FF, D_MODEL), D_FF ** -0.5),
        'g_mix': 1.0 + nrm((DEPTH, D_MODEL), 0.02),
        'w_in': nrm((DEPTH, D_MODEL, PROJ_DIM), D_MODEL ** -0.5),
        'a_re': -0.5 + nrm((DEPTH, N_GROUPS, STATE_DIM), 0.01),
        'a_im': a_im_base + nrm((DEPTH, N_GROUPS, STATE_DIM), 0.01),
        'log_dt': jax.random.uniform(next(keys), (DEPTH, N_GROUPS), f32, math.log(1e-3), math.log(1e-1)),
        'b_re': nrm((DEPTH, N_GROUPS, STATE_DIM, GROUP_CH), (2 * GROUP_CH) ** -0.5),
        'b_im': nrm((DEPTH, N_GROUPS, STATE_DIM, GROUP_CH), (2 * GROUP_CH) ** -0.5),
        'c_re': nrm((DEPTH, N_GROUPS, GROUP_CH, STATE_DIM), (2 * STATE_DIM) ** -0.5),
        'c_im': nrm((DEPTH, N_GROUPS, GROUP_CH, STATE_DIM), (2 * STATE_DIM) ** -0.5),
        'd_skip': nrm((DEPTH, SSM_WIDTH), 1.0),
        'w_glu': nrm((DEPTH, SSM_WIDTH, SSM_WIDTH), SSM_WIDTH ** -0.5),
        'b_glu': nrm((DEPTH, SSM_WIDTH), 0.01),
        'w_pa': nrm((DEPTH, D_ATTN, D_MODEL), D_ATTN ** -0.5),
        'w_pb': nrm((DEPTH, SSM_WIDTH, D_MODEL), SSM_WIDTH ** -0.5),
        'w_out': nrm((DEPTH, D_MODEL, D_MODEL), D_MODEL ** -0.5),
        'g_ffn2': 1.0 + nrm((DEPTH, D_MODEL), 0.02),
        'w2_gate': nrm((DEPTH, D_MODEL, D_FF), D_MODEL ** -0.5),
        'w2_up': nrm((DEPTH, D_MODEL, D_FF), D_MODEL ** -0.5),
        'w2_down': nrm((DEPTH, D_FF, D_MODEL), D_FF ** -0.5),
        'g_final': 1.0 + nrm((D_MODEL,), 0.02),
    }


def reference(x_prompt, x_sample, cache_k, cache_v, cache_kidx, state_ssm_re, state_ssm_im, page_table,
              g_ffn1, w1_gate, w1_up, w1_down, g_mix, w_in, a_re, a_im, log_dt, b_re, b_im, c_re, c_im,
              d_skip, w_glu, b_glu, w_pa, w_pb, w_out, g_ffn2, w2_gate, w2_up, w2_down, g_final):
    page_size = cache_k.shape[2]
    n_pages = page_table.shape[1]
    past_len = n_pages * page_size

    def attend_prompt(l, q, k, v, qi, ki, wi):
        b, t = q.shape[:2]
        nb = t // Q_BLOCK
        k_top = n_selected(t)

        def to_blocks(a):
            return jnp.swapaxes(a.reshape((b, nb, Q_BLOCK) + a.shape[2:]), 0, 1)

        def block(args):
            qb, qib, wib, t0 = args
            q_pos = t0 + jnp.arange(Q_BLOCK)
            idx, valid = select_keys(index_scores(qib, wib, ki), q_pos, t, k_top)
            return sparse_attend(qb, gather_rows(k, idx), gather_rows(v, idx), valid)

        out = lax.map(block, (to_blocks(q), to_blocks(qi), to_blocks(wi), jnp.arange(nb) * Q_BLOCK))
        return jnp.swapaxes(out, 0, 1).reshape(b, t, N_HEADS, HEAD_DIM)

    def attend_sample(l, q, k, v, qi, ki, wi):
        db, ds = q.shape[:2]
        n_keys = past_len + ds
        ki_past = cache_kidx[l][page_table].reshape(db, past_len, IDX_DIM).astype(ki.dtype)
        ki_all = jnp.concatenate([ki_past, ki], axis=1)
        q_pos = past_len + jnp.arange(ds)
        idx, valid = select_keys(index_scores(qi, wi, ki_all), q_pos, n_keys, n_selected(n_keys))
        in_cache = idx < past_len
        i_c = jnp.minimum(idx, past_len - 1)
        phys = gather_rows(page_table, i_c // page_size)
        off = i_c % page_size
        i_n = jnp.clip(idx - past_len, 0, ds - 1)
        sel = in_cache[..., None, None]
        kg = jnp.where(sel, cache_k[l][phys, off].astype(k.dtype), gather_rows(k, i_n))
        vg = jnp.where(sel, cache_v[l][phys, off].astype(v.dtype), gather_rows(v, i_n))
        return sparse_attend(q, kg, vg, valid)

    def layer(l, x, attend, x0_re, x0_im):
        x = x + 0.5 * swiglu(rmsnorm(x, g_ffn1[l]), w1_gate[l], w1_up[l], w1_down[l])
        h = rmsnorm(x, g_mix[l])
        b, t = x.shape[:2]
        q, k, v, qi, ki, wi, u, gates = jnp.split(h @ w_in[l], PROJ_SPLITS, axis=-1)
        q = q.reshape(b, t, N_HEADS, HEAD_DIM)
        k = k.reshape(b, t, N_HEADS, HEAD_DIM)
        v = v.reshape(b, t, N_HEADS, HEAD_DIM)
        qi = qi.reshape(b, t, IDX_HEADS, IDX_DIM)
        o_a = attend(l, q, k, v, qi, ki, wi).reshape(b, t, D_ATTN)
        y_s, xr, xi = s5_scan(u, x0_re, x0_im, a_re[l], a_im[l], log_dt[l], b_re[l], b_im[l],
                              c_re[l], c_im[l], d_skip[l])
        y_s = jax.nn.gelu(y_s)
        o_b = y_s * jax.nn.sigmoid(y_s @ w_glu[l] + b_glu[l])
        g_a, g_b = jnp.split(jax.nn.sigmoid(gates), 2, axis=-1)
        merged = g_a * (o_a @ w_pa[l]) + g_b * (o_b @ w_pb[l])
        x = x + merged @ w_out[l]
        x = x + 0.5 * swiglu(rmsnorm(x, g_ffn2[l]), w2_gate[l], w2_up[l], w2_down[l])
        return x, (k, v, ki, xr, xi)

    xp, xs = x_prompt, x_sample
    zero_state = jnp.zeros((xp.shape[0], N_GROUPS, STATE_DIM), state_ssm_re.dtype)
    new_p, new_s = [], []
    for l in range(DEPTH):
        xp, st_p = layer(l, xp, attend_prompt, zero_state, zero_state)
        xs, st_s = layer(l, xs, attend_sample, state_ssm_re[l], state_ssm_im[l])
        new_p.append(st_p)
        new_s.append(st_s)

    def stack(states, i):
        return jnp.stack([s[i] for s in states])

    y_prompt = rmsnorm(xp, g_final)
    y_sample = rmsnorm(xs, g_final)
    return (y_prompt, y_sample,
            stack(new_p, 0), stack(new_p, 1), stack(new_p, 2), stack(new_p, 3), stack(new_p, 4),
            stack(new_s, 0), stack(new_s, 1), stack(new_s, 2), stack(new_s, 3), stack(new_s, 4))
```

```python
import functools

import numpy as np
import jax
import jax.numpy as jnp
from jax import lax
from jax.experimental import pallas as pl
from jax.experimental.pallas import tpu as pltpu

F32 = jnp.float32
BF16 = jnp.bfloat16
I32 = jnp.int32

D_MODEL = 1024
N_HEADS = 8
HEAD_DIM = 64
D_ATTN = N_HEADS * HEAD_DIM
IDX_HEADS = 4
IDX_DIM = 64
TOPK_MAX = 256
SSM_WIDTH = 512
GROUP_CH = 16
N_GROUPS = SSM_WIDTH // GROUP_CH
STATE_DIM = 64
N_STATE = N_GROUPS * STATE_DIM
NORM_EPS = 1e-6
NEG_BIG = -1e30

INT_MIN = -(2 ** 31)
INT_MAX = 2 ** 31 - 1


def _float_key(x):
    b = int(np.float32(x).view(np.int32))
    return b ^ 0x7FFFFFFF if b < 0 else b


KEY_NEG_BIG = _float_key(NEG_BIG)

VMEM_LIMIT_BYTES = 56 * 1024 * 1024


def _cparams(*sem):
    return pltpu.CompilerParams(dimension_semantics=sem, vmem_limit_bytes=VMEM_LIMIT_BYTES)


def _rmsnorm(x, g):
    y = x * lax.rsqrt(jnp.mean(x * x, axis=-1, keepdims=True) + NORM_EPS)
    return y * g


def _sigmoid(x):
    return 1.0 / (1.0 + jnp.exp(-x))


def _ffn_kernel(x_ref, g_ref, wg_ref, wu_ref, wd_ref, gf_ref, o_ref, xn_ref, acc_ref, *, final_norm):
    f = pl.program_id(1)

    @pl.when(f == 0)
    def _():
        xn_ref[...] = _rmsnorm(x_ref[...], g_ref[...]).astype(BF16)
        acc_ref[...] = jnp.zeros_like(acc_ref)

    xn = xn_ref[...]
    gate = jnp.dot(xn, wg_ref[...], preferred_element_type=F32)
    up = jnp.dot(xn, wu_ref[...], preferred_element_type=F32)
    act = (gate * _sigmoid(gate)) * up
    acc_ref[...] += jnp.dot(act.astype(BF16), wd_ref[...], preferred_element_type=F32)

    @pl.when(f == pl.num_programs(1) - 1)
    def _():
        y = x_ref[...] + 0.5 * acc_ref[...]
        if final_norm:
            y = _rmsnorm(y, gf_ref[...])
        o_ref[...] = y


def _ffn(x, g, wg, wu, wd, g_final, *, final_norm, tm, tf):
    m, d = x.shape
    dff = wg.shape[1]
    return pl.pallas_call(
        functools.partial(_ffn_kernel, final_norm=final_norm),
        grid=(m // tm, dff // tf),
        in_specs=[
            pl.BlockSpec((tm, d), lambda i, f: (i, 0)),
            pl.BlockSpec((1, d), lambda i, f: (0, 0)),
            pl.BlockSpec((d, tf), lambda i, f: (0, f)),
            pl.BlockSpec((d, tf), lambda i, f: (0, f)),
            pl.BlockSpec((tf, d), lambda i, f: (f, 0)),
            pl.BlockSpec((1, d), lambda i, f: (0, 0)),
        ],
        out_specs=pl.BlockSpec((tm, d), lambda i, f: (i, 0)),
        out_shape=jax.ShapeDtypeStruct((m, d), F32),
        scratch_shapes=[pltpu.VMEM((tm, d), BF16), pltpu.VMEM((tm, d), F32)],
        compiler_params=_cparams("parallel", "arbitrary"),
        name="ffn",
    )(x, g, wg, wu, wd, g_final)


_PQ, _PK, _PV, _PQI, _PKI, _PWI, _PU, _PEND = 0, 512, 1024, 1536, 1792, 1920, 2048, 2560


def _proj_kernel(x_ref, g_ref, w_ref, q_ref, kf_ref, vf_ref, kb_ref, vb_ref,
                 qi_ref, kif_ref, kib_ref, wi_ref, u_ref):
    h = _rmsnorm(x_ref[...], g_ref[...]).astype(BF16)
    p = jnp.dot(h, w_ref[...], preferred_element_type=F32)
    q_ref[...] = (p[:, _PQ:_PK] * (HEAD_DIM ** -0.5)).astype(BF16)
    k = p[:, _PK:_PV]
    v = p[:, _PV:_PQI]
    kf_ref[...] = k
    vf_ref[...] = v
    kb_ref[...] = k.astype(BF16)
    vb_ref[...] = v.astype(BF16)
    qi_ref[...] = p[:, _PQI:_PKI].astype(BF16)
    ki = p[:, _PKI:_PKI + IDX_DIM]
    kif_ref[...] = ki
    kib_ref[...] = ki.astype(BF16)
    wi_ref[...] = p[:, _PWI:_PU] * ((IDX_HEADS * IDX_DIM) ** -0.5)
    u_ref[...] = p[:, _PU:_PEND]


def _proj(x, g, w_all, *, tm):
    m, d = x.shape
    row = lambda n: pl.BlockSpec((tm, n), lambda i: (i, 0))
    outs = [
        (D_ATTN, BF16), (D_ATTN, F32), (D_ATTN, F32), (D_ATTN, BF16), (D_ATTN, BF16),
        (IDX_HEADS * IDX_DIM, BF16), (IDX_DIM, F32), (IDX_DIM, BF16), (128, F32), (SSM_WIDTH, F32),
    ]
    return pl.pallas_call(
        _proj_kernel,
        grid=(m // tm,),
        in_specs=[row(d), pl.BlockSpec((1, d), lambda i: (0, 0)),
                  pl.BlockSpec((d, _PEND), lambda i: (0, 0))],
        out_specs=[row(n) for n, _ in outs],
        out_shape=[jax.ShapeDtypeStruct((m, n), dt) for n, dt in outs],
        compiler_params=_cparams("parallel"),
        name="proj",
    )(x, g, w_all)


def _scores_t(kc, qit, w):
    s = None
    for h in range(IDX_HEADS):
        d = jnp.dot(kc, qit[h], preferred_element_type=F32)
        term = jnp.maximum(d, 0.0) * w[h:h + 1, :]
        s = term if s is None else s + term
    return s


def _to_key(s):
    s = jnp.where(s == 0.0, 0.0, s)
    b = lax.bitcast_convert_type(s, I32)
    return jnp.where(b < 0, b ^ INT_MAX, b)


def _key_to_float(k):
    return lax.bitcast_convert_type(jnp.where(k < 0, k ^ INT_MAX, k), F32)


def _select_bias(s, vf, kidx, cut):
    tie = jnp.where(kidx < cut, 0.0, -jnp.inf)
    return jnp.where(s > vf, 0.0, jnp.where(s == vf, tie, -jnp.inf))


def _kth_search(s_ref, n_chunks, tk, tq, n_nc, k_top, idx_bits):
    row_iota = lax.broadcasted_iota(I32, (tk, tq), 0)

    def count(pred):
        def body(c, acc):
            k0 = pl.multiple_of(c * tk, tk)
            m = pred(s_ref[pl.ds(k0, tk), :], k0)
            return acc + jnp.sum(m.reshape(tk // 8, 8, tq), axis=0)

        acc = lax.fori_loop(0, n_chunks, body, jnp.zeros((8, tq), I32))
        return jnp.sum(acc, axis=0, keepdims=True)

    def count_ge(cand):
        return count(lambda blk, k0: jnp.where(blk >= cand, 1, 0))

    def virtual_ge(cand):
        return jnp.where(cand <= KEY_NEG_BIG, n_nc, 0)

    def bit_step(it, v):
        cand = v + jnp.left_shift(jnp.int32(1), 31 - it)
        ok = (count_ge(cand) + virtual_ge(cand)) >= k_top
        return jnp.where(ok, cand, v)

    v = lax.fori_loop(0, 32, bit_step, jnp.full((1, tq), INT_MIN, I32))
    n_gt_real = count_ge(v + 1)
    n_ge_real = count_ge(v)
    need = k_top - (n_gt_real + virtual_ge(v + 1))
    partial = (n_ge_real - n_gt_real) > need
    return v, need, partial, row_iota, count


def _tie_cut(v, need, row_iota, count, idx_bits):
    def ties_lt(c):
        return count(lambda blk, k0: jnp.where(blk == v, jnp.where(k0 + row_iota < c, 1, 0), 0))

    def bit_step(it, c):
        cand = c + jnp.left_shift(jnp.int32(1), idx_bits - 1 - it)
        return jnp.where(ties_lt(cand) < need, cand, c)

    c = lax.fori_loop(0, idx_bits, bit_step, jnp.zeros_like(v))
    return c + 1


def _select_outputs(s_ref, n_chunks, tk, tq, n_nc, k_top, idx_bits, vf_ref, cut_ref):
    v, need, partial, row_iota, count = _kth_search(s_ref, n_chunks, tk, tq, n_nc, k_top, idx_bits)
    vf_ref[0] = _key_to_float(v)
    cut_ref[0] = jnp.full((1, tq), INT_MAX, I32)

    @pl.when(jnp.max(partial.astype(I32)) > 0)
    def _():
        cut = _tie_cut(v, need, row_iota, count, idx_bits)
        cut_ref[0] = jnp.where(partial, cut, INT_MAX)


def _select_kernel(qit_ref, ki_ref, wit_ref, vf_ref, cut_ref, s_ref, *, tq, tk, n_keys, k_top, idx_bits):
    i = pl.program_id(1)
    q0 = i * tq
    n_chunks = (q0 + tq + tk - 1) // tk
    t_row = q0 + lax.broadcasted_iota(I32, (1, tq), 1)
    row_iota = lax.broadcasted_iota(I32, (tk, tq), 0)
    qit = qit_ref[0]
    w = wit_ref[0]

    def build(c, carry):
        k0 = pl.multiple_of(c * tk, tk)
        s = _scores_t(ki_ref[0, pl.ds(k0, tk), :], qit, w)
        causal = (k0 + row_iota) <= t_row
        s_ref[pl.ds(k0, tk), :] = jnp.where(causal, _to_key(s), INT_MIN)
        return carry

    lax.fori_loop(0, n_chunks, build, 0)
    n_nc = (n_keys - 1) - t_row
    _select_outputs(s_ref, n_chunks, tk, tq, n_nc, k_top, idx_bits, vf_ref, cut_ref)


def _select(qit, ki, wit, *, tq, tk, k_top):
    b, _, _, t = qit.shape
    idx_bits = max(1, int(np.ceil(np.log2(t + 1))))
    kern = functools.partial(_select_kernel, tq=tq, tk=tk, n_keys=t, k_top=k_top, idx_bits=idx_bits)
    return pl.pallas_call(
        kern,
        grid=(b, t // tq),
        in_specs=[
            pl.BlockSpec((1, IDX_HEADS, IDX_DIM, tq), lambda bb, i: (bb, 0, 0, i)),
            pl.BlockSpec((1, t, IDX_DIM), lambda bb, i: (bb, 0, 0)),
            pl.BlockSpec((1, 8, tq), lambda bb, i: (bb, 0, i)),
        ],
        out_specs=[pl.BlockSpec((1, 1, tq), lambda bb, i: (bb, 0, i)),
                   pl.BlockSpec((1, 1, tq), lambda bb, i: (bb, 0, i))],
        out_shape=[jax.ShapeDtypeStruct((b, 1, t), F32), jax.ShapeDtypeStruct((b, 1, t), I32)],
        scratch_shapes=[pltpu.VMEM((t, tq), I32)],
        compiler_params=_cparams("parallel", "parallel"),
        name="select",
    )(qit, ki, wit)


def _n_key_tiles(i, tq, tk):
    return (i * tq + tq + tk - 1) // tk


def _attn_kernel(qt_ref, k_ref, vt_ref, qit_ref, ki_ref, wit_ref, vf_ref, cut_ref, o_ref,
                 m_ref, l_ref, acc_ref, *, tq, tk):
    i = pl.program_id(1)
    j = pl.program_id(2)
    nkt = _n_key_tiles(i, tq, tk)

    @pl.when(j == 0)
    def _():
        m_ref[...] = jnp.full_like(m_ref, -jnp.inf)
        l_ref[...] = jnp.zeros_like(l_ref)
        acc_ref[...] = jnp.zeros_like(acc_ref)

    @pl.when(j < nkt)
    def _():
        s = _scores_t(ki_ref[0], qit_ref[0], wit_ref[0])
        kidx = j * tk + lax.broadcasted_iota(I32, (tk, tq), 0)
        t_row = i * tq + lax.broadcasted_iota(I32, (1, tq), 1)
        vf = vf_ref[0]
        bias = _select_bias(s, vf, kidx, cut_ref[0])
        bias = jnp.where(kidx <= t_row, bias, -jnp.inf)
        for h in range(N_HEADS):
            logit = jnp.dot(k_ref[0, h], qt_ref[0, h], preferred_element_type=F32) + bias
            m_old = m_ref[h:h + 1, :]
            m_new = jnp.maximum(m_old, jnp.max(logit, axis=0, keepdims=True))
            m_safe = jnp.where(m_new == -jnp.inf, 0.0, m_new)
            p = jnp.exp(logit - m_safe)
            alpha = jnp.exp(m_old - m_safe)
            l_ref[h:h + 1, :] = alpha * l_ref[h:h + 1, :] + jnp.sum(p, axis=0, keepdims=True)
            acc_ref[h] = alpha * acc_ref[h] + jnp.dot(vt_ref[0, h], p.astype(BF16),
                                                     preferred_element_type=F32)
            m_ref[h:h + 1, :] = m_new

    @pl.when(j == nkt - 1)
    def _():
        for h in range(N_HEADS):
            o_ref[0, h] = (acc_ref[h] / l_ref[h:h + 1, :]).astype(o_ref.dtype)


def _attn(qt, k, vt, qit, ki, wit, vf, cut, *, tq, tk):
    b, _, _, t = qt.shape
    kj = lambda i, j: jnp.minimum(j, _n_key_tiles(i, tq, tk) - 1)
    return pl.pallas_call(
        functools.partial(_attn_kernel, tq=tq, tk=tk),
        grid=(b, t // tq, t // tk),
        in_specs=[
            pl.BlockSpec((1, N_HEADS, HEAD_DIM, tq), lambda bb, i, j: (bb, 0, 0, i)),
            pl.BlockSpec((1, N_HEADS, tk, HEAD_DIM), lambda bb, i, j: (bb, 0, kj(i, j), 0)),
            pl.BlockSpec((1, N_HEADS, HEAD_DIM, tk), lambda bb, i, j: (bb, 0, 0, kj(i, j))),
            pl.BlockSpec((1, IDX_HEADS, IDX_DIM, tq), lambda bb, i, j: (bb, 0, 0, i)),
            pl.BlockSpec((1, tk, IDX_DIM), lambda bb, i, j: (bb, kj(i, j), 0)),
            pl.BlockSpec((1, 8, tq), lambda bb, i, j: (bb, 0, i)),
            pl.BlockSpec((1, 1, tq), lambda bb, i, j: (bb, 0, i)),
            pl.BlockSpec((1, 1, tq), lambda bb, i, j: (bb, 0, i)),
        ],
        out_specs=pl.BlockSpec((1, N_HEADS, HEAD_DIM, tq), lambda bb, i, j: (bb, 0, 0, i)),
        out_shape=jax.ShapeDtypeStruct((b, N_HEADS, HEAD_DIM, t), BF16),
        scratch_shapes=[pltpu.VMEM((N_HEADS, tq), F32), pltpu.VMEM((N_HEADS, tq), F32),
                        pltpu.VMEM((N_HEADS, HEAD_DIM, tq), F32)],
        compiler_params=_cparams("parallel", "parallel", "arbitrary"),
        name="attn",
    )(qt, k, vt, qit, ki, wit, vf, cut)


def _zoh(ar, ai, ldt):
    dt = jnp.exp(ldt)
    mag = jnp.exp(dt * ar)
    abr = mag * jnp.cos(dt * ai)
    abi = mag * jnp.sin(dt * ai)
    den = ar * ar + ai * ai
    nr = abr - 1.0
    ni = abi
    return abr, abi, (nr * ar + ni * ai) / den, (ni * ar - nr * ai) / den


def _disc_kernel(ar_ref, ai_ref, ldt_ref, ar16_ref, ai16_ref, ldt16_ref, br_ref, bi_ref,
                 pwr_ref, pwi_ref, bbr_ref, bbi_ref):
    abr, abi, _, _ = _zoh(ar_ref[...], ai_ref[...], ldt_ref[...])
    pr, pi = abr, abi
    for j in range(8):
        pwr_ref[j:j + 1, :] = pr
        pwi_ref[j:j + 1, :] = pi
        pr, pi = pr * abr - pi * abi, pr * abi + pi * abr
    _, _, fr, fi = _zoh(ar16_ref[...], ai16_ref[...], ldt16_ref[...])
    br = br_ref[...]
    bi = bi_ref[...]
    bbr_ref[...] = fr * br - fi * bi
    bbi_ref[...] = fr * bi + fi * br


def _discretize(a_re, a_im, log_dt, b_re, b_im):
    flat = lambda a: a.reshape(1, -1)
    ldt = jnp.broadcast_to(log_dt[:, None], (N_GROUPS, STATE_DIM))
    rep = lambda a: flat(jnp.broadcast_to(a[:, :, None], (N_GROUPS, STATE_DIM, GROUP_CH)))
    n16 = N_STATE * GROUP_CH
    pwr, pwi, bbr, bbi = pl.pallas_call(
        _disc_kernel,
        out_shape=[jax.ShapeDtypeStruct((8, N_STATE), F32), jax.ShapeDtypeStruct((8, N_STATE), F32),
                   jax.ShapeDtypeStruct((1, n16), F32), jax.ShapeDtypeStruct((1, n16), F32)],
        name="s5_discretize",
    )(flat(a_re), flat(a_im), flat(ldt), rep(a_re), rep(a_im), rep(ldt), flat(b_re), flat(b_im))
    return pwr, pwi, bbr.reshape(N_GROUPS, STATE_DIM, GROUP_CH), bbi.reshape(N_GROUPS, STATE_DIM, GROUP_CH)


def _block_diag_in(bb):
    eye = jnp.eye(N_GROUPS, dtype=bb.dtype)
    return jnp.einsum('gpc,gh->gchp', bb, eye).reshape(SSM_WIDTH, N_STATE)


def _block_diag_out(c):
    eye = jnp.eye(N_GROUPS, dtype=c.dtype)
    return jnp.einsum('gcp,gh->gphc', c, eye).reshape(N_STATE, SSM_WIDTH)


_LANE_CHUNK = 512


def _ssm_kernel(u_ref, bblk_ref, cblk_ref, pw_ref, d_ref, y_ref, xf_ref, x_ref, carry_ref, *, tt):
    ts = pl.program_id(1)

    @pl.when(ts == 0)
    def _():
        carry_ref[...] = jnp.zeros_like(carry_ref)

    u = u_ref[0]
    ub = u.astype(BF16)
    n_chunks = 2 * N_STATE // _LANE_CHUNK
    for c in range(n_chunks):
        cols = slice(c * _LANE_CHUNK, (c + 1) * _LANE_CHUNK)
        x_ref[:, cols] = jnp.dot(ub, bblk_ref[:, cols], preferred_element_type=F32)
    rid = lax.broadcasted_iota(I32, (8, _LANE_CHUNK), 0)

    def group(r, carry):
        r0 = pl.multiple_of(r * 8, 8)
        for c in range(N_STATE // _LANE_CHUNK):
            re = pl.ds(c * _LANE_CHUNK, _LANE_CHUNK)
            im = pl.ds(N_STATE + c * _LANE_CHUNK, _LANE_CHUNK)
            xr = x_ref[pl.ds(r0, 8), re]
            xi = x_ref[pl.ds(r0, 8), im]
            for d in (1, 2, 4):
                ar = pw_ref[d - 1:d, re]
                ai = pw_ref[d - 1:d, im]
                sr = jnp.where(rid >= d, pltpu.roll(xr, d, 0), 0.0)
                si = jnp.where(rid >= d, pltpu.roll(xi, d, 0), 0.0)
                xr, xi = xr + (ar * sr - ai * si), xi + (ar * si + ai * sr)
            cr = carry_ref[:, re]
            ci = carry_ref[:, im]
            pr = pw_ref[:, re]
            pi = pw_ref[:, im]
            xr, xi = xr + (pr * cr - pi * ci), xi + (pr * ci + pi * cr)
            x_ref[pl.ds(r0, 8), re] = xr
            x_ref[pl.ds(r0, 8), im] = xi
            carry_ref[:, re] = xr[7:8, :]
            carry_ref[:, im] = xi[7:8, :]
        return carry

    lax.fori_loop(0, tt // 8, group, 0)
    y = d_ref[...] * u
    for c in range(n_chunks):
        cols = slice(c * _LANE_CHUNK, (c + 1) * _LANE_CHUNK)
        y = y + jnp.dot(x_ref[:, cols].astype(BF16), cblk_ref[cols, :], preferred_element_type=F32)
    y_ref[0] = y
    xf_ref[0] = carry_ref[...]


def _ssm(u, bblk, cblk, pw, d_skip, *, tt):
    b, t, _ = u.shape
    return pl.pallas_call(
        functools.partial(_ssm_kernel, tt=tt),
        grid=(b, t // tt),
        in_specs=[
            pl.BlockSpec((1, tt, SSM_WIDTH), lambda bb, s: (bb, s, 0)),
            pl.BlockSpec((SSM_WIDTH, 2 * N_STATE), lambda bb, s: (0, 0)),
            pl.BlockSpec((2 * N_STATE, SSM_WIDTH), lambda bb, s: (0, 0)),
            pl.BlockSpec((8, 2 * N_STATE), lambda bb, s: (0, 0)),
            pl.BlockSpec((1, SSM_WIDTH), lambda bb, s: (0, 0)),
        ],
        out_specs=[pl.BlockSpec((1, tt, SSM_WIDTH), lambda bb, s: (bb, s, 0)),
                   pl.BlockSpec((1, 1, 2 * N_STATE), lambda bb, s: (bb, 0, 0))],
        out_shape=[jax.ShapeDtypeStruct((b, t, SSM_WIDTH), F32),
                   jax.ShapeDtypeStruct((b, 1, 2 * N_STATE), F32)],
        scratch_shapes=[pltpu.VMEM((tt, 2 * N_STATE), F32), pltpu.VMEM((1, 2 * N_STATE), F32)],
        compiler_params=_cparams("parallel", "arbitrary"),
        name="s5_scan",
    )(u, bblk, cblk, pw, d_skip)


def _ssm_step_kernel(u_ref, x0_ref, bblk_ref, cblk_ref, pw_ref, d_ref, y_ref, x_ref):
    u = u_ref[...]
    bu = jnp.dot(u, bblk_ref[...], preferred_element_type=F32, precision=lax.Precision.HIGHEST)
    ar = pw_ref[0:1, :N_STATE]
    ai = pw_ref[0:1, N_STATE:]
    x0r = x0_ref[:, :N_STATE]
    x0i = x0_ref[:, N_STATE:]
    xr = ar * x0r - ai * x0i + bu[:, :N_STATE]
    xi = ar * x0i + ai * x0r + bu[:, N_STATE:]
    x_ref[:, :N_STATE] = xr
    x_ref[:, N_STATE:] = xi
    y = jnp.dot(x_ref[...], cblk_ref[...], preferred_element_type=F32, precision=lax.Precision.HIGHEST)
    y_ref[...] = y + d_ref[...] * u


def _ssm_step(u, x0, bblk, cblk, pw, d_skip):
    n = u.shape[0]
    return pl.pallas_call(
        _ssm_step_kernel,
        out_shape=[jax.ShapeDtypeStruct((n, SSM_WIDTH), F32), jax.ShapeDtypeStruct((n, 2 * N_STATE), F32)],
        compiler_params=pltpu.CompilerParams(vmem_limit_bytes=VMEM_LIMIT_BYTES),
        name="s5_step",
    )(u, x0, bblk, cblk, pw, d_skip)


def _gelu_tanh(x):
    c = np.float32(np.sqrt(2.0 / np.pi))
    return 0.5 * x * (1.0 + jnp.tanh(c * (x + 0.044715 * (x * x * x))))


def _merge_kernel(x_ref, oa_ref, ys_ref, g_ref, wgt_ref, wglu_ref, bglu_ref, wpa_ref, wpb_ref, wout_ref, o_ref):
    x = x_ref[...]
    h = _rmsnorm(x, g_ref[...]).astype(BF16)
    gates = _sigmoid(jnp.dot(h, wgt_ref[...], preferred_element_type=F32))
    ys = _gelu_tanh(ys_ref[...])
    glu = jnp.dot(ys.astype(BF16), wglu_ref[...], preferred_element_type=F32) + bglu_ref[...]
    ob = ys * _sigmoid(glu)
    pa = jnp.dot(oa_ref[...], wpa_ref[...], preferred_element_type=F32)
    pb = jnp.dot(ob.astype(BF16), wpb_ref[...], preferred_element_type=F32)
    merged = gates[:, :D_MODEL] * pa + gates[:, D_MODEL:] * pb
    o_ref[...] = x + jnp.dot(merged.astype(BF16), wout_ref[...], preferred_element_type=F32)


def _merge(x, oa, ys, g, wgt, wglu, bglu, wpa, wpb, wout, *, tm):
    m, d = x.shape
    row = lambda n: pl.BlockSpec((tm, n), lambda i: (i, 0))
    full = lambda a: pl.BlockSpec(a.shape, lambda i: (0, 0))
    return pl.pallas_call(
        _merge_kernel,
        grid=(m // tm,),
        in_specs=[row(d), row(D_ATTN), row(SSM_WIDTH), full(g), full(wgt), full(wglu), full(bglu),
                  full(wpa), full(wpb), full(wout)],
        out_specs=row(d),
        out_shape=jax.ShapeDtypeStruct((m, d), F32),
        compiler_params=_cparams("parallel"),
        name="merge",
    )(x, oa, ys, g, wgt, wglu, bglu, wpa, wpb, wout)


_PAGES_PER_STEP = 8


def _page_specs(block, n_lead, page_size):
    def spec(r):
        def index_map(b, p, pt):
            return (pt[b, p * _PAGES_PER_STEP + r],) + (0,) * n_lead
        return pl.BlockSpec(block, index_map)
    return [spec(r) for r in range(_PAGES_PER_STEP)]


def _sample_scores_kernel(pt_ref, qi_ref, w_ref, knew_ref, *refs, page_size):
    pages = refs[:_PAGES_PER_STEP]
    o_ref, oself_ref = refs[_PAGES_PER_STEP:]
    qi = qi_ref[0]
    w = w_ref[0]

    def score(k_rows):
        d = jnp.dot(k_rows, qi, preferred_element_type=F32)
        return jnp.sum(jnp.maximum(d, 0.0) * w, axis=1, keepdims=True)

    for r in range(_PAGES_PER_STEP):
        o_ref[0, r * page_size:(r + 1) * page_size, :] = score(pages[r][0].astype(BF16))
    oself_ref[0] = score(jnp.broadcast_to(knew_ref[0], (8, IDX_DIM)))


def _sample_scores(page_table, qi_cols, w_row, ki_new, cache_kidx):
    db, n_pages = page_table.shape
    _, page_size, _ = cache_kidx.shape
    steps = n_pages // _PAGES_PER_STEP
    per_b = lambda shape: pl.BlockSpec((1,) + shape, lambda b, p, pt: (b,) + (0,) * len(shape))
    return pl.pallas_call(
        functools.partial(_sample_scores_kernel, page_size=page_size),
        grid_spec=pltpu.PrefetchScalarGridSpec(
            num_scalar_prefetch=1,
            grid=(db, steps),
            in_specs=[per_b((IDX_DIM, 128)), per_b((1, 128)), per_b((1, IDX_DIM))]
                     + _page_specs((1, page_size, IDX_DIM), 2, page_size),
            out_specs=[pl.BlockSpec((1, _PAGES_PER_STEP * page_size, 1), lambda b, p, pt: (b, p, 0)),
                       per_b((8, 1))],
        ),
        out_shape=[jax.ShapeDtypeStruct((db, n_pages * page_size, 1), F32),
                   jax.ShapeDtypeStruct((db, 8, 1), F32)],
        compiler_params=_cparams("parallel", "arbitrary"),
        name="sample_scores",
    )(page_table, qi_cols, w_row, ki_new, *([cache_kidx] * _PAGES_PER_STEP))


def _sample_select_kernel(st_ref, vf_ref, cut_ref, s_ref, *, tk, tq, n_keys, k_top, idx_bits):
    n_rows = st_ref.shape[0]
    row_iota = lax.broadcasted_iota(I32, (tk, tq), 0)
    for c in range(n_rows // tk):
        key = _to_key(st_ref[c * tk:(c + 1) * tk, :])
        s_ref[c * tk:(c + 1) * tk, :] = jnp.where(c * tk + row_iota < n_keys, key, INT_MIN)
    n_nc = jnp.zeros((1, tq), I32)
    _select_outputs(s_ref, n_rows // tk, tk, tq, n_nc, k_top, idx_bits, vf_ref, cut_ref)


def _sample_select(scores_t, *, n_keys, k_top, tk):
    n_rows, tq = scores_t.shape
    idx_bits = max(1, int(np.ceil(np.log2(n_keys + 1))))
    kern = functools.partial(_sample_select_kernel, tk=tk, tq=tq, n_keys=n_keys, k_top=k_top,
                             idx_bits=idx_bits)
    return pl.pallas_call(
        kern,
        out_shape=[jax.ShapeDtypeStruct((1, 1, tq), F32), jax.ShapeDtypeStruct((1, 1, tq), I32)],
        scratch_shapes=[pltpu.VMEM((n_rows, tq), I32)],
        compiler_params=pltpu.CompilerParams(vmem_limit_bytes=VMEM_LIMIT_BYTES),
        name="sample_select",
    )(scores_t)


def _sample_attn_kernel(pt_ref, qbd_ref, s_ref, sself_ref, vf_ref, cut_ref, knew_ref, vnew_ref, *refs,
                        page_size, past_len):
    kp = refs[:_PAGES_PER_STEP]
    vp = refs[_PAGES_PER_STEP:2 * _PAGES_PER_STEP]
    o_ref, m_ref, l_ref, acc_ref = refs[2 * _PAGES_PER_STEP:]
    p_step = pl.program_id(1)
    qbd = qbd_ref[0]
    vf = vf_ref[0]
    cut = cut_ref[0]
    lane_head = lax.broadcasted_iota(I32, (N_HEADS, D_ATTN), 1) // HEAD_DIM
    head_mask = lane_head == lax.broadcasted_iota(I32, (N_HEADS, D_ATTN), 0)

    @pl.when(p_step == 0)
    def _():
        m_ref[...] = jnp.full_like(m_ref, -jnp.inf)
        l_ref[...] = jnp.zeros_like(l_ref)
        acc_ref[...] = jnp.zeros_like(acc_ref)

    def to_col(row):
        return jnp.transpose(jnp.broadcast_to(row, (128, 128)))[:N_HEADS, 0:1]

    def absorb(k_rows, v_rows, s_col, kidx):
        bias = _select_bias(s_col, vf, kidx, cut)
        logit = jnp.dot(k_rows.astype(BF16), qbd, preferred_element_type=F32) + bias
        m_old = m_ref[...]
        m_new = jnp.maximum(m_old, jnp.max(logit, axis=0, keepdims=True))
        m_safe = jnp.where(m_new == -jnp.inf, 0.0, m_new)
        p = jnp.exp(logit - m_safe)
        alpha = jnp.exp(m_old - m_safe)
        l_ref[...] = alpha * l_ref[...] + jnp.sum(p, axis=0, keepdims=True)
        p_rows = jnp.transpose(p)[:N_HEADS, :].astype(BF16)
        pv = jnp.dot(p_rows, v_rows.astype(BF16), preferred_element_type=F32)
        acc_ref[...] = to_col(alpha) * acc_ref[...] + pv
        m_ref[...] = m_new

    for r in range(_PAGES_PER_STEP):
        base = (p_step * _PAGES_PER_STEP + r) * page_size
        kidx = base + lax.broadcasted_iota(I32, (page_size, 1), 0)
        absorb(kp[r][0], vp[r][0], s_ref[0, r * page_size:(r + 1) * page_size, :], kidx)

    @pl.when(p_step == pl.num_programs(1) - 1)
    def _():
        rows = lax.broadcasted_iota(I32, (128, 1), 0)
        s_col = jnp.where(rows == 0, sself_ref[0], -jnp.inf)
        kidx = jnp.where(rows == 0, past_len, INT_MAX)
        absorb(jnp.broadcast_to(knew_ref[0], (128, D_ATTN)), jnp.broadcast_to(vnew_ref[0], (128, D_ATTN)),
               s_col, kidx)
        out = jnp.where(head_mask, acc_ref[...] / to_col(l_ref[...]), 0.0)
        o_ref[0] = jnp.sum(out, axis=0, keepdims=True)


def _sample_attn(page_table, qbd, scores, s_self, vf, cut, k_new, v_new, cache_k, cache_v):
    db, n_pages = page_table.shape
    _, page_size, _ = cache_k.shape
    steps = n_pages // _PAGES_PER_STEP
    past_len = n_pages * page_size
    per_b = lambda shape: pl.BlockSpec((1,) + shape, lambda b, p, pt: (b,) + (0,) * len(shape))
    return pl.pallas_call(
        functools.partial(_sample_attn_kernel, page_size=page_size, past_len=past_len),
        grid_spec=pltpu.PrefetchScalarGridSpec(
            num_scalar_prefetch=1,
            grid=(db, steps),
            in_specs=[per_b((D_ATTN, 128)),
                      pl.BlockSpec((1, _PAGES_PER_STEP * page_size, 1), lambda b, p, pt: (b, p, 0)),
                      per_b((1, 1)), per_b((1, 128)), per_b((1, 128)),
                      per_b((1, D_ATTN)), per_b((1, D_ATTN))]
                     + _page_specs((1, page_size, D_ATTN), 2, page_size)
                     + _page_specs((1, page_size, D_ATTN), 2, page_size),
            out_specs=per_b((1, D_ATTN)),
            scratch_shapes=[pltpu.VMEM((1, 128), F32), pltpu.VMEM((1, 128), F32),
                            pltpu.VMEM((N_HEADS, D_ATTN), F32)],
        ),
        out_shape=jax.ShapeDtypeStruct((db, 1, D_ATTN), F32),
        compiler_params=_cparams("parallel", "arbitrary"),
        name="sample_attn",
    )(page_table, qbd, scores, s_self, vf, cut, k_new, v_new,
      *([cache_k] * _PAGES_PER_STEP), *([cache_v] * _PAGES_PER_STEP))


def _pick(n, pref):
    for t in pref:
        if n % t == 0:
            return t
    return n


def _pack_w_in(w_in):
    o = np.cumsum([0, D_ATTN, D_ATTN, D_ATTN, IDX_HEADS * IDX_DIM, IDX_DIM, IDX_HEADS, SSM_WIDTH, 2 * D_MODEL])
    z = lambda n: jnp.zeros((D_MODEL, n), w_in.dtype)
    w_all = jnp.concatenate([
        w_in[:, o[0]:o[4]],
        w_in[:, o[4]:o[5]], z(_PWI - _PKI - IDX_DIM),
        w_in[:, o[5]:o[6]], z(_PU - _PWI - IDX_HEADS),
        w_in[:, o[6]:o[7]],
    ], axis=1)
    return w_all.astype(BF16), w_in[:, o[7]:o[8]].astype(BF16)


def _prompt_layer(x, lw, ssm, final_norm):
    bsz, seq, d = x.shape
    m = bsz * seq
    tm = _pick(m, (512, 256, 128, 64, 32, 16, 8))
    x1 = _ffn(x.reshape(m, d), *lw["ffn1"], lw["g_final"], final_norm=False, tm=tm, tf=lw["tf"])
    q_b, k_f, v_f, k_b, v_b, qi_b, ki_f, ki_b, wi_f, u_f = _proj(x1, lw["g_mix"], lw["w_all"], tm=tm)

    heads_t = lambda a, nh: a.reshape(bsz, seq, nh, -1).transpose(0, 2, 3, 1)
    qt = heads_t(q_b, N_HEADS)
    vt = heads_t(v_b, N_HEADS)
    kh = k_b.reshape(bsz, seq, N_HEADS, HEAD_DIM).transpose(0, 2, 1, 3)
    qit = heads_t(qi_b, IDX_HEADS)
    kib = ki_b.reshape(bsz, seq, IDX_DIM)
    wit = wi_f[:, :8].reshape(bsz, seq, 8).transpose(0, 2, 1)

    k_top = min(TOPK_MAX, seq // 4)
    vf, cut = _select(qit, kib, wit, tq=_pick(seq, (128,)), tk=_pick(seq, (256, 128)), k_top=k_top)
    o_t = _attn(qt, kh, vt, qit, kib, wit, vf, cut,
                tq=_pick(seq, (256, 128)), tk=_pick(seq, (512, 256, 128)))
    oa = o_t.transpose(0, 3, 1, 2).reshape(m, D_ATTN)

    tt = _pick(seq, (256, 128, 64, 32, 16, 8))
    ys, xfin = _ssm(u_f.reshape(bsz, seq, SSM_WIDTH), ssm["bblk"].astype(BF16), ssm["cblk"].astype(BF16),
                    ssm["pw"], ssm["d_skip"], tt=tt)
    x2 = _merge(x1, oa, ys.reshape(m, SSM_WIDTH), *lw["merge"], tm=tm)
    y = _ffn(x2, *lw["ffn2"], lw["g_final"], final_norm=final_norm, tm=tm, tf=lw["tf"])
    new = (k_f.reshape(bsz, seq, N_HEADS, HEAD_DIM), v_f.reshape(bsz, seq, N_HEADS, HEAD_DIM),
           ki_f.reshape(bsz, seq, IDX_DIM),
           xfin[:, 0, :N_STATE].reshape(bsz, N_GROUPS, STATE_DIM),
           xfin[:, 0, N_STATE:].reshape(bsz, N_GROUPS, STATE_DIM))
    return y.reshape(bsz, seq, d), new


def _sample_layer(x, lw, ssm, final_norm, cache_k, cache_v, cache_kidx, st_re, st_im, page_table):
    db, ds, d = x.shape
    assert ds == 1, "one new token per sample sequence"
    n_pool, page_size = cache_k.shape[0], cache_k.shape[1]
    past_len = page_table.shape[1] * page_size
    s1 = _ffn(x.reshape(db, d), *lw["ffn1"], lw["g_final"], final_norm=False, tm=db, tf=lw["tf"])
    q_b, k_f, v_f, _, _, qi_b, ki_f, ki_b, wi_f, u_f = _proj(s1, lw["g_mix"], lw["w_all"], tm=db)

    qi_cols = jnp.zeros((db, IDX_DIM, 128), BF16).at[:, :, :IDX_HEADS].set(
        qi_b.reshape(db, IDX_HEADS, IDX_DIM).transpose(0, 2, 1))
    past_scores, self_scores = _sample_scores(page_table, qi_cols, wi_f.reshape(db, 1, 128),
                                              ki_b.reshape(db, 1, IDX_DIM), cache_kidx)
    s_self = self_scores[:, 0, 0]

    n_keys = past_len + 1
    k_top = min(TOPK_MAX, n_keys // 4)
    tk = 256
    n_rows = -(-n_keys // tk) * tk
    lanes = -(-db // 128) * 128
    all_scores = jnp.concatenate([past_scores[:, :, 0], s_self[:, None]], axis=1)
    scores_t = jnp.zeros((n_rows, lanes), F32).at[:n_keys, :db].set(all_scores.T)
    vf, cut = _sample_select(scores_t, n_keys=n_keys, k_top=k_top, tk=tk)
    vf_b = jnp.broadcast_to(vf[0, 0, :db, None, None], (db, 1, 128))
    cut_b = jnp.broadcast_to(cut[0, 0, :db, None, None], (db, 1, 128))

    head_of = jnp.arange(D_ATTN) // HEAD_DIM
    qbd = jnp.where(head_of[None, :, None] == jnp.arange(128)[None, None, :],
                    q_b[:, :, None], jnp.zeros((), BF16))
    o_s = _sample_attn(page_table, qbd, past_scores, s_self.reshape(db, 1, 1), vf_b, cut_b,
                       k_f.reshape(db, 1, D_ATTN), v_f.reshape(db, 1, D_ATTN),
                       cache_k.reshape(n_pool, page_size, D_ATTN),
                       cache_v.reshape(n_pool, page_size, D_ATTN))
    oa = o_s.reshape(db, D_ATTN).astype(BF16)

    x0 = jnp.concatenate([st_re.reshape(db, N_STATE), st_im.reshape(db, N_STATE)], axis=1)
    ys, x_new = _ssm_step(u_f, x0, ssm["bblk"], ssm["cblk"], ssm["pw"], ssm["d_skip"])
    s2 = _merge(s1, oa, ys, *lw["merge"], tm=db)
    y = _ffn(s2, *lw["ffn2"], lw["g_final"], final_norm=final_norm, tm=db, tf=lw["tf"])
    new = (k_f.reshape(db, ds, N_HEADS, HEAD_DIM), v_f.reshape(db, ds, N_HEADS, HEAD_DIM),
           ki_f.reshape(db, ds, IDX_DIM),
           x_new[:, :N_STATE].reshape(db, N_GROUPS, STATE_DIM),
           x_new[:, N_STATE:].reshape(db, N_GROUPS, STATE_DIM))
    return y.reshape(db, ds, d), new


def kernel(x_prompt, x_sample, cache_k, cache_v, cache_kidx, state_ssm_re, state_ssm_im, page_table,
           g_ffn1, w1_gate, w1_up, w1_down, g_mix, w_in, a_re, a_im, log_dt, b_re, b_im, c_re, c_im,
           d_skip, w_glu, b_glu, w_pa, w_pb, w_out, g_ffn2, w2_gate, w2_up, w2_down, g_final):
    depth = w_in.shape[0]
    row = lambda a: a.reshape(1, -1)
    bf = lambda a: a.astype(BF16)
    xp, xs = x_prompt, x_sample
    new_p, new_s = [], []
    for l in range(depth):
        w_all, w_gates = _pack_w_in(w_in[l])
        lw = dict(
            ffn1=(row(g_ffn1[l]), bf(w1_gate[l]), bf(w1_up[l]), bf(w1_down[l])),
            ffn2=(row(g_ffn2[l]), bf(w2_gate[l]), bf(w2_up[l]), bf(w2_down[l])),
            merge=(row(g_mix[l]), w_gates, bf(w_glu[l]), row(b_glu[l]), bf(w_pa[l]), bf(w_pb[l]), bf(w_out[l])),
            g_mix=row(g_mix[l]), w_all=w_all, g_final=row(g_final),
            tf=_pick(w1_gate.shape[2], (1408, 1024, 512, 256, 128)),
        )
        pwr, pwi, bbr, bbi = _discretize(a_re[l], a_im[l], log_dt[l], b_re[l], b_im[l])
        ssm = dict(
            pw=jnp.concatenate([pwr, pwi], axis=1),
            bblk=jnp.concatenate([_block_diag_in(bbr), _block_diag_in(bbi)], axis=1),
            cblk=jnp.concatenate([_block_diag_out(c_re[l]), -_block_diag_out(c_im[l])], axis=0),
            d_skip=row(d_skip[l]),
        )
        last = l == depth - 1
        xp, st_p = _prompt_layer(xp, lw, ssm, last)
        xs, st_s = _sample_layer(xs, lw, ssm, last, cache_k[l], cache_v[l], cache_kidx[l],
                                 state_ssm_re[l], state_ssm_im[l], page_table)
        new_p.append(st_p)
        new_s.append(st_s)
    stack = lambda states, i: jnp.stack([s[i] for s in states])
    return (xp, xs) + tuple(stack(new_p, i) for i in range(5)) + tuple(stack(new_s, i) for i in range(5))
```

```python
import functools

import numpy as np
import jax
import jax.numpy as jnp
from jax import lax
from jax.experimental import pallas as pl
from jax.experimental.pallas import tpu as pltpu

F32 = jnp.float32
BF16 = jnp.bfloat16
I32 = jnp.int32

D_MODEL = 1024
N_HEADS = 8
HEAD_DIM = 64
D_ATTN = N_HEADS * HEAD_DIM
IDX_HEADS = 4
IDX_DIM = 64
TOPK_MAX = 256
SSM_WIDTH = 512
GROUP_CH = 16
N_GROUPS = SSM_WIDTH // GROUP_CH
STATE_DIM = 64
N_STATE = N_GROUPS * STATE_DIM
NORM_EPS = 1e-6
NEG_BIG = -1e30

INT_MAX = 2 ** 31 - 1

VMEM_LIMIT_BYTES = 56 * 1024 * 1024


def _cparams(*sem):
    return pltpu.CompilerParams(dimension_semantics=sem, vmem_limit_bytes=VMEM_LIMIT_BYTES)


def _rmsnorm(x, g):
    y = x * lax.rsqrt(jnp.mean(x * x, axis=-1, keepdims=True) + NORM_EPS)
    return y * g


def _sigmoid(x):
    return 1.0 / (1.0 + jnp.exp(-x))


def _ffn_kernel(x_ref, g_ref, wg_ref, wu_ref, wd_ref, gf_ref, o_ref, xn_ref, acc_ref, *, final_norm):
    f = pl.program_id(1)

    @pl.when(f == 0)
    def _():
        xn_ref[...] = _rmsnorm(x_ref[...], g_ref[...]).astype(BF16)
        acc_ref[...] = jnp.zeros_like(acc_ref)

    xn = xn_ref[...]
    gate = jnp.dot(xn, wg_ref[...], preferred_element_type=F32)
    up = jnp.dot(xn, wu_ref[...], preferred_element_type=F32)
    act = (gate * _sigmoid(gate)) * up
    acc_ref[...] += jnp.dot(act.astype(BF16), wd_ref[...], preferred_element_type=F32)

    @pl.when(f == pl.num_programs(1) - 1)
    def _():
        y = x_ref[...] + 0.5 * acc_ref[...]
        if final_norm:
            y = _rmsnorm(y, gf_ref[...])
        o_ref[...] = y


def _ffn(x, g, wg, wu, wd, g_final, *, final_norm, tm, tf):
    m, d = x.shape
    dff = wg.shape[1]
    return pl.pallas_call(
        functools.partial(_ffn_kernel, final_norm=final_norm),
        grid=(m // tm, dff // tf),
        in_specs=[
            pl.BlockSpec((tm, d), lambda i, f: (i, 0)),
            pl.BlockSpec((1, d), lambda i, f: (0, 0)),
            pl.BlockSpec((d, tf), lambda i, f: (0, f)),
            pl.BlockSpec((d, tf), lambda i, f: (0, f)),
            pl.BlockSpec((tf, d), lambda i, f: (f, 0)),
            pl.BlockSpec((1, d), lambda i, f: (0, 0)),
        ],
        out_specs=pl.BlockSpec((tm, d), lambda i, f: (i, 0)),
        out_shape=jax.ShapeDtypeStruct((m, d), F32),
        scratch_shapes=[pltpu.VMEM((tm, d), BF16), pltpu.VMEM((tm, d), F32)],
        compiler_params=_cparams("parallel", "arbitrary"),
        name="ffn",
    )(x, g, wg, wu, wd, g_final)


_PQ, _PK, _PV, _PQI, _PKI, _PWI, _PU, _PEND = 0, 512, 1024, 1536, 1792, 1920, 2048, 2560


def _proj_kernel(x_ref, g_ref, w_ref, q_ref, kf_ref, vf_ref, kb_ref, vb_ref,
                 qi_ref, kif_ref, kib_ref, wi_ref, u_ref):
    h = _rmsnorm(x_ref[...], g_ref[...]).astype(BF16)
    p = jnp.dot(h, w_ref[...], preferred_element_type=F32)
    q_ref[...] = (p[:, _PQ:_PK] * (HEAD_DIM ** -0.5)).astype(BF16)
    k = p[:, _PK:_PV]
    v = p[:, _PV:_PQI]
    kf_ref[...] = k
    vf_ref[...] = v
    kb_ref[...] = k.astype(BF16)
    vb_ref[...] = v.astype(BF16)
    qi_ref[...] = p[:, _PQI:_PKI].astype(BF16)
    ki = p[:, _PKI:_PKI + IDX_DIM]
    kif_ref[...] = ki
    kib_ref[...] = ki.astype(BF16)
    wi_ref[...] = p[:, _PWI:_PU] * ((IDX_HEADS * IDX_DIM) ** -0.5)
    u_ref[...] = p[:, _PU:_PEND]


def _proj(x, g, w_all, *, tm):
    m, d = x.shape
    row = lambda n: pl.BlockSpec((tm, n), lambda i: (i, 0))
    outs = [
        (D_ATTN, BF16), (D_ATTN, F32), (D_ATTN, F32), (D_ATTN, BF16), (D_ATTN, BF16),
        (IDX_HEADS * IDX_DIM, BF16), (IDX_DIM, F32), (IDX_DIM, BF16), (128, F32), (SSM_WIDTH, F32),
    ]
    return pl.pallas_call(
        _proj_kernel,
        grid=(m // tm,),
        in_specs=[row(d), pl.BlockSpec((1, d), lambda i: (0, 0)),
                  pl.BlockSpec((d, _PEND), lambda i: (0, 0))],
        out_specs=[row(n) for n, _ in outs],
        out_shape=[jax.ShapeDtypeStruct((m, n), dt) for n, dt in outs],
        compiler_params=_cparams("parallel"),
        name="proj",
    )(x, g, w_all)


def _scores_t(kc, qit, w):
    s = None
    for h in range(IDX_HEADS):
        d = jnp.dot(kc, qit[h], preferred_element_type=F32)
        term = jnp.maximum(d, 0.0) * w[h:h + 1, :]
        s = term if s is None else s + term
    return s


def _select_bias(s, vf, kidx, cut):
    tie = jnp.where(kidx < cut, 0.0, -jnp.inf)
    return jnp.where(s > vf, 0.0, jnp.where(s == vf, tie, -jnp.inf))


def _fold8(x, op):
    return op(x.reshape(x.shape[0] // 8, 8, x.shape[1]), axis=0)


_SETTLE_FIRST, _SETTLE_EVERY, _BISECT_CAP = 10, 5, 320


def _topk_threshold(s_ref, n_chunks, tk, tq, mx, mn, n_hi, n_nc, k_top, thr_ref, cut_ref, fst_ref, ist_ref):
    row_iota = lax.broadcasted_iota(I32, (tk, tq), 0)
    zeros_i = jnp.zeros((8, tq), I32)

    def chunk(c):
        k0 = pl.multiple_of(c * tk, tk)
        return s_ref[pl.ds(k0, tk), :], k0

    def count_gt(v):
        def body(c, acc):
            blk, _ = chunk(c)
            return acc + _fold8(jnp.where(blk > v, 1, 0), jnp.sum)
        real = jnp.sum(lax.fori_loop(0, n_chunks, body, zeros_i), axis=0, keepdims=True)
        return real + jnp.where(NEG_BIG > v, n_nc, 0)

    def count_gt_eq(v):
        def body(c, acc):
            blk, _ = chunk(c)
            return (acc[0] + _fold8(jnp.where(blk > v, 1, 0), jnp.sum),
                    acc[1] + _fold8(jnp.where(blk == v, 1, 0), jnp.sum))
        g, e = lax.fori_loop(0, n_chunks, body, (zeros_i, zeros_i))
        g = jnp.sum(g, axis=0, keepdims=True) + jnp.where(NEG_BIG > v, n_nc, 0)
        return g, jnp.sum(e, axis=0, keepdims=True)

    def max_le(v):
        def body(c, acc):
            blk, _ = chunk(c)
            return jnp.maximum(acc, _fold8(jnp.where(blk <= v, blk, -jnp.inf), jnp.max))
        m = jnp.max(lax.fori_loop(0, n_chunks, body, jnp.full((8, tq), -jnp.inf, F32)), axis=0, keepdims=True)
        return jnp.where((n_nc > 0) & (NEG_BIG <= v), jnp.maximum(m, NEG_BIG), m)

    def settle(m):
        gt, eq_real = count_gt_eq(m)
        eq = eq_real + jnp.where(m == NEG_BIG, n_nc, 0)
        found = (ist_ref[0:1, :] == 0) & (gt + eq >= k_top)
        need = k_top - gt
        fst_ref[2:3, :] = jnp.where(found, m, fst_ref[2:3, :])
        ist_ref[1:2, :] = jnp.where(found, need, ist_ref[1:2, :])
        ist_ref[2:3, :] = jnp.where(found, jnp.where(eq_real > need, 1, 0), ist_ref[2:3, :])
        ist_ref[3:4, :] = jnp.where(found, INT_MAX, ist_ref[3:4, :])
        ist_ref[0:1, :] = jnp.where(found, 1, ist_ref[0:1, :])

    low = n_hi < k_top
    hi0 = jnp.where(low, NEG_BIG, mx)
    fst_ref[0:1, :] = jnp.where(low, jnp.minimum(mn, NEG_BIG), jnp.maximum(mn, NEG_BIG))
    fst_ref[1:2, :] = hi0
    fst_ref[2:3, :] = hi0
    ist_ref[...] = jnp.zeros_like(ist_ref)
    settle(hi0)

    def active_lanes():
        return jnp.max(1 - ist_ref[0:1, :])

    def step(carry):
        it, _ = carry
        lo = fst_ref[0:1, :]
        hi = fst_ref[1:2, :]
        active = ist_ref[0:1, :] == 0
        mid = 0.5 * lo + 0.5 * hi
        c = count_gt(mid)
        hit = active & (c == k_top)
        fst_ref[2:3, :] = jnp.where(hit, mid, fst_ref[2:3, :])
        ist_ref[0:1, :] = jnp.where(hit, 1, ist_ref[0:1, :])
        fst_ref[0:1, :] = jnp.where(active & (c > k_top), mid, lo)
        fst_ref[1:2, :] = jnp.where(active & (c < k_top), mid, hi)
        stuck = active & ((mid <= lo) | (mid >= hi))
        due = (it >= _SETTLE_FIRST) & ((it - _SETTLE_FIRST) % _SETTLE_EVERY == 0)

        @pl.when(due | (jnp.max(jnp.where(stuck, 1, 0)) > 0))
        def _():
            settle(max_le(fst_ref[1:2, :]))

        return it + 1, active_lanes()

    lax.while_loop(lambda carry: (carry[1] > 0) & (carry[0] < _BISECT_CAP), step,
                   (jnp.int32(0), active_lanes()))
    thr_ref[0] = fst_ref[2:3, :]
    cut_ref[0] = ist_ref[3:4, :]

    @pl.when(jnp.max(ist_ref[2:3, :]) > 0)
    def _():
        thr = fst_ref[2:3, :]
        need = ist_ref[1:2, :].astype(F32)
        tri = jnp.where(lax.broadcasted_iota(I32, (tk, tk), 1) <= lax.broadcasted_iota(I32, (tk, tk), 0),
                        1.0, 0.0).astype(BF16)

        def body(c, carry):
            seen, cut = carry
            blk, k0 = chunk(c)
            tie = blk == thr
            rank = jnp.dot(tri, jnp.where(tie, 1.0, 0.0).astype(BF16), preferred_element_type=F32) + seen
            bound = jnp.where(tie, jnp.where(rank <= need, k0 + row_iota + 1, 0), 0)
            return rank[tk - 1:tk, :], jnp.maximum(cut, jnp.max(bound, axis=0, keepdims=True))

        _, cut = lax.fori_loop(0, n_chunks, body, (jnp.zeros((1, tq), F32), jnp.zeros((1, tq), I32)))
        cut_ref[0] = jnp.where(ist_ref[2:3, :] > 0, cut, ist_ref[3:4, :])


def _select_kernel(qit_ref, ki_ref, wit_ref, thr_ref, cut_ref, s_ref, fst_ref, ist_ref, *, tq, tk, n_keys, k_top):
    i = pl.program_id(1)
    q0 = i * tq
    n_chunks = (q0 + tq + tk - 1) // tk
    t_row = q0 + lax.broadcasted_iota(I32, (1, tq), 1)
    row_iota = lax.broadcasted_iota(I32, (tk, tq), 0)
    qit = qit_ref[0]
    w = wit_ref[0]

    def build(c, carry):
        mx, mn, n_hi = carry
        k0 = pl.multiple_of(c * tk, tk)
        s = _scores_t(ki_ref[0, pl.ds(k0, tk), :], qit, w)
        causal = (k0 + row_iota) <= t_row
        sm = jnp.where(causal, s, -jnp.inf)
        s_ref[pl.ds(k0, tk), :] = sm
        return (jnp.maximum(mx, _fold8(sm, jnp.max)),
                jnp.minimum(mn, _fold8(jnp.where(causal, s, jnp.inf), jnp.min)),
                n_hi + _fold8(jnp.where(sm > NEG_BIG, 1, 0), jnp.sum))

    mx, mn, n_hi = lax.fori_loop(
        0, n_chunks, build,
        (jnp.full((8, tq), -jnp.inf, F32), jnp.full((8, tq), jnp.inf, F32), jnp.zeros((8, tq), I32)))
    n_nc = (n_keys - 1) - t_row
    _topk_threshold(s_ref, n_chunks, tk, tq, jnp.max(mx, axis=0, keepdims=True),
                    jnp.min(mn, axis=0, keepdims=True), jnp.sum(n_hi, axis=0, keepdims=True),
                    n_nc, k_top, thr_ref, cut_ref, fst_ref, ist_ref)


def _select(qit, ki, wit, *, tq, tk, k_top):
    b, _, _, t = qit.shape
    kern = functools.partial(_select_kernel, tq=tq, tk=tk, n_keys=t, k_top=k_top)
    return pl.pallas_call(
        kern,
        grid=(b, t // tq),
        in_specs=[
            pl.BlockSpec((1, IDX_HEADS, IDX_DIM, tq), lambda bb, i: (bb, 0, 0, i)),
            pl.BlockSpec((1, t, IDX_DIM), lambda bb, i: (bb, 0, 0)),
            pl.BlockSpec((1, 8, tq), lambda bb, i: (bb, 0, i)),
        ],
        out_specs=[pl.BlockSpec((1, 1, tq), lambda bb, i: (bb, 0, i)),
                   pl.BlockSpec((1, 1, tq), lambda bb, i: (bb, 0, i))],
        out_shape=[jax.ShapeDtypeStruct((b, 1, t), F32), jax.ShapeDtypeStruct((b, 1, t), I32)],
        scratch_shapes=[pltpu.VMEM((t, tq), F32), pltpu.VMEM((8, tq), F32), pltpu.VMEM((8, tq), I32)],
        compiler_params=_cparams("parallel", "parallel"),
        name="select",
    )(qit, ki, wit)


def _n_key_tiles(i, tq, tk):
    return (i * tq + tq + tk - 1) // tk


def _attn_kernel(qt_ref, k_ref, vt_ref, qit_ref, ki_ref, wit_ref, vf_ref, cut_ref, o_ref,
                 m_ref, l_ref, acc_ref, *, tq, tk):
    i = pl.program_id(1)
    j = pl.program_id(2)
    nkt = _n_key_tiles(i, tq, tk)

    @pl.when(j == 0)
    def _():
        m_ref[...] = jnp.full_like(m_ref, -jnp.inf)
        l_ref[...] = jnp.zeros_like(l_ref)
        acc_ref[...] = jnp.zeros_like(acc_ref)

    @pl.when(j < nkt)
    def _():
        s = _scores_t(ki_ref[0], qit_ref[0], wit_ref[0])
        kidx = j * tk + lax.broadcasted_iota(I32, (tk, tq), 0)
        t_row = i * tq + lax.broadcasted_iota(I32, (1, tq), 1)
        vf = vf_ref[0]
        bias = _select_bias(s, vf, kidx, cut_ref[0])
        bias = jnp.where(kidx <= t_row, bias, -jnp.inf)
        for h in range(N_HEADS):
            logit = jnp.dot(k_ref[0, h], qt_ref[0, h], preferred_element_type=F32) + bias
            m_old = m_ref[h:h + 1, :]
            m_new = jnp.maximum(m_old, jnp.max(logit, axis=0, keepdims=True))
            m_safe = jnp.where(m_new == -jnp.inf, 0.0, m_new)
            p = jnp.exp(logit - m_safe)
            alpha = jnp.exp(m_old - m_safe)
            l_ref[h:h + 1, :] = alpha * l_ref[h:h + 1, :] + jnp.sum(p, axis=0, keepdims=True)
            acc_ref[h] = alpha * acc_ref[h] + jnp.dot(vt_ref[0, h], p.astype(BF16),
                                                     preferred_element_type=F32)
            m_ref[h:h + 1, :] = m_new

    @pl.when(j == nkt - 1)
    def _():
        for h in range(N_HEADS):
            o_ref[0, h] = (acc_ref[h] / l_ref[h:h + 1, :]).astype(o_ref.dtype)


def _attn(qt, k, vt, qit, ki, wit, vf, cut, *, tq, tk):
    b, _, _, t = qt.shape
    kj = lambda i, j: jnp.minimum(j, _n_key_tiles(i, tq, tk) - 1)
    return pl.pallas_call(
        functools.partial(_attn_kernel, tq=tq, tk=tk),
        grid=(b, t // tq, t // tk),
        in_specs=[
            pl.BlockSpec((1, N_HEADS, HEAD_DIM, tq), lambda bb, i, j: (bb, 0, 0, i)),
            pl.BlockSpec((1, N_HEADS, tk, HEAD_DIM), lambda bb, i, j: (bb, 0, kj(i, j), 0)),
            pl.BlockSpec((1, N_HEADS, HEAD_DIM, tk), lambda bb, i, j: (bb, 0, 0, kj(i, j))),
            pl.BlockSpec((1, IDX_HEADS, IDX_DIM, tq), lambda bb, i, j: (bb, 0, 0, i)),
            pl.BlockSpec((1, tk, IDX_DIM), lambda bb, i, j: (bb, kj(i, j), 0)),
            pl.BlockSpec((1, 8, tq), lambda bb, i, j: (bb, 0, i)),
            pl.BlockSpec((1, 1, tq), lambda bb, i, j: (bb, 0, i)),
            pl.BlockSpec((1, 1, tq), lambda bb, i, j: (bb, 0, i)),
        ],
        out_specs=pl.BlockSpec((1, N_HEADS, HEAD_DIM, tq), lambda bb, i, j: (bb, 0, 0, i)),
        out_shape=jax.ShapeDtypeStruct((b, N_HEADS, HEAD_DIM, t), BF16),
        scratch_shapes=[pltpu.VMEM((N_HEADS, tq), F32), pltpu.VMEM((N_HEADS, tq), F32),
                        pltpu.VMEM((N_HEADS, HEAD_DIM, tq), F32)],
        compiler_params=_cparams("parallel", "parallel", "arbitrary"),
        name="attn",
    )(qt, k, vt, qit, ki, wit, vf, cut)


def _zoh(ar, ai, ldt):
    dt = jnp.exp(ldt)
    mag = jnp.exp(dt * ar)
    abr = mag * jnp.cos(dt * ai)
    abi = mag * jnp.sin(dt * ai)
    den = ar * ar + ai * ai
    nr = abr - 1.0
    ni = abi
    return abr, abi, (nr * ar + ni * ai) / den, (ni * ar - nr * ai) / den


def _disc_kernel(ar_ref, ai_ref, ldt_ref, ar16_ref, ai16_ref, ldt16_ref, br_ref, bi_ref,
                 pwr_ref, pwi_ref, bbr_ref, bbi_ref):
    abr, abi, _, _ = _zoh(ar_ref[...], ai_ref[...], ldt_ref[...])
    pr, pi = abr, abi
    for j in range(8):
        pwr_ref[j:j + 1, :] = pr
        pwi_ref[j:j + 1, :] = pi
        pr, pi = pr * abr - pi * abi, pr * abi + pi * abr
    _, _, fr, fi = _zoh(ar16_ref[...], ai16_ref[...], ldt16_ref[...])
    br = br_ref[...]
    bi = bi_ref[...]
    bbr_ref[...] = fr * br - fi * bi
    bbi_ref[...] = fr * bi + fi * br


def _discretize(a_re, a_im, log_dt, b_re, b_im):
    flat = lambda a: a.reshape(1, -1)
    ldt = jnp.broadcast_to(log_dt[:, None], (N_GROUPS, STATE_DIM))
    rep = lambda a: flat(jnp.broadcast_to(a[:, :, None], (N_GROUPS, STATE_DIM, GROUP_CH)))
    n16 = N_STATE * GROUP_CH
    pwr, pwi, bbr, bbi = pl.pallas_call(
        _disc_kernel,
        out_shape=[jax.ShapeDtypeStruct((8, N_STATE), F32), jax.ShapeDtypeStruct((8, N_STATE), F32),
                   jax.ShapeDtypeStruct((1, n16), F32), jax.ShapeDtypeStruct((1, n16), F32)],
        name="s5_discretize",
    )(flat(a_re), flat(a_im), flat(ldt), rep(a_re), rep(a_im), rep(ldt), flat(b_re), flat(b_im))
    return pwr, pwi, bbr.reshape(N_GROUPS, STATE_DIM, GROUP_CH), bbi.reshape(N_GROUPS, STATE_DIM, GROUP_CH)


def _block_diag_in(bb):
    eye = jnp.eye(N_GROUPS, dtype=bb.dtype)
    return jnp.einsum('gpc,gh->gchp', bb, eye).reshape(SSM_WIDTH, N_STATE)


def _block_diag_out(c):
    eye = jnp.eye(N_GROUPS, dtype=c.dtype)
    return jnp.einsum('gcp,gh->gphc', c, eye).reshape(N_STATE, SSM_WIDTH)


_LANE_CHUNK = 512


def _ssm_kernel(u_ref, bblk_ref, cblk_ref, pw_ref, d_ref, y_ref, xf_ref, x_ref, carry_ref, *, tt):
    ts = pl.program_id(1)

    @pl.when(ts == 0)
    def _():
        carry_ref[...] = jnp.zeros_like(carry_ref)

    u = u_ref[0]
    ub = u.astype(BF16)
    n_chunks = 2 * N_STATE // _LANE_CHUNK
    for c in range(n_chunks):
        cols = slice(c * _LANE_CHUNK, (c + 1) * _LANE_CHUNK)
        x_ref[:, cols] = jnp.dot(ub, bblk_ref[:, cols], preferred_element_type=F32)
    rid = lax.broadcasted_iota(I32, (8, _LANE_CHUNK), 0)

    def group(r, carry):
        r0 = pl.multiple_of(r * 8, 8)
        for c in range(N_STATE // _LANE_CHUNK):
            re = pl.ds(c * _LANE_CHUNK, _LANE_CHUNK)
            im = pl.ds(N_STATE + c * _LANE_CHUNK, _LANE_CHUNK)
            xr = x_ref[pl.ds(r0, 8), re]
            xi = x_ref[pl.ds(r0, 8), im]
            for d in (1, 2, 4):
                ar = pw_ref[d - 1:d, re]
                ai = pw_ref[d - 1:d, im]
                sr = jnp.where(rid >= d, pltpu.roll(xr, d, 0), 0.0)
                si = jnp.where(rid >= d, pltpu.roll(xi, d, 0), 0.0)
                xr, xi = xr + (ar * sr - ai * si), xi + (ar * si + ai * sr)
            cr = carry_ref[:, re]
            ci = carry_ref[:, im]
            pr = pw_ref[:, re]
            pi = pw_ref[:, im]
            xr, xi = xr + (pr * cr - pi * ci), xi + (pr * ci + pi * cr)
            x_ref[pl.ds(r0, 8), re] = xr
            x_ref[pl.ds(r0, 8), im] = xi
            carry_ref[:, re] = xr[7:8, :]
            carry_ref[:, im] = xi[7:8, :]
        return carry

    lax.fori_loop(0, tt // 8, group, 0)
    y = d_ref[...] * u
    for c in range(n_chunks):
        cols = slice(c * _LANE_CHUNK, (c + 1) * _LANE_CHUNK)
        y = y + jnp.dot(x_ref[:, cols].astype(BF16), cblk_ref[cols, :], preferred_element_type=F32)
    y_ref[0] = y
    xf_ref[0] = carry_ref[...]


def _ssm(u, bblk, cblk, pw, d_skip, *, tt):
    b, t, _ = u.shape
    return pl.pallas_call(
        functools.partial(_ssm_kernel, tt=tt),
        grid=(b, t // tt),
        in_specs=[
            pl.BlockSpec((1, tt, SSM_WIDTH), lambda bb, s: (bb, s, 0)),
            pl.BlockSpec((SSM_WIDTH, 2 * N_STATE), lambda bb, s: (0, 0)),
            pl.BlockSpec((2 * N_STATE, SSM_WIDTH), lambda bb, s: (0, 0)),
            pl.BlockSpec((8, 2 * N_STATE), lambda bb, s: (0, 0)),
            pl.BlockSpec((1, SSM_WIDTH), lambda bb, s: (0, 0)),
        ],
        out_specs=[pl.BlockSpec((1, tt, SSM_WIDTH), lambda bb, s: (bb, s, 0)),
                   pl.BlockSpec((1, 1, 2 * N_STATE), lambda bb, s: (bb, 0, 0))],
        out_shape=[jax.ShapeDtypeStruct((b, t, SSM_WIDTH), F32),
                   jax.ShapeDtypeStruct((b, 1, 2 * N_STATE), F32)],
        scratch_shapes=[pltpu.VMEM((tt, 2 * N_STATE), F32), pltpu.VMEM((1, 2 * N_STATE), F32)],
        compiler_params=_cparams("parallel", "arbitrary"),
        name="s5_scan",
    )(u, bblk, cblk, pw, d_skip)


def _ssm_step_kernel(u_ref, x0_ref, bblk_ref, cblk_ref, pw_ref, d_ref, y_ref, x_ref):
    u = u_ref[...]
    bu = jnp.dot(u, bblk_ref[...], preferred_element_type=F32, precision=lax.Precision.HIGHEST)
    ar = pw_ref[0:1, :N_STATE]
    ai = pw_ref[0:1, N_STATE:]
    x0r = x0_ref[:, :N_STATE]
    x0i = x0_ref[:, N_STATE:]
    xr = ar * x0r - ai * x0i + bu[:, :N_STATE]
    xi = ar * x0i + ai * x0r + bu[:, N_STATE:]
    x_ref[:, :N_STATE] = xr
    x_ref[:, N_STATE:] = xi
    y = jnp.dot(x_ref[...], cblk_ref[...], preferred_element_type=F32, precision=lax.Precision.HIGHEST)
    y_ref[...] = y + d_ref[...] * u


def _ssm_step(u, x0, bblk, cblk, pw, d_skip):
    n = u.shape[0]
    return pl.pallas_call(
        _ssm_step_kernel,
        out_shape=[jax.ShapeDtypeStruct((n, SSM_WIDTH), F32), jax.ShapeDtypeStruct((n, 2 * N_STATE), F32)],
        compiler_params=pltpu.CompilerParams(vmem_limit_bytes=VMEM_LIMIT_BYTES),
        name="s5_step",
    )(u, x0, bblk, cblk, pw, d_skip)


def _gelu_tanh(x):
    c = np.float32(np.sqrt(2.0 / np.pi))
    return 0.5 * x * (1.0 + jnp.tanh(c * (x + 0.044715 * (x * x * x))))


def _merge_kernel(x_ref, oa_ref, ys_ref, g_ref, wgt_ref, wglu_ref, bglu_ref, wpa_ref, wpb_ref, wout_ref, o_ref):
    x = x_ref[...]
    h = _rmsnorm(x, g_ref[...]).astype(BF16)
    gates = _sigmoid(jnp.dot(h, wgt_ref[...], preferred_element_type=F32))
    ys = _gelu_tanh(ys_ref[...])
    glu = jnp.dot(ys.astype(BF16), wglu_ref[...], preferred_element_type=F32) + bglu_ref[...]
    ob = ys * _sigmoid(glu)
    pa = jnp.dot(oa_ref[...], wpa_ref[...], preferred_element_type=F32)
    pb = jnp.dot(ob.astype(BF16), wpb_ref[...], preferred_element_type=F32)
    merged = gates[:, :D_MODEL] * pa + gates[:, D_MODEL:] * pb
    o_ref[...] = x + jnp.dot(merged.astype(BF16), wout_ref[...], preferred_element_type=F32)


def _merge(x, oa, ys, g, wgt, wglu, bglu, wpa, wpb, wout, *, tm):
    m, d = x.shape
    row = lambda n: pl.BlockSpec((tm, n), lambda i: (i, 0))
    full = lambda a: pl.BlockSpec(a.shape, lambda i: (0, 0))
    return pl.pallas_call(
        _merge_kernel,
        grid=(m // tm,),
        in_specs=[row(d), row(D_ATTN), row(SSM_WIDTH), full(g), full(wgt), full(wglu), full(bglu),
                  full(wpa), full(wpb), full(wout)],
        out_specs=row(d),
        out_shape=jax.ShapeDtypeStruct((m, d), F32),
        compiler_params=_cparams("parallel"),
        name="merge",
    )(x, oa, ys, g, wgt, wglu, bglu, wpa, wpb, wout)


_PAGES_PER_STEP = 8


def _page_specs(block, n_lead, page_size):
    def spec(r):
        def index_map(b, p, pt):
            return (pt[b, p * _PAGES_PER_STEP + r],) + (0,) * n_lead
        return pl.BlockSpec(block, index_map)
    return [spec(r) for r in range(_PAGES_PER_STEP)]


def _sample_scores_kernel(pt_ref, qi_ref, w_ref, knew_ref, *refs, page_size):
    pages = refs[:_PAGES_PER_STEP]
    o_ref, oself_ref = refs[_PAGES_PER_STEP:]
    qi = qi_ref[0]
    w = w_ref[0]

    def score(k_rows):
        d = jnp.dot(k_rows, qi, preferred_element_type=F32)
        return jnp.sum(jnp.maximum(d, 0.0) * w, axis=1, keepdims=True)

    for r in range(_PAGES_PER_STEP):
        o_ref[0, r * page_size:(r + 1) * page_size, :] = score(pages[r][0].astype(BF16))
    oself_ref[0] = score(jnp.broadcast_to(knew_ref[0], (8, IDX_DIM)))


def _sample_scores(page_table, qi_cols, w_row, ki_new, cache_kidx):
    db, n_pages = page_table.shape
    _, page_size, _ = cache_kidx.shape
    steps = n_pages // _PAGES_PER_STEP
    per_b = lambda shape: pl.BlockSpec((1,) + shape, lambda b, p, pt: (b,) + (0,) * len(shape))
    return pl.pallas_call(
        functools.partial(_sample_scores_kernel, page_size=page_size),
        grid_spec=pltpu.PrefetchScalarGridSpec(
            num_scalar_prefetch=1,
            grid=(db, steps),
            in_specs=[per_b((IDX_DIM, 128)), per_b((1, 128)), per_b((1, IDX_DIM))]
                     + _page_specs((1, page_size, IDX_DIM), 2, page_size),
            out_specs=[pl.BlockSpec((1, _PAGES_PER_STEP * page_size, 1), lambda b, p, pt: (b, p, 0)),
                       per_b((8, 1))],
        ),
        out_shape=[jax.ShapeDtypeStruct((db, n_pages * page_size, 1), F32),
                   jax.ShapeDtypeStruct((db, 8, 1), F32)],
        compiler_params=_cparams("parallel", "arbitrary"),
        name="sample_scores",
    )(page_table, qi_cols, w_row, ki_new, *([cache_kidx] * _PAGES_PER_STEP))


def _sample_select_kernel(st_ref, thr_ref, cut_ref, s_ref, fst_ref, ist_ref, *, tk, tq, n_keys, k_top):
    n_chunks = st_ref.shape[0] // tk
    row_iota = lax.broadcasted_iota(I32, (tk, tq), 0)

    def build(c, carry):
        mx, mn, n_hi = carry
        k0 = pl.multiple_of(c * tk, tk)
        s = st_ref[pl.ds(k0, tk), :]
        real = (k0 + row_iota) < n_keys
        sm = jnp.where(real, s, -jnp.inf)
        s_ref[pl.ds(k0, tk), :] = sm
        return (jnp.maximum(mx, _fold8(sm, jnp.max)),
                jnp.minimum(mn, _fold8(jnp.where(real, s, jnp.inf), jnp.min)),
                n_hi + _fold8(jnp.where(sm > NEG_BIG, 1, 0), jnp.sum))

    mx, mn, n_hi = lax.fori_loop(
        0, n_chunks, build,
        (jnp.full((8, tq), -jnp.inf, F32), jnp.full((8, tq), jnp.inf, F32), jnp.zeros((8, tq), I32)))
    _topk_threshold(s_ref, n_chunks, tk, tq, jnp.max(mx, axis=0, keepdims=True),
                    jnp.min(mn, axis=0, keepdims=True), jnp.sum(n_hi, axis=0, keepdims=True),
                    jnp.zeros((1, tq), I32), k_top, thr_ref, cut_ref, fst_ref, ist_ref)


def _sample_select(scores_t, *, n_keys, k_top, tk):
    n_rows, tq = scores_t.shape
    kern = functools.partial(_sample_select_kernel, tk=tk, tq=tq, n_keys=n_keys, k_top=k_top)
    return pl.pallas_call(
        kern,
        out_shape=[jax.ShapeDtypeStruct((1, 1, tq), F32), jax.ShapeDtypeStruct((1, 1, tq), I32)],
        scratch_shapes=[pltpu.VMEM((n_rows, tq), F32), pltpu.VMEM((8, tq), F32), pltpu.VMEM((8, tq), I32)],
        compiler_params=pltpu.CompilerParams(vmem_limit_bytes=VMEM_LIMIT_BYTES),
        name="sample_select",
    )(scores_t)


def _sample_attn_kernel(pt_ref, qbd_ref, s_ref, sself_ref, vf_ref, cut_ref, knew_ref, vnew_ref, *refs,
                        page_size, past_len):
    kp = refs[:_PAGES_PER_STEP]
    vp = refs[_PAGES_PER_STEP:2 * _PAGES_PER_STEP]
    o_ref, m_ref, l_ref, acc_ref = refs[2 * _PAGES_PER_STEP:]
    p_step = pl.program_id(1)
    qbd = qbd_ref[0]
    vf = vf_ref[0]
    cut = cut_ref[0]
    lane_head = lax.broadcasted_iota(I32, (N_HEADS, D_ATTN), 1) // HEAD_DIM
    head_mask = lane_head == lax.broadcasted_iota(I32, (N_HEADS, D_ATTN), 0)

    @pl.when(p_step == 0)
    def _():
        m_ref[...] = jnp.full_like(m_ref, -jnp.inf)
        l_ref[...] = jnp.zeros_like(l_ref)
        acc_ref[...] = jnp.zeros_like(acc_ref)

    def to_col(row):
        return jnp.transpose(jnp.broadcast_to(row, (128, 128)))[:N_HEADS, 0:1]

    def absorb(k_rows, v_rows, s_col, kidx):
        bias = _select_bias(s_col, vf, kidx, cut)
        logit = jnp.dot(k_rows.astype(BF16), qbd, preferred_element_type=F32) + bias
        m_old = m_ref[...]
        m_new = jnp.maximum(m_old, jnp.max(logit, axis=0, keepdims=True))
        m_safe = jnp.where(m_new == -jnp.inf, 0.0, m_new)
        p = jnp.exp(logit - m_safe)
        alpha = jnp.exp(m_old - m_safe)
        l_ref[...] = alpha * l_ref[...] + jnp.sum(p, axis=0, keepdims=True)
        p_rows = jnp.transpose(p)[:N_HEADS, :].astype(BF16)
        pv = jnp.dot(p_rows, v_rows.astype(BF16), preferred_element_type=F32)
        acc_ref[...] = to_col(alpha) * acc_ref[...] + pv
        m_ref[...] = m_new

    for r in range(_PAGES_PER_STEP):
        base = (p_step * _PAGES_PER_STEP + r) * page_size
        kidx = base + lax.broadcasted_iota(I32, (page_size, 1), 0)
        absorb(kp[r][0], vp[r][0], s_ref[0, r * page_size:(r + 1) * page_size, :], kidx)

    @pl.when(p_step == pl.num_programs(1) - 1)
    def _():
        rows = lax.broadcasted_iota(I32, (128, 1), 0)
        s_col = jnp.where(rows == 0, sself_ref[0], -jnp.inf)
        kidx = jnp.where(rows == 0, past_len, INT_MAX)
        absorb(jnp.broadcast_to(knew_ref[0], (128, D_ATTN)), jnp.broadcast_to(vnew_ref[0], (128, D_ATTN)),
               s_col, kidx)
        out = jnp.where(head_mask, acc_ref[...] / to_col(l_ref[...]), 0.0)
        o_ref[0] = jnp.sum(out, axis=0, keepdims=True)


def _sample_attn(page_table, qbd, scores, s_self, vf, cut, k_new, v_new, cache_k, cache_v):
    db, n_pages = page_table.shape
    _, page_size, _ = cache_k.shape
    steps = n_pages // _PAGES_PER_STEP
    past_len = n_pages * page_size
    per_b = lambda shape: pl.BlockSpec((1,) + shape, lambda b, p, pt: (b,) + (0,) * len(shape))
    return pl.pallas_call(
        functools.partial(_sample_attn_kernel, page_size=page_size, past_len=past_len),
        grid_spec=pltpu.PrefetchScalarGridSpec(
            num_scalar_prefetch=1,
            grid=(db, steps),
            in_specs=[per_b((D_ATTN, 128)),
                      pl.BlockSpec((1, _PAGES_PER_STEP * page_size, 1), lambda b, p, pt: (b, p, 0)),
                      per_b((1, 1)), per_b((1, 128)), per_b((1, 128)),
                      per_b((1, D_ATTN)), per_b((1, D_ATTN))]
                     + _page_specs((1, page_size, D_ATTN), 2, page_size)
                     + _page_specs((1, page_size, D_ATTN), 2, page_size),
            out_specs=per_b((1, D_ATTN)),
            scratch_shapes=[pltpu.VMEM((1, 128), F32), pltpu.VMEM((1, 128), F32),
                            pltpu.VMEM((N_HEADS, D_ATTN), F32)],
        ),
        out_shape=jax.ShapeDtypeStruct((db, 1, D_ATTN), F32),
        compiler_params=_cparams("parallel", "arbitrary"),
        name="sample_attn",
    )(page_table, qbd, scores, s_self, vf, cut, k_new, v_new,
      *([cache_k] * _PAGES_PER_STEP), *([cache_v] * _PAGES_PER_STEP))


def _pick(n, pref):
    for t in pref:
        if n % t == 0:
            return t
    return n


def _pack_w_in(w_in):
    o = np.cumsum([0, D_ATTN, D_ATTN, D_ATTN, IDX_HEADS * IDX_DIM, IDX_DIM, IDX_HEADS, SSM_WIDTH, 2 * D_MODEL])
    z = lambda n: jnp.zeros((D_MODEL, n), w_in.dtype)
    w_all = jnp.concatenate([
        w_in[:, o[0]:o[4]],
        w_in[:, o[4]:o[5]], z(_PWI - _PKI - IDX_DIM),
        w_in[:, o[5]:o[6]], z(_PU - _PWI - IDX_HEADS),
        w_in[:, o[6]:o[7]],
    ], axis=1)
    return w_all.astype(BF16), w_in[:, o[7]:o[8]].astype(BF16)


def _prompt_layer(x, lw, ssm, final_norm):
    bsz, seq, d = x.shape
    m = bsz * seq
    tm = _pick(m, (512, 256, 128, 64, 32, 16, 8))
    x1 = _ffn(x.reshape(m, d), *lw["ffn1"], lw["g_final"], final_norm=False, tm=tm, tf=lw["tf"])
    q_b, k_f, v_f, k_b, v_b, qi_b, ki_f, ki_b, wi_f, u_f = _proj(x1, lw["g_mix"], lw["w_all"], tm=tm)

    heads_t = lambda a, nh: a.reshape(bsz, seq, nh, -1).transpose(0, 2, 3, 1)
    qt = heads_t(q_b, N_HEADS)
    vt = heads_t(v_b, N_HEADS)
    kh = k_b.reshape(bsz, seq, N_HEADS, HEAD_DIM).transpose(0, 2, 1, 3)
    qit = heads_t(qi_b, IDX_HEADS)
    kib = ki_b.reshape(bsz, seq, IDX_DIM)
    wit = wi_f[:, :8].reshape(bsz, seq, 8).transpose(0, 2, 1)

    k_top = min(TOPK_MAX, seq // 4)
    vf, cut = _select(qit, kib, wit, tq=_pick(seq, (256, 128)), tk=_pick(seq, (256, 128)), k_top=k_top)
    o_t = _attn(qt, kh, vt, qit, kib, wit, vf, cut,
                tq=_pick(seq, (256, 128)), tk=_pick(seq, (512, 256, 128)))
    oa = o_t.transpose(0, 3, 1, 2).reshape(m, D_ATTN)

    tt = _pick(seq, (256, 128, 64, 32, 16, 8))
    ys, xfin = _ssm(u_f.reshape(bsz, seq, SSM_WIDTH), ssm["bblk"].astype(BF16), ssm["cblk"].astype(BF16),
                    ssm["pw"], ssm["d_skip"], tt=tt)
    x2 = _merge(x1, oa, ys.reshape(m, SSM_WIDTH), *lw["merge"], tm=tm)
    y = _ffn(x2, *lw["ffn2"], lw["g_final"], final_norm=final_norm, tm=tm, tf=lw["tf"])
    new = (k_f.reshape(bsz, seq, N_HEADS, HEAD_DIM), v_f.reshape(bsz, seq, N_HEADS, HEAD_DIM),
           ki_f.reshape(bsz, seq, IDX_DIM),
           xfin[:, 0, :N_STATE].reshape(bsz, N_GROUPS, STATE_DIM),
           xfin[:, 0, N_STATE:].reshape(bsz, N_GROUPS, STATE_DIM))
    return y.reshape(bsz, seq, d), new


def _sample_layer(x, lw, ssm, final_norm, cache_k, cache_v, cache_kidx, st_re, st_im, page_table):
    db, ds, d = x.shape
    assert ds == 1, "one new token per sample sequence"
    n_pool, page_size = cache_k.shape[0], cache_k.shape[1]
    past_len = page_table.shape[1] * page_size
    s1 = _ffn(x.reshape(db, d), *lw["ffn1"], lw["g_final"], final_norm=False, tm=db, tf=lw["tf"])
    q_b, k_f, v_f, _, _, qi_b, ki_f, ki_b, wi_f, u_f = _proj(s1, lw["g_mix"], lw["w_all"], tm=db)

    qi_cols = jnp.zeros((db, IDX_DIM, 128), BF16).at[:, :, :IDX_HEADS].set(
        qi_b.reshape(db, IDX_HEADS, IDX_DIM).transpose(0, 2, 1))
    past_scores, self_scores = _sample_scores(page_table, qi_cols, wi_f.reshape(db, 1, 128),
                                              ki_b.reshape(db, 1, IDX_DIM), cache_kidx)
    s_self = self_scores[:, 0, 0]

    n_keys = past_len + 1
    k_top = min(TOPK_MAX, n_keys // 4)
    tk = 256
    n_rows = -(-n_keys // tk) * tk
    lanes = -(-db // 128) * 128
    all_scores = jnp.concatenate([past_scores[:, :, 0], s_self[:, None]], axis=1)
    scores_t = jnp.zeros((n_rows, lanes), F32).at[:n_keys, :db].set(all_scores.T)
    vf, cut = _sample_select(scores_t, n_keys=n_keys, k_top=k_top, tk=tk)
    vf_b = jnp.broadcast_to(vf[0, 0, :db, None, None], (db, 1, 128))
    cut_b = jnp.broadcast_to(cut[0, 0, :db, None, None], (db, 1, 128))

    head_of = jnp.arange(D_ATTN) // HEAD_DIM
    qbd = jnp.where(head_of[None, :, None] == jnp.arange(128)[None, None, :],
                    q_b[:, :, None], jnp.zeros((), BF16))
    o_s = _sample_attn(page_table, qbd, past_scores, s_self.reshape(db, 1, 1), vf_b, cut_b,
                       k_f.reshape(db, 1, D_ATTN), v_f.reshape(db, 1, D_ATTN),
                       cache_k.reshape(n_pool, page_size, D_ATTN),
                       cache_v.reshape(n_pool, page_size, D_ATTN))
    oa = o_s.reshape(db, D_ATTN).astype(BF16)

    x0 = jnp.concatenate([st_re.reshape(db, N_STATE), st_im.reshape(db, N_STATE)], axis=1)
    ys, x_new = _ssm_step(u_f, x0, ssm["bblk"], ssm["cblk"], ssm["pw"], ssm["d_skip"])
    s2 = _merge(s1, oa, ys, *lw["merge"], tm=db)
    y = _ffn(s2, *lw["ffn2"], lw["g_final"], final_norm=final_norm, tm=db, tf=lw["tf"])
    new = (k_f.reshape(db, ds, N_HEADS, HEAD_DIM), v_f.reshape(db, ds, N_HEADS, HEAD_DIM),
           ki_f.reshape(db, ds, IDX_DIM),
           x_new[:, :N_STATE].reshape(db, N_GROUPS, STATE_DIM),
           x_new[:, N_STATE:].reshape(db, N_GROUPS, STATE_DIM))
    return y.reshape(db, ds, d), new


def kernel(x_prompt, x_sample, cache_k, cache_v, cache_kidx, state_ssm_re, state_ssm_im, page_table,
           g_ffn1, w1_gate, w1_up, w1_down, g_mix, w_in, a_re, a_im, log_dt, b_re, b_im, c_re, c_im,
           d_skip, w_glu, b_glu, w_pa, w_pb, w_out, g_ffn2, w2_gate, w2_up, w2_down, g_final):
    depth = w_in.shape[0]
    row = lambda a: a.reshape(1, -1)
    bf = lambda a: a.astype(BF16)
    xp, xs = x_prompt, x_sample
    new_p, new_s = [], []
    for l in range(depth):
        w_all, w_gates = _pack_w_in(w_in[l])
        lw = dict(
            ffn1=(row(g_ffn1[l]), bf(w1_gate[l]), bf(w1_up[l]), bf(w1_down[l])),
            ffn2=(row(g_ffn2[l]), bf(w2_gate[l]), bf(w2_up[l]), bf(w2_down[l])),
            merge=(row(g_mix[l]), w_gates, bf(w_glu[l]), row(b_glu[l]), bf(w_pa[l]), bf(w_pb[l]), bf(w_out[l])),
            g_mix=row(g_mix[l]), w_all=w_all, g_final=row(g_final),
            tf=_pick(w1_gate.shape[2], (1408, 1024, 512, 256, 128)),
        )
        pwr, pwi, bbr, bbi = _discretize(a_re[l], a_im[l], log_dt[l], b_re[l], b_im[l])
        ssm = dict(
            pw=jnp.concatenate([pwr, pwi], axis=1),
            bblk=jnp.concatenate([_block_diag_in(bbr), _block_diag_in(bbi)], axis=1),
            cblk=jnp.concatenate([_block_diag_out(c_re[l]), -_block_diag_out(c_im[l])], axis=0),
            d_skip=row(d_skip[l]),
        )
        last = l == depth - 1
        xp, st_p = _prompt_layer(xp, lw, ssm, last)
        xs, st_s = _sample_layer(xs, lw, ssm, last, cache_k[l], cache_v[l], cache_kidx[l],
                                 state_ssm_re[l], state_ssm_im[l], page_table)
        new_p.append(st_p)
        new_s.append(st_s)
    stack = lambda states, i: jnp.stack([s[i] for s in states])
    return (xp, xs) + tuple(stack(new_p, i) for i in range(5)) + tuple(stack(new_s, i) for i in range(5))
```

```python
import functools

import numpy as np
import jax
import jax.numpy as jnp
from jax import lax
from jax.experimental import pallas as pl
from jax.experimental.pallas import tpu as pltpu

F32 = jnp.float32
BF16 = jnp.bfloat16
I32 = jnp.int32

D_MODEL = 1024
N_HEADS = 8
HEAD_DIM = 64
D_ATTN = N_HEADS * HEAD_DIM
IDX_HEADS = 4
IDX_DIM = 64
TOPK_MAX = 256
SSM_WIDTH = 512
GROUP_CH = 16
N_GROUPS = SSM_WIDTH // GROUP_CH
STATE_DIM = 64
N_STATE = N_GROUPS * STATE_DIM
NORM_EPS = 1e-6
NEG_BIG = -1e30

INT_MAX = 2 ** 31 - 1

VMEM_LIMIT_BYTES = 56 * 1024 * 1024


def _cparams(*sem):
    return pltpu.CompilerParams(dimension_semantics=sem, vmem_limit_bytes=VMEM_LIMIT_BYTES)


def _rmsnorm(x, g):
    y = x * lax.rsqrt(jnp.mean(x * x, axis=-1, keepdims=True) + NORM_EPS)
    return y * g


def _sigmoid(x):
    return 1.0 / (1.0 + jnp.exp(-x))


def _ffn_kernel(x_ref, g_ref, wg_ref, wu_ref, wd_ref, gf_ref, o_ref, xn_ref, acc_ref, *, final_norm):
    f = pl.program_id(1)

    @pl.when(f == 0)
    def _():
        xn_ref[...] = _rmsnorm(x_ref[...], g_ref[...]).astype(BF16)
        acc_ref[...] = jnp.zeros_like(acc_ref)

    xn = xn_ref[...]
    gate = jnp.dot(xn, wg_ref[...], preferred_element_type=F32)
    up = jnp.dot(xn, wu_ref[...], preferred_element_type=F32)
    act = (gate * _sigmoid(gate)) * up
    acc_ref[...] += jnp.dot(act.astype(BF16), wd_ref[...], preferred_element_type=F32)

    @pl.when(f == pl.num_programs(1) - 1)
    def _():
        y = x_ref[...] + 0.5 * acc_ref[...]
        if final_norm:
            y = _rmsnorm(y, gf_ref[...])
        o_ref[...] = y


def _ffn(x, g, wg, wu, wd, g_final, *, final_norm, tm, tf):
    m, d = x.shape
    dff = wg.shape[1]
    return pl.pallas_call(
        functools.partial(_ffn_kernel, final_norm=final_norm),
        grid=(m // tm, dff // tf),
        in_specs=[
            pl.BlockSpec((tm, d), lambda i, f: (i, 0)),
            pl.BlockSpec((1, d), lambda i, f: (0, 0)),
            pl.BlockSpec((d, tf), lambda i, f: (0, f)),
            pl.BlockSpec((d, tf), lambda i, f: (0, f)),
            pl.BlockSpec((tf, d), lambda i, f: (f, 0)),
            pl.BlockSpec((1, d), lambda i, f: (0, 0)),
        ],
        out_specs=pl.BlockSpec((tm, d), lambda i, f: (i, 0)),
        out_shape=jax.ShapeDtypeStruct((m, d), F32),
        scratch_shapes=[pltpu.VMEM((tm, d), BF16), pltpu.VMEM((tm, d), F32)],
        compiler_params=_cparams("parallel", "arbitrary"),
        name="ffn",
    )(x, g, wg, wu, wd, g_final)


_PQ, _PK, _PV, _PQI, _PKI, _PWI, _PU, _PEND = 0, 512, 1024, 1536, 1792, 1920, 2048, 2560


def _proj_kernel(x_ref, g_ref, w_ref, q_ref, kf_ref, vf_ref, kb_ref, vb_ref,
                 qi_ref, kif_ref, kib_ref, wi_ref, u_ref):
    h = _rmsnorm(x_ref[...], g_ref[...]).astype(BF16)
    p = jnp.dot(h, w_ref[...], preferred_element_type=F32)
    q_ref[...] = (p[:, _PQ:_PK] * (HEAD_DIM ** -0.5)).astype(BF16)
    k = p[:, _PK:_PV]
    v = p[:, _PV:_PQI]
    kf_ref[...] = k
    vf_ref[...] = v
    kb_ref[...] = k.astype(BF16)
    vb_ref[...] = v.astype(BF16)
    qi_ref[...] = p[:, _PQI:_PKI].astype(BF16)
    ki = p[:, _PKI:_PKI + IDX_DIM]
    kif_ref[...] = ki
    kib_ref[...] = ki.astype(BF16)
    wi_ref[...] = p[:, _PWI:_PU] * ((IDX_HEADS * IDX_DIM) ** -0.5)
    u_ref[...] = p[:, _PU:_PEND]


def _proj(x, g, w_all, *, tm):
    m, d = x.shape
    row = lambda n: pl.BlockSpec((tm, n), lambda i: (i, 0))
    outs = [
        (D_ATTN, BF16), (D_ATTN, F32), (D_ATTN, F32), (D_ATTN, BF16), (D_ATTN, BF16),
        (IDX_HEADS * IDX_DIM, BF16), (IDX_DIM, F32), (IDX_DIM, BF16), (128, F32), (SSM_WIDTH, F32),
    ]
    return pl.pallas_call(
        _proj_kernel,
        grid=(m // tm,),
        in_specs=[row(d), pl.BlockSpec((1, d), lambda i: (0, 0)),
                  pl.BlockSpec((d, _PEND), lambda i: (0, 0))],
        out_specs=[row(n) for n, _ in outs],
        out_shape=[jax.ShapeDtypeStruct((m, n), dt) for n, dt in outs],
        compiler_params=_cparams("parallel"),
        name="proj",
    )(x, g, w_all)


def _scores_t(kc, qit, w):
    s = None
    for h in range(IDX_HEADS):
        d = jnp.dot(kc, qit[h], preferred_element_type=F32)
        term = jnp.maximum(d, 0.0) * w[h:h + 1, :]
        s = term if s is None else s + term
    return s


def _select_bias(s, vf, kidx, cut):
    tie = jnp.where(kidx < cut, 0.0, -jnp.inf)
    return jnp.where(s > vf, 0.0, jnp.where(s == vf, tie, -jnp.inf))


def _fold8(x, op):
    return op(x.reshape(x.shape[0] // 8, 8, x.shape[1]), axis=0)


_SETTLE_FIRST, _SETTLE_EVERY, _BISECT_CAP = 10, 5, 320


def _topk_threshold(s_ref, n_chunks, tk, tq, mx, mn, n_hi, n_nc, k_top, thr_ref, cut_ref, fst_ref, ist_ref):
    row_iota = lax.broadcasted_iota(I32, (tk, tq), 0)
    zeros_i = jnp.zeros((8, tq), I32)

    def chunk(c):
        k0 = pl.multiple_of(c * tk, tk)
        return s_ref[pl.ds(k0, tk), :], k0

    def count_gt(v):
        def body(c, acc):
            blk, _ = chunk(c)
            return acc + _fold8(jnp.where(blk > v, 1, 0), jnp.sum)
        real = jnp.sum(lax.fori_loop(0, n_chunks, body, zeros_i), axis=0, keepdims=True)
        return real + jnp.where(NEG_BIG > v, n_nc, 0)

    def count_gt_eq(v):
        def body(c, acc):
            blk, _ = chunk(c)
            return (acc[0] + _fold8(jnp.where(blk > v, 1, 0), jnp.sum),
                    acc[1] + _fold8(jnp.where(blk == v, 1, 0), jnp.sum))
        g, e = lax.fori_loop(0, n_chunks, body, (zeros_i, zeros_i))
        g = jnp.sum(g, axis=0, keepdims=True) + jnp.where(NEG_BIG > v, n_nc, 0)
        return g, jnp.sum(e, axis=0, keepdims=True)

    def max_le(v):
        def body(c, acc):
            blk, _ = chunk(c)
            return jnp.maximum(acc, _fold8(jnp.where(blk <= v, blk, -jnp.inf), jnp.max))
        m = jnp.max(lax.fori_loop(0, n_chunks, body, jnp.full((8, tq), -jnp.inf, F32)), axis=0, keepdims=True)
        return jnp.where((n_nc > 0) & (NEG_BIG <= v), jnp.maximum(m, NEG_BIG), m)

    def settle(m):
        gt, eq_real = count_gt_eq(m)
        eq = eq_real + jnp.where(m == NEG_BIG, n_nc, 0)
        found = (ist_ref[0:1, :] == 0) & (gt + eq >= k_top)
        need = k_top - gt
        fst_ref[2:3, :] = jnp.where(found, m, fst_ref[2:3, :])
        ist_ref[1:2, :] = jnp.where(found, need, ist_ref[1:2, :])
        ist_ref[2:3, :] = jnp.where(found, jnp.where(eq_real > need, 1, 0), ist_ref[2:3, :])
        ist_ref[3:4, :] = jnp.where(found, INT_MAX, ist_ref[3:4, :])
        ist_ref[0:1, :] = jnp.where(found, 1, ist_ref[0:1, :])

    low = n_hi < k_top
    hi0 = jnp.where(low, NEG_BIG, mx)
    fst_ref[0:1, :] = jnp.where(low, jnp.minimum(mn, NEG_BIG), jnp.maximum(mn, NEG_BIG))
    fst_ref[1:2, :] = hi0
    fst_ref[2:3, :] = hi0
    ist_ref[...] = jnp.zeros_like(ist_ref)
    settle(hi0)

    def active_lanes():
        return jnp.max(1 - ist_ref[0:1, :])

    def step(carry):
        it, _ = carry
        lo = fst_ref[0:1, :]
        hi = fst_ref[1:2, :]
        active = ist_ref[0:1, :] == 0
        mid = 0.5 * lo + 0.5 * hi
        c = count_gt(mid)
        hit = active & (c == k_top)
        fst_ref[2:3, :] = jnp.where(hit, mid, fst_ref[2:3, :])
        ist_ref[0:1, :] = jnp.where(hit, 1, ist_ref[0:1, :])
        fst_ref[0:1, :] = jnp.where(active & (c > k_top), mid, lo)
        fst_ref[1:2, :] = jnp.where(active & (c < k_top), mid, hi)
        stuck = active & ((mid <= lo) | (mid >= hi))
        due = (it >= _SETTLE_FIRST) & ((it - _SETTLE_FIRST) % _SETTLE_EVERY == 0)

        @pl.when(due | (jnp.max(jnp.where(stuck, 1, 0)) > 0))
        def _():
            settle(max_le(fst_ref[1:2, :]))

        return it + 1, active_lanes()

    lax.while_loop(lambda carry: (carry[1] > 0) & (carry[0] < _BISECT_CAP), step,
                   (jnp.int32(0), active_lanes()))
    thr_ref[0] = fst_ref[2:3, :]
    cut_ref[0] = ist_ref[3:4, :]

    @pl.when(jnp.max(ist_ref[2:3, :]) > 0)
    def _():
        thr = fst_ref[2:3, :]
        need = ist_ref[1:2, :].astype(F32)
        tri = jnp.where(lax.broadcasted_iota(I32, (tk, tk), 1) <= lax.broadcasted_iota(I32, (tk, tk), 0),
                        1.0, 0.0).astype(BF16)

        def body(c, carry):
            seen, cut = carry
            blk, k0 = chunk(c)
            tie = blk == thr
            rank = jnp.dot(tri, jnp.where(tie, 1.0, 0.0).astype(BF16), preferred_element_type=F32) + seen
            bound = jnp.where(tie, jnp.where(rank <= need, k0 + row_iota + 1, 0), 0)
            return rank[tk - 1:tk, :], jnp.maximum(cut, jnp.max(bound, axis=0, keepdims=True))

        _, cut = lax.fori_loop(0, n_chunks, body, (jnp.zeros((1, tq), F32), jnp.zeros((1, tq), I32)))
        cut_ref[0] = jnp.where(ist_ref[2:3, :] > 0, cut, ist_ref[3:4, :])


def _select_kernel(qit_ref, ki_ref, wit_ref, thr_ref, cut_ref, s_ref, fst_ref, ist_ref, *, tq, tk, n_keys, k_top):
    i = pl.program_id(1)
    q0 = i * tq
    n_chunks = (q0 + tq + tk - 1) // tk
    t_row = q0 + lax.broadcasted_iota(I32, (1, tq), 1)
    row_iota = lax.broadcasted_iota(I32, (tk, tq), 0)
    qit = qit_ref[0]
    w = wit_ref[0]

    def build(c, carry):
        mx, mn, n_hi = carry
        k0 = pl.multiple_of(c * tk, tk)
        s = _scores_t(ki_ref[0, pl.ds(k0, tk), :], qit, w)
        causal = (k0 + row_iota) <= t_row
        sm = jnp.where(causal, s, -jnp.inf)
        s_ref[pl.ds(k0, tk), :] = sm
        return (jnp.maximum(mx, _fold8(sm, jnp.max)),
                jnp.minimum(mn, _fold8(jnp.where(causal, s, jnp.inf), jnp.min)),
                n_hi + _fold8(jnp.where(sm > NEG_BIG, 1, 0), jnp.sum))

    mx, mn, n_hi = lax.fori_loop(
        0, n_chunks, build,
        (jnp.full((8, tq), -jnp.inf, F32), jnp.full((8, tq), jnp.inf, F32), jnp.zeros((8, tq), I32)))
    n_nc = (n_keys - 1) - t_row
    _topk_threshold(s_ref, n_chunks, tk, tq, jnp.max(mx, axis=0, keepdims=True),
                    jnp.min(mn, axis=0, keepdims=True), jnp.sum(n_hi, axis=0, keepdims=True),
                    n_nc, k_top, thr_ref, cut_ref, fst_ref, ist_ref)


def _select(qit, ki, wit, *, tq, tk, k_top):
    b, _, _, t = qit.shape
    kern = functools.partial(_select_kernel, tq=tq, tk=tk, n_keys=t, k_top=k_top)
    return pl.pallas_call(
        kern,
        grid=(b, t // tq),
        in_specs=[
            pl.BlockSpec((1, IDX_HEADS, IDX_DIM, tq), lambda bb, i: (bb, 0, 0, i)),
            pl.BlockSpec((1, t, IDX_DIM), lambda bb, i: (bb, 0, 0)),
            pl.BlockSpec((1, 8, tq), lambda bb, i: (bb, 0, i)),
        ],
        out_specs=[pl.BlockSpec((1, 1, tq), lambda bb, i: (bb, 0, i)),
                   pl.BlockSpec((1, 1, tq), lambda bb, i: (bb, 0, i))],
        out_shape=[jax.ShapeDtypeStruct((b, 1, t), F32), jax.ShapeDtypeStruct((b, 1, t), I32)],
        scratch_shapes=[pltpu.VMEM((t, tq), F32), pltpu.VMEM((8, tq), F32), pltpu.VMEM((8, tq), I32)],
        compiler_params=_cparams("parallel", "parallel"),
        name="select",
    )(qit, ki, wit)


def _n_key_tiles(i, tq, tk):
    return (i * tq + tq + tk - 1) // tk


def _attn_kernel(qt_ref, k_ref, vt_ref, qit_ref, ki_ref, wit_ref, vf_ref, cut_ref, o_ref,
                 m_ref, l_ref, acc_ref, *, tq, tk):
    i = pl.program_id(1)
    j = pl.program_id(2)
    nkt = _n_key_tiles(i, tq, tk)

    @pl.when(j == 0)
    def _():
        m_ref[...] = jnp.full_like(m_ref, -jnp.inf)
        l_ref[...] = jnp.zeros_like(l_ref)
        acc_ref[...] = jnp.zeros_like(acc_ref)

    @pl.when(j < nkt)
    def _():
        s = _scores_t(ki_ref[0], qit_ref[0], wit_ref[0])
        kidx = j * tk + lax.broadcasted_iota(I32, (tk, tq), 0)
        t_row = i * tq + lax.broadcasted_iota(I32, (1, tq), 1)
        vf = vf_ref[0]
        bias = _select_bias(s, vf, kidx, cut_ref[0])
        bias = jnp.where(kidx <= t_row, bias, -jnp.inf)
        for h in range(N_HEADS):
            logit = jnp.dot(k_ref[0, h], qt_ref[0, h], preferred_element_type=F32) + bias
            m_old = m_ref[h:h + 1, :]
            m_new = jnp.maximum(m_old, jnp.max(logit, axis=0, keepdims=True))
            m_safe = jnp.where(m_new == -jnp.inf, 0.0, m_new)
            p = jnp.exp(logit - m_safe)
            alpha = jnp.exp(m_old - m_safe)
            l_ref[h:h + 1, :] = alpha * l_ref[h:h + 1, :] + jnp.sum(p, axis=0, keepdims=True)
            acc_ref[h] = alpha * acc_ref[h] + jnp.dot(vt_ref[0, h], p.astype(BF16),
                                                     preferred_element_type=F32)
            m_ref[h:h + 1, :] = m_new

    @pl.when(j == nkt - 1)
    def _():
        for h in range(N_HEADS):
            o_ref[0, h] = (acc_ref[h] / l_ref[h:h + 1, :]).astype(o_ref.dtype)


def _attn(qt, k, vt, qit, ki, wit, vf, cut, *, tq, tk):
    b, _, _, t = qt.shape
    kj = lambda i, j: jnp.minimum(j, _n_key_tiles(i, tq, tk) - 1)
    return pl.pallas_call(
        functools.partial(_attn_kernel, tq=tq, tk=tk),
        grid=(b, t // tq, t // tk),
        in_specs=[
            pl.BlockSpec((1, N_HEADS, HEAD_DIM, tq), lambda bb, i, j: (bb, 0, 0, i)),
            pl.BlockSpec((1, N_HEADS, tk, HEAD_DIM), lambda bb, i, j: (bb, 0, kj(i, j), 0)),
            pl.BlockSpec((1, N_HEADS, HEAD_DIM, tk), lambda bb, i, j: (bb, 0, 0, kj(i, j))),
            pl.BlockSpec((1, IDX_HEADS, IDX_DIM, tq), lambda bb, i, j: (bb, 0, 0, i)),
            pl.BlockSpec((1, tk, IDX_DIM), lambda bb, i, j: (bb, kj(i, j), 0)),
            pl.BlockSpec((1, 8, tq), lambda bb, i, j: (bb, 0, i)),
            pl.BlockSpec((1, 1, tq), lambda bb, i, j: (bb, 0, i)),
            pl.BlockSpec((1, 1, tq), lambda bb, i, j: (bb, 0, i)),
        ],
        out_specs=pl.BlockSpec((1, N_HEADS, HEAD_DIM, tq), lambda bb, i, j: (bb, 0, 0, i)),
        out_shape=jax.ShapeDtypeStruct((b, N_HEADS, HEAD_DIM, t), BF16),
        scratch_shapes=[pltpu.VMEM((N_HEADS, tq), F32), pltpu.VMEM((N_HEADS, tq), F32),
                        pltpu.VMEM((N_HEADS, HEAD_DIM, tq), F32)],
        compiler_params=_cparams("parallel", "parallel", "arbitrary"),
        name="attn",
    )(qt, k, vt, qit, ki, wit, vf, cut)


def _zoh(ar, ai, ldt):
    dt = jnp.exp(ldt)
    mag = jnp.exp(dt * ar)
    abr = mag * jnp.cos(dt * ai)
    abi = mag * jnp.sin(dt * ai)
    den = ar * ar + ai * ai
    nr = abr - 1.0
    ni = abi
    return abr, abi, (nr * ar + ni * ai) / den, (ni * ar - nr * ai) / den


def _disc_kernel(ar_ref, ai_ref, ldt_ref, ar16_ref, ai16_ref, ldt16_ref, br_ref, bi_ref,
                 pwr_ref, pwi_ref, bbr_ref, bbi_ref):
    abr, abi, _, _ = _zoh(ar_ref[...], ai_ref[...], ldt_ref[...])
    pr, pi = abr, abi
    for j in range(8):
        pwr_ref[j:j + 1, :] = pr
        pwi_ref[j:j + 1, :] = pi
        pr, pi = pr * abr - pi * abi, pr * abi + pi * abr
    _, _, fr, fi = _zoh(ar16_ref[...], ai16_ref[...], ldt16_ref[...])
    br = br_ref[...]
    bi = bi_ref[...]
    bbr_ref[...] = fr * br - fi * bi
    bbi_ref[...] = fr * bi + fi * br


def _discretize(a_re, a_im, log_dt, b_re, b_im):
    flat = lambda a: a.reshape(1, -1)
    ldt = jnp.broadcast_to(log_dt[:, None], (N_GROUPS, STATE_DIM))
    rep = lambda a: flat(jnp.broadcast_to(a[:, :, None], (N_GROUPS, STATE_DIM, GROUP_CH)))
    n16 = N_STATE * GROUP_CH
    pwr, pwi, bbr, bbi = pl.pallas_call(
        _disc_kernel,
        out_shape=[jax.ShapeDtypeStruct((8, N_STATE), F32), jax.ShapeDtypeStruct((8, N_STATE), F32),
                   jax.ShapeDtypeStruct((1, n16), F32), jax.ShapeDtypeStruct((1, n16), F32)],
        name="s5_discretize",
    )(flat(a_re), flat(a_im), flat(ldt), rep(a_re), rep(a_im), rep(ldt), flat(b_re), flat(b_im))
    return pwr, pwi, bbr.reshape(N_GROUPS, STATE_DIM, GROUP_CH), bbi.reshape(N_GROUPS, STATE_DIM, GROUP_CH)


def _block_diag_in(bb):
    eye = jnp.eye(N_GROUPS, dtype=bb.dtype)
    return jnp.einsum('gpc,gh->gchp', bb, eye).reshape(SSM_WIDTH, N_STATE)


def _block_diag_out(c):
    eye = jnp.eye(N_GROUPS, dtype=c.dtype)
    return jnp.einsum('gcp,gh->gphc', c, eye).reshape(N_STATE, SSM_WIDTH)


_LANE_CHUNK = 512


def _ssm_kernel(u_ref, bblk_ref, cblk_ref, pw_ref, d_ref, y_ref, xf_ref, x_ref, carry_ref, *, tt):
    ts = pl.program_id(1)

    @pl.when(ts == 0)
    def _():
        carry_ref[...] = jnp.zeros_like(carry_ref)

    u = u_ref[0]
    ub = u.astype(BF16)
    n_chunks = 2 * N_STATE // _LANE_CHUNK
    for c in range(n_chunks):
        cols = slice(c * _LANE_CHUNK, (c + 1) * _LANE_CHUNK)
        x_ref[:, cols] = jnp.dot(ub, bblk_ref[:, cols], preferred_element_type=F32)
    rid = lax.broadcasted_iota(I32, (8, _LANE_CHUNK), 0)

    def group(r, carry):
        r0 = pl.multiple_of(r * 8, 8)
        for c in range(N_STATE // _LANE_CHUNK):
            re = pl.ds(c * _LANE_CHUNK, _LANE_CHUNK)
            im = pl.ds(N_STATE + c * _LANE_CHUNK, _LANE_CHUNK)
            xr = x_ref[pl.ds(r0, 8), re]
            xi = x_ref[pl.ds(r0, 8), im]
            for d in (1, 2, 4):
                ar = pw_ref[d - 1:d, re]
                ai = pw_ref[d - 1:d, im]
                sr = jnp.where(rid >= d, pltpu.roll(xr, d, 0), 0.0)
                si = jnp.where(rid >= d, pltpu.roll(xi, d, 0), 0.0)
                xr, xi = xr + (ar * sr - ai * si), xi + (ar * si + ai * sr)
            cr = carry_ref[:, re]
            ci = carry_ref[:, im]
            pr = pw_ref[:, re]
            pi = pw_ref[:, im]
            xr, xi = xr + (pr * cr - pi * ci), xi + (pr * ci + pi * cr)
            x_ref[pl.ds(r0, 8), re] = xr
            x_ref[pl.ds(r0, 8), im] = xi
            carry_ref[:, re] = xr[7:8, :]
            carry_ref[:, im] = xi[7:8, :]
        return carry

    lax.fori_loop(0, tt // 8, group, 0)
    y = d_ref[...] * u
    for c in range(n_chunks):
        cols = slice(c * _LANE_CHUNK, (c + 1) * _LANE_CHUNK)
        y = y + jnp.dot(x_ref[:, cols].astype(BF16), cblk_ref[cols, :], preferred_element_type=F32)
    y_ref[0] = y
    xf_ref[0] = carry_ref[...]


def _ssm(u, bblk, cblk, pw, d_skip, *, tt):
    b, t, _ = u.shape
    return pl.pallas_call(
        functools.partial(_ssm_kernel, tt=tt),
        grid=(b, t // tt),
        in_specs=[
            pl.BlockSpec((1, tt, SSM_WIDTH), lambda bb, s: (bb, s, 0)),
            pl.BlockSpec((SSM_WIDTH, 2 * N_STATE), lambda bb, s: (0, 0)),
            pl.BlockSpec((2 * N_STATE, SSM_WIDTH), lambda bb, s: (0, 0)),
            pl.BlockSpec((8, 2 * N_STATE), lambda bb, s: (0, 0)),
            pl.BlockSpec((1, SSM_WIDTH), lambda bb, s: (0, 0)),
        ],
        out_specs=[pl.BlockSpec((1, tt, SSM_WIDTH), lambda bb, s: (bb, s, 0)),
                   pl.BlockSpec((1, 1, 2 * N_STATE), lambda bb, s: (bb, 0, 0))],
        out_shape=[jax.ShapeDtypeStruct((b, t, SSM_WIDTH), F32),
                   jax.ShapeDtypeStruct((b, 1, 2 * N_STATE), F32)],
        scratch_shapes=[pltpu.VMEM((tt, 2 * N_STATE), F32), pltpu.VMEM((1, 2 * N_STATE), F32)],
        compiler_params=_cparams("parallel", "arbitrary"),
        name="s5_scan",
    )(u, bblk, cblk, pw, d_skip)


def _ssm_step_kernel(u_ref, x0_ref, bblk_ref, cblk_ref, pw_ref, d_ref, y_ref, x_ref):
    u = u_ref[...]
    bu = jnp.dot(u, bblk_ref[...], preferred_element_type=F32, precision=lax.Precision.HIGHEST)
    ar = pw_ref[0:1, :N_STATE]
    ai = pw_ref[0:1, N_STATE:]
    x0r = x0_ref[:, :N_STATE]
    x0i = x0_ref[:, N_STATE:]
    xr = ar * x0r - ai * x0i + bu[:, :N_STATE]
    xi = ar * x0i + ai * x0r + bu[:, N_STATE:]
    x_ref[:, :N_STATE] = xr
    x_ref[:, N_STATE:] = xi
    y = jnp.dot(x_ref[...], cblk_ref[...], preferred_element_type=F32, precision=lax.Precision.HIGHEST)
    y_ref[...] = y + d_ref[...] * u


def _ssm_step(u, x0, bblk, cblk, pw, d_skip):
    n = u.shape[0]
    return pl.pallas_call(
        _ssm_step_kernel,
        out_shape=[jax.ShapeDtypeStruct((n, SSM_WIDTH), F32), jax.ShapeDtypeStruct((n, 2 * N_STATE), F32)],
        compiler_params=pltpu.CompilerParams(vmem_limit_bytes=VMEM_LIMIT_BYTES),
        name="s5_step",
    )(u, x0, bblk, cblk, pw, d_skip)


def _gelu_tanh(x):
    c = np.float32(np.sqrt(2.0 / np.pi))
    return 0.5 * x * (1.0 + jnp.tanh(c * (x + 0.044715 * (x * x * x))))


def _merge_kernel(x_ref, oa_ref, ys_ref, g_ref, wgt_ref, wglu_ref, bglu_ref, wpa_ref, wpb_ref, wout_ref, o_ref):
    x = x_ref[...]
    h = _rmsnorm(x, g_ref[...]).astype(BF16)
    gates = _sigmoid(jnp.dot(h, wgt_ref[...], preferred_element_type=F32))
    ys = _gelu_tanh(ys_ref[...])
    glu = jnp.dot(ys.astype(BF16), wglu_ref[...], preferred_element_type=F32) + bglu_ref[...]
    ob = ys * _sigmoid(glu)
    pa = jnp.dot(oa_ref[...], wpa_ref[...], preferred_element_type=F32)
    pb = jnp.dot(ob.astype(BF16), wpb_ref[...], preferred_element_type=F32)
    merged = gates[:, :D_MODEL] * pa + gates[:, D_MODEL:] * pb
    o_ref[...] = x + jnp.dot(merged.astype(BF16), wout_ref[...], preferred_element_type=F32)


def _merge(x, oa, ys, g, wgt, wglu, bglu, wpa, wpb, wout, *, tm):
    m, d = x.shape
    row = lambda n: pl.BlockSpec((tm, n), lambda i: (i, 0))
    full = lambda a: pl.BlockSpec(a.shape, lambda i: (0, 0))
    return pl.pallas_call(
        _merge_kernel,
        grid=(m // tm,),
        in_specs=[row(d), row(D_ATTN), row(SSM_WIDTH), full(g), full(wgt), full(wglu), full(bglu),
                  full(wpa), full(wpb), full(wout)],
        out_specs=row(d),
        out_shape=jax.ShapeDtypeStruct((m, d), F32),
        compiler_params=_cparams("parallel"),
        name="merge",
    )(x, oa, ys, g, wgt, wglu, bglu, wpa, wpb, wout)


_PAGES_PER_STEP = 8


def _sample_scores_kernel(pt_ref, qi_ref, w_ref, knew_ref, *refs, page_size, n_steps):
    pages = refs[:_PAGES_PER_STEP]
    o_ref = refs[_PAGES_PER_STEP]
    p = pl.program_id(0)
    b = pl.program_id(1)
    qi = qi_ref[0]
    w = w_ref[0]
    lane = lax.broadcasted_iota(I32, (page_size, 128), 1)

    def score(k_rows):
        d = jnp.dot(k_rows, qi, preferred_element_type=F32)
        return jnp.sum(jnp.maximum(d, 0.0) * w, axis=1, keepdims=True)

    @pl.when(b == 0)
    def _():
        o_ref[...] = jnp.zeros_like(o_ref)

    @pl.when(p < n_steps)
    def _():
        for r in range(_PAGES_PER_STEP):
            rows = slice(r * page_size, (r + 1) * page_size)
            o_ref[rows, :] = jnp.where(lane == b, score(pages[r][0].astype(BF16)), o_ref[rows, :])

    @pl.when(p == n_steps)
    def _():
        s_self = score(jnp.broadcast_to(knew_ref[0], (page_size, IDX_DIM)))
        row = lax.broadcasted_iota(I32, (page_size, 128), 0)
        mine = jnp.where(row == 0, jnp.where(lane == b, 1, 0), 0) > 0
        o_ref[0:page_size, :] = jnp.where(mine, s_self, o_ref[0:page_size, :])


def _sample_scores(page_table, qi_cols, w_row, ki_new, cache_kidx):
    db, n_pages = page_table.shape
    _, page_size, _ = cache_kidx.shape
    assert db <= 128 and n_pages % _PAGES_PER_STEP == 0
    steps = n_pages // _PAGES_PER_STEP
    block_rows = _PAGES_PER_STEP * page_size
    per_b = lambda shape: pl.BlockSpec((1,) + shape, lambda p, b, pt: (b,) + (0,) * len(shape))

    def page_spec(r):
        def index_map(p, b, pt):
            return (pt[b, jnp.minimum(p * _PAGES_PER_STEP + r, n_pages - 1)], 0, 0)
        return pl.BlockSpec((1, page_size, IDX_DIM), index_map)

    return pl.pallas_call(
        functools.partial(_sample_scores_kernel, page_size=page_size, n_steps=steps),
        grid_spec=pltpu.PrefetchScalarGridSpec(
            num_scalar_prefetch=1,
            grid=(steps + 1, db),
            in_specs=[per_b((IDX_DIM, 128)), per_b((1, 128)), per_b((1, IDX_DIM))]
                     + [page_spec(r) for r in range(_PAGES_PER_STEP)],
            out_specs=pl.BlockSpec((block_rows, 128), lambda p, b, pt: (p, 0)),
        ),
        out_shape=jax.ShapeDtypeStruct(((steps + 1) * block_rows, 128), F32),
        compiler_params=_cparams("parallel", "arbitrary"),
        name="sample_scores",
    )(page_table, qi_cols, w_row, ki_new, *([cache_kidx] * _PAGES_PER_STEP))


def _sample_select_kernel(st_ref, thr_ref, cut_ref, s_ref, fst_ref, ist_ref, *, tk, tq, n_keys, k_top):
    n_chunks = st_ref.shape[0] // tk
    row_iota = lax.broadcasted_iota(I32, (tk, tq), 0)

    def build(c, carry):
        mx, mn, n_hi = carry
        k0 = pl.multiple_of(c * tk, tk)
        s = st_ref[pl.ds(k0, tk), :]
        real = (k0 + row_iota) < n_keys
        sm = jnp.where(real, s, -jnp.inf)
        s_ref[pl.ds(k0, tk), :] = sm
        return (jnp.maximum(mx, _fold8(sm, jnp.max)),
                jnp.minimum(mn, _fold8(jnp.where(real, s, jnp.inf), jnp.min)),
                n_hi + _fold8(jnp.where(sm > NEG_BIG, 1, 0), jnp.sum))

    mx, mn, n_hi = lax.fori_loop(
        0, n_chunks, build,
        (jnp.full((8, tq), -jnp.inf, F32), jnp.full((8, tq), jnp.inf, F32), jnp.zeros((8, tq), I32)))
    _topk_threshold(s_ref, n_chunks, tk, tq, jnp.max(mx, axis=0, keepdims=True),
                    jnp.min(mn, axis=0, keepdims=True), jnp.sum(n_hi, axis=0, keepdims=True),
                    jnp.zeros((1, tq), I32), k_top, thr_ref, cut_ref, fst_ref, ist_ref)


def _sample_select(scores_t, *, n_keys, k_top, tk):
    n_rows, tq = scores_t.shape
    kern = functools.partial(_sample_select_kernel, tk=tk, tq=tq, n_keys=n_keys, k_top=k_top)
    return pl.pallas_call(
        kern,
        out_shape=[jax.ShapeDtypeStruct((1, 1, tq), F32), jax.ShapeDtypeStruct((1, 1, tq), I32)],
        scratch_shapes=[pltpu.VMEM((n_rows, tq), F32), pltpu.VMEM((8, tq), F32), pltpu.VMEM((8, tq), I32)],
        compiler_params=pltpu.CompilerParams(vmem_limit_bytes=VMEM_LIMIT_BYTES),
        name="sample_select",
    )(scores_t)


_COMPACT_CHUNK = 256


def _sample_compact_kernel(s_ref, thr_ref, cut_ref, o_ref, pos_ref, acc_ref, *, k_top, n_keys):
    db, n = s_ref.shape
    ck = _COMPACT_CHUNK
    lane = lax.broadcasted_iota(I32, (db, ck), 1)
    before = jnp.where(lax.broadcasted_iota(I32, (ck, ck), 0) < lax.broadcasted_iota(I32, (ck, ck), 1),
                       1.0, 0.0).astype(BF16)
    thr = thr_ref[...]
    cut = cut_ref[...]
    seen = jnp.zeros((db, 1), F32)
    for c in range(n // ck):
        cols = slice(c * ck, (c + 1) * ck)
        kidx = c * ck + lane
        sel = _select_bias(jnp.where(kidx < n_keys, s_ref[:, cols], -jnp.inf), thr, kidx, cut) == 0.0
        one = jnp.where(sel, 1.0, 0.0)
        rank = jnp.dot(one.astype(BF16), before, preferred_element_type=F32) + seen
        pos_ref[:, cols] = jnp.where(sel, rank, -1.0)
        seen = seen + jnp.sum(one, axis=1, keepdims=True)

    slot = lax.broadcasted_iota(I32, (k_top, ck), 0).astype(F32)
    key_lane = lax.broadcasted_iota(I32, (k_top, ck), 1).astype(F32)

    def per_seq(b, carry):
        acc_ref[...] = jnp.zeros_like(acc_ref)
        for c in range(n // ck):
            pos = pos_ref[pl.ds(b, 1), c * ck:(c + 1) * ck]
            acc_ref[...] += jnp.where(pos == slot, c * ck + key_lane, 0.0)
        o_ref[pl.ds(b, 1)] = jnp.sum(acc_ref[...], axis=1, keepdims=True).astype(I32)[None]
        return carry

    lax.fori_loop(0, db, per_seq, 0)


def _sample_compact(scores, thr, cut, *, k_top, n_keys):
    db, n = scores.shape
    return pl.pallas_call(
        functools.partial(_sample_compact_kernel, k_top=k_top, n_keys=n_keys),
        out_shape=jax.ShapeDtypeStruct((db, k_top, 1), I32),
        scratch_shapes=[pltpu.VMEM((db, n), F32), pltpu.VMEM((k_top, _COMPACT_CHUNK), F32)],
        compiler_params=pltpu.CompilerParams(vmem_limit_bytes=VMEM_LIMIT_BYTES),
        name="sample_compact",
    )(scores, thr, cut)


def _sample_attn_kernel(idx_ref, pt_ref, q_ref, knew_ref, vnew_ref, ck_ref, cv_ref, o_ref,
                        kbuf, vbuf, sem, *, page_size, past_len, k_top):
    b = pl.program_id(0)

    def row_copy(src_k, src_v, r):
        return (pltpu.make_async_copy(src_k, kbuf.at[r], sem.at[0]),
                pltpu.make_async_copy(src_v, vbuf.at[r], sem.at[1]))

    def issue(r, carry):
        j = idx_ref[b, r]

        @pl.when(j < past_len)
        def _():
            page = pt_ref[b, j // page_size]
            off = j % page_size
            for cp in row_copy(ck_ref.at[page, off], cv_ref.at[page, off], r):
                cp.start()

        @pl.when(j >= past_len)
        def _():
            for cp in row_copy(knew_ref.at[b], vnew_ref.at[b], r):
                cp.start()

        return carry

    def drain(r, carry):
        for cp in row_copy(ck_ref.at[0, 0], cv_ref.at[0, 0], r):
            cp.wait()
        return carry

    lax.fori_loop(0, k_top, issue, 0)
    lax.fori_loop(0, k_top, drain, 0)

    n = k_top * N_HEADS
    k2 = kbuf[...].reshape(n, HEAD_DIM).astype(BF16)
    v2 = vbuf[...].reshape(n, HEAD_DIM).astype(BF16)
    logit = jnp.dot(k2, q_ref[0], preferred_element_type=F32)
    row_head = lax.broadcasted_iota(I32, (n, 128), 0) % N_HEADS
    logit = jnp.where(row_head == lax.broadcasted_iota(I32, (n, 128), 1), logit, -jnp.inf)
    m = jnp.max(logit, axis=0, keepdims=True)
    p = jnp.exp(logit - jnp.where(m == -jnp.inf, 0.0, m))
    l = jnp.sum(p, axis=0, keepdims=True)
    out = jnp.dot(jnp.transpose(p)[:N_HEADS, :].astype(BF16), v2, preferred_element_type=F32)
    l_col = jnp.transpose(jnp.broadcast_to(l, (128, 128)))[:N_HEADS, 0:1]
    o_ref[0] = out / l_col


def _sample_attn(idx, page_table, q_cols, k_new, v_new, cache_k, cache_v):
    db, k_top = idx.shape
    _, page_size, nh, hd = cache_k.shape
    past_len = page_table.shape[1] * page_size
    hbm = pl.BlockSpec(memory_space=pl.ANY)
    return pl.pallas_call(
        functools.partial(_sample_attn_kernel, page_size=page_size, past_len=past_len, k_top=k_top),
        grid_spec=pltpu.PrefetchScalarGridSpec(
            num_scalar_prefetch=2,
            grid=(db,),
            in_specs=[pl.BlockSpec((1, HEAD_DIM, 128), lambda b, idx, pt: (b, 0, 0)), hbm, hbm, hbm, hbm],
            out_specs=pl.BlockSpec((1, nh, hd), lambda b, idx, pt: (b, 0, 0)),
            scratch_shapes=[pltpu.VMEM((k_top, nh, hd), F32), pltpu.VMEM((k_top, nh, hd), F32),
                            pltpu.SemaphoreType.DMA((2,))],
        ),
        out_shape=jax.ShapeDtypeStruct((db, nh, hd), F32),
        compiler_params=_cparams("arbitrary"),
        name="sample_attn",
    )(idx, page_table, q_cols, k_new, v_new, cache_k, cache_v)


def _pick(n, pref):
    for t in pref:
        if n % t == 0:
            return t
    return n


def _pack_w_in(w_in):
    o = np.cumsum([0, D_ATTN, D_ATTN, D_ATTN, IDX_HEADS * IDX_DIM, IDX_DIM, IDX_HEADS, SSM_WIDTH, 2 * D_MODEL])
    z = lambda n: jnp.zeros((D_MODEL, n), w_in.dtype)
    w_all = jnp.concatenate([
        w_in[:, o[0]:o[4]],
        w_in[:, o[4]:o[5]], z(_PWI - _PKI - IDX_DIM),
        w_in[:, o[5]:o[6]], z(_PU - _PWI - IDX_HEADS),
        w_in[:, o[6]:o[7]],
    ], axis=1)
    return w_all.astype(BF16), w_in[:, o[7]:o[8]].astype(BF16)


def _prompt_layer(x, lw, ssm, final_norm):
    bsz, seq, d = x.shape
    m = bsz * seq
    tm = _pick(m, (512, 256, 128, 64, 32, 16, 8))
    x1 = _ffn(x.reshape(m, d), *lw["ffn1"], lw["g_final"], final_norm=False, tm=tm, tf=lw["tf"])
    q_b, k_f, v_f, k_b, v_b, qi_b, ki_f, ki_b, wi_f, u_f = _proj(x1, lw["g_mix"], lw["w_all"], tm=tm)

    heads_t = lambda a, nh: a.reshape(bsz, seq, nh, -1).transpose(0, 2, 3, 1)
    qt = heads_t(q_b, N_HEADS)
    vt = heads_t(v_b, N_HEADS)
    kh = k_b.reshape(bsz, seq, N_HEADS, HEAD_DIM).transpose(0, 2, 1, 3)
    qit = heads_t(qi_b, IDX_HEADS)
    kib = ki_b.reshape(bsz, seq, IDX_DIM)
    wit = wi_f[:, :8].reshape(bsz, seq, 8).transpose(0, 2, 1)

    k_top = min(TOPK_MAX, seq // 4)
    vf, cut = _select(qit, kib, wit, tq=_pick(seq, (256, 128)), tk=_pick(seq, (256, 128)), k_top=k_top)
    o_t = _attn(qt, kh, vt, qit, kib, wit, vf, cut,
                tq=_pick(seq, (256, 128)), tk=_pick(seq, (512, 256, 128)))
    oa = o_t.transpose(0, 3, 1, 2).reshape(m, D_ATTN)

    tt = _pick(seq, (256, 128, 64, 32, 16, 8))
    ys, xfin = _ssm(u_f.reshape(bsz, seq, SSM_WIDTH), ssm["bblk"].astype(BF16), ssm["cblk"].astype(BF16),
                    ssm["pw"], ssm["d_skip"], tt=tt)
    x2 = _merge(x1, oa, ys.reshape(m, SSM_WIDTH), *lw["merge"], tm=tm)
    y = _ffn(x2, *lw["ffn2"], lw["g_final"], final_norm=final_norm, tm=tm, tf=lw["tf"])
    new = (k_f.reshape(bsz, seq, N_HEADS, HEAD_DIM), v_f.reshape(bsz, seq, N_HEADS, HEAD_DIM),
           ki_f.reshape(bsz, seq, IDX_DIM),
           xfin[:, 0, :N_STATE].reshape(bsz, N_GROUPS, STATE_DIM),
           xfin[:, 0, N_STATE:].reshape(bsz, N_GROUPS, STATE_DIM))
    return y.reshape(bsz, seq, d), new


def _sample_layer(x, lw, ssm, final_norm, cache_k, cache_v, cache_kidx, st_re, st_im, page_table):
    db, ds, d = x.shape
    assert ds == 1, "one new token per sample sequence"
    n_pool, page_size = cache_k.shape[0], cache_k.shape[1]
    past_len = page_table.shape[1] * page_size
    s1 = _ffn(x.reshape(db, d), *lw["ffn1"], lw["g_final"], final_norm=False, tm=db, tf=lw["tf"])
    q_b, k_f, v_f, _, _, qi_b, ki_f, ki_b, wi_f, u_f = _proj(s1, lw["g_mix"], lw["w_all"], tm=db)

    qi_cols = jnp.zeros((db, IDX_DIM, 128), BF16).at[:, :, :IDX_HEADS].set(
        qi_b.reshape(db, IDX_HEADS, IDX_DIM).transpose(0, 2, 1))
    scores_t = _sample_scores(page_table, qi_cols, wi_f.reshape(db, 1, 128),
                              ki_b.reshape(db, 1, IDX_DIM), cache_kidx)

    n_keys = past_len + 1
    k_top = min(TOPK_MAX, n_keys // 4)
    thr, cut = _sample_select(scores_t, n_keys=n_keys, k_top=k_top, tk=256)
    idx = _sample_compact(scores_t[:, :db].T, thr[0, 0, :db, None], cut[0, 0, :db, None],
                          k_top=k_top, n_keys=n_keys)[:, :, 0]

    q_cols = jnp.zeros((db, HEAD_DIM, 128), BF16).at[:, :, :N_HEADS].set(
        q_b.reshape(db, N_HEADS, HEAD_DIM).transpose(0, 2, 1))
    o_s = _sample_attn(idx, page_table, q_cols, k_f.reshape(db, N_HEADS, HEAD_DIM),
                       v_f.reshape(db, N_HEADS, HEAD_DIM), cache_k, cache_v)
    oa = o_s.reshape(db, D_ATTN).astype(BF16)

    x0 = jnp.concatenate([st_re.reshape(db, N_STATE), st_im.reshape(db, N_STATE)], axis=1)
    ys, x_new = _ssm_step(u_f, x0, ssm["bblk"], ssm["cblk"], ssm["pw"], ssm["d_skip"])
    s2 = _merge(s1, oa, ys, *lw["merge"], tm=db)
    y = _ffn(s2, *lw["ffn2"], lw["g_final"], final_norm=final_norm, tm=db, tf=lw["tf"])
    new = (k_f.reshape(db, ds, N_HEADS, HEAD_DIM), v_f.reshape(db, ds, N_HEADS, HEAD_DIM),
           ki_f.reshape(db, ds, IDX_DIM),
           x_new[:, :N_STATE].reshape(db, N_GROUPS, STATE_DIM),
           x_new[:, N_STATE:].reshape(db, N_GROUPS, STATE_DIM))
    return y.reshape(db, ds, d), new


def kernel(x_prompt, x_sample, cache_k, cache_v, cache_kidx, state_ssm_re, state_ssm_im, page_table,
           g_ffn1, w1_gate, w1_up, w1_down, g_mix, w_in, a_re, a_im, log_dt, b_re, b_im, c_re, c_im,
           d_skip, w_glu, b_glu, w_pa, w_pb, w_out, g_ffn2, w2_gate, w2_up, w2_down, g_final):
    depth = w_in.shape[0]
    row = lambda a: a.reshape(1, -1)
    bf = lambda a: a.astype(BF16)
    xp, xs = x_prompt, x_sample
    new_p, new_s = [], []
    for l in range(depth):
        w_all, w_gates = _pack_w_in(w_in[l])
        lw = dict(
            ffn1=(row(g_ffn1[l]), bf(w1_gate[l]), bf(w1_up[l]), bf(w1_down[l])),
            ffn2=(row(g_ffn2[l]), bf(w2_gate[l]), bf(w2_up[l]), bf(w2_down[l])),
            merge=(row(g_mix[l]), w_gates, bf(w_glu[l]), row(b_glu[l]), bf(w_pa[l]), bf(w_pb[l]), bf(w_out[l])),
            g_mix=row(g_mix[l]), w_all=w_all, g_final=row(g_final),
            tf=_pick(w1_gate.shape[2], (1408, 1024, 512, 256, 128)),
        )
        pwr, pwi, bbr, bbi = _discretize(a_re[l], a_im[l], log_dt[l], b_re[l], b_im[l])
        ssm = dict(
            pw=jnp.concatenate([pwr, pwi], axis=1),
            bblk=jnp.concatenate([_block_diag_in(bbr), _block_diag_in(bbi)], axis=1),
            cblk=jnp.concatenate([_block_diag_out(c_re[l]), -_block_diag_out(c_im[l])], axis=0),
            d_skip=row(d_skip[l]),
        )
        last = l == depth - 1
        xp, st_p = _prompt_layer(xp, lw, ssm, last)
        xs, st_s = _sample_layer(xs, lw, ssm, last, cache_k[l], cache_v[l], cache_kidx[l],
                                 state_ssm_re[l], state_ssm_im[l], page_table)
        new_p.append(st_p)
        new_s.append(st_s)
    stack = lambda states, i: jnp.stack([s[i] for s in states])
    return (xp, xs) + tuple(stack(new_p, i) for i in range(5)) + tuple(stack(new_s, i) for i in range(5))
```

```python
import functools

import numpy as np
import jax
import jax.numpy as jnp
from jax import lax
from jax.experimental import pallas as pl
from jax.experimental.pallas import tpu as pltpu

F32 = jnp.float32
BF16 = jnp.bfloat16
I32 = jnp.int32

D_MODEL = 1024
N_HEADS = 8
HEAD_DIM = 64
D_ATTN = N_HEADS * HEAD_DIM
IDX_HEADS = 4
IDX_DIM = 64
TOPK_MAX = 256
SSM_WIDTH = 512
GROUP_CH = 16
N_GROUPS = SSM_WIDTH // GROUP_CH
STATE_DIM = 64
N_STATE = N_GROUPS * STATE_DIM
NORM_EPS = 1e-6
NEG_BIG = -1e30

INT_MAX = 2 ** 31 - 1

VMEM_LIMIT_BYTES = 56 * 1024 * 1024


def _cparams(*sem):
    return pltpu.CompilerParams(dimension_semantics=sem, vmem_limit_bytes=VMEM_LIMIT_BYTES)


def _rmsnorm(x, g):
    y = x * lax.rsqrt(jnp.mean(x * x, axis=-1, keepdims=True) + NORM_EPS)
    return y * g


def _sigmoid(x):
    return 1.0 / (1.0 + jnp.exp(-x))


def _ffn_kernel(x_ref, g_ref, wg_ref, wu_ref, wd_ref, gf_ref, o_ref, xn_ref, acc_ref, *, final_norm):
    f = pl.program_id(1)

    @pl.when(f == 0)
    def _():
        xn_ref[...] = _rmsnorm(x_ref[...], g_ref[...]).astype(BF16)
        acc_ref[...] = jnp.zeros_like(acc_ref)

    xn = xn_ref[...]
    gate = jnp.dot(xn, wg_ref[...], preferred_element_type=F32)
    up = jnp.dot(xn, wu_ref[...], preferred_element_type=F32)
    act = (gate * _sigmoid(gate)) * up
    acc_ref[...] += jnp.dot(act.astype(BF16), wd_ref[...], preferred_element_type=F32)

    @pl.when(f == pl.num_programs(1) - 1)
    def _():
        y = x_ref[...] + 0.5 * acc_ref[...]
        if final_norm:
            y = _rmsnorm(y, gf_ref[...])
        o_ref[...] = y


def _ffn(x, g, wg, wu, wd, g_final, *, final_norm, tm, tf):
    m, d = x.shape
    dff = wg.shape[1]
    return pl.pallas_call(
        functools.partial(_ffn_kernel, final_norm=final_norm),
        grid=(m // tm, dff // tf),
        in_specs=[
            pl.BlockSpec((tm, d), lambda i, f: (i, 0)),
            pl.BlockSpec((1, d), lambda i, f: (0, 0)),
            pl.BlockSpec((d, tf), lambda i, f: (0, f)),
            pl.BlockSpec((d, tf), lambda i, f: (0, f)),
            pl.BlockSpec((tf, d), lambda i, f: (f, 0)),
            pl.BlockSpec((1, d), lambda i, f: (0, 0)),
        ],
        out_specs=pl.BlockSpec((tm, d), lambda i, f: (i, 0)),
        out_shape=jax.ShapeDtypeStruct((m, d), F32),
        scratch_shapes=[pltpu.VMEM((tm, d), BF16), pltpu.VMEM((tm, d), F32)],
        compiler_params=_cparams("parallel", "arbitrary"),
        name="ffn",
    )(x, g, wg, wu, wd, g_final)


_PQ, _PK, _PV, _PQI, _PKI, _PWI, _PU, _PEND = 0, 512, 1024, 1536, 1792, 1920, 2048, 2560


def _proj_kernel(x_ref, g_ref, w_ref, q_ref, kf_ref, vf_ref, kb_ref, vb_ref,
                 qi_ref, kif_ref, kib_ref, wi_ref, u_ref):
    h = _rmsnorm(x_ref[...], g_ref[...]).astype(BF16)
    p = jnp.dot(h, w_ref[...], preferred_element_type=F32)
    q_ref[...] = (p[:, _PQ:_PK] * (HEAD_DIM ** -0.5)).astype(BF16)
    k = p[:, _PK:_PV]
    v = p[:, _PV:_PQI]
    kf_ref[...] = k
    vf_ref[...] = v
    kb_ref[...] = k.astype(BF16)
    vb_ref[...] = v.astype(BF16)
    qi_ref[...] = p[:, _PQI:_PKI].astype(BF16)
    ki = p[:, _PKI:_PKI + IDX_DIM]
    kif_ref[...] = ki
    kib_ref[...] = ki.astype(BF16)
    wi_ref[...] = p[:, _PWI:_PU] * ((IDX_HEADS * IDX_DIM) ** -0.5)
    u_ref[...] = p[:, _PU:_PEND]


def _proj(x, g, w_all, *, tm):
    m, d = x.shape
    row = lambda n: pl.BlockSpec((tm, n), lambda i: (i, 0))
    outs = [
        (D_ATTN, BF16), (D_ATTN, F32), (D_ATTN, F32), (D_ATTN, BF16), (D_ATTN, BF16),
        (IDX_HEADS * IDX_DIM, BF16), (IDX_DIM, F32), (IDX_DIM, BF16), (128, F32), (SSM_WIDTH, F32),
    ]
    return pl.pallas_call(
        _proj_kernel,
        grid=(m // tm,),
        in_specs=[row(d), pl.BlockSpec((1, d), lambda i: (0, 0)),
                  pl.BlockSpec((d, _PEND), lambda i: (0, 0))],
        out_specs=[row(n) for n, _ in outs],
        out_shape=[jax.ShapeDtypeStruct((m, n), dt) for n, dt in outs],
        compiler_params=_cparams("parallel"),
        name="proj",
    )(x, g, w_all)


def _scores_t(kc, qit, w):
    s = None
    for h in range(IDX_HEADS):
        d = jnp.dot(kc, qit[h], preferred_element_type=F32)
        term = jnp.maximum(d, 0.0) * w[h:h + 1, :]
        s = term if s is None else s + term
    return s


def _select_bias(s, vf, kidx, cut):
    tie = jnp.where(kidx < cut, 0.0, -jnp.inf)
    return jnp.where(s > vf, 0.0, jnp.where(s == vf, tie, -jnp.inf))


def _fold8(x, op):
    return op(x.reshape(x.shape[0] // 8, 8, x.shape[1]), axis=0)


_SETTLE_FIRST, _SETTLE_EVERY, _BISECT_CAP = 10, 5, 320


def _topk_threshold(s_ref, n_chunks, tk, tq, mx, mn, n_hi, n_nc, k_top, thr_ref, cut_ref, fst_ref, ist_ref):
    row_iota = lax.broadcasted_iota(I32, (tk, tq), 0)
    zeros_i = jnp.zeros((8, tq), I32)

    def chunk(c):
        k0 = pl.multiple_of(c * tk, tk)
        return s_ref[pl.ds(k0, tk), :], k0

    def count_gt(v):
        def body(c, acc):
            blk, _ = chunk(c)
            return acc + _fold8(jnp.where(blk > v, 1, 0), jnp.sum)
        real = jnp.sum(lax.fori_loop(0, n_chunks, body, zeros_i), axis=0, keepdims=True)
        return real + jnp.where(NEG_BIG > v, n_nc, 0)

    def count_gt_eq(v):
        def body(c, acc):
            blk, _ = chunk(c)
            return (acc[0] + _fold8(jnp.where(blk > v, 1, 0), jnp.sum),
                    acc[1] + _fold8(jnp.where(blk == v, 1, 0), jnp.sum))
        g, e = lax.fori_loop(0, n_chunks, body, (zeros_i, zeros_i))
        g = jnp.sum(g, axis=0, keepdims=True) + jnp.where(NEG_BIG > v, n_nc, 0)
        return g, jnp.sum(e, axis=0, keepdims=True)

    def max_le(v):
        def body(c, acc):
            blk, _ = chunk(c)
            return jnp.maximum(acc, _fold8(jnp.where(blk <= v, blk, -jnp.inf), jnp.max))
        m = jnp.max(lax.fori_loop(0, n_chunks, body, jnp.full((8, tq), -jnp.inf, F32)), axis=0, keepdims=True)
        return jnp.where((n_nc > 0) & (NEG_BIG <= v), jnp.maximum(m, NEG_BIG), m)

    def settle(m):
        gt, eq_real = count_gt_eq(m)
        eq = eq_real + jnp.where(m == NEG_BIG, n_nc, 0)
        found = (ist_ref[0:1, :] == 0) & (gt + eq >= k_top)
        need = k_top - gt
        fst_ref[2:3, :] = jnp.where(found, m, fst_ref[2:3, :])
        ist_ref[1:2, :] = jnp.where(found, need, ist_ref[1:2, :])
        ist_ref[2:3, :] = jnp.where(found, jnp.where(eq_real > need, 1, 0), ist_ref[2:3, :])
        ist_ref[3:4, :] = jnp.where(found, INT_MAX, ist_ref[3:4, :])
        ist_ref[0:1, :] = jnp.where(found, 1, ist_ref[0:1, :])

    low = n_hi < k_top
    hi0 = jnp.where(low, NEG_BIG, mx)
    fst_ref[0:1, :] = jnp.where(low, jnp.minimum(mn, NEG_BIG), jnp.maximum(mn, NEG_BIG))
    fst_ref[1:2, :] = hi0
    fst_ref[2:3, :] = hi0
    ist_ref[...] = jnp.zeros_like(ist_ref)
    settle(hi0)

    def active_lanes():
        return jnp.max(1 - ist_ref[0:1, :])

    def step(carry):
        it, _ = carry
        lo = fst_ref[0:1, :]
        hi = fst_ref[1:2, :]
        active = ist_ref[0:1, :] == 0
        mid = 0.5 * lo + 0.5 * hi
        c = count_gt(mid)
        hit = active & (c == k_top)
        fst_ref[2:3, :] = jnp.where(hit, mid, fst_ref[2:3, :])
        ist_ref[0:1, :] = jnp.where(hit, 1, ist_ref[0:1, :])
        fst_ref[0:1, :] = jnp.where(active & (c > k_top), mid, lo)
        fst_ref[1:2, :] = jnp.where(active & (c < k_top), mid, hi)
        stuck = active & ((mid <= lo) | (mid >= hi))
        due = (it >= _SETTLE_FIRST) & ((it - _SETTLE_FIRST) % _SETTLE_EVERY == 0)

        @pl.when(due | (jnp.max(jnp.where(stuck, 1, 0)) > 0))
        def _():
            settle(max_le(fst_ref[1:2, :]))

        return it + 1, active_lanes()

    lax.while_loop(lambda carry: (carry[1] > 0) & (carry[0] < _BISECT_CAP), step,
                   (jnp.int32(0), active_lanes()))
    thr_ref[0] = fst_ref[2:3, :]
    cut_ref[0] = ist_ref[3:4, :]

    @pl.when(jnp.max(ist_ref[2:3, :]) > 0)
    def _():
        thr = fst_ref[2:3, :]
        need = ist_ref[1:2, :].astype(F32)
        tri = jnp.where(lax.broadcasted_iota(I32, (tk, tk), 1) <= lax.broadcasted_iota(I32, (tk, tk), 0),
                        1.0, 0.0).astype(BF16)

        def body(c, carry):
            seen, cut = carry
            blk, k0 = chunk(c)
            tie = blk == thr
            rank = jnp.dot(tri, jnp.where(tie, 1.0, 0.0).astype(BF16), preferred_element_type=F32) + seen
            bound = jnp.where(tie, jnp.where(rank <= need, k0 + row_iota + 1, 0), 0)
            return rank[tk - 1:tk, :], jnp.maximum(cut, jnp.max(bound, axis=0, keepdims=True))

        _, cut = lax.fori_loop(0, n_chunks, body, (jnp.zeros((1, tq), F32), jnp.zeros((1, tq), I32)))
        cut_ref[0] = jnp.where(ist_ref[2:3, :] > 0, cut, ist_ref[3:4, :])


def _select_kernel(qit_ref, ki_ref, wit_ref, thr_ref, cut_ref, s_ref, fst_ref, ist_ref, *, tq, tk, n_keys, k_top):
    i = pl.program_id(1)
    q0 = i * tq
    n_chunks = (q0 + tq + tk - 1) // tk
    t_row = q0 + lax.broadcasted_iota(I32, (1, tq), 1)
    row_iota = lax.broadcasted_iota(I32, (tk, tq), 0)
    qit = qit_ref[0]
    w = wit_ref[0]

    def build(c, carry):
        mx, mn, n_hi = carry
        k0 = pl.multiple_of(c * tk, tk)
        s = _scores_t(ki_ref[0, pl.ds(k0, tk), :], qit, w)
        causal = (k0 + row_iota) <= t_row
        sm = jnp.where(causal, s, -jnp.inf)
        s_ref[pl.ds(k0, tk), :] = sm
        return (jnp.maximum(mx, _fold8(sm, jnp.max)),
                jnp.minimum(mn, _fold8(jnp.where(causal, s, jnp.inf), jnp.min)),
                n_hi + _fold8(jnp.where(sm > NEG_BIG, 1, 0), jnp.sum))

    mx, mn, n_hi = lax.fori_loop(
        0, n_chunks, build,
        (jnp.full((8, tq), -jnp.inf, F32), jnp.full((8, tq), jnp.inf, F32), jnp.zeros((8, tq), I32)))
    n_nc = (n_keys - 1) - t_row
    _topk_threshold(s_ref, n_chunks, tk, tq, jnp.max(mx, axis=0, keepdims=True),
                    jnp.min(mn, axis=0, keepdims=True), jnp.sum(n_hi, axis=0, keepdims=True),
                    n_nc, k_top, thr_ref, cut_ref, fst_ref, ist_ref)


def _select(qit, ki, wit, *, tq, tk, k_top):
    b, _, _, t = qit.shape
    kern = functools.partial(_select_kernel, tq=tq, tk=tk, n_keys=t, k_top=k_top)
    return pl.pallas_call(
        kern,
        grid=(b, t // tq),
        in_specs=[
            pl.BlockSpec((1, IDX_HEADS, IDX_DIM, tq), lambda bb, i: (bb, 0, 0, i)),
            pl.BlockSpec((1, t, IDX_DIM), lambda bb, i: (bb, 0, 0)),
            pl.BlockSpec((1, 8, tq), lambda bb, i: (bb, 0, i)),
        ],
        out_specs=[pl.BlockSpec((1, 1, tq), lambda bb, i: (bb, 0, i)),
                   pl.BlockSpec((1, 1, tq), lambda bb, i: (bb, 0, i))],
        out_shape=[jax.ShapeDtypeStruct((b, 1, t), F32), jax.ShapeDtypeStruct((b, 1, t), I32)],
        scratch_shapes=[pltpu.VMEM((t, tq), F32), pltpu.VMEM((8, tq), F32), pltpu.VMEM((8, tq), I32)],
        compiler_params=_cparams("parallel", "parallel"),
        name="select",
    )(qit, ki, wit)


def _n_key_tiles(i, tq, tk):
    return (i * tq + tq + tk - 1) // tk


def _attn_kernel(qt_ref, k_ref, vt_ref, qit_ref, ki_ref, wit_ref, vf_ref, cut_ref, o_ref,
                 m_ref, l_ref, acc_ref, *, tq, tk):
    i = pl.program_id(1)
    j = pl.program_id(2)
    nkt = _n_key_tiles(i, tq, tk)

    @pl.when(j == 0)
    def _():
        m_ref[...] = jnp.full_like(m_ref, -jnp.inf)
        l_ref[...] = jnp.zeros_like(l_ref)
        acc_ref[...] = jnp.zeros_like(acc_ref)

    @pl.when(j < nkt)
    def _():
        s = _scores_t(ki_ref[0], qit_ref[0], wit_ref[0])
        kidx = j * tk + lax.broadcasted_iota(I32, (tk, tq), 0)
        t_row = i * tq + lax.broadcasted_iota(I32, (1, tq), 1)
        vf = vf_ref[0]
        bias = _select_bias(s, vf, kidx, cut_ref[0])
        bias = jnp.where(kidx <= t_row, bias, -jnp.inf)
        for h in range(N_HEADS):
            logit = jnp.dot(k_ref[0, h], qt_ref[0, h], preferred_element_type=F32) + bias
            m_old = m_ref[h:h + 1, :]
            m_new = jnp.maximum(m_old, jnp.max(logit, axis=0, keepdims=True))
            m_safe = jnp.where(m_new == -jnp.inf, 0.0, m_new)
            p = jnp.exp(logit - m_safe)
            alpha = jnp.exp(m_old - m_safe)
            l_ref[h:h + 1, :] = alpha * l_ref[h:h + 1, :] + jnp.sum(p, axis=0, keepdims=True)
            acc_ref[h] = alpha * acc_ref[h] + jnp.dot(vt_ref[0, h], p.astype(BF16),
                                                     preferred_element_type=F32)
            m_ref[h:h + 1, :] = m_new

    @pl.when(j == nkt - 1)
    def _():
        for h in range(N_HEADS):
            o_ref[0, h] = (acc_ref[h] / l_ref[h:h + 1, :]).astype(o_ref.dtype)


def _attn(qt, k, vt, qit, ki, wit, vf, cut, *, tq, tk):
    b, _, _, t = qt.shape
    kj = lambda i, j: jnp.minimum(j, _n_key_tiles(i, tq, tk) - 1)
    return pl.pallas_call(
        functools.partial(_attn_kernel, tq=tq, tk=tk),
        grid=(b, t // tq, t // tk),
        in_specs=[
            pl.BlockSpec((1, N_HEADS, HEAD_DIM, tq), lambda bb, i, j: (bb, 0, 0, i)),
            pl.BlockSpec((1, N_HEADS, tk, HEAD_DIM), lambda bb, i, j: (bb, 0, kj(i, j), 0)),
            pl.BlockSpec((1, N_HEADS, HEAD_DIM, tk), lambda bb, i, j: (bb, 0, 0, kj(i, j))),
            pl.BlockSpec((1, IDX_HEADS, IDX_DIM, tq), lambda bb, i, j: (bb, 0, 0, i)),
            pl.BlockSpec((1, tk, IDX_DIM), lambda bb, i, j: (bb, kj(i, j), 0)),
            pl.BlockSpec((1, 8, tq), lambda bb, i, j: (bb, 0, i)),
            pl.BlockSpec((1, 1, tq), lambda bb, i, j: (bb, 0, i)),
            pl.BlockSpec((1, 1, tq), lambda bb, i, j: (bb, 0, i)),
        ],
        out_specs=pl.BlockSpec((1, N_HEADS, HEAD_DIM, tq), lambda bb, i, j: (bb, 0, 0, i)),
        out_shape=jax.ShapeDtypeStruct((b, N_HEADS, HEAD_DIM, t), BF16),
        scratch_shapes=[pltpu.VMEM((N_HEADS, tq), F32), pltpu.VMEM((N_HEADS, tq), F32),
                        pltpu.VMEM((N_HEADS, HEAD_DIM, tq), F32)],
        compiler_params=_cparams("parallel", "parallel", "arbitrary"),
        name="attn",
    )(qt, k, vt, qit, ki, wit, vf, cut)


def _zoh(ar, ai, ldt):
    dt = jnp.exp(ldt)
    mag = jnp.exp(dt * ar)
    abr = mag * jnp.cos(dt * ai)
    abi = mag * jnp.sin(dt * ai)
    den = ar * ar + ai * ai
    nr = abr - 1.0
    ni = abi
    return abr, abi, (nr * ar + ni * ai) / den, (ni * ar - nr * ai) / den


def _disc_kernel(ar_ref, ai_ref, ldt_ref, ar16_ref, ai16_ref, ldt16_ref, br_ref, bi_ref,
                 pwr_ref, pwi_ref, bbr_ref, bbi_ref):
    abr, abi, _, _ = _zoh(ar_ref[...], ai_ref[...], ldt_ref[...])
    pr, pi = abr, abi
    for j in range(8):
        pwr_ref[j:j + 1, :] = pr
        pwi_ref[j:j + 1, :] = pi
        pr, pi = pr * abr - pi * abi, pr * abi + pi * abr
    _, _, fr, fi = _zoh(ar16_ref[...], ai16_ref[...], ldt16_ref[...])
    br = br_ref[...]
    bi = bi_ref[...]
    bbr_ref[...] = fr * br - fi * bi
    bbi_ref[...] = fr * bi + fi * br


def _discretize(a_re, a_im, log_dt, b_re, b_im):
    flat = lambda a: a.reshape(1, -1)
    ldt = jnp.broadcast_to(log_dt[:, None], (N_GROUPS, STATE_DIM))
    rep = lambda a: flat(jnp.broadcast_to(a[:, :, None], (N_GROUPS, STATE_DIM, GROUP_CH)))
    n16 = N_STATE * GROUP_CH
    pwr, pwi, bbr, bbi = pl.pallas_call(
        _disc_kernel,
        out_shape=[jax.ShapeDtypeStruct((8, N_STATE), F32), jax.ShapeDtypeStruct((8, N_STATE), F32),
                   jax.ShapeDtypeStruct((1, n16), F32), jax.ShapeDtypeStruct((1, n16), F32)],
        name="s5_discretize",
    )(flat(a_re), flat(a_im), flat(ldt), rep(a_re), rep(a_im), rep(ldt), flat(b_re), flat(b_im))
    return pwr, pwi, bbr.reshape(N_GROUPS, STATE_DIM, GROUP_CH), bbi.reshape(N_GROUPS, STATE_DIM, GROUP_CH)


def _block_diag_in(bb):
    eye = jnp.eye(N_GROUPS, dtype=bb.dtype)
    return jnp.einsum('gpc,gh->gchp', bb, eye).reshape(SSM_WIDTH, N_STATE)


def _block_diag_out(c):
    eye = jnp.eye(N_GROUPS, dtype=c.dtype)
    return jnp.einsum('gcp,gh->gphc', c, eye).reshape(N_STATE, SSM_WIDTH)


_LANE_CHUNK = 512


def _ssm_kernel(u_ref, bblk_ref, cblk_ref, pw_ref, d_ref, y_ref, xf_ref, x_ref, carry_ref, *, tt):
    ts = pl.program_id(1)

    @pl.when(ts == 0)
    def _():
        carry_ref[...] = jnp.zeros_like(carry_ref)

    u = u_ref[0]
    ub = u.astype(BF16)
    n_chunks = 2 * N_STATE // _LANE_CHUNK
    for c in range(n_chunks):
        cols = slice(c * _LANE_CHUNK, (c + 1) * _LANE_CHUNK)
        x_ref[:, cols] = jnp.dot(ub, bblk_ref[:, cols], preferred_element_type=F32)
    rid = lax.broadcasted_iota(I32, (8, _LANE_CHUNK), 0)

    def group(r, carry):
        r0 = pl.multiple_of(r * 8, 8)
        for c in range(N_STATE // _LANE_CHUNK):
            re = pl.ds(c * _LANE_CHUNK, _LANE_CHUNK)
            im = pl.ds(N_STATE + c * _LANE_CHUNK, _LANE_CHUNK)
            xr = x_ref[pl.ds(r0, 8), re]
            xi = x_ref[pl.ds(r0, 8), im]
            for d in (1, 2, 4):
                ar = pw_ref[d - 1:d, re]
                ai = pw_ref[d - 1:d, im]
                sr = jnp.where(rid >= d, pltpu.roll(xr, d, 0), 0.0)
                si = jnp.where(rid >= d, pltpu.roll(xi, d, 0), 0.0)
                xr, xi = xr + (ar * sr - ai * si), xi + (ar * si + ai * sr)
            cr = carry_ref[:, re]
            ci = carry_ref[:, im]
            pr = pw_ref[:, re]
            pi = pw_ref[:, im]
            xr, xi = xr + (pr * cr - pi * ci), xi + (pr * ci + pi * cr)
            x_ref[pl.ds(r0, 8), re] = xr
            x_ref[pl.ds(r0, 8), im] = xi
            carry_ref[:, re] = xr[7:8, :]
            carry_ref[:, im] = xi[7:8, :]
        return carry

    lax.fori_loop(0, tt // 8, group, 0)
    y = d_ref[...] * u
    for c in range(n_chunks):
        cols = slice(c * _LANE_CHUNK, (c + 1) * _LANE_CHUNK)
        y = y + jnp.dot(x_ref[:, cols].astype(BF16), cblk_ref[cols, :], preferred_element_type=F32)
    y_ref[0] = y
    xf_ref[0] = carry_ref[...]


def _ssm(u, bblk, cblk, pw, d_skip, *, tt):
    b, t, _ = u.shape
    return pl.pallas_call(
        functools.partial(_ssm_kernel, tt=tt),
        grid=(b, t // tt),
        in_specs=[
            pl.BlockSpec((1, tt, SSM_WIDTH), lambda bb, s: (bb, s, 0)),
            pl.BlockSpec((SSM_WIDTH, 2 * N_STATE), lambda bb, s: (0, 0)),
            pl.BlockSpec((2 * N_STATE, SSM_WIDTH), lambda bb, s: (0, 0)),
            pl.BlockSpec((8, 2 * N_STATE), lambda bb, s: (0, 0)),
            pl.BlockSpec((1, SSM_WIDTH), lambda bb, s: (0, 0)),
        ],
        out_specs=[pl.BlockSpec((1, tt, SSM_WIDTH), lambda bb, s: (bb, s, 0)),
                   pl.BlockSpec((1, 1, 2 * N_STATE), lambda bb, s: (bb, 0, 0))],
        out_shape=[jax.ShapeDtypeStruct((b, t, SSM_WIDTH), F32),
                   jax.ShapeDtypeStruct((b, 1, 2 * N_STATE), F32)],
        scratch_shapes=[pltpu.VMEM((tt, 2 * N_STATE), F32), pltpu.VMEM((1, 2 * N_STATE), F32)],
        compiler_params=_cparams("parallel", "arbitrary"),
        name="s5_scan",
    )(u, bblk, cblk, pw, d_skip)


def _ssm_step_kernel(u_ref, x0_ref, bblk_ref, cblk_ref, pw_ref, d_ref, y_ref, x_ref):
    u = u_ref[...]
    bu = jnp.dot(u, bblk_ref[...], preferred_element_type=F32, precision=lax.Precision.HIGHEST)
    ar = pw_ref[0:1, :N_STATE]
    ai = pw_ref[0:1, N_STATE:]
    x0r = x0_ref[:, :N_STATE]
    x0i = x0_ref[:, N_STATE:]
    xr = ar * x0r - ai * x0i + bu[:, :N_STATE]
    xi = ar * x0i + ai * x0r + bu[:, N_STATE:]
    x_ref[:, :N_STATE] = xr
    x_ref[:, N_STATE:] = xi
    y = jnp.dot(x_ref[...], cblk_ref[...], preferred_element_type=F32, precision=lax.Precision.HIGHEST)
    y_ref[...] = y + d_ref[...] * u


def _ssm_step(u, x0, bblk, cblk, pw, d_skip):
    n = u.shape[0]
    return pl.pallas_call(
        _ssm_step_kernel,
        out_shape=[jax.ShapeDtypeStruct((n, SSM_WIDTH), F32), jax.ShapeDtypeStruct((n, 2 * N_STATE), F32)],
        compiler_params=pltpu.CompilerParams(vmem_limit_bytes=VMEM_LIMIT_BYTES),
        name="s5_step",
    )(u, x0, bblk, cblk, pw, d_skip)


def _gelu_tanh(x):
    c = np.float32(np.sqrt(2.0 / np.pi))
    return 0.5 * x * (1.0 + jnp.tanh(c * (x + 0.044715 * (x * x * x))))


def _merge_kernel(x_ref, oa_ref, ys_ref, g_ref, wgt_ref, wglu_ref, bglu_ref, wpa_ref, wpb_ref, wout_ref, o_ref):
    x = x_ref[...]
    h = _rmsnorm(x, g_ref[...]).astype(BF16)
    gates = _sigmoid(jnp.dot(h, wgt_ref[...], preferred_element_type=F32))
    ys = _gelu_tanh(ys_ref[...])
    glu = jnp.dot(ys.astype(BF16), wglu_ref[...], preferred_element_type=F32) + bglu_ref[...]
    ob = ys * _sigmoid(glu)
    pa = jnp.dot(oa_ref[...], wpa_ref[...], preferred_element_type=F32)
    pb = jnp.dot(ob.astype(BF16), wpb_ref[...], preferred_element_type=F32)
    merged = gates[:, :D_MODEL] * pa + gates[:, D_MODEL:] * pb
    o_ref[...] = x + jnp.dot(merged.astype(BF16), wout_ref[...], preferred_element_type=F32)


def _merge(x, oa, ys, g, wgt, wglu, bglu, wpa, wpb, wout, *, tm):
    m, d = x.shape
    row = lambda n: pl.BlockSpec((tm, n), lambda i: (i, 0))
    full = lambda a: pl.BlockSpec(a.shape, lambda i: (0, 0))
    return pl.pallas_call(
        _merge_kernel,
        grid=(m // tm,),
        in_specs=[row(d), row(D_ATTN), row(SSM_WIDTH), full(g), full(wgt), full(wglu), full(bglu),
                  full(wpa), full(wpb), full(wout)],
        out_specs=row(d),
        out_shape=jax.ShapeDtypeStruct((m, d), F32),
        compiler_params=_cparams("parallel"),
        name="merge",
    )(x, oa, ys, g, wgt, wglu, bglu, wpa, wpb, wout)


_PAGES_PER_STEP = 8


def _page_specs(block, n_pages):
    def spec(r):
        def index_map(b, p, pt):
            return (pt[b, jnp.minimum(p * _PAGES_PER_STEP + r, n_pages - 1)],) + (0,) * (len(block) - 1)
        return pl.BlockSpec(block, index_map)
    return [spec(r) for r in range(_PAGES_PER_STEP)]


def _per_seq(shape):
    return pl.BlockSpec((1,) + shape, lambda b, p, pt: (b,) + (0,) * len(shape))


def _sample_scores_kernel(pt_ref, qi_ref, w_ref, knew_ref, *refs, page_size, n_steps):
    pages = refs[:_PAGES_PER_STEP]
    o_ref = refs[_PAGES_PER_STEP]
    p = pl.program_id(1)
    qi = qi_ref[0]
    w = w_ref[0]

    def score(kt):
        d = jnp.dot(qi, kt, preferred_element_type=F32)
        return jnp.sum(jnp.maximum(d, 0.0) * w, axis=0, keepdims=True)

    @pl.when(p < n_steps)
    def _():
        for r in range(_PAGES_PER_STEP):
            o_ref[0, :, r * page_size:(r + 1) * page_size] = score(pages[r][0].astype(BF16))

    @pl.when(p == n_steps)
    def _():
        o_ref[0] = jnp.full(o_ref.shape[1:], -jnp.inf, F32)
        lane = lax.broadcasted_iota(I32, (1, page_size), 1)
        o_ref[0, :, 0:page_size] = jnp.where(lane == 0, score(knew_ref[0]), -jnp.inf)


def _sample_scores(page_table, qi_rows, w_col, ki_new_t, kidx_t):
    db, n_pages = page_table.shape
    _, _, page_size = kidx_t.shape
    assert n_pages % _PAGES_PER_STEP == 0
    steps = n_pages // _PAGES_PER_STEP
    block = _PAGES_PER_STEP * page_size
    return pl.pallas_call(
        functools.partial(_sample_scores_kernel, page_size=page_size, n_steps=steps),
        grid_spec=pltpu.PrefetchScalarGridSpec(
            num_scalar_prefetch=1,
            grid=(db, steps + 1),
            in_specs=[_per_seq((8, IDX_DIM)), _per_seq((8, 1)), _per_seq((IDX_DIM, page_size))]
                     + _page_specs((1, IDX_DIM, page_size), n_pages),
            out_specs=pl.BlockSpec((1, 1, block), lambda b, p, pt: (b, 0, p)),
        ),
        out_shape=jax.ShapeDtypeStruct((db, 1, (steps + 1) * block), F32),
        compiler_params=_cparams("parallel", "arbitrary"),
        name="sample_scores",
    )(page_table, qi_rows, w_col, ki_new_t, *([kidx_t] * _PAGES_PER_STEP))


def _sample_select_kernel(st_ref, thr_ref, cut_ref, s_ref, fst_ref, ist_ref, *, tk, tq, n_keys, k_top):
    n_chunks = st_ref.shape[0] // tk
    row_iota = lax.broadcasted_iota(I32, (tk, tq), 0)

    def build(c, carry):
        mx, mn, n_hi = carry
        k0 = pl.multiple_of(c * tk, tk)
        s = st_ref[pl.ds(k0, tk), :]
        real = (k0 + row_iota) < n_keys
        sm = jnp.where(real, s, -jnp.inf)
        s_ref[pl.ds(k0, tk), :] = sm
        return (jnp.maximum(mx, _fold8(sm, jnp.max)),
                jnp.minimum(mn, _fold8(jnp.where(real, s, jnp.inf), jnp.min)),
                n_hi + _fold8(jnp.where(sm > NEG_BIG, 1, 0), jnp.sum))

    mx, mn, n_hi = lax.fori_loop(
        0, n_chunks, build,
        (jnp.full((8, tq), -jnp.inf, F32), jnp.full((8, tq), jnp.inf, F32), jnp.zeros((8, tq), I32)))
    _topk_threshold(s_ref, n_chunks, tk, tq, jnp.max(mx, axis=0, keepdims=True),
                    jnp.min(mn, axis=0, keepdims=True), jnp.sum(n_hi, axis=0, keepdims=True),
                    jnp.zeros((1, tq), I32), k_top, thr_ref, cut_ref, fst_ref, ist_ref)


def _sample_select(scores_t, *, n_keys, k_top, tk):
    n_rows, tq = scores_t.shape
    kern = functools.partial(_sample_select_kernel, tk=tk, tq=tq, n_keys=n_keys, k_top=k_top)
    return pl.pallas_call(
        kern,
        out_shape=[jax.ShapeDtypeStruct((1, 1, tq), F32), jax.ShapeDtypeStruct((1, 1, tq), I32)],
        scratch_shapes=[pltpu.VMEM((n_rows, tq), F32), pltpu.VMEM((8, tq), F32), pltpu.VMEM((8, tq), I32)],
        compiler_params=pltpu.CompilerParams(vmem_limit_bytes=VMEM_LIMIT_BYTES),
        name="sample_select",
    )(scores_t)


def _sample_attn_kernel(pt_ref, qb_ref, s_ref, thr_ref, cut_ref, kself_ref, vself_ref, *refs,
                        page_size, n_steps):
    kp = refs[:_PAGES_PER_STEP]
    vp = refs[_PAGES_PER_STEP:2 * _PAGES_PER_STEP]
    o_ref, m_ref, l_ref, acc_ref = refs[2 * _PAGES_PER_STEP:]
    step = pl.program_id(1)
    thr = thr_ref[0]
    cut = cut_ref[0]
    lane = lax.broadcasted_iota(I32, (1, page_size), 1)
    is_self = step == n_steps

    @pl.when(step == 0)
    def _():
        m_ref[...] = jnp.full_like(m_ref, -jnp.inf)
        l_ref[...] = jnp.zeros_like(l_ref)
        acc_ref[...] = jnp.zeros_like(acc_ref)

    logits = []
    for r in range(_PAGES_PER_STEP):
        kidx = (step * _PAGES_PER_STEP + r) * page_size + lane
        bias = _select_bias(s_ref[0, :, r * page_size:(r + 1) * page_size], thr, kidx, cut)
        rows = []
        for h in range(N_HEADS):
            kt = kp[r][0, h]
            if r == 0:
                kt = jnp.where(is_self, kself_ref[0, h], kt)
            rows.append(jnp.sum(kt * qb_ref[0, h], axis=0, keepdims=True))
        logits.append(jnp.concatenate(rows, axis=0) + bias)

    m_old = m_ref[...]
    m_new = m_old
    for lg in logits:
        m_new = jnp.maximum(m_new, jnp.max(lg, axis=1, keepdims=True))
    m_safe = jnp.where(m_new == -jnp.inf, 0.0, m_new)
    alpha = jnp.exp(m_old - m_safe)
    probs = [jnp.exp(lg - m_safe) for lg in logits]
    l_new = alpha * l_ref[...]
    for p in probs:
        l_new = l_new + jnp.sum(p, axis=1, keepdims=True)
    l_ref[...] = l_new
    m_ref[...] = m_new
    for h in range(N_HEADS):
        acc = acc_ref[h] * alpha[h:h + 1, :]
        for r in range(_PAGES_PER_STEP):
            vt = vp[r][0, h]
            if r == 0:
                vt = jnp.where(is_self, vself_ref[0, h], vt)
            acc = acc + vt * probs[r][h:h + 1, :]
        acc_ref[h] = acc

    @pl.when(is_self)
    def _():
        for h in range(N_HEADS):
            o_ref[0, h] = jnp.sum(acc_ref[h], axis=1, keepdims=True) / l_ref[h:h + 1, :]


def _sample_attn(page_table, qb, scores, thr, cut, k_self, v_self, ck_t, cv_t):
    db, n_pages = page_table.shape
    _, nh, hd, page_size = ck_t.shape
    steps = n_pages // _PAGES_PER_STEP
    block = _PAGES_PER_STEP * page_size
    page = (1, nh, hd, page_size)
    return pl.pallas_call(
        functools.partial(_sample_attn_kernel, page_size=page_size, n_steps=steps),
        grid_spec=pltpu.PrefetchScalarGridSpec(
            num_scalar_prefetch=1,
            grid=(db, steps + 1),
            in_specs=[_per_seq(page[1:]),
                      pl.BlockSpec((1, 1, block), lambda b, p, pt: (b, 0, p)),
                      _per_seq((1, 1)), _per_seq((1, 1)), _per_seq(page[1:]), _per_seq(page[1:])]
                     + _page_specs(page, n_pages) + _page_specs(page, n_pages),
            out_specs=_per_seq((nh, hd, 1)),
            scratch_shapes=[pltpu.VMEM((nh, 1), F32), pltpu.VMEM((nh, 1), F32),
                            pltpu.VMEM((nh, hd, page_size), F32)],
        ),
        out_shape=jax.ShapeDtypeStruct((db, nh, hd, 1), F32),
        compiler_params=_cparams("parallel", "arbitrary"),
        name="sample_attn",
    )(page_table, qb, scores, thr, cut, k_self, v_self,
      *([ck_t] * _PAGES_PER_STEP), *([cv_t] * _PAGES_PER_STEP))


def _pick(n, pref):
    for t in pref:
        if n % t == 0:
            return t
    return n


def _pack_w_in(w_in):
    o = np.cumsum([0, D_ATTN, D_ATTN, D_ATTN, IDX_HEADS * IDX_DIM, IDX_DIM, IDX_HEADS, SSM_WIDTH, 2 * D_MODEL])
    z = lambda n: jnp.zeros((D_MODEL, n), w_in.dtype)
    w_all = jnp.concatenate([
        w_in[:, o[0]:o[4]],
        w_in[:, o[4]:o[5]], z(_PWI - _PKI - IDX_DIM),
        w_in[:, o[5]:o[6]], z(_PU - _PWI - IDX_HEADS),
        w_in[:, o[6]:o[7]],
    ], axis=1)
    return w_all.astype(BF16), w_in[:, o[7]:o[8]].astype(BF16)


def _prompt_layer(x, lw, ssm, final_norm):
    bsz, seq, d = x.shape
    m = bsz * seq
    tm = _pick(m, (512, 256, 128, 64, 32, 16, 8))
    x1 = _ffn(x.reshape(m, d), *lw["ffn1"], lw["g_final"], final_norm=False, tm=tm, tf=lw["tf"])
    q_b, k_f, v_f, k_b, v_b, qi_b, ki_f, ki_b, wi_f, u_f = _proj(x1, lw["g_mix"], lw["w_all"], tm=tm)

    heads_t = lambda a, nh: a.reshape(bsz, seq, nh, -1).transpose(0, 2, 3, 1)
    qt = heads_t(q_b, N_HEADS)
    vt = heads_t(v_b, N_HEADS)
    kh = k_b.reshape(bsz, seq, N_HEADS, HEAD_DIM).transpose(0, 2, 1, 3)
    qit = heads_t(qi_b, IDX_HEADS)
    kib = ki_b.reshape(bsz, seq, IDX_DIM)
    wit = wi_f[:, :8].reshape(bsz, seq, 8).transpose(0, 2, 1)

    k_top = min(TOPK_MAX, seq // 4)
    vf, cut = _select(qit, kib, wit, tq=_pick(seq, (256, 128)), tk=_pick(seq, (256, 128)), k_top=k_top)
    o_t = _attn(qt, kh, vt, qit, kib, wit, vf, cut,
                tq=_pick(seq, (256, 128)), tk=_pick(seq, (512, 256, 128)))
    oa = o_t.transpose(0, 3, 1, 2).reshape(m, D_ATTN)

    tt = _pick(seq, (256, 128, 64, 32, 16, 8))
    ys, xfin = _ssm(u_f.reshape(bsz, seq, SSM_WIDTH), ssm["bblk"].astype(BF16), ssm["cblk"].astype(BF16),
                    ssm["pw"], ssm["d_skip"], tt=tt)
    x2 = _merge(x1, oa, ys.reshape(m, SSM_WIDTH), *lw["merge"], tm=tm)
    y = _ffn(x2, *lw["ffn2"], lw["g_final"], final_norm=final_norm, tm=tm, tf=lw["tf"])
    new = (k_f.reshape(bsz, seq, N_HEADS, HEAD_DIM), v_f.reshape(bsz, seq, N_HEADS, HEAD_DIM),
           ki_f.reshape(bsz, seq, IDX_DIM),
           xfin[:, 0, :N_STATE].reshape(bsz, N_GROUPS, STATE_DIM),
           xfin[:, 0, N_STATE:].reshape(bsz, N_GROUPS, STATE_DIM))
    return y.reshape(bsz, seq, d), new


def _sample_layer(x, lw, ssm, final_norm, cache_k, cache_v, cache_kidx, st_re, st_im, page_table):
    db, ds, d = x.shape
    assert ds == 1, "one new token per sample sequence"
    n_pool, page_size = cache_k.shape[0], cache_k.shape[1]
    past_len = page_table.shape[1] * page_size
    s1 = _ffn(x.reshape(db, d), *lw["ffn1"], lw["g_final"], final_norm=False, tm=db, tf=lw["tf"])
    q_b, k_f, v_f, _, _, qi_b, ki_f, ki_b, wi_f, u_f = _proj(s1, lw["g_mix"], lw["w_all"], tm=db)

    kidx_t = cache_kidx.transpose(0, 2, 1)
    ck_t = cache_k.transpose(0, 2, 3, 1)
    cv_t = cache_v.transpose(0, 2, 3, 1)
    lane0 = lambda a: jnp.zeros(a.shape + (page_size,), a.dtype).at[..., 0].set(a)

    qi_rows = jnp.zeros((db, 8, IDX_DIM), BF16).at[:, :IDX_HEADS].set(qi_b.reshape(db, IDX_HEADS, IDX_DIM))
    scores = _sample_scores(page_table, qi_rows, wi_f[:, :8, None], lane0(ki_b), kidx_t)

    n_keys = past_len + 1
    k_top = min(TOPK_MAX, n_keys // 4)
    n = scores.shape[2]
    scores_t = jnp.zeros((n, -(-db // 128) * 128), F32).at[:, :db].set(scores[:, 0, :].T)
    thr, cut = _sample_select(scores_t, n_keys=n_keys, k_top=k_top, tk=256)

    qb = jnp.broadcast_to(q_b.astype(F32).reshape(db, N_HEADS, HEAD_DIM, 1), (db, N_HEADS, HEAD_DIM, page_size))
    o_s = _sample_attn(page_table, qb, scores, thr[0, 0, :db].reshape(db, 1, 1), cut[0, 0, :db].reshape(db, 1, 1),
                       lane0(k_f.reshape(db, N_HEADS, HEAD_DIM)), lane0(v_f.reshape(db, N_HEADS, HEAD_DIM)),
                       ck_t, cv_t)
    oa = o_s.reshape(db, D_ATTN).astype(BF16)

    x0 = jnp.concatenate([st_re.reshape(db, N_STATE), st_im.reshape(db, N_STATE)], axis=1)
    ys, x_new = _ssm_step(u_f, x0, ssm["bblk"], ssm["cblk"], ssm["pw"], ssm["d_skip"])
    s2 = _merge(s1, oa, ys, *lw["merge"], tm=db)
    y = _ffn(s2, *lw["ffn2"], lw["g_final"], final_norm=final_norm, tm=db, tf=lw["tf"])
    new = (k_f.reshape(db, ds, N_HEADS, HEAD_DIM), v_f.reshape(db, ds, N_HEADS, HEAD_DIM),
           ki_f.reshape(db, ds, IDX_DIM),
           x_new[:, :N_STATE].reshape(db, N_GROUPS, STATE_DIM),
           x_new[:, N_STATE:].reshape(db, N_GROUPS, STATE_DIM))
    return y.reshape(db, ds, d), new


def kernel(x_prompt, x_sample, cache_k, cache_v, cache_kidx, state_ssm_re, state_ssm_im, page_table,
           g_ffn1, w1_gate, w1_up, w1_down, g_mix, w_in, a_re, a_im, log_dt, b_re, b_im, c_re, c_im,
           d_skip, w_glu, b_glu, w_pa, w_pb, w_out, g_ffn2, w2_gate, w2_up, w2_down, g_final):
    depth = w_in.shape[0]
    row = lambda a: a.reshape(1, -1)
    bf = lambda a: a.astype(BF16)
    xp, xs = x_prompt, x_sample
    new_p, new_s = [], []
    for l in range(depth):
        w_all, w_gates = _pack_w_in(w_in[l])
        lw = dict(
            ffn1=(row(g_ffn1[l]), bf(w1_gate[l]), bf(w1_up[l]), bf(w1_down[l])),
            ffn2=(row(g_ffn2[l]), bf(w2_gate[l]), bf(w2_up[l]), bf(w2_down[l])),
            merge=(row(g_mix[l]), w_gates, bf(w_glu[l]), row(b_glu[l]), bf(w_pa[l]), bf(w_pb[l]), bf(w_out[l])),
            g_mix=row(g_mix[l]), w_all=w_all, g_final=row(g_final),
            tf=_pick(w1_gate.shape[2], (1408, 1024, 512, 256, 128)),
        )
        pwr, pwi, bbr, bbi = _discretize(a_re[l], a_im[l], log_dt[l], b_re[l], b_im[l])
        ssm = dict(
            pw=jnp.concatenate([pwr, pwi], axis=1),
            bblk=jnp.concatenate([_block_diag_in(bbr), _block_diag_in(bbi)], axis=1),
            cblk=jnp.concatenate([_block_diag_out(c_re[l]), -_block_diag_out(c_im[l])], axis=0),
            d_skip=row(d_skip[l]),
        )
        last = l == depth - 1
        xp, st_p = _prompt_layer(xp, lw, ssm, last)
        xs, st_s = _sample_layer(xs, lw, ssm, last, cache_k[l], cache_v[l], cache_kidx[l],
                                 state_ssm_re[l], state_ssm_im[l], page_table)
        new_p.append(st_p)
        new_s.append(st_s)
    stack = lambda states, i: jnp.stack([s[i] for s in states])
    return (xp, xs) + tuple(stack(new_p, i) for i in range(5)) + tuple(stack(new_s, i) for i in range(5))
```

```python
import functools

import numpy as np
import jax
import jax.numpy as jnp
from jax import lax
from jax.experimental import pallas as pl
from jax.experimental.pallas import tpu as pltpu

F32 = jnp.float32
BF16 = jnp.bfloat16
I32 = jnp.int32

D_MODEL = 1024
N_HEADS = 8
HEAD_DIM = 64
D_ATTN = N_HEADS * HEAD_DIM
IDX_HEADS = 4
IDX_DIM = 64
TOPK_MAX = 256
SSM_WIDTH = 512
GROUP_CH = 16
N_GROUPS = SSM_WIDTH // GROUP_CH
STATE_DIM = 64
N_STATE = N_GROUPS * STATE_DIM
NORM_EPS = 1e-6
NEG_BIG = -1e30

INT_MAX = 2 ** 31 - 1

VMEM_LIMIT_BYTES = 56 * 1024 * 1024


def _cparams(*sem):
    return pltpu.CompilerParams(dimension_semantics=sem, vmem_limit_bytes=VMEM_LIMIT_BYTES)


def _rmsnorm(x, g):
    y = x * lax.rsqrt(jnp.mean(x * x, axis=-1, keepdims=True) + NORM_EPS)
    return y * g


def _sigmoid(x):
    return 1.0 / (1.0 + jnp.exp(-x))


def _ffn_kernel(x_ref, g_ref, wg_ref, wu_ref, wd_ref, gf_ref, o_ref, xn_ref, acc_ref, *, final_norm):
    f = pl.program_id(1)

    @pl.when(f == 0)
    def _():
        xn_ref[...] = _rmsnorm(x_ref[...], g_ref[...]).astype(BF16)
        acc_ref[...] = jnp.zeros_like(acc_ref)

    xn = xn_ref[...]
    gate = jnp.dot(xn, wg_ref[...], preferred_element_type=F32)
    up = jnp.dot(xn, wu_ref[...], preferred_element_type=F32)
    act = (gate * _sigmoid(gate)) * up
    acc_ref[...] += jnp.dot(act.astype(BF16), wd_ref[...], preferred_element_type=F32)

    @pl.when(f == pl.num_programs(1) - 1)
    def _():
        y = x_ref[...] + 0.5 * acc_ref[...]
        if final_norm:
            y = _rmsnorm(y, gf_ref[...])
        o_ref[...] = y


def _ffn(x, g, wg, wu, wd, g_final, *, final_norm, tm, tf):
    m, d = x.shape
    dff = wg.shape[1]
    return pl.pallas_call(
        functools.partial(_ffn_kernel, final_norm=final_norm),
        grid=(m // tm, dff // tf),
        in_specs=[
            pl.BlockSpec((tm, d), lambda i, f: (i, 0)),
            pl.BlockSpec((1, d), lambda i, f: (0, 0)),
            pl.BlockSpec((d, tf), lambda i, f: (0, f)),
            pl.BlockSpec((d, tf), lambda i, f: (0, f)),
            pl.BlockSpec((tf, d), lambda i, f: (f, 0)),
            pl.BlockSpec((1, d), lambda i, f: (0, 0)),
        ],
        out_specs=pl.BlockSpec((tm, d), lambda i, f: (i, 0)),
        out_shape=jax.ShapeDtypeStruct((m, d), F32),
        scratch_shapes=[pltpu.VMEM((tm, d), BF16), pltpu.VMEM((tm, d), F32)],
        compiler_params=_cparams("parallel", "arbitrary"),
        name="ffn",
    )(x, g, wg, wu, wd, g_final)


_PQ, _PK, _PV, _PQI, _PKI, _PWI, _PU, _PEND = 0, 512, 1024, 1536, 1792, 1920, 2048, 2560


def _proj_kernel(x_ref, g_ref, w_ref, q_ref, kf_ref, vf_ref, kb_ref, vb_ref,
                 qi_ref, kif_ref, kib_ref, wi_ref, u_ref):
    h = _rmsnorm(x_ref[...], g_ref[...]).astype(BF16)
    p = jnp.dot(h, w_ref[...], preferred_element_type=F32)
    q_ref[...] = (p[:, _PQ:_PK] * (HEAD_DIM ** -0.5)).astype(BF16)
    k = p[:, _PK:_PV]
    v = p[:, _PV:_PQI]
    kf_ref[...] = k
    vf_ref[...] = v
    kb_ref[...] = k.astype(BF16)
    vb_ref[...] = v.astype(BF16)
    qi_ref[...] = p[:, _PQI:_PKI].astype(BF16)
    ki = p[:, _PKI:_PKI + IDX_DIM]
    kif_ref[...] = ki
    kib_ref[...] = ki.astype(BF16)
    wi_ref[...] = p[:, _PWI:_PU] * ((IDX_HEADS * IDX_DIM) ** -0.5)
    u_ref[...] = p[:, _PU:_PEND]


def _proj(x, g, w_all, *, tm):
    m, d = x.shape
    row = lambda n: pl.BlockSpec((tm, n), lambda i: (i, 0))
    outs = [
        (D_ATTN, BF16), (D_ATTN, F32), (D_ATTN, F32), (D_ATTN, BF16), (D_ATTN, BF16),
        (IDX_HEADS * IDX_DIM, BF16), (IDX_DIM, F32), (IDX_DIM, BF16), (128, F32), (SSM_WIDTH, F32),
    ]
    return pl.pallas_call(
        _proj_kernel,
        grid=(m // tm,),
        in_specs=[row(d), pl.BlockSpec((1, d), lambda i: (0, 0)),
                  pl.BlockSpec((d, _PEND), lambda i: (0, 0))],
        out_specs=[row(n) for n, _ in outs],
        out_shape=[jax.ShapeDtypeStruct((m, n), dt) for n, dt in outs],
        compiler_params=_cparams("parallel"),
        name="proj",
    )(x, g, w_all)


def _scores_t(kc, qit, w):
    s = None
    for h in range(IDX_HEADS):
        d = jnp.dot(kc, qit[h], preferred_element_type=F32)
        term = jnp.maximum(d, 0.0) * w[h:h + 1, :]
        s = term if s is None else s + term
    return s


def _select_bias(s, vf, kidx, cut):
    tie = jnp.where(kidx < cut, 0.0, -jnp.inf)
    return jnp.where(s > vf, 0.0, jnp.where(s == vf, tie, -jnp.inf))


_FOLD_ROWS = 32


def _fold8(x, op):
    return op(x.reshape(x.shape[0] // _FOLD_ROWS, _FOLD_ROWS, x.shape[1]), axis=0)


_PROBES_PER_ROUND = 4
_SETTLE_FIRST, _SETTLE_EVERY, _BISECT_CAP = 4, 2, 80


def _topk_threshold(s_ref, n_chunks, tk, tq, mx, mn, n_hi, n_nc, k_top, thr_ref, cut_ref, fst_ref, ist_ref):
    row_iota = lax.broadcasted_iota(I32, (tk, tq), 0)
    zeros_i = jnp.zeros((_FOLD_ROWS, tq), I32)

    def chunk(c):
        k0 = pl.multiple_of(c * tk, tk)
        return s_ref[pl.ds(k0, tk), :], k0

    def count_gt(v):
        def body(c, acc):
            blk, _ = chunk(c)
            return acc + _fold8(jnp.where(blk > v, 1, 0), jnp.sum)
        real = jnp.sum(lax.fori_loop(0, n_chunks, body, zeros_i), axis=0, keepdims=True)
        return real + jnp.where(NEG_BIG > v, n_nc, 0)

    def count_gt_eq(v):
        def body(c, acc):
            blk, _ = chunk(c)
            return (acc[0] + _fold8(jnp.where(blk > v, 1, 0), jnp.sum),
                    acc[1] + _fold8(jnp.where(blk == v, 1, 0), jnp.sum))
        g, e = lax.fori_loop(0, n_chunks, body, (zeros_i, zeros_i))
        g = jnp.sum(g, axis=0, keepdims=True) + jnp.where(NEG_BIG > v, n_nc, 0)
        return g, jnp.sum(e, axis=0, keepdims=True)

    def max_le(v):
        def body(c, acc):
            blk, _ = chunk(c)
            return jnp.maximum(acc, _fold8(jnp.where(blk <= v, blk, -jnp.inf), jnp.max))
        m = jnp.max(lax.fori_loop(0, n_chunks, body, jnp.full((_FOLD_ROWS, tq), -jnp.inf, F32)),
                    axis=0, keepdims=True)
        return jnp.where((n_nc > 0) & (NEG_BIG <= v), jnp.maximum(m, NEG_BIG), m)

    def settle(m):
        gt, eq_real = count_gt_eq(m)
        eq = eq_real + jnp.where(m == NEG_BIG, n_nc, 0)
        found = (ist_ref[0:1, :] == 0) & (gt + eq >= k_top)
        need = k_top - gt
        fst_ref[2:3, :] = jnp.where(found, m, fst_ref[2:3, :])
        ist_ref[1:2, :] = jnp.where(found, need, ist_ref[1:2, :])
        ist_ref[2:3, :] = jnp.where(found, jnp.where(eq_real > need, 1, 0), ist_ref[2:3, :])
        ist_ref[3:4, :] = jnp.where(found, INT_MAX, ist_ref[3:4, :])
        ist_ref[0:1, :] = jnp.where(found, 1, ist_ref[0:1, :])

    low = n_hi < k_top
    hi0 = jnp.where(low, NEG_BIG, mx)
    fst_ref[0:1, :] = jnp.where(low, jnp.minimum(mn, NEG_BIG), jnp.maximum(mn, NEG_BIG))
    fst_ref[1:2, :] = hi0
    fst_ref[2:3, :] = hi0
    ist_ref[...] = jnp.zeros_like(ist_ref)
    settle(hi0)

    def probe(lo, hi, thr, done):
        mid = 0.5 * lo + 0.5 * hi
        c = count_gt(mid)
        live = done == 0
        hit = live & (c == k_top)
        stuck = jnp.where(live & ((mid <= lo) | (mid >= hi)), 2, 0)
        return (jnp.where(live & (c > k_top), mid, lo), jnp.where(live & (c < k_top), mid, hi),
                jnp.where(hit, mid, thr), jnp.where(hit, 1, done), stuck)

    def step(carry):
        it, _ = carry
        lo, hi, thr, done = fst_ref[0:1, :], fst_ref[1:2, :], fst_ref[2:3, :], ist_ref[0:1, :]
        stuck = jnp.zeros_like(done)
        for _ in range(_PROBES_PER_ROUND):
            lo, hi, thr, done, s = probe(lo, hi, thr, done)
            stuck = jnp.maximum(stuck, s)
        fst_ref[0:1, :] = lo
        fst_ref[1:2, :] = hi
        fst_ref[2:3, :] = thr
        ist_ref[0:1, :] = done
        code = jnp.max(jnp.maximum(stuck, 1 - done))
        due = (it >= _SETTLE_FIRST) & ((it - _SETTLE_FIRST) % _SETTLE_EVERY == 0)

        check = (code > 0) & (due | (code >= 2))

        @pl.when(check)
        def _():
            settle(max_le(hi))

        return it + 1, lax.cond(check, lambda: jnp.max(1 - ist_ref[0:1, :]), lambda: code)

    lax.while_loop(lambda carry: (carry[1] > 0) & (carry[0] < _BISECT_CAP), step,
                   (jnp.int32(0), jnp.max(1 - ist_ref[0:1, :])))
    thr_ref[0] = fst_ref[2:3, :]
    cut_ref[0] = ist_ref[3:4, :]

    @pl.when(jnp.max(ist_ref[2:3, :]) > 0)
    def _():
        thr = fst_ref[2:3, :]
        need = ist_ref[1:2, :].astype(F32)
        tri = jnp.where(lax.broadcasted_iota(I32, (tk, tk), 1) <= lax.broadcasted_iota(I32, (tk, tk), 0),
                        1.0, 0.0).astype(BF16)

        def body(c, carry):
            seen, cut = carry
            blk, k0 = chunk(c)
            tie = blk == thr
            rank = jnp.dot(tri, jnp.where(tie, 1.0, 0.0).astype(BF16), preferred_element_type=F32) + seen
            bound = jnp.where(tie, jnp.where(rank <= need, k0 + row_iota + 1, 0), 0)
            return rank[tk - 1:tk, :], jnp.maximum(cut, jnp.max(bound, axis=0, keepdims=True))

        _, cut = lax.fori_loop(0, n_chunks, body, (jnp.zeros((1, tq), F32), jnp.zeros((1, tq), I32)))
        cut_ref[0] = jnp.where(ist_ref[2:3, :] > 0, cut, ist_ref[3:4, :])


def _select_kernel(qit_ref, ki_ref, wit_ref, thr_ref, cut_ref, s_ref, fst_ref, ist_ref, *, tq, tk, n_keys, k_top):
    i = pl.program_id(1)
    q0 = i * tq
    n_chunks = (q0 + tq + tk - 1) // tk
    t_row = q0 + lax.broadcasted_iota(I32, (1, tq), 1)
    row_iota = lax.broadcasted_iota(I32, (tk, tq), 0)
    qit = qit_ref[0]
    w = wit_ref[0]

    def build(c, carry):
        mx, mn, n_hi = carry
        k0 = pl.multiple_of(c * tk, tk)
        s = _scores_t(ki_ref[0, pl.ds(k0, tk), :], qit, w)
        causal = (k0 + row_iota) <= t_row
        sm = jnp.where(causal, s, -jnp.inf)
        s_ref[pl.ds(k0, tk), :] = sm
        return (jnp.maximum(mx, _fold8(sm, jnp.max)),
                jnp.minimum(mn, _fold8(jnp.where(causal, s, jnp.inf), jnp.min)),
                n_hi + _fold8(jnp.where(sm > NEG_BIG, 1, 0), jnp.sum))

    mx, mn, n_hi = lax.fori_loop(
        0, n_chunks, build,
        (jnp.full((_FOLD_ROWS, tq), -jnp.inf, F32), jnp.full((_FOLD_ROWS, tq), jnp.inf, F32),
         jnp.zeros((_FOLD_ROWS, tq), I32)))
    n_nc = (n_keys - 1) - t_row
    _topk_threshold(s_ref, n_chunks, tk, tq, jnp.max(mx, axis=0, keepdims=True),
                    jnp.min(mn, axis=0, keepdims=True), jnp.sum(n_hi, axis=0, keepdims=True),
                    n_nc, k_top, thr_ref, cut_ref, fst_ref, ist_ref)


def _select(qit, ki, wit, *, tq, tk, k_top):
    b, _, _, t = qit.shape
    kern = functools.partial(_select_kernel, tq=tq, tk=tk, n_keys=t, k_top=k_top)
    return pl.pallas_call(
        kern,
        grid=(b, t // tq),
        in_specs=[
            pl.BlockSpec((1, IDX_HEADS, IDX_DIM, tq), lambda bb, i: (bb, 0, 0, i)),
            pl.BlockSpec((1, t, IDX_DIM), lambda bb, i: (bb, 0, 0)),
            pl.BlockSpec((1, 8, tq), lambda bb, i: (bb, 0, i)),
        ],
        out_specs=[pl.BlockSpec((1, 1, tq), lambda bb, i: (bb, 0, i)),
                   pl.BlockSpec((1, 1, tq), lambda bb, i: (bb, 0, i))],
        out_shape=[jax.ShapeDtypeStruct((b, 1, t), F32), jax.ShapeDtypeStruct((b, 1, t), I32)],
        scratch_shapes=[pltpu.VMEM((t, tq), F32), pltpu.VMEM((8, tq), F32), pltpu.VMEM((8, tq), I32)],
        compiler_params=_cparams("parallel", "parallel"),
        name="select",
    )(qit, ki, wit)


def _n_key_tiles(i, tq, tk):
    return (i * tq + tq + tk - 1) // tk


def _attn_kernel(qt_ref, k_ref, vt_ref, qit_ref, ki_ref, wit_ref, vf_ref, cut_ref, o_ref,
                 m_ref, l_ref, acc_ref, *, tq, tk):
    i = pl.program_id(1)
    j = pl.program_id(2)
    nkt = _n_key_tiles(i, tq, tk)

    @pl.when(j == 0)
    def _():
        m_ref[...] = jnp.full_like(m_ref, -jnp.inf)
        l_ref[...] = jnp.zeros_like(l_ref)
        acc_ref[...] = jnp.zeros_like(acc_ref)

    @pl.when(j < nkt)
    def _():
        s = _scores_t(ki_ref[0], qit_ref[0], wit_ref[0])
        kidx = j * tk + lax.broadcasted_iota(I32, (tk, tq), 0)
        t_row = i * tq + lax.broadcasted_iota(I32, (1, tq), 1)
        vf = vf_ref[0]
        bias = _select_bias(s, vf, kidx, cut_ref[0])
        bias = jnp.where(kidx <= t_row, bias, -jnp.inf)
        for h in range(N_HEADS):
            logit = jnp.dot(k_ref[0, h], qt_ref[0, h], preferred_element_type=F32) + bias
            m_old = m_ref[h:h + 1, :]
            m_new = jnp.maximum(m_old, jnp.max(logit, axis=0, keepdims=True))
            m_safe = jnp.where(m_new == -jnp.inf, 0.0, m_new)
            p = jnp.exp(logit - m_safe)
            alpha = jnp.exp(m_old - m_safe)
            l_ref[h:h + 1, :] = alpha * l_ref[h:h + 1, :] + jnp.sum(p, axis=0, keepdims=True)
            acc_ref[h] = alpha * acc_ref[h] + jnp.dot(vt_ref[0, h], p.astype(BF16),
                                                     preferred_element_type=F32)
            m_ref[h:h + 1, :] = m_new

    @pl.when(j == nkt - 1)
    def _():
        for h in range(N_HEADS):
            o_ref[0, h] = (acc_ref[h] / l_ref[h:h + 1, :]).astype(o_ref.dtype)


def _attn(qt, k, vt, qit, ki, wit, vf, cut, *, tq, tk):
    b, _, _, t = qt.shape
    kj = lambda i, j: jnp.minimum(j, _n_key_tiles(i, tq, tk) - 1)
    return pl.pallas_call(
        functools.partial(_attn_kernel, tq=tq, tk=tk),
        grid=(b, t // tq, t // tk),
        in_specs=[
            pl.BlockSpec((1, N_HEADS, HEAD_DIM, tq), lambda bb, i, j: (bb, 0, 0, i)),
            pl.BlockSpec((1, N_HEADS, tk, HEAD_DIM), lambda bb, i, j: (bb, 0, kj(i, j), 0)),
            pl.BlockSpec((1, N_HEADS, HEAD_DIM, tk), lambda bb, i, j: (bb, 0, 0, kj(i, j))),
            pl.BlockSpec((1, IDX_HEADS, IDX_DIM, tq), lambda bb, i, j: (bb, 0, 0, i)),
            pl.BlockSpec((1, tk, IDX_DIM), lambda bb, i, j: (bb, kj(i, j), 0)),
            pl.BlockSpec((1, 8, tq), lambda bb, i, j: (bb, 0, i)),
            pl.BlockSpec((1, 1, tq), lambda bb, i, j: (bb, 0, i)),
            pl.BlockSpec((1, 1, tq), lambda bb, i, j: (bb, 0, i)),
        ],
        out_specs=pl.BlockSpec((1, N_HEADS, HEAD_DIM, tq), lambda bb, i, j: (bb, 0, 0, i)),
        out_shape=jax.ShapeDtypeStruct((b, N_HEADS, HEAD_DIM, t), BF16),
        scratch_shapes=[pltpu.VMEM((N_HEADS, tq), F32), pltpu.VMEM((N_HEADS, tq), F32),
                        pltpu.VMEM((N_HEADS, HEAD_DIM, tq), F32)],
        compiler_params=_cparams("parallel", "parallel", "arbitrary"),
        name="attn",
    )(qt, k, vt, qit, ki, wit, vf, cut)


def _zoh(ar, ai, ldt):
    dt = jnp.exp(ldt)
    mag = jnp.exp(dt * ar)
    abr = mag * jnp.cos(dt * ai)
    abi = mag * jnp.sin(dt * ai)
    den = ar * ar + ai * ai
    nr = abr - 1.0
    ni = abi
    return abr, abi, (nr * ar + ni * ai) / den, (ni * ar - nr * ai) / den


def _disc_kernel(ar_ref, ai_ref, ldt_ref, ar16_ref, ai16_ref, ldt16_ref, br_ref, bi_ref,
                 pwr_ref, pwi_ref, bbr_ref, bbi_ref):
    abr, abi, _, _ = _zoh(ar_ref[...], ai_ref[...], ldt_ref[...])
    pr, pi = abr, abi
    for j in range(8):
        pwr_ref[j:j + 1, :] = pr
        pwi_ref[j:j + 1, :] = pi
        pr, pi = pr * abr - pi * abi, pr * abi + pi * abr
    _, _, fr, fi = _zoh(ar16_ref[...], ai16_ref[...], ldt16_ref[...])
    br = br_ref[...]
    bi = bi_ref[...]
    bbr_ref[...] = fr * br - fi * bi
    bbi_ref[...] = fr * bi + fi * br


def _discretize(a_re, a_im, log_dt, b_re, b_im):
    flat = lambda a: a.reshape(1, -1)
    ldt = jnp.broadcast_to(log_dt[:, None], (N_GROUPS, STATE_DIM))
    rep = lambda a: flat(jnp.broadcast_to(a[:, :, None], (N_GROUPS, STATE_DIM, GROUP_CH)))
    n16 = N_STATE * GROUP_CH
    pwr, pwi, bbr, bbi = pl.pallas_call(
        _disc_kernel,
        out_shape=[jax.ShapeDtypeStruct((8, N_STATE), F32), jax.ShapeDtypeStruct((8, N_STATE), F32),
                   jax.ShapeDtypeStruct((1, n16), F32), jax.ShapeDtypeStruct((1, n16), F32)],
        name="s5_discretize",
    )(flat(a_re), flat(a_im), flat(ldt), rep(a_re), rep(a_im), rep(ldt), flat(b_re), flat(b_im))
    return pwr, pwi, bbr.reshape(N_GROUPS, STATE_DIM, GROUP_CH), bbi.reshape(N_GROUPS, STATE_DIM, GROUP_CH)


def _block_diag_in(bb):
    eye = jnp.eye(N_GROUPS, dtype=bb.dtype)
    return jnp.einsum('gpc,gh->gchp', bb, eye).reshape(SSM_WIDTH, N_STATE)


def _block_diag_out(c):
    eye = jnp.eye(N_GROUPS, dtype=c.dtype)
    return jnp.einsum('gcp,gh->gphc', c, eye).reshape(N_STATE, SSM_WIDTH)


_LANE_CHUNK = 512


def _ssm_kernel(u_ref, bblk_ref, cblk_ref, pw_ref, d_ref, y_ref, xf_ref, x_ref, carry_ref, *, tt):
    ts = pl.program_id(1)

    @pl.when(ts == 0)
    def _():
        carry_ref[...] = jnp.zeros_like(carry_ref)

    u = u_ref[0]
    ub = u.astype(BF16)
    n_chunks = 2 * N_STATE // _LANE_CHUNK
    for c in range(n_chunks):
        cols = slice(c * _LANE_CHUNK, (c + 1) * _LANE_CHUNK)
        x_ref[:, cols] = jnp.dot(ub, bblk_ref[:, cols], preferred_element_type=F32)
    rid = lax.broadcasted_iota(I32, (8, _LANE_CHUNK), 0)

    def group(r, carry):
        r0 = pl.multiple_of(r * 8, 8)
        for c in range(N_STATE // _LANE_CHUNK):
            re = pl.ds(c * _LANE_CHUNK, _LANE_CHUNK)
            im = pl.ds(N_STATE + c * _LANE_CHUNK, _LANE_CHUNK)
            xr = x_ref[pl.ds(r0, 8), re]
            xi = x_ref[pl.ds(r0, 8), im]
            for d in (1, 2, 4):
                ar = pw_ref[d - 1:d, re]
                ai = pw_ref[d - 1:d, im]
                sr = jnp.where(rid >= d, pltpu.roll(xr, d, 0), 0.0)
                si = jnp.where(rid >= d, pltpu.roll(xi, d, 0), 0.0)
                xr, xi = xr + (ar * sr - ai * si), xi + (ar * si + ai * sr)
            cr = carry_ref[:, re]
            ci = carry_ref[:, im]
            pr = pw_ref[:, re]
            pi = pw_ref[:, im]
            xr, xi = xr + (pr * cr - pi * ci), xi + (pr * ci + pi * cr)
            x_ref[pl.ds(r0, 8), re] = xr
            x_ref[pl.ds(r0, 8), im] = xi
            carry_ref[:, re] = xr[7:8, :]
            carry_ref[:, im] = xi[7:8, :]
        return carry

    lax.fori_loop(0, tt // 8, group, 0)
    y = d_ref[...] * u
    for c in range(n_chunks):
        cols = slice(c * _LANE_CHUNK, (c + 1) * _LANE_CHUNK)
        y = y + jnp.dot(x_ref[:, cols].astype(BF16), cblk_ref[cols, :], preferred_element_type=F32)
    y_ref[0] = y
    xf_ref[0] = carry_ref[...]


def _ssm(u, bblk, cblk, pw, d_skip, *, tt):
    b, t, _ = u.shape
    return pl.pallas_call(
        functools.partial(_ssm_kernel, tt=tt),
        grid=(b, t // tt),
        in_specs=[
            pl.BlockSpec((1, tt, SSM_WIDTH), lambda bb, s: (bb, s, 0)),
            pl.BlockSpec((SSM_WIDTH, 2 * N_STATE), lambda bb, s: (0, 0)),
            pl.BlockSpec((2 * N_STATE, SSM_WIDTH), lambda bb, s: (0, 0)),
            pl.BlockSpec((8, 2 * N_STATE), lambda bb, s: (0, 0)),
            pl.BlockSpec((1, SSM_WIDTH), lambda bb, s: (0, 0)),
        ],
        out_specs=[pl.BlockSpec((1, tt, SSM_WIDTH), lambda bb, s: (bb, s, 0)),
                   pl.BlockSpec((1, 1, 2 * N_STATE), lambda bb, s: (bb, 0, 0))],
        out_shape=[jax.ShapeDtypeStruct((b, t, SSM_WIDTH), F32),
                   jax.ShapeDtypeStruct((b, 1, 2 * N_STATE), F32)],
        scratch_shapes=[pltpu.VMEM((tt, 2 * N_STATE), F32), pltpu.VMEM((1, 2 * N_STATE), F32)],
        compiler_params=_cparams("parallel", "arbitrary"),
        name="s5_scan",
    )(u, bblk, cblk, pw, d_skip)


def _ssm_step_kernel(u_ref, x0_ref, bblk_ref, cblk_ref, pw_ref, d_ref, y_ref, x_ref):
    u = u_ref[...]
    bu = jnp.dot(u, bblk_ref[...], preferred_element_type=F32, precision=lax.Precision.HIGHEST)
    ar = pw_ref[0:1, :N_STATE]
    ai = pw_ref[0:1, N_STATE:]
    x0r = x0_ref[:, :N_STATE]
    x0i = x0_ref[:, N_STATE:]
    xr = ar * x0r - ai * x0i + bu[:, :N_STATE]
    xi = ar * x0i + ai * x0r + bu[:, N_STATE:]
    x_ref[:, :N_STATE] = xr
    x_ref[:, N_STATE:] = xi
    y = jnp.dot(x_ref[...], cblk_ref[...], preferred_element_type=F32, precision=lax.Precision.HIGHEST)
    y_ref[...] = y + d_ref[...] * u


def _ssm_step(u, x0, bblk, cblk, pw, d_skip):
    n = u.shape[0]
    return pl.pallas_call(
        _ssm_step_kernel,
        out_shape=[jax.ShapeDtypeStruct((n, SSM_WIDTH), F32), jax.ShapeDtypeStruct((n, 2 * N_STATE), F32)],
        compiler_params=pltpu.CompilerParams(vmem_limit_bytes=VMEM_LIMIT_BYTES),
        name="s5_step",
    )(u, x0, bblk, cblk, pw, d_skip)


def _gelu_tanh(x):
    c = np.float32(np.sqrt(2.0 / np.pi))
    return 0.5 * x * (1.0 + jnp.tanh(c * (x + 0.044715 * (x * x * x))))


def _merge_kernel(x_ref, oa_ref, ys_ref, g_ref, wgt_ref, wglu_ref, bglu_ref, wpa_ref, wpb_ref, wout_ref, o_ref):
    x = x_ref[...]
    h = _rmsnorm(x, g_ref[...]).astype(BF16)
    gates = _sigmoid(jnp.dot(h, wgt_ref[...], preferred_element_type=F32))
    ys = _gelu_tanh(ys_ref[...])
    glu = jnp.dot(ys.astype(BF16), wglu_ref[...], preferred_element_type=F32) + bglu_ref[...]
    ob = ys * _sigmoid(glu)
    pa = jnp.dot(oa_ref[...], wpa_ref[...], preferred_element_type=F32)
    pb = jnp.dot(ob.astype(BF16), wpb_ref[...], preferred_element_type=F32)
    merged = gates[:, :D_MODEL] * pa + gates[:, D_MODEL:] * pb
    o_ref[...] = x + jnp.dot(merged.astype(BF16), wout_ref[...], preferred_element_type=F32)


def _merge(x, oa, ys, g, wgt, wglu, bglu, wpa, wpb, wout, *, tm):
    m, d = x.shape
    row = lambda n: pl.BlockSpec((tm, n), lambda i: (i, 0))
    full = lambda a: pl.BlockSpec(a.shape, lambda i: (0, 0))
    return pl.pallas_call(
        _merge_kernel,
        grid=(m // tm,),
        in_specs=[row(d), row(D_ATTN), row(SSM_WIDTH), full(g), full(wgt), full(wglu), full(bglu),
                  full(wpa), full(wpb), full(wout)],
        out_specs=row(d),
        out_shape=jax.ShapeDtypeStruct((m, d), F32),
        compiler_params=_cparams("parallel"),
        name="merge",
    )(x, oa, ys, g, wgt, wglu, bglu, wpa, wpb, wout)


_PAGES_PER_STEP = 8


def _page_specs(block, n_pages):
    def spec(r):
        def index_map(b, p, pt):
            return (pt[b, jnp.minimum(p * _PAGES_PER_STEP + r, n_pages - 1)],) + (0,) * (len(block) - 1)
        return pl.BlockSpec(block, index_map)
    return [spec(r) for r in range(_PAGES_PER_STEP)]


def _per_seq(shape):
    return pl.BlockSpec((1,) + shape, lambda b, p, pt: (b,) + (0,) * len(shape))


def _sample_scores_kernel(pt_ref, qi_ref, w_ref, knew_ref, *refs, page_size, n_steps):
    pages = refs[:_PAGES_PER_STEP]
    o_ref = refs[_PAGES_PER_STEP]
    p = pl.program_id(1)
    qi = qi_ref[0]
    w = w_ref[0]

    def score(kt):
        d = jnp.dot(qi, kt, preferred_element_type=F32)
        return jnp.sum(jnp.maximum(d, 0.0) * w, axis=0, keepdims=True)

    @pl.when(p < n_steps)
    def _():
        for r in range(_PAGES_PER_STEP):
            o_ref[0, :, r * page_size:(r + 1) * page_size] = score(pages[r][0].astype(BF16))

    @pl.when(p == n_steps)
    def _():
        o_ref[0] = jnp.full(o_ref.shape[1:], -jnp.inf, F32)
        lane = lax.broadcasted_iota(I32, (1, page_size), 1)
        o_ref[0, :, 0:page_size] = jnp.where(lane == 0, score(knew_ref[0]), -jnp.inf)


def _sample_scores(page_table, qi_rows, w_col, ki_new_t, kidx_t):
    db, n_pages = page_table.shape
    _, _, page_size = kidx_t.shape
    assert n_pages % _PAGES_PER_STEP == 0
    steps = n_pages // _PAGES_PER_STEP
    block = _PAGES_PER_STEP * page_size
    return pl.pallas_call(
        functools.partial(_sample_scores_kernel, page_size=page_size, n_steps=steps),
        grid_spec=pltpu.PrefetchScalarGridSpec(
            num_scalar_prefetch=1,
            grid=(db, steps + 1),
            in_specs=[_per_seq((8, IDX_DIM)), _per_seq((8, 1)), _per_seq((IDX_DIM, page_size))]
                     + _page_specs((1, IDX_DIM, page_size), n_pages),
            out_specs=pl.BlockSpec((1, 1, block), lambda b, p, pt: (b, 0, p)),
        ),
        out_shape=jax.ShapeDtypeStruct((db, 1, (steps + 1) * block), F32),
        compiler_params=_cparams("parallel", "arbitrary"),
        name="sample_scores",
    )(page_table, qi_rows, w_col, ki_new_t, *([kidx_t] * _PAGES_PER_STEP))


def _sample_select_kernel(st_ref, thr_ref, cut_ref, s_ref, fst_ref, ist_ref, *, tk, tq, n_keys, k_top):
    n_chunks = st_ref.shape[0] // tk
    row_iota = lax.broadcasted_iota(I32, (tk, tq), 0)

    def build(c, carry):
        mx, mn, n_hi = carry
        k0 = pl.multiple_of(c * tk, tk)
        s = st_ref[pl.ds(k0, tk), :]
        real = (k0 + row_iota) < n_keys
        sm = jnp.where(real, s, -jnp.inf)
        s_ref[pl.ds(k0, tk), :] = sm
        return (jnp.maximum(mx, _fold8(sm, jnp.max)),
                jnp.minimum(mn, _fold8(jnp.where(real, s, jnp.inf), jnp.min)),
                n_hi + _fold8(jnp.where(sm > NEG_BIG, 1, 0), jnp.sum))

    mx, mn, n_hi = lax.fori_loop(
        0, n_chunks, build,
        (jnp.full((_FOLD_ROWS, tq), -jnp.inf, F32), jnp.full((_FOLD_ROWS, tq), jnp.inf, F32),
         jnp.zeros((_FOLD_ROWS, tq), I32)))
    _topk_threshold(s_ref, n_chunks, tk, tq, jnp.max(mx, axis=0, keepdims=True),
                    jnp.min(mn, axis=0, keepdims=True), jnp.sum(n_hi, axis=0, keepdims=True),
                    jnp.zeros((1, tq), I32), k_top, thr_ref, cut_ref, fst_ref, ist_ref)


def _sample_select(scores_t, *, n_keys, k_top, tk):
    n_rows, tq = scores_t.shape
    kern = functools.partial(_sample_select_kernel, tk=tk, tq=tq, n_keys=n_keys, k_top=k_top)
    return pl.pallas_call(
        kern,
        out_shape=[jax.ShapeDtypeStruct((1, 1, tq), F32), jax.ShapeDtypeStruct((1, 1, tq), I32)],
        scratch_shapes=[pltpu.VMEM((n_rows, tq), F32), pltpu.VMEM((8, tq), F32), pltpu.VMEM((8, tq), I32)],
        compiler_params=pltpu.CompilerParams(vmem_limit_bytes=VMEM_LIMIT_BYTES),
        name="sample_select",
    )(scores_t)


def _sample_attn_kernel(pt_ref, qb_ref, s_ref, thr_ref, cut_ref, kself_ref, vself_ref, *refs,
                        page_size, n_steps):
    kp = refs[:_PAGES_PER_STEP]
    vp = refs[_PAGES_PER_STEP:2 * _PAGES_PER_STEP]
    o_ref, m_ref, l_ref, acc_ref = refs[2 * _PAGES_PER_STEP:]
    step = pl.program_id(1)
    thr = thr_ref[0]
    cut = cut_ref[0]
    lane = lax.broadcasted_iota(I32, (1, page_size), 1)
    is_self = step == n_steps

    @pl.when(step == 0)
    def _():
        m_ref[...] = jnp.full_like(m_ref, -jnp.inf)
        l_ref[...] = jnp.zeros_like(l_ref)
        acc_ref[...] = jnp.zeros_like(acc_ref)

    logits = []
    for r in range(_PAGES_PER_STEP):
        kidx = (step * _PAGES_PER_STEP + r) * page_size + lane
        bias = _select_bias(s_ref[0, :, r * page_size:(r + 1) * page_size], thr, kidx, cut)
        rows = []
        for h in range(N_HEADS):
            kt = kp[r][0, h]
            if r == 0:
                kt = jnp.where(is_self, kself_ref[0, h], kt)
            rows.append(jnp.sum(kt * qb_ref[0, h], axis=0, keepdims=True))
        logits.append(jnp.concatenate(rows, axis=0) + bias)

    m_old = m_ref[...]
    m_new = m_old
    for lg in logits:
        m_new = jnp.maximum(m_new, jnp.max(lg, axis=1, keepdims=True))
    m_safe = jnp.where(m_new == -jnp.inf, 0.0, m_new)
    alpha = jnp.exp(m_old - m_safe)
    probs = [jnp.exp(lg - m_safe) for lg in logits]
    l_new = alpha * l_ref[...]
    for p in probs:
        l_new = l_new + jnp.sum(p, axis=1, keepdims=True)
    l_ref[...] = l_new
    m_ref[...] = m_new
    for h in range(N_HEADS):
        acc = acc_ref[h] * alpha[h:h + 1, :]
        for r in range(_PAGES_PER_STEP):
            vt = vp[r][0, h]
            if r == 0:
                vt = jnp.where(is_self, vself_ref[0, h], vt)
            acc = acc + vt * probs[r][h:h + 1, :]
        acc_ref[h] = acc

    @pl.when(is_self)
    def _():
        for h in range(N_HEADS):
            o_ref[0, h] = jnp.sum(acc_ref[h], axis=1, keepdims=True) / l_ref[h:h + 1, :]


def _sample_attn(page_table, qb, scores, thr, cut, k_self, v_self, ck_t, cv_t):
    db, n_pages = page_table.shape
    _, nh, hd, page_size = ck_t.shape
    steps = n_pages // _PAGES_PER_STEP
    block = _PAGES_PER_STEP * page_size
    page = (1, nh, hd, page_size)
    return pl.pallas_call(
        functools.partial(_sample_attn_kernel, page_size=page_size, n_steps=steps),
        grid_spec=pltpu.PrefetchScalarGridSpec(
            num_scalar_prefetch=1,
            grid=(db, steps + 1),
            in_specs=[_per_seq(page[1:]),
                      pl.BlockSpec((1, 1, block), lambda b, p, pt: (b, 0, p)),
                      _per_seq((1, 1)), _per_seq((1, 1)), _per_seq(page[1:]), _per_seq(page[1:])]
                     + _page_specs(page, n_pages) + _page_specs(page, n_pages),
            out_specs=_per_seq((nh, hd, 1)),
            scratch_shapes=[pltpu.VMEM((nh, 1), F32), pltpu.VMEM((nh, 1), F32),
                            pltpu.VMEM((nh, hd, page_size), F32)],
        ),
        out_shape=jax.ShapeDtypeStruct((db, nh, hd, 1), F32),
        compiler_params=_cparams("parallel", "arbitrary"),
        name="sample_attn",
    )(page_table, qb, scores, thr, cut, k_self, v_self,
      *([ck_t] * _PAGES_PER_STEP), *([cv_t] * _PAGES_PER_STEP))


def _pick(n, pref):
    for t in pref:
        if n % t == 0:
            return t
    return n


def _pack_w_in(w_in):
    o = np.cumsum([0, D_ATTN, D_ATTN, D_ATTN, IDX_HEADS * IDX_DIM, IDX_DIM, IDX_HEADS, SSM_WIDTH, 2 * D_MODEL])
    z = lambda n: jnp.zeros((D_MODEL, n), w_in.dtype)
    w_all = jnp.concatenate([
        w_in[:, o[0]:o[4]],
        w_in[:, o[4]:o[5]], z(_PWI - _PKI - IDX_DIM),
        w_in[:, o[5]:o[6]], z(_PU - _PWI - IDX_HEADS),
        w_in[:, o[6]:o[7]],
    ], axis=1)
    return w_all.astype(BF16), w_in[:, o[7]:o[8]].astype(BF16)


def _prompt_layer(x, lw, ssm, final_norm):
    bsz, seq, d = x.shape
    m = bsz * seq
    tm = _pick(m, (512, 256, 128, 64, 32, 16, 8))
    x1 = _ffn(x.reshape(m, d), *lw["ffn1"], lw["g_final"], final_norm=False, tm=tm, tf=lw["tf"])
    q_b, k_f, v_f, k_b, v_b, qi_b, ki_f, ki_b, wi_f, u_f = _proj(x1, lw["g_mix"], lw["w_all"], tm=tm)

    heads_t = lambda a, nh: a.reshape(bsz, seq, nh, -1).transpose(0, 2, 3, 1)
    qt = heads_t(q_b, N_HEADS)
    vt = heads_t(v_b, N_HEADS)
    kh = k_b.reshape(bsz, seq, N_HEADS, HEAD_DIM).transpose(0, 2, 1, 3)
    qit = heads_t(qi_b, IDX_HEADS)
    kib = ki_b.reshape(bsz, seq, IDX_DIM)
    wit = wi_f[:, :8].reshape(bsz, seq, 8).transpose(0, 2, 1)

    k_top = min(TOPK_MAX, seq // 4)
    vf, cut = _select(qit, kib, wit, tq=_pick(seq, (256, 128)), tk=_pick(seq, (512, 256, 128)), k_top=k_top)
    o_t = _attn(qt, kh, vt, qit, kib, wit, vf, cut,
                tq=_pick(seq, (256, 128)), tk=_pick(seq, (512, 256, 128)))
    oa = o_t.transpose(0, 3, 1, 2).reshape(m, D_ATTN)

    tt = _pick(seq, (256, 128, 64, 32, 16, 8))
    ys, xfin = _ssm(u_f.reshape(bsz, seq, SSM_WIDTH), ssm["bblk"].astype(BF16), ssm["cblk"].astype(BF16),
                    ssm["pw"], ssm["d_skip"], tt=tt)
    x2 = _merge(x1, oa, ys.reshape(m, SSM_WIDTH), *lw["merge"], tm=tm)
    y = _ffn(x2, *lw["ffn2"], lw["g_final"], final_norm=final_norm, tm=tm, tf=lw["tf"])
    new = (k_f.reshape(bsz, seq, N_HEADS, HEAD_DIM), v_f.reshape(bsz, seq, N_HEADS, HEAD_DIM),
           ki_f.reshape(bsz, seq, IDX_DIM),
           xfin[:, 0, :N_STATE].reshape(bsz, N_GROUPS, STATE_DIM),
           xfin[:, 0, N_STATE:].reshape(bsz, N_GROUPS, STATE_DIM))
    return y.reshape(bsz, seq, d), new


def _sample_layer(x, lw, ssm, final_norm, cache_k, cache_v, cache_kidx, st_re, st_im, page_table):
    db, ds, d = x.shape
    assert ds == 1, "one new token per sample sequence"
    n_pool, page_size = cache_k.shape[0], cache_k.shape[1]
    past_len = page_table.shape[1] * page_size
    s1 = _ffn(x.reshape(db, d), *lw["ffn1"], lw["g_final"], final_norm=False, tm=db, tf=lw["tf"])
    q_b, k_f, v_f, _, _, qi_b, ki_f, ki_b, wi_f, u_f = _proj(s1, lw["g_mix"], lw["w_all"], tm=db)

    kidx_t = cache_kidx.transpose(0, 2, 1)
    ck_t = cache_k.transpose(0, 2, 3, 1)
    cv_t = cache_v.transpose(0, 2, 3, 1)
    lane0 = lambda a: jnp.zeros(a.shape + (page_size,), a.dtype).at[..., 0].set(a)

    qi_rows = jnp.zeros((db, 8, IDX_DIM), BF16).at[:, :IDX_HEADS].set(qi_b.reshape(db, IDX_HEADS, IDX_DIM))
    scores = _sample_scores(page_table, qi_rows, wi_f[:, :8, None], lane0(ki_b), kidx_t)

    n_keys = past_len + 1
    k_top = min(TOPK_MAX, n_keys // 4)
    n = scores.shape[2]
    scores_t = jnp.zeros((n, -(-db // 128) * 128), F32).at[:, :db].set(scores[:, 0, :].T)
    thr, cut = _sample_select(scores_t, n_keys=n_keys, k_top=k_top, tk=256)

    qb = jnp.broadcast_to(q_b.astype(F32).reshape(db, N_HEADS, HEAD_DIM, 1), (db, N_HEADS, HEAD_DIM, page_size))
    o_s = _sample_attn(page_table, qb, scores, thr[0, 0, :db].reshape(db, 1, 1), cut[0, 0, :db].reshape(db, 1, 1),
                       lane0(k_f.reshape(db, N_HEADS, HEAD_DIM)), lane0(v_f.reshape(db, N_HEADS, HEAD_DIM)),
                       ck_t, cv_t)
    oa = o_s.reshape(db, D_ATTN).astype(BF16)

    x0 = jnp.concatenate([st_re.reshape(db, N_STATE), st_im.reshape(db, N_STATE)], axis=1)
    ys, x_new = _ssm_step(u_f, x0, ssm["bblk"], ssm["cblk"], ssm["pw"], ssm["d_skip"])
    s2 = _merge(s1, oa, ys, *lw["merge"], tm=db)
    y = _ffn(s2, *lw["ffn2"], lw["g_final"], final_norm=final_norm, tm=db, tf=lw["tf"])
    new = (k_f.reshape(db, ds, N_HEADS, HEAD_DIM), v_f.reshape(db, ds, N_HEADS, HEAD_DIM),
           ki_f.reshape(db, ds, IDX_DIM),
           x_new[:, :N_STATE].reshape(db, N_GROUPS, STATE_DIM),
           x_new[:, N_STATE:].reshape(db, N_GROUPS, STATE_DIM))
    return y.reshape(db, ds, d), new


def kernel(x_prompt, x_sample, cache_k, cache_v, cache_kidx, state_ssm_re, state_ssm_im, page_table,
           g_ffn1, w1_gate, w1_up, w1_down, g_mix, w_in, a_re, a_im, log_dt, b_re, b_im, c_re, c_im,
           d_skip, w_glu, b_glu, w_pa, w_pb, w_out, g_ffn2, w2_gate, w2_up, w2_down, g_final):
    depth = w_in.shape[0]
    row = lambda a: a.reshape(1, -1)
    bf = lambda a: a.astype(BF16)
    xp, xs = x_prompt, x_sample
    new_p, new_s = [], []
    for l in range(depth):
        w_all, w_gates = _pack_w_in(w_in[l])
        lw = dict(
            ffn1=(row(g_ffn1[l]), bf(w1_gate[l]), bf(w1_up[l]), bf(w1_down[l])),
            ffn2=(row(g_ffn2[l]), bf(w2_gate[l]), bf(w2_up[l]), bf(w2_down[l])),
            merge=(row(g_mix[l]), w_gates, bf(w_glu[l]), row(b_glu[l]), bf(w_pa[l]), bf(w_pb[l]), bf(w_out[l])),
            g_mix=row(g_mix[l]), w_all=w_all, g_final=row(g_final),
            tf=_pick(w1_gate.shape[2], (1408, 1024, 512, 256, 128)),
        )
        pwr, pwi, bbr, bbi = _discretize(a_re[l], a_im[l], log_dt[l], b_re[l], b_im[l])
        ssm = dict(
            pw=jnp.concatenate([pwr, pwi], axis=1),
            bblk=jnp.concatenate([_block_diag_in(bbr), _block_diag_in(bbi)], axis=1),
            cblk=jnp.concatenate([_block_diag_out(c_re[l]), -_block_diag_out(c_im[l])], axis=0),
            d_skip=row(d_skip[l]),
        )
        last = l == depth - 1
        xp, st_p = _prompt_layer(xp, lw, ssm, last)
        xs, st_s = _sample_layer(xs, lw, ssm, last, cache_k[l], cache_v[l], cache_kidx[l],
                                 state_ssm_re[l], state_ssm_im[l], page_table)
        new_p.append(st_p)
        new_s.append(st_s)
    stack = lambda states, i: jnp.stack([s[i] for s in states])
    return (xp, xs) + tuple(stack(new_p, i) for i in range(5)) + tuple(stack(new_s, i) for i in range(5))
```

```python
import functools

import numpy as np
import jax
import jax.numpy as jnp
from jax import lax
from jax.experimental import pallas as pl
from jax.experimental.pallas import tpu as pltpu

F32 = jnp.float32
BF16 = jnp.bfloat16
I32 = jnp.int32

D_MODEL = 1024
N_HEADS = 8
HEAD_DIM = 64
D_ATTN = N_HEADS * HEAD_DIM
IDX_HEADS = 4
IDX_DIM = 64
TOPK_MAX = 256
SSM_WIDTH = 512
GROUP_CH = 16
N_GROUPS = SSM_WIDTH // GROUP_CH
STATE_DIM = 64
N_STATE = N_GROUPS * STATE_DIM
NORM_EPS = 1e-6
NEG_BIG = -1e30

INT_MAX = 2 ** 31 - 1

VMEM_LIMIT_BYTES = 56 * 1024 * 1024


def _cparams(*sem):
    return pltpu.CompilerParams(dimension_semantics=sem, vmem_limit_bytes=VMEM_LIMIT_BYTES)


def _rmsnorm(x, g):
    y = x * lax.rsqrt(jnp.mean(x * x, axis=-1, keepdims=True) + NORM_EPS)
    return y * g


def _sigmoid(x):
    return 1.0 / (1.0 + jnp.exp(-x))


def _ffn_kernel(x_ref, g_ref, wg_ref, wu_ref, wd_ref, gf_ref, o_ref, xn_ref, acc_ref, *, final_norm):
    f = pl.program_id(1)

    @pl.when(f == 0)
    def _():
        xn_ref[...] = _rmsnorm(x_ref[...], g_ref[...]).astype(BF16)
        acc_ref[...] = jnp.zeros_like(acc_ref)

    xn = xn_ref[...]
    gate = jnp.dot(xn, wg_ref[...], preferred_element_type=F32)
    up = jnp.dot(xn, wu_ref[...], preferred_element_type=F32)
    act = (gate * _sigmoid(gate)) * up
    acc_ref[...] += jnp.dot(act.astype(BF16), wd_ref[...], preferred_element_type=F32)

    @pl.when(f == pl.num_programs(1) - 1)
    def _():
        y = x_ref[...] + 0.5 * acc_ref[...]
        if final_norm:
            y = _rmsnorm(y, gf_ref[...])
        o_ref[...] = y


def _ffn(x, g, wg, wu, wd, g_final, *, final_norm, tm, tf):
    m, d = x.shape
    dff = wg.shape[1]
    return pl.pallas_call(
        functools.partial(_ffn_kernel, final_norm=final_norm),
        grid=(m // tm, dff // tf),
        in_specs=[
            pl.BlockSpec((tm, d), lambda i, f: (i, 0)),
            pl.BlockSpec((1, d), lambda i, f: (0, 0)),
            pl.BlockSpec((d, tf), lambda i, f: (0, f)),
            pl.BlockSpec((d, tf), lambda i, f: (0, f)),
            pl.BlockSpec((tf, d), lambda i, f: (f, 0)),
            pl.BlockSpec((1, d), lambda i, f: (0, 0)),
        ],
        out_specs=pl.BlockSpec((tm, d), lambda i, f: (i, 0)),
        out_shape=jax.ShapeDtypeStruct((m, d), F32),
        scratch_shapes=[pltpu.VMEM((tm, d), BF16), pltpu.VMEM((tm, d), F32)],
        compiler_params=_cparams("parallel", "arbitrary"),
        name="ffn",
    )(x, g, wg, wu, wd, g_final)


_PQ, _PK, _PV, _PQI, _PKI, _PWI, _PU, _PEND = 0, 512, 1024, 1536, 1792, 1920, 2048, 2560


def _proj_kernel(x_ref, g_ref, w_ref, q_ref, kf_ref, vf_ref, kb_ref, vb_ref,
                 qi_ref, kif_ref, kib_ref, wi_ref, u_ref):
    h = _rmsnorm(x_ref[...], g_ref[...]).astype(BF16)
    p = jnp.dot(h, w_ref[...], preferred_element_type=F32)
    q_ref[...] = (p[:, _PQ:_PK] * (HEAD_DIM ** -0.5)).astype(BF16)
    k = p[:, _PK:_PV]
    v = p[:, _PV:_PQI]
    kf_ref[...] = k
    vf_ref[...] = v
    kb_ref[...] = k.astype(BF16)
    vb_ref[...] = v.astype(BF16)
    qi_ref[...] = p[:, _PQI:_PKI].astype(BF16)
    ki = p[:, _PKI:_PKI + IDX_DIM]
    kif_ref[...] = ki
    kib_ref[...] = ki.astype(BF16)
    wi_ref[...] = p[:, _PWI:_PU] * ((IDX_HEADS * IDX_DIM) ** -0.5)
    u_ref[...] = p[:, _PU:_PEND]


def _proj(x, g, w_all, *, tm):
    m, d = x.shape
    row = lambda n: pl.BlockSpec((tm, n), lambda i: (i, 0))
    outs = [
        (D_ATTN, BF16), (D_ATTN, F32), (D_ATTN, F32), (D_ATTN, BF16), (D_ATTN, BF16),
        (IDX_HEADS * IDX_DIM, BF16), (IDX_DIM, F32), (IDX_DIM, BF16), (128, F32), (SSM_WIDTH, F32),
    ]
    return pl.pallas_call(
        _proj_kernel,
        grid=(m // tm,),
        in_specs=[row(d), pl.BlockSpec((1, d), lambda i: (0, 0)),
                  pl.BlockSpec((d, _PEND), lambda i: (0, 0))],
        out_specs=[row(n) for n, _ in outs],
        out_shape=[jax.ShapeDtypeStruct((m, n), dt) for n, dt in outs],
        compiler_params=_cparams("parallel"),
        name="proj",
    )(x, g, w_all)


def _scores_t(kc, qit, w):
    s = None
    for h in range(IDX_HEADS):
        d = jnp.dot(kc, qit[h], preferred_element_type=F32)
        term = jnp.maximum(d, 0.0) * w[h:h + 1, :]
        s = term if s is None else s + term
    return s


def _select_bias(s, vf, kidx, cut):
    tie = jnp.where(kidx < cut, 0.0, -jnp.inf)
    return jnp.where(s > vf, 0.0, jnp.where(s == vf, tie, -jnp.inf))


_FOLD_ROWS = 32


def _fold8(x, op):
    return op(x.reshape(x.shape[0] // _FOLD_ROWS, _FOLD_ROWS, x.shape[1]), axis=0)


_PROBES_PER_ROUND = 4
_SETTLE_FIRST, _SETTLE_EVERY, _BISECT_CAP = 4, 2, 80


def _topk_threshold(s_ref, n_chunks, tk, tq, mx, mn, n_hi, n_nc, k_top, thr_ref, cut_ref, fst_ref, ist_ref):
    row_iota = lax.broadcasted_iota(I32, (tk, tq), 0)
    zeros_i = jnp.zeros((_FOLD_ROWS, tq), I32)

    def chunk(c):
        k0 = pl.multiple_of(c * tk, tk)
        return s_ref[pl.ds(k0, tk), :], k0

    def count_gt(v):
        def body(c, acc):
            blk, _ = chunk(c)
            return acc + _fold8(jnp.where(blk > v, 1, 0), jnp.sum)
        real = jnp.sum(lax.fori_loop(0, n_chunks, body, zeros_i), axis=0, keepdims=True)
        return real + jnp.where(NEG_BIG > v, n_nc, 0)

    def count_gt_eq(v):
        def body(c, acc):
            blk, _ = chunk(c)
            return (acc[0] + _fold8(jnp.where(blk > v, 1, 0), jnp.sum),
                    acc[1] + _fold8(jnp.where(blk == v, 1, 0), jnp.sum))
        g, e = lax.fori_loop(0, n_chunks, body, (zeros_i, zeros_i))
        g = jnp.sum(g, axis=0, keepdims=True) + jnp.where(NEG_BIG > v, n_nc, 0)
        return g, jnp.sum(e, axis=0, keepdims=True)

    def max_le(v):
        def body(c, acc):
            blk, _ = chunk(c)
            return jnp.maximum(acc, _fold8(jnp.where(blk <= v, blk, -jnp.inf), jnp.max))
        m = jnp.max(lax.fori_loop(0, n_chunks, body, jnp.full((_FOLD_ROWS, tq), -jnp.inf, F32)),
                    axis=0, keepdims=True)
        return jnp.where((n_nc > 0) & (NEG_BIG <= v), jnp.maximum(m, NEG_BIG), m)

    def settle(m):
        gt, eq_real = count_gt_eq(m)
        eq = eq_real + jnp.where(m == NEG_BIG, n_nc, 0)
        found = (ist_ref[0:1, :] == 0) & (gt + eq >= k_top)
        need = k_top - gt
        fst_ref[2:3, :] = jnp.where(found, m, fst_ref[2:3, :])
        ist_ref[1:2, :] = jnp.where(found, need, ist_ref[1:2, :])
        ist_ref[2:3, :] = jnp.where(found, jnp.where(eq_real > need, 1, 0), ist_ref[2:3, :])
        ist_ref[3:4, :] = jnp.where(found, INT_MAX, ist_ref[3:4, :])
        ist_ref[0:1, :] = jnp.where(found, 1, ist_ref[0:1, :])

    low = n_hi < k_top
    hi0 = jnp.where(low, NEG_BIG, mx)
    fst_ref[0:1, :] = jnp.where(low, jnp.minimum(mn, NEG_BIG), jnp.maximum(mn, NEG_BIG))
    fst_ref[1:2, :] = hi0
    fst_ref[2:3, :] = hi0
    ist_ref[...] = jnp.zeros_like(ist_ref)
    settle(hi0)

    def probe(lo, hi, thr, done):
        mid = 0.5 * lo + 0.5 * hi
        c = count_gt(mid)
        live = done == 0
        hit = live & (c == k_top)
        stuck = jnp.where(live & ((mid <= lo) | (mid >= hi)), 2, 0)
        return (jnp.where(live & (c > k_top), mid, lo), jnp.where(live & (c < k_top), mid, hi),
                jnp.where(hit, mid, thr), jnp.where(hit, 1, done), stuck)

    def step(carry):
        it, _ = carry
        lo, hi, thr, done = fst_ref[0:1, :], fst_ref[1:2, :], fst_ref[2:3, :], ist_ref[0:1, :]
        stuck = jnp.zeros_like(done)
        for _ in range(_PROBES_PER_ROUND):
            lo, hi, thr, done, s = probe(lo, hi, thr, done)
            stuck = jnp.maximum(stuck, s)
        fst_ref[0:1, :] = lo
        fst_ref[1:2, :] = hi
        fst_ref[2:3, :] = thr
        ist_ref[0:1, :] = done
        code = jnp.max(jnp.maximum(stuck, 1 - done))
        due = (it >= _SETTLE_FIRST) & ((it - _SETTLE_FIRST) % _SETTLE_EVERY == 0)

        check = (code > 0) & (due | (code >= 2))

        @pl.when(check)
        def _():
            settle(max_le(hi))

        return it + 1, lax.cond(check, lambda: jnp.max(1 - ist_ref[0:1, :]), lambda: code)

    lax.while_loop(lambda carry: (carry[1] > 0) & (carry[0] < _BISECT_CAP), step,
                   (jnp.int32(0), jnp.max(1 - ist_ref[0:1, :])))
    thr_ref[0] = fst_ref[2:3, :]
    cut_ref[0] = ist_ref[3:4, :]

    @pl.when(jnp.max(ist_ref[2:3, :]) > 0)
    def _():
        thr = fst_ref[2:3, :]
        need = ist_ref[1:2, :].astype(F32)
        tri = jnp.where(lax.broadcasted_iota(I32, (tk, tk), 1) <= lax.broadcasted_iota(I32, (tk, tk), 0),
                        1.0, 0.0).astype(BF16)

        def body(c, carry):
            seen, cut = carry
            blk, k0 = chunk(c)
            tie = blk == thr
            rank = jnp.dot(tri, jnp.where(tie, 1.0, 0.0).astype(BF16), preferred_element_type=F32) + seen
            bound = jnp.where(tie, jnp.where(rank <= need, k0 + row_iota + 1, 0), 0)
            return rank[tk - 1:tk, :], jnp.maximum(cut, jnp.max(bound, axis=0, keepdims=True))

        _, cut = lax.fori_loop(0, n_chunks, body, (jnp.zeros((1, tq), F32), jnp.zeros((1, tq), I32)))
        cut_ref[0] = jnp.where(ist_ref[2:3, :] > 0, cut, ist_ref[3:4, :])


def _select_kernel(qit_ref, ki_ref, wit_ref, thr_ref, cut_ref, s_ref, fst_ref, ist_ref, *, tq, tk, n_keys, k_top):
    i = pl.program_id(1)
    q0 = i * tq
    n_chunks = (q0 + tq + tk - 1) // tk
    t_row = q0 + lax.broadcasted_iota(I32, (1, tq), 1)
    row_iota = lax.broadcasted_iota(I32, (tk, tq), 0)
    qit = qit_ref[0]
    w = wit_ref[0]

    def build(c, carry):
        mx, mn, n_hi = carry
        k0 = pl.multiple_of(c * tk, tk)
        s = _scores_t(ki_ref[0, pl.ds(k0, tk), :], qit, w)
        causal = (k0 + row_iota) <= t_row
        sm = jnp.where(causal, s, -jnp.inf)
        s_ref[pl.ds(k0, tk), :] = sm
        return (jnp.maximum(mx, _fold8(sm, jnp.max)),
                jnp.minimum(mn, _fold8(jnp.where(causal, s, jnp.inf), jnp.min)),
                n_hi + _fold8(jnp.where(sm > NEG_BIG, 1, 0), jnp.sum))

    mx, mn, n_hi = lax.fori_loop(
        0, n_chunks, build,
        (jnp.full((_FOLD_ROWS, tq), -jnp.inf, F32), jnp.full((_FOLD_ROWS, tq), jnp.inf, F32),
         jnp.zeros((_FOLD_ROWS, tq), I32)))
    n_nc = (n_keys - 1) - t_row
    _topk_threshold(s_ref, n_chunks, tk, tq, jnp.max(mx, axis=0, keepdims=True),
                    jnp.min(mn, axis=0, keepdims=True), jnp.sum(n_hi, axis=0, keepdims=True),
                    n_nc, k_top, thr_ref, cut_ref, fst_ref, ist_ref)


def _select(qit, ki, wit, *, tq, tk, k_top):
    b, _, _, t = qit.shape
    kern = functools.partial(_select_kernel, tq=tq, tk=tk, n_keys=t, k_top=k_top)
    return pl.pallas_call(
        kern,
        grid=(b, t // tq),
        in_specs=[
            pl.BlockSpec((1, IDX_HEADS, IDX_DIM, tq), lambda bb, i: (bb, 0, 0, i)),
            pl.BlockSpec((1, t, IDX_DIM), lambda bb, i: (bb, 0, 0)),
            pl.BlockSpec((1, 8, tq), lambda bb, i: (bb, 0, i)),
        ],
        out_specs=[pl.BlockSpec((1, 1, tq), lambda bb, i: (bb, 0, i)),
                   pl.BlockSpec((1, 1, tq), lambda bb, i: (bb, 0, i))],
        out_shape=[jax.ShapeDtypeStruct((b, 1, t), F32), jax.ShapeDtypeStruct((b, 1, t), I32)],
        scratch_shapes=[pltpu.VMEM((t, tq), F32), pltpu.VMEM((8, tq), F32), pltpu.VMEM((8, tq), I32)],
        compiler_params=_cparams("parallel", "parallel"),
        name="select",
    )(qit, ki, wit)


_ATTN_SUB_ROWS = 128


def _n_key_tiles(i, tq, tk):
    return (i * tq + tq + tk - 1) // tk


def _attn_kernel(qt_ref, k_ref, vt_ref, qit_ref, ki_ref, wit_ref, vf_ref, cut_ref, o_ref,
                 m_ref, l_ref, acc_ref, bias_ref, lga_ref, lgb_ref, pa_ref, pb_ref, *, tq, tk):
    i = pl.program_id(1)
    j = pl.program_id(2)
    nkt = _n_key_tiles(i, tq, tk)
    sub = _ATTN_SUB_ROWS
    chunks = [slice(c * sub, (c + 1) * sub) for c in range(tk // sub)]

    @pl.when(j == 0)
    def _():
        m_ref[...] = jnp.full_like(m_ref, -jnp.inf)
        l_ref[...] = jnp.zeros_like(l_ref)
        acc_ref[...] = jnp.zeros_like(acc_ref)

    @pl.when(j < nkt)
    def _():
        qit = qit_ref[0]
        w = wit_ref[0]
        vf = vf_ref[0]
        cut = cut_ref[0]
        t_row = i * tq + lax.broadcasted_iota(I32, (1, tq), 1)
        row = lax.broadcasted_iota(I32, (sub, tq), 0)
        for c, rows in enumerate(chunks):
            kidx = j * tk + c * sub + row
            bias = _select_bias(_scores_t(ki_ref[0, rows, :], qit, w), vf, kidx, cut)
            bias_ref[rows, :] = jnp.where(kidx <= t_row, bias, -jnp.inf)

        def logits(h, lg_ref):
            qt = qt_ref[0, h]
            part = jnp.full((8, tq), -jnp.inf, F32)
            for rows in chunks:
                lg = jnp.dot(k_ref[0, h, rows, :], qt, preferred_element_type=F32) + bias_ref[rows, :]
                lg_ref[rows, :] = lg
                part = jnp.maximum(part, jnp.max(lg.reshape(sub // 8, 8, tq), axis=0))
            return jnp.max(part, axis=0, keepdims=True)

        def absorb(h, lg_ref, p_ref, tile_max):
            hrow = pl.ds(h, 1)
            m_old = m_ref[hrow, :]
            m_new = jnp.maximum(m_old, tile_max)
            m_safe = jnp.where(m_new == -jnp.inf, 0.0, m_new)
            alpha = jnp.exp(m_old - m_safe)
            psum = jnp.zeros((8, tq), F32)
            for rows in chunks:
                p = jnp.exp(lg_ref[rows, :] - m_safe)
                psum = psum + jnp.sum(p.reshape(sub // 8, 8, tq), axis=0)
                p_ref[rows, :] = p.astype(BF16)
            l_ref[hrow, :] = alpha * l_ref[hrow, :] + jnp.sum(psum, axis=0, keepdims=True)
            acc_ref[h] = alpha * acc_ref[h] + jnp.dot(vt_ref[0, h], p_ref[...], preferred_element_type=F32)
            m_ref[hrow, :] = m_new

        def two_heads(t, max_a):
            h = 2 * t
            max_b = logits(h + 1, lgb_ref)
            absorb(h, lga_ref, pa_ref, max_a)
            max_a = logits(h + 2, lga_ref)
            absorb(h + 1, lgb_ref, pb_ref, max_b)
            return max_a

        max_a = lax.fori_loop(0, N_HEADS // 2 - 1, two_heads, logits(0, lga_ref))
        max_b = logits(N_HEADS - 1, lgb_ref)
        absorb(N_HEADS - 2, lga_ref, pa_ref, max_a)
        absorb(N_HEADS - 1, lgb_ref, pb_ref, max_b)

    @pl.when(j == nkt - 1)
    def _():
        for h in range(N_HEADS):
            o_ref[0, h] = (acc_ref[h] / l_ref[h:h + 1, :]).astype(o_ref.dtype)


def _attn(qt, k, vt, qit, ki, wit, vf, cut, *, tq, tk):
    b, _, _, t = qt.shape
    kj = lambda i, j: jnp.minimum(j, _n_key_tiles(i, tq, tk) - 1)
    return pl.pallas_call(
        functools.partial(_attn_kernel, tq=tq, tk=tk),
        grid=(b, t // tq, t // tk),
        in_specs=[
            pl.BlockSpec((1, N_HEADS, HEAD_DIM, tq), lambda bb, i, j: (bb, 0, 0, i)),
            pl.BlockSpec((1, N_HEADS, tk, HEAD_DIM), lambda bb, i, j: (bb, 0, kj(i, j), 0)),
            pl.BlockSpec((1, N_HEADS, HEAD_DIM, tk), lambda bb, i, j: (bb, 0, 0, kj(i, j))),
            pl.BlockSpec((1, IDX_HEADS, IDX_DIM, tq), lambda bb, i, j: (bb, 0, 0, i)),
            pl.BlockSpec((1, tk, IDX_DIM), lambda bb, i, j: (bb, kj(i, j), 0)),
            pl.BlockSpec((1, 8, tq), lambda bb, i, j: (bb, 0, i)),
            pl.BlockSpec((1, 1, tq), lambda bb, i, j: (bb, 0, i)),
            pl.BlockSpec((1, 1, tq), lambda bb, i, j: (bb, 0, i)),
        ],
        out_specs=pl.BlockSpec((1, N_HEADS, HEAD_DIM, tq), lambda bb, i, j: (bb, 0, 0, i)),
        out_shape=jax.ShapeDtypeStruct((b, N_HEADS, HEAD_DIM, t), BF16),
        scratch_shapes=[pltpu.VMEM((N_HEADS, tq), F32), pltpu.VMEM((N_HEADS, tq), F32),
                        pltpu.VMEM((N_HEADS, HEAD_DIM, tq), F32),
                        pltpu.VMEM((tk, tq), F32), pltpu.VMEM((tk, tq), F32), pltpu.VMEM((tk, tq), F32),
                        pltpu.VMEM((tk, tq), BF16), pltpu.VMEM((tk, tq), BF16)],
        compiler_params=_cparams("parallel", "parallel", "arbitrary"),
        name="attn",
    )(qt, k, vt, qit, ki, wit, vf, cut)


def _zoh(ar, ai, ldt):
    dt = jnp.exp(ldt)
    mag = jnp.exp(dt * ar)
    abr = mag * jnp.cos(dt * ai)
    abi = mag * jnp.sin(dt * ai)
    den = ar * ar + ai * ai
    nr = abr - 1.0
    ni = abi
    return abr, abi, (nr * ar + ni * ai) / den, (ni * ar - nr * ai) / den


def _disc_kernel(ar_ref, ai_ref, ldt_ref, ar16_ref, ai16_ref, ldt16_ref, br_ref, bi_ref,
                 pwr_ref, pwi_ref, bbr_ref, bbi_ref):
    abr, abi, _, _ = _zoh(ar_ref[...], ai_ref[...], ldt_ref[...])
    pr, pi = abr, abi
    for j in range(8):
        pwr_ref[j:j + 1, :] = pr
        pwi_ref[j:j + 1, :] = pi
        pr, pi = pr * abr - pi * abi, pr * abi + pi * abr
    _, _, fr, fi = _zoh(ar16_ref[...], ai16_ref[...], ldt16_ref[...])
    br = br_ref[...]
    bi = bi_ref[...]
    bbr_ref[...] = fr * br - fi * bi
    bbi_ref[...] = fr * bi + fi * br


def _discretize(a_re, a_im, log_dt, b_re, b_im):
    flat = lambda a: a.reshape(1, -1)
    ldt = jnp.broadcast_to(log_dt[:, None], (N_GROUPS, STATE_DIM))
    rep = lambda a: flat(jnp.broadcast_to(a[:, :, None], (N_GROUPS, STATE_DIM, GROUP_CH)))
    n16 = N_STATE * GROUP_CH
    pwr, pwi, bbr, bbi = pl.pallas_call(
        _disc_kernel,
        out_shape=[jax.ShapeDtypeStruct((8, N_STATE), F32), jax.ShapeDtypeStruct((8, N_STATE), F32),
                   jax.ShapeDtypeStruct((1, n16), F32), jax.ShapeDtypeStruct((1, n16), F32)],
        name="s5_discretize",
    )(flat(a_re), flat(a_im), flat(ldt), rep(a_re), rep(a_im), rep(ldt), flat(b_re), flat(b_im))
    return pwr, pwi, bbr.reshape(N_GROUPS, STATE_DIM, GROUP_CH), bbi.reshape(N_GROUPS, STATE_DIM, GROUP_CH)


def _block_diag_in(bb):
    eye = jnp.eye(N_GROUPS, dtype=bb.dtype)
    return jnp.einsum('gpc,gh->gchp', bb, eye).reshape(SSM_WIDTH, N_STATE)


def _block_diag_out(c):
    eye = jnp.eye(N_GROUPS, dtype=c.dtype)
    return jnp.einsum('gcp,gh->gphc', c, eye).reshape(N_STATE, SSM_WIDTH)


_LANE_CHUNK = 512


def _ssm_kernel(u_ref, bblk_ref, cblk_ref, pw_ref, d_ref, y_ref, xf_ref, x_ref, carry_ref, *, tt):
    ts = pl.program_id(1)

    @pl.when(ts == 0)
    def _():
        carry_ref[...] = jnp.zeros_like(carry_ref)

    u = u_ref[0]
    ub = u.astype(BF16)
    n_chunks = 2 * N_STATE // _LANE_CHUNK
    for c in range(n_chunks):
        cols = slice(c * _LANE_CHUNK, (c + 1) * _LANE_CHUNK)
        x_ref[:, cols] = jnp.dot(ub, bblk_ref[:, cols], preferred_element_type=F32)
    rid = lax.broadcasted_iota(I32, (8, _LANE_CHUNK), 0)

    def group(r, carry):
        r0 = pl.multiple_of(r * 8, 8)
        for c in range(N_STATE // _LANE_CHUNK):
            re = pl.ds(c * _LANE_CHUNK, _LANE_CHUNK)
            im = pl.ds(N_STATE + c * _LANE_CHUNK, _LANE_CHUNK)
            xr = x_ref[pl.ds(r0, 8), re]
            xi = x_ref[pl.ds(r0, 8), im]
            for d in (1, 2, 4):
                ar = pw_ref[d - 1:d, re]
                ai = pw_ref[d - 1:d, im]
                sr = jnp.where(rid >= d, pltpu.roll(xr, d, 0), 0.0)
                si = jnp.where(rid >= d, pltpu.roll(xi, d, 0), 0.0)
                xr, xi = xr + (ar * sr - ai * si), xi + (ar * si + ai * sr)
            cr = carry_ref[:, re]
            ci = carry_ref[:, im]
            pr = pw_ref[:, re]
            pi = pw_ref[:, im]
            xr, xi = xr + (pr * cr - pi * ci), xi + (pr * ci + pi * cr)
            x_ref[pl.ds(r0, 8), re] = xr
            x_ref[pl.ds(r0, 8), im] = xi
            carry_ref[:, re] = xr[7:8, :]
            carry_ref[:, im] = xi[7:8, :]
        return carry

    lax.fori_loop(0, tt // 8, group, 0)
    y = d_ref[...] * u
    for c in range(n_chunks):
        cols = slice(c * _LANE_CHUNK, (c + 1) * _LANE_CHUNK)
        y = y + jnp.dot(x_ref[:, cols].astype(BF16), cblk_ref[cols, :], preferred_element_type=F32)
    y_ref[0] = y
    xf_ref[0] = carry_ref[...]


def _ssm(u, bblk, cblk, pw, d_skip, *, tt):
    b, t, _ = u.shape
    return pl.pallas_call(
        functools.partial(_ssm_kernel, tt=tt),
        grid=(b, t // tt),
        in_specs=[
            pl.BlockSpec((1, tt, SSM_WIDTH), lambda bb, s: (bb, s, 0)),
            pl.BlockSpec((SSM_WIDTH, 2 * N_STATE), lambda bb, s: (0, 0)),
            pl.BlockSpec((2 * N_STATE, SSM_WIDTH), lambda bb, s: (0, 0)),
            pl.BlockSpec((8, 2 * N_STATE), lambda bb, s: (0, 0)),
            pl.BlockSpec((1, SSM_WIDTH), lambda bb, s: (0, 0)),
        ],
        out_specs=[pl.BlockSpec((1, tt, SSM_WIDTH), lambda bb, s: (bb, s, 0)),
                   pl.BlockSpec((1, 1, 2 * N_STATE), lambda bb, s: (bb, 0, 0))],
        out_shape=[jax.ShapeDtypeStruct((b, t, SSM_WIDTH), F32),
                   jax.ShapeDtypeStruct((b, 1, 2 * N_STATE), F32)],
        scratch_shapes=[pltpu.VMEM((tt, 2 * N_STATE), F32), pltpu.VMEM((1, 2 * N_STATE), F32)],
        compiler_params=_cparams("parallel", "arbitrary"),
        name="s5_scan",
    )(u, bblk, cblk, pw, d_skip)


def _ssm_step_kernel(u_ref, x0_ref, bblk_ref, cblk_ref, pw_ref, d_ref, y_ref, x_ref):
    u = u_ref[...]
    bu = jnp.dot(u, bblk_ref[...], preferred_element_type=F32, precision=lax.Precision.HIGHEST)
    ar = pw_ref[0:1, :N_STATE]
    ai = pw_ref[0:1, N_STATE:]
    x0r = x0_ref[:, :N_STATE]
    x0i = x0_ref[:, N_STATE:]
    xr = ar * x0r - ai * x0i + bu[:, :N_STATE]
    xi = ar * x0i + ai * x0r + bu[:, N_STATE:]
    x_ref[:, :N_STATE] = xr
    x_ref[:, N_STATE:] = xi
    y = jnp.dot(x_ref[...], cblk_ref[...], preferred_element_type=F32, precision=lax.Precision.HIGHEST)
    y_ref[...] = y + d_ref[...] * u


def _ssm_step(u, x0, bblk, cblk, pw, d_skip):
    n = u.shape[0]
    return pl.pallas_call(
        _ssm_step_kernel,
        out_shape=[jax.ShapeDtypeStruct((n, SSM_WIDTH), F32), jax.ShapeDtypeStruct((n, 2 * N_STATE), F32)],
        compiler_params=pltpu.CompilerParams(vmem_limit_bytes=VMEM_LIMIT_BYTES),
        name="s5_step",
    )(u, x0, bblk, cblk, pw, d_skip)


def _gelu_tanh(x):
    c = np.float32(np.sqrt(2.0 / np.pi))
    return 0.5 * x * (1.0 + jnp.tanh(c * (x + 0.044715 * (x * x * x))))


def _merge_kernel(x_ref, oa_ref, ys_ref, g_ref, wgt_ref, wglu_ref, bglu_ref, wpa_ref, wpb_ref, wout_ref, o_ref):
    x = x_ref[...]
    h = _rmsnorm(x, g_ref[...]).astype(BF16)
    gates = _sigmoid(jnp.dot(h, wgt_ref[...], preferred_element_type=F32))
    ys = _gelu_tanh(ys_ref[...])
    glu = jnp.dot(ys.astype(BF16), wglu_ref[...], preferred_element_type=F32) + bglu_ref[...]
    ob = ys * _sigmoid(glu)
    pa = jnp.dot(oa_ref[...], wpa_ref[...], preferred_element_type=F32)
    pb = jnp.dot(ob.astype(BF16), wpb_ref[...], preferred_element_type=F32)
    merged = gates[:, :D_MODEL] * pa + gates[:, D_MODEL:] * pb
    o_ref[...] = x + jnp.dot(merged.astype(BF16), wout_ref[...], preferred_element_type=F32)


def _merge(x, oa, ys, g, wgt, wglu, bglu, wpa, wpb, wout, *, tm):
    m, d = x.shape
    row = lambda n: pl.BlockSpec((tm, n), lambda i: (i, 0))
    full = lambda a: pl.BlockSpec(a.shape, lambda i: (0, 0))
    return pl.pallas_call(
        _merge_kernel,
        grid=(m // tm,),
        in_specs=[row(d), row(D_ATTN), row(SSM_WIDTH), full(g), full(wgt), full(wglu), full(bglu),
                  full(wpa), full(wpb), full(wout)],
        out_specs=row(d),
        out_shape=jax.ShapeDtypeStruct((m, d), F32),
        compiler_params=_cparams("parallel"),
        name="merge",
    )(x, oa, ys, g, wgt, wglu, bglu, wpa, wpb, wout)


_PAGES_PER_STEP = 8


def _page_specs(block, n_pages):
    def spec(r):
        def index_map(b, p, pt):
            return (pt[b, jnp.minimum(p * _PAGES_PER_STEP + r, n_pages - 1)],) + (0,) * (len(block) - 1)
        return pl.BlockSpec(block, index_map)
    return [spec(r) for r in range(_PAGES_PER_STEP)]


def _per_seq(shape):
    return pl.BlockSpec((1,) + shape, lambda b, p, pt: (b,) + (0,) * len(shape))


def _sample_scores_kernel(pt_ref, qi_ref, w_ref, knew_ref, *refs, page_size, n_steps):
    pages = refs[:_PAGES_PER_STEP]
    o_ref = refs[_PAGES_PER_STEP]
    p = pl.program_id(1)
    qi = qi_ref[0]
    w = w_ref[0]

    def score(kt):
        d = jnp.dot(qi, kt, preferred_element_type=F32)
        return jnp.sum(jnp.maximum(d, 0.0) * w, axis=0, keepdims=True)

    @pl.when(p < n_steps)
    def _():
        for r in range(_PAGES_PER_STEP):
            o_ref[0, :, r * page_size:(r + 1) * page_size] = score(pages[r][0].astype(BF16))

    @pl.when(p == n_steps)
    def _():
        o_ref[0] = jnp.full(o_ref.shape[1:], -jnp.inf, F32)
        lane = lax.broadcasted_iota(I32, (1, page_size), 1)
        o_ref[0, :, 0:page_size] = jnp.where(lane == 0, score(knew_ref[0]), -jnp.inf)


def _sample_scores(page_table, qi_rows, w_col, ki_new_t, kidx_t):
    db, n_pages = page_table.shape
    _, _, page_size = kidx_t.shape
    assert n_pages % _PAGES_PER_STEP == 0
    steps = n_pages // _PAGES_PER_STEP
    block = _PAGES_PER_STEP * page_size
    return pl.pallas_call(
        functools.partial(_sample_scores_kernel, page_size=page_size, n_steps=steps),
        grid_spec=pltpu.PrefetchScalarGridSpec(
            num_scalar_prefetch=1,
            grid=(db, steps + 1),
            in_specs=[_per_seq((8, IDX_DIM)), _per_seq((8, 1)), _per_seq((IDX_DIM, page_size))]
                     + _page_specs((1, IDX_DIM, page_size), n_pages),
            out_specs=pl.BlockSpec((1, 1, block), lambda b, p, pt: (b, 0, p)),
        ),
        out_shape=jax.ShapeDtypeStruct((db, 1, (steps + 1) * block), F32),
        compiler_params=_cparams("parallel", "arbitrary"),
        name="sample_scores",
    )(page_table, qi_rows, w_col, ki_new_t, *([kidx_t] * _PAGES_PER_STEP))


def _sample_select_kernel(st_ref, thr_ref, cut_ref, s_ref, fst_ref, ist_ref, *, tk, tq, n_keys, k_top):
    n_chunks = st_ref.shape[0] // tk
    row_iota = lax.broadcasted_iota(I32, (tk, tq), 0)

    def build(c, carry):
        mx, mn, n_hi = carry
        k0 = pl.multiple_of(c * tk, tk)
        s = st_ref[pl.ds(k0, tk), :]
        real = (k0 + row_iota) < n_keys
        sm = jnp.where(real, s, -jnp.inf)
        s_ref[pl.ds(k0, tk), :] = sm
        return (jnp.maximum(mx, _fold8(sm, jnp.max)),
                jnp.minimum(mn, _fold8(jnp.where(real, s, jnp.inf), jnp.min)),
                n_hi + _fold8(jnp.where(sm > NEG_BIG, 1, 0), jnp.sum))

    mx, mn, n_hi = lax.fori_loop(
        0, n_chunks, build,
        (jnp.full((_FOLD_ROWS, tq), -jnp.inf, F32), jnp.full((_FOLD_ROWS, tq), jnp.inf, F32),
         jnp.zeros((_FOLD_ROWS, tq), I32)))
    _topk_threshold(s_ref, n_chunks, tk, tq, jnp.max(mx, axis=0, keepdims=True),
                    jnp.min(mn, axis=0, keepdims=True), jnp.sum(n_hi, axis=0, keepdims=True),
                    jnp.zeros((1, tq), I32), k_top, thr_ref, cut_ref, fst_ref, ist_ref)


def _sample_select(scores_t, *, n_keys, k_top, tk):
    n_rows, tq = scores_t.shape
    kern = functools.partial(_sample_select_kernel, tk=tk, tq=tq, n_keys=n_keys, k_top=k_top)
    return pl.pallas_call(
        kern,
        out_shape=[jax.ShapeDtypeStruct((1, 1, tq), F32), jax.ShapeDtypeStruct((1, 1, tq), I32)],
        scratch_shapes=[pltpu.VMEM((n_rows, tq), F32), pltpu.VMEM((8, tq), F32), pltpu.VMEM((8, tq), I32)],
        compiler_params=pltpu.CompilerParams(vmem_limit_bytes=VMEM_LIMIT_BYTES),
        name="sample_select",
    )(scores_t)


def _sample_attn_kernel(pt_ref, qb_ref, s_ref, thr_ref, cut_ref, kself_ref, vself_ref, *refs,
                        page_size, n_steps):
    kp = refs[:_PAGES_PER_STEP]
    vp = refs[_PAGES_PER_STEP:2 * _PAGES_PER_STEP]
    o_ref, m_ref, l_ref, acc_ref = refs[2 * _PAGES_PER_STEP:]
    step = pl.program_id(1)
    thr = thr_ref[0]
    cut = cut_ref[0]
    lane = lax.broadcasted_iota(I32, (1, page_size), 1)
    is_self = step == n_steps

    @pl.when(step == 0)
    def _():
        m_ref[...] = jnp.full_like(m_ref, -jnp.inf)
        l_ref[...] = jnp.zeros_like(l_ref)
        acc_ref[...] = jnp.zeros_like(acc_ref)

    logits = []
    for r in range(_PAGES_PER_STEP):
        kidx = (step * _PAGES_PER_STEP + r) * page_size + lane
        bias = _select_bias(s_ref[0, :, r * page_size:(r + 1) * page_size], thr, kidx, cut)
        rows = []
        for h in range(N_HEADS):
            kt = kp[r][0, h]
            if r == 0:
                kt = jnp.where(is_self, kself_ref[0, h], kt)
            rows.append(jnp.sum(kt * qb_ref[0, h], axis=0, keepdims=True))
        logits.append(jnp.concatenate(rows, axis=0) + bias)

    m_old = m_ref[...]
    m_new = m_old
    for lg in logits:
        m_new = jnp.maximum(m_new, jnp.max(lg, axis=1, keepdims=True))
    m_safe = jnp.where(m_new == -jnp.inf, 0.0, m_new)
    alpha = jnp.exp(m_old - m_safe)
    probs = [jnp.exp(lg - m_safe) for lg in logits]
    l_new = alpha * l_ref[...]
    for p in probs:
        l_new = l_new + jnp.sum(p, axis=1, keepdims=True)
    l_ref[...] = l_new
    m_ref[...] = m_new
    for h in range(N_HEADS):
        acc = acc_ref[h] * alpha[h:h + 1, :]
        for r in range(_PAGES_PER_STEP):
            vt = vp[r][0, h]
            if r == 0:
                vt = jnp.where(is_self, vself_ref[0, h], vt)
            acc = acc + vt * probs[r][h:h + 1, :]
        acc_ref[h] = acc

    @pl.when(is_self)
    def _():
        for h in range(N_HEADS):
            o_ref[0, h] = jnp.sum(acc_ref[h], axis=1, keepdims=True) / l_ref[h:h + 1, :]


def _sample_attn(page_table, qb, scores, thr, cut, k_self, v_self, ck_t, cv_t):
    db, n_pages = page_table.shape
    _, nh, hd, page_size = ck_t.shape
    steps = n_pages // _PAGES_PER_STEP
    block = _PAGES_PER_STEP * page_size
    page = (1, nh, hd, page_size)
    return pl.pallas_call(
        functools.partial(_sample_attn_kernel, page_size=page_size, n_steps=steps),
        grid_spec=pltpu.PrefetchScalarGridSpec(
            num_scalar_prefetch=1,
            grid=(db, steps + 1),
            in_specs=[_per_seq(page[1:]),
                      pl.BlockSpec((1, 1, block), lambda b, p, pt: (b, 0, p)),
                      _per_seq((1, 1)), _per_seq((1, 1)), _per_seq(page[1:]), _per_seq(page[1:])]
                     + _page_specs(page, n_pages) + _page_specs(page, n_pages),
            out_specs=_per_seq((nh, hd, 1)),
            scratch_shapes=[pltpu.VMEM((nh, 1), F32), pltpu.VMEM((nh, 1), F32),
                            pltpu.VMEM((nh, hd, page_size), F32)],
        ),
        out_shape=jax.ShapeDtypeStruct((db, nh, hd, 1), F32),
        compiler_params=_cparams("parallel", "arbitrary"),
        name="sample_attn",
    )(page_table, qb, scores, thr, cut, k_self, v_self,
      *([ck_t] * _PAGES_PER_STEP), *([cv_t] * _PAGES_PER_STEP))


def _pick(n, pref):
    for t in pref:
        if n % t == 0:
            return t
    return n


def _pack_w_in(w_in):
    o = np.cumsum([0, D_ATTN, D_ATTN, D_ATTN, IDX_HEADS * IDX_DIM, IDX_DIM, IDX_HEADS, SSM_WIDTH, 2 * D_MODEL])
    z = lambda n: jnp.zeros((D_MODEL, n), w_in.dtype)
    w_all = jnp.concatenate([
        w_in[:, o[0]:o[4]],
        w_in[:, o[4]:o[5]], z(_PWI - _PKI - IDX_DIM),
        w_in[:, o[5]:o[6]], z(_PU - _PWI - IDX_HEADS),
        w_in[:, o[6]:o[7]],
    ], axis=1)
    return w_all.astype(BF16), w_in[:, o[7]:o[8]].astype(BF16)


def _prompt_layer(x, lw, ssm, final_norm):
    bsz, seq, d = x.shape
    m = bsz * seq
    tm = _pick(m, (512, 256, 128, 64, 32, 16, 8))
    x1 = _ffn(x.reshape(m, d), *lw["ffn1"], lw["g_final"], final_norm=False, tm=tm, tf=lw["tf"])
    q_b, k_f, v_f, k_b, v_b, qi_b, ki_f, ki_b, wi_f, u_f = _proj(x1, lw["g_mix"], lw["w_all"], tm=tm)

    heads_t = lambda a, nh: a.reshape(bsz, seq, nh, -1).transpose(0, 2, 3, 1)
    qt = heads_t(q_b, N_HEADS)
    vt = heads_t(v_b, N_HEADS)
    kh = k_b.reshape(bsz, seq, N_HEADS, HEAD_DIM).transpose(0, 2, 1, 3)
    qit = heads_t(qi_b, IDX_HEADS)
    kib = ki_b.reshape(bsz, seq, IDX_DIM)
    wit = wi_f[:, :8].reshape(bsz, seq, 8).transpose(0, 2, 1)

    k_top = min(TOPK_MAX, seq // 4)
    vf, cut = _select(qit, kib, wit, tq=_pick(seq, (256, 128)), tk=_pick(seq, (512, 256, 128)), k_top=k_top)
    o_t = _attn(qt, kh, vt, qit, kib, wit, vf, cut,
                tq=_pick(seq, (512, 256, 128)), tk=_pick(seq, (512, 256, 128)))
    oa = o_t.transpose(0, 3, 1, 2).reshape(m, D_ATTN)

    tt = _pick(seq, (256, 128, 64, 32, 16, 8))
    ys, xfin = _ssm(u_f.reshape(bsz, seq, SSM_WIDTH), ssm["bblk"].astype(BF16), ssm["cblk"].astype(BF16),
                    ssm["pw"], ssm["d_skip"], tt=tt)
    x2 = _merge(x1, oa, ys.reshape(m, SSM_WIDTH), *lw["merge"], tm=tm)
    y = _ffn(x2, *lw["ffn2"], lw["g_final"], final_norm=final_norm, tm=tm, tf=lw["tf"])
    new = (k_f.reshape(bsz, seq, N_HEADS, HEAD_DIM), v_f.reshape(bsz, seq, N_HEADS, HEAD_DIM),
           ki_f.reshape(bsz, seq, IDX_DIM),
           xfin[:, 0, :N_STATE].reshape(bsz, N_GROUPS, STATE_DIM),
           xfin[:, 0, N_STATE:].reshape(bsz, N_GROUPS, STATE_DIM))
    return y.reshape(bsz, seq, d), new


def _sample_layer(x, lw, ssm, final_norm, cache_k, cache_v, cache_kidx, st_re, st_im, page_table):
    db, ds, d = x.shape
    assert ds == 1, "one new token per sample sequence"
    n_pool, page_size = cache_k.shape[0], cache_k.shape[1]
    past_len = page_table.shape[1] * page_size
    s1 = _ffn(x.reshape(db, d), *lw["ffn1"], lw["g_final"], final_norm=False, tm=db, tf=lw["tf"])
    q_b, k_f, v_f, _, _, qi_b, ki_f, ki_b, wi_f, u_f = _proj(s1, lw["g_mix"], lw["w_all"], tm=db)

    kidx_t = cache_kidx.transpose(0, 2, 1)
    ck_t = cache_k.transpose(0, 2, 3, 1)
    cv_t = cache_v.transpose(0, 2, 3, 1)
    lane0 = lambda a: jnp.zeros(a.shape + (page_size,), a.dtype).at[..., 0].set(a)

    qi_rows = jnp.zeros((db, 8, IDX_DIM), BF16).at[:, :IDX_HEADS].set(qi_b.reshape(db, IDX_HEADS, IDX_DIM))
    scores = _sample_scores(page_table, qi_rows, wi_f[:, :8, None], lane0(ki_b), kidx_t)

    n_keys = past_len + 1
    k_top = min(TOPK_MAX, n_keys // 4)
    n = scores.shape[2]
    scores_t = jnp.zeros((n, -(-db // 128) * 128), F32).at[:, :db].set(scores[:, 0, :].T)
    thr, cut = _sample_select(scores_t, n_keys=n_keys, k_top=k_top, tk=256)

    qb = jnp.broadcast_to(q_b.astype(F32).reshape(db, N_HEADS, HEAD_DIM, 1), (db, N_HEADS, HEAD_DIM, page_size))
    o_s = _sample_attn(page_table, qb, scores, thr[0, 0, :db].reshape(db, 1, 1), cut[0, 0, :db].reshape(db, 1, 1),
                       lane0(k_f.reshape(db, N_HEADS, HEAD_DIM)), lane0(v_f.reshape(db, N_HEADS, HEAD_DIM)),
                       ck_t, cv_t)
    oa = o_s.reshape(db, D_ATTN).astype(BF16)

    x0 = jnp.concatenate([st_re.reshape(db, N_STATE), st_im.reshape(db, N_STATE)], axis=1)
    ys, x_new = _ssm_step(u_f, x0, ssm["bblk"], ssm["cblk"], ssm["pw"], ssm["d_skip"])
    s2 = _merge(s1, oa, ys, *lw["merge"], tm=db)
    y = _ffn(s2, *lw["ffn2"], lw["g_final"], final_norm=final_norm, tm=db, tf=lw["tf"])
    new = (k_f.reshape(db, ds, N_HEADS, HEAD_DIM), v_f.reshape(db, ds, N_HEADS, HEAD_DIM),
           ki_f.reshape(db, ds, IDX_DIM),
           x_new[:, :N_STATE].reshape(db, N_GROUPS, STATE_DIM),
           x_new[:, N_STATE:].reshape(db, N_GROUPS, STATE_DIM))
    return y.reshape(db, ds, d), new


def kernel(x_prompt, x_sample, cache_k, cache_v, cache_kidx, state_ssm_re, state_ssm_im, page_table,
           g_ffn1, w1_gate, w1_up, w1_down, g_mix, w_in, a_re, a_im, log_dt, b_re, b_im, c_re, c_im,
           d_skip, w_glu, b_glu, w_pa, w_pb, w_out, g_ffn2, w2_gate, w2_up, w2_down, g_final):
    depth = w_in.shape[0]
    row = lambda a: a.reshape(1, -1)
    bf = lambda a: a.astype(BF16)
    xp, xs = x_prompt, x_sample
    new_p, new_s = [], []
    for l in range(depth):
        w_all, w_gates = _pack_w_in(w_in[l])
        lw = dict(
            ffn1=(row(g_ffn1[l]), bf(w1_gate[l]), bf(w1_up[l]), bf(w1_down[l])),
            ffn2=(row(g_ffn2[l]), bf(w2_gate[l]), bf(w2_up[l]), bf(w2_down[l])),
            merge=(row(g_mix[l]), w_gates, bf(w_glu[l]), row(b_glu[l]), bf(w_pa[l]), bf(w_pb[l]), bf(w_out[l])),
            g_mix=row(g_mix[l]), w_all=w_all, g_final=row(g_final),
            tf=_pick(w1_gate.shape[2], (1408, 1024, 512, 256, 128)),
        )
        pwr, pwi, bbr, bbi = _discretize(a_re[l], a_im[l], log_dt[l], b_re[l], b_im[l])
        ssm = dict(
            pw=jnp.concatenate([pwr, pwi], axis=1),
            bblk=jnp.concatenate([_block_diag_in(bbr), _block_diag_in(bbi)], axis=1),
            cblk=jnp.concatenate([_block_diag_out(c_re[l]), -_block_diag_out(c_im[l])], axis=0),
            d_skip=row(d_skip[l]),
        )
        last = l == depth - 1
        xp, st_p = _prompt_layer(xp, lw, ssm, last)
        xs, st_s = _sample_layer(xs, lw, ssm, last, cache_k[l], cache_v[l], cache_kidx[l],
                                 state_ssm_re[l], state_ssm_im[l], page_table)
        new_p.append(st_p)
        new_s.append(st_s)
    stack = lambda states, i: jnp.stack([s[i] for s in states])
    return (xp, xs) + tuple(stack(new_p, i) for i in range(5)) + tuple(stack(new_s, i) for i in range(5))
```

```python
import functools

import numpy as np
import jax
import jax.numpy as jnp
from jax import lax
from jax.experimental import pallas as pl
from jax.experimental.pallas import tpu as pltpu

F32 = jnp.float32
BF16 = jnp.bfloat16
I32 = jnp.int32

D_MODEL = 1024
N_HEADS = 8
HEAD_DIM = 64
D_ATTN = N_HEADS * HEAD_DIM
IDX_HEADS = 4
IDX_DIM = 64
TOPK_MAX = 256
SSM_WIDTH = 512
GROUP_CH = 16
N_GROUPS = SSM_WIDTH // GROUP_CH
STATE_DIM = 64
N_STATE = N_GROUPS * STATE_DIM
NORM_EPS = 1e-6
NEG_BIG = -1e30

INT_MAX = 2 ** 31 - 1

VMEM_LIMIT_BYTES = 56 * 1024 * 1024


def _cparams(*sem):
    return pltpu.CompilerParams(dimension_semantics=sem, vmem_limit_bytes=VMEM_LIMIT_BYTES)


def _rmsnorm(x, g):
    y = x * lax.rsqrt(jnp.mean(x * x, axis=-1, keepdims=True) + NORM_EPS)
    return y * g


def _sigmoid(x):
    return 1.0 / (1.0 + jnp.exp(-x))


def _ffn_kernel(x_ref, g_ref, wg_ref, wu_ref, wd_ref, gf_ref, o_ref, xn_ref, acc_ref, *, final_norm):
    f = pl.program_id(1)

    @pl.when(f == 0)
    def _():
        xn_ref[...] = _rmsnorm(x_ref[...], g_ref[...]).astype(BF16)
        acc_ref[...] = jnp.zeros_like(acc_ref)

    xn = xn_ref[...]
    gate = jnp.dot(xn, wg_ref[...], preferred_element_type=F32)
    up = jnp.dot(xn, wu_ref[...], preferred_element_type=F32)
    act = (gate * _sigmoid(gate)) * up
    acc_ref[...] += jnp.dot(act.astype(BF16), wd_ref[...], preferred_element_type=F32)

    @pl.when(f == pl.num_programs(1) - 1)
    def _():
        y = x_ref[...] + 0.5 * acc_ref[...]
        if final_norm:
            y = _rmsnorm(y, gf_ref[...])
        o_ref[...] = y


def _ffn(x, g, wg, wu, wd, g_final, *, final_norm, tm, tf):
    m, d = x.shape
    dff = wg.shape[1]
    return pl.pallas_call(
        functools.partial(_ffn_kernel, final_norm=final_norm),
        grid=(m // tm, dff // tf),
        in_specs=[
            pl.BlockSpec((tm, d), lambda i, f: (i, 0)),
            pl.BlockSpec((1, d), lambda i, f: (0, 0)),
            pl.BlockSpec((d, tf), lambda i, f: (0, f)),
            pl.BlockSpec((d, tf), lambda i, f: (0, f)),
            pl.BlockSpec((tf, d), lambda i, f: (f, 0)),
            pl.BlockSpec((1, d), lambda i, f: (0, 0)),
        ],
        out_specs=pl.BlockSpec((tm, d), lambda i, f: (i, 0)),
        out_shape=jax.ShapeDtypeStruct((m, d), F32),
        scratch_shapes=[pltpu.VMEM((tm, d), BF16), pltpu.VMEM((tm, d), F32)],
        compiler_params=_cparams("parallel", "arbitrary"),
        name="ffn",
    )(x, g, wg, wu, wd, g_final)


_TQ, _TK, _TV, _TQI, _TKI, _TWI, _TEND = 0, 512, 1024, 1536, 1792, 1856, 1872
_RK, _RKI, _RU, _REND = 0, 512, 640, 1152


def _proj_kernel(x_ref, g_ref, wt_ref, wr_ref, qt_ref, ktf_ref, vtf_ref, vt_ref, kh_ref,
                 qit_ref, kitf_ref, kib_ref, wit_ref, u_ref):
    h = _rmsnorm(x_ref[0], g_ref[...]).astype(BF16)
    pt = lax.dot_general(wt_ref[...], h, (((1,), (1,)), ((), ())), preferred_element_type=F32)
    pr = jnp.dot(h, wr_ref[...], preferred_element_type=F32)
    qt_ref[0] = (pt[_TQ:_TK] * (HEAD_DIM ** -0.5)).astype(BF16)
    ktf_ref[0] = pt[_TK:_TV]
    v = pt[_TV:_TQI]
    vtf_ref[0] = v
    vt_ref[0] = v.astype(BF16)
    qit_ref[0] = pt[_TQI:_TKI].astype(BF16)
    kitf_ref[0] = pt[_TKI:_TWI]
    wit_ref[0] = pt[_TWI:_TWI + 8] * ((IDX_HEADS * IDX_DIM) ** -0.5)
    k = pr[:, _RK:_RKI].astype(BF16)
    for hd in range(N_HEADS):
        kh_ref[0, hd] = k[:, hd * HEAD_DIM:(hd + 1) * HEAD_DIM]
    kib_ref[0] = pr[:, _RKI:_RKI + IDX_DIM].astype(BF16)
    u_ref[0] = pr[:, _RU:_REND]


def _proj(x, g, wt, wr, *, tm):
    b, t, d = x.shape
    feat = lambda n: pl.BlockSpec((1, n, tm), lambda bb, i: (bb, 0, i))
    tok = lambda n: pl.BlockSpec((1, tm, n), lambda bb, i: (bb, i, 0))
    outs = [
        (feat(D_ATTN), (b, D_ATTN, t), BF16), (feat(D_ATTN), (b, D_ATTN, t), F32),
        (feat(D_ATTN), (b, D_ATTN, t), F32), (feat(D_ATTN), (b, D_ATTN, t), BF16),
        (pl.BlockSpec((1, N_HEADS, tm, HEAD_DIM), lambda bb, i: (bb, 0, i, 0)), (b, N_HEADS, t, HEAD_DIM), BF16),
        (feat(IDX_HEADS * IDX_DIM), (b, IDX_HEADS * IDX_DIM, t), BF16), (feat(IDX_DIM), (b, IDX_DIM, t), F32),
        (tok(IDX_DIM), (b, t, IDX_DIM), BF16), (feat(8), (b, 8, t), F32), (tok(SSM_WIDTH), (b, t, SSM_WIDTH), F32),
    ]
    return pl.pallas_call(
        _proj_kernel,
        grid=(b, t // tm),
        in_specs=[tok(d), pl.BlockSpec((1, d), lambda bb, i: (0, 0)),
                  pl.BlockSpec((_TEND, d), lambda bb, i: (0, 0)), pl.BlockSpec((d, _REND), lambda bb, i: (0, 0))],
        out_specs=[spec for spec, _, _ in outs],
        out_shape=[jax.ShapeDtypeStruct(shape, dt) for _, shape, dt in outs],
        compiler_params=_cparams("parallel", "parallel"),
        name="proj",
    )(x, g, wt, wr)


def _scores_t(kc, qit, w):
    s = None
    for h in range(IDX_HEADS):
        d = jnp.dot(kc, qit[h], preferred_element_type=F32)
        term = jnp.maximum(d, 0.0) * w[h:h + 1, :]
        s = term if s is None else s + term
    return s


def _select_bias(s, vf, kidx, cut):
    tie = jnp.where(kidx < cut, 0.0, -jnp.inf)
    return jnp.where(s > vf, 0.0, jnp.where(s == vf, tie, -jnp.inf))


_FOLD_ROWS = 16


def _fold8(x, op):
    return op(x.reshape(x.shape[0] // _FOLD_ROWS, _FOLD_ROWS, x.shape[1]), axis=0)


_PROBES_PER_ROUND = 4
_SETTLE_FIRST, _SETTLE_EVERY, _BISECT_CAP = 4, 2, 80


def _topk_threshold(s_ref, n_chunks, tk, tq, mx, mn, n_hi, n_nc, k_top, thr_ref, cut_ref, fst_ref, ist_ref):
    row_iota = lax.broadcasted_iota(I32, (tk, tq), 0)
    zeros_i = jnp.zeros((_FOLD_ROWS, tq), I32)

    def chunk(c):
        k0 = pl.multiple_of(c * tk, tk)
        return s_ref[pl.ds(k0, tk), :], k0

    def count_gt(v):
        def body(c, acc):
            blk, _ = chunk(c)
            return acc + _fold8(jnp.where(blk > v, 1, 0), jnp.sum)
        real = jnp.sum(lax.fori_loop(0, n_chunks, body, zeros_i), axis=0, keepdims=True)
        return real + jnp.where(NEG_BIG > v, n_nc, 0)

    def count_gt_eq(v):
        def body(c, acc):
            blk, _ = chunk(c)
            return (acc[0] + _fold8(jnp.where(blk > v, 1, 0), jnp.sum),
                    acc[1] + _fold8(jnp.where(blk == v, 1, 0), jnp.sum))
        g, e = lax.fori_loop(0, n_chunks, body, (zeros_i, zeros_i))
        g = jnp.sum(g, axis=0, keepdims=True) + jnp.where(NEG_BIG > v, n_nc, 0)
        return g, jnp.sum(e, axis=0, keepdims=True)

    def max_le(v):
        def body(c, acc):
            blk, _ = chunk(c)
            return jnp.maximum(acc, _fold8(jnp.where(blk <= v, blk, -jnp.inf), jnp.max))
        m = jnp.max(lax.fori_loop(0, n_chunks, body, jnp.full((_FOLD_ROWS, tq), -jnp.inf, F32)),
                    axis=0, keepdims=True)
        return jnp.where((n_nc > 0) & (NEG_BIG <= v), jnp.maximum(m, NEG_BIG), m)

    def settle(m):
        gt, eq_real = count_gt_eq(m)
        eq = eq_real + jnp.where(m == NEG_BIG, n_nc, 0)
        found = (ist_ref[0:1, :] == 0) & (gt + eq >= k_top)
        need = k_top - gt
        fst_ref[2:3, :] = jnp.where(found, m, fst_ref[2:3, :])
        ist_ref[1:2, :] = jnp.where(found, need, ist_ref[1:2, :])
        ist_ref[2:3, :] = jnp.where(found, jnp.where(eq_real > need, 1, 0), ist_ref[2:3, :])
        ist_ref[3:4, :] = jnp.where(found, INT_MAX, ist_ref[3:4, :])
        ist_ref[0:1, :] = jnp.where(found, 1, ist_ref[0:1, :])

    low = n_hi < k_top
    hi0 = jnp.where(low, NEG_BIG, mx)
    fst_ref[0:1, :] = jnp.where(low, jnp.minimum(mn, NEG_BIG), jnp.maximum(mn, NEG_BIG))
    fst_ref[1:2, :] = hi0
    fst_ref[2:3, :] = hi0
    ist_ref[...] = jnp.zeros_like(ist_ref)
    settle(hi0)

    def probe(lo, hi, thr, done):
        mid = 0.5 * lo + 0.5 * hi
        c = count_gt(mid)
        live = done == 0
        hit = live & (c == k_top)
        stuck = jnp.where(live & ((mid <= lo) | (mid >= hi)), 2, 0)
        return (jnp.where(live & (c > k_top), mid, lo), jnp.where(live & (c < k_top), mid, hi),
                jnp.where(hit, mid, thr), jnp.where(hit, 1, done), stuck)

    def step(carry):
        it, _ = carry
        lo, hi, thr, done = fst_ref[0:1, :], fst_ref[1:2, :], fst_ref[2:3, :], ist_ref[0:1, :]
        stuck = jnp.zeros_like(done)
        for _ in range(_PROBES_PER_ROUND):
            lo, hi, thr, done, s = probe(lo, hi, thr, done)
            stuck = jnp.maximum(stuck, s)
        fst_ref[0:1, :] = lo
        fst_ref[1:2, :] = hi
        fst_ref[2:3, :] = thr
        ist_ref[0:1, :] = done
        code = jnp.max(jnp.maximum(stuck, 1 - done))
        due = (it >= _SETTLE_FIRST) & ((it - _SETTLE_FIRST) % _SETTLE_EVERY == 0)

        check = (code > 0) & (due | (code >= 2))

        @pl.when(check)
        def _():
            settle(max_le(hi))

        return it + 1, lax.cond(check, lambda: jnp.max(1 - ist_ref[0:1, :]), lambda: code)

    lax.while_loop(lambda carry: (carry[1] > 0) & (carry[0] < _BISECT_CAP), step,
                   (jnp.int32(0), jnp.max(1 - ist_ref[0:1, :])))
    thr_ref[0] = fst_ref[2:3, :]
    cut_ref[0] = ist_ref[3:4, :]

    @pl.when(jnp.max(ist_ref[2:3, :]) > 0)
    def _():
        thr = fst_ref[2:3, :]
        need = ist_ref[1:2, :].astype(F32)
        tri = jnp.where(lax.broadcasted_iota(I32, (tk, tk), 1) <= lax.broadcasted_iota(I32, (tk, tk), 0),
                        1.0, 0.0).astype(BF16)

        def body(c, carry):
            seen, cut = carry
            blk, k0 = chunk(c)
            tie = blk == thr
            rank = jnp.dot(tri, jnp.where(tie, 1.0, 0.0).astype(BF16), preferred_element_type=F32) + seen
            bound = jnp.where(tie, jnp.where(rank <= need, k0 + row_iota + 1, 0), 0)
            return rank[tk - 1:tk, :], jnp.maximum(cut, jnp.max(bound, axis=0, keepdims=True))

        _, cut = lax.fori_loop(0, n_chunks, body, (jnp.zeros((1, tq), F32), jnp.zeros((1, tq), I32)))
        cut_ref[0] = jnp.where(ist_ref[2:3, :] > 0, cut, ist_ref[3:4, :])


def _select_kernel(qit_ref, ki_ref, wit_ref, thr_ref, cut_ref, s_ref, fst_ref, ist_ref, *, tq, tk, n_keys, k_top):
    i = pl.program_id(1)
    q0 = i * tq
    n_chunks = (q0 + tq + tk - 1) // tk
    t_row = q0 + lax.broadcasted_iota(I32, (1, tq), 1)
    row_iota = lax.broadcasted_iota(I32, (tk, tq), 0)
    qit = qit_ref[0]
    w = wit_ref[0]

    def build(c, carry):
        mx, mn, n_hi = carry
        k0 = pl.multiple_of(c * tk, tk)
        s = _scores_t(ki_ref[0, pl.ds(k0, tk), :], qit, w)
        causal = (k0 + row_iota) <= t_row
        sm = jnp.where(causal, s, -jnp.inf)
        s_ref[pl.ds(k0, tk), :] = sm
        return (jnp.maximum(mx, _fold8(sm, jnp.max)),
                jnp.minimum(mn, _fold8(jnp.where(causal, s, jnp.inf), jnp.min)),
                n_hi + _fold8(jnp.where(sm > NEG_BIG, 1, 0), jnp.sum))

    mx, mn, n_hi = lax.fori_loop(
        0, n_chunks, build,
        (jnp.full((_FOLD_ROWS, tq), -jnp.inf, F32), jnp.full((_FOLD_ROWS, tq), jnp.inf, F32),
         jnp.zeros((_FOLD_ROWS, tq), I32)))
    n_nc = (n_keys - 1) - t_row
    _topk_threshold(s_ref, n_chunks, tk, tq, jnp.max(mx, axis=0, keepdims=True),
                    jnp.min(mn, axis=0, keepdims=True), jnp.sum(n_hi, axis=0, keepdims=True),
                    n_nc, k_top, thr_ref, cut_ref, fst_ref, ist_ref)


def _select(qit, ki, wit, *, tq, tk, k_top):
    b, _, _, t = qit.shape
    kern = functools.partial(_select_kernel, tq=tq, tk=tk, n_keys=t, k_top=k_top)
    return pl.pallas_call(
        kern,
        grid=(b, t // tq),
        in_specs=[
            pl.BlockSpec((1, IDX_HEADS, IDX_DIM, tq), lambda bb, i: (bb, 0, 0, i)),
            pl.BlockSpec((1, t, IDX_DIM), lambda bb, i: (bb, 0, 0)),
            pl.BlockSpec((1, 8, tq), lambda bb, i: (bb, 0, i)),
        ],
        out_specs=[pl.BlockSpec((1, 1, tq), lambda bb, i: (bb, 0, i)),
                   pl.BlockSpec((1, 1, tq), lambda bb, i: (bb, 0, i))],
        out_shape=[jax.ShapeDtypeStruct((b, 1, t), F32), jax.ShapeDtypeStruct((b, 1, t), I32)],
        scratch_shapes=[pltpu.VMEM((t, tq), F32), pltpu.VMEM((8, tq), F32), pltpu.VMEM((8, tq), I32)],
        compiler_params=_cparams("parallel", "parallel"),
        name="select",
    )(qit, ki, wit)


_ATTN_SUB_ROWS = 128


def _n_key_tiles(i, tq, tk):
    return (i * tq + tq + tk - 1) // tk


def _attn_kernel(qt_ref, k_ref, vt_ref, qit_ref, ki_ref, wit_ref, vf_ref, cut_ref, o_ref,
                 m_ref, l_ref, acc_ref, bias_ref, lga_ref, lgb_ref, pa_ref, pb_ref, *, tq, tk):
    i = pl.program_id(1)
    j = pl.program_id(2)
    nkt = _n_key_tiles(i, tq, tk)
    sub = _ATTN_SUB_ROWS
    chunks = [slice(c * sub, (c + 1) * sub) for c in range(tk // sub)]

    @pl.when(j == 0)
    def _():
        m_ref[...] = jnp.full_like(m_ref, -jnp.inf)
        l_ref[...] = jnp.zeros_like(l_ref)
        acc_ref[...] = jnp.zeros_like(acc_ref)

    @pl.when(j < nkt)
    def _():
        qit = qit_ref[0]
        w = wit_ref[0]
        vf = vf_ref[0]
        cut = cut_ref[0]
        t_row = i * tq + lax.broadcasted_iota(I32, (1, tq), 1)
        row = lax.broadcasted_iota(I32, (sub, tq), 0)
        for c, rows in enumerate(chunks):
            kidx = j * tk + c * sub + row
            bias = _select_bias(_scores_t(ki_ref[0, rows, :], qit, w), vf, kidx, cut)
            bias_ref[rows, :] = jnp.where(kidx <= t_row, bias, -jnp.inf)

        def logits(h, lg_ref):
            qt = qt_ref[0, h]
            part = jnp.full((8, tq), -jnp.inf, F32)
            for rows in chunks:
                lg = jnp.dot(k_ref[0, h, rows, :], qt, preferred_element_type=F32) + bias_ref[rows, :]
                lg_ref[rows, :] = lg
                part = jnp.maximum(part, jnp.max(lg.reshape(sub // 8, 8, tq), axis=0))
            return jnp.max(part, axis=0, keepdims=True)

        def absorb(h, lg_ref, p_ref, tile_max):
            hrow = pl.ds(h, 1)
            m_old = m_ref[hrow, :]
            m_new = jnp.maximum(m_old, tile_max)
            m_safe = jnp.where(m_new == -jnp.inf, 0.0, m_new)
            alpha = jnp.exp(m_old - m_safe)
            psum = jnp.zeros((8, tq), F32)
            for rows in chunks:
                p = jnp.exp(lg_ref[rows, :] - m_safe)
                psum = psum + jnp.sum(p.reshape(sub // 8, 8, tq), axis=0)
                p_ref[rows, :] = p.astype(BF16)
            l_ref[hrow, :] = alpha * l_ref[hrow, :] + jnp.sum(psum, axis=0, keepdims=True)
            acc_ref[h] = alpha * acc_ref[h] + jnp.dot(vt_ref[0, h], p_ref[...], preferred_element_type=F32)
            m_ref[hrow, :] = m_new

        def two_heads(t, max_a):
            h = 2 * t
            max_b = logits(h + 1, lgb_ref)
            absorb(h, lga_ref, pa_ref, max_a)
            max_a = logits(h + 2, lga_ref)
            absorb(h + 1, lgb_ref, pb_ref, max_b)
            return max_a

        max_a = lax.fori_loop(0, N_HEADS // 2 - 1, two_heads, logits(0, lga_ref))
        max_b = logits(N_HEADS - 1, lgb_ref)
        absorb(N_HEADS - 2, lga_ref, pa_ref, max_a)
        absorb(N_HEADS - 1, lgb_ref, pb_ref, max_b)

    @pl.when(j == nkt - 1)
    def _():
        for h in range(N_HEADS):
            o_ref[0, h] = (acc_ref[h] / l_ref[h:h + 1, :]).astype(o_ref.dtype)


def _attn(qt, k, vt, qit, ki, wit, vf, cut, *, tq, tk):
    b, _, _, t = qt.shape
    kj = lambda i, j: jnp.minimum(j, _n_key_tiles(i, tq, tk) - 1)
    return pl.pallas_call(
        functools.partial(_attn_kernel, tq=tq, tk=tk),
        grid=(b, t // tq, t // tk),
        in_specs=[
            pl.BlockSpec((1, N_HEADS, HEAD_DIM, tq), lambda bb, i, j: (bb, 0, 0, i)),
            pl.BlockSpec((1, N_HEADS, tk, HEAD_DIM), lambda bb, i, j: (bb, 0, kj(i, j), 0)),
            pl.BlockSpec((1, N_HEADS, HEAD_DIM, tk), lambda bb, i, j: (bb, 0, 0, kj(i, j))),
            pl.BlockSpec((1, IDX_HEADS, IDX_DIM, tq), lambda bb, i, j: (bb, 0, 0, i)),
            pl.BlockSpec((1, tk, IDX_DIM), lambda bb, i, j: (bb, kj(i, j), 0)),
            pl.BlockSpec((1, 8, tq), lambda bb, i, j: (bb, 0, i)),
            pl.BlockSpec((1, 1, tq), lambda bb, i, j: (bb, 0, i)),
            pl.BlockSpec((1, 1, tq), lambda bb, i, j: (bb, 0, i)),
        ],
        out_specs=pl.BlockSpec((1, N_HEADS, HEAD_DIM, tq), lambda bb, i, j: (bb, 0, 0, i)),
        out_shape=jax.ShapeDtypeStruct((b, N_HEADS, HEAD_DIM, t), BF16),
        scratch_shapes=[pltpu.VMEM((N_HEADS, tq), F32), pltpu.VMEM((N_HEADS, tq), F32),
                        pltpu.VMEM((N_HEADS, HEAD_DIM, tq), F32),
                        pltpu.VMEM((tk, tq), F32), pltpu.VMEM((tk, tq), F32), pltpu.VMEM((tk, tq), F32),
                        pltpu.VMEM((tk, tq), BF16), pltpu.VMEM((tk, tq), BF16)],
        compiler_params=_cparams("parallel", "parallel", "arbitrary"),
        name="attn",
    )(qt, k, vt, qit, ki, wit, vf, cut)


def _zoh(ar, ai, ldt):
    dt = jnp.exp(ldt)
    mag = jnp.exp(dt * ar)
    abr = mag * jnp.cos(dt * ai)
    abi = mag * jnp.sin(dt * ai)
    den = ar * ar + ai * ai
    nr = abr - 1.0
    ni = abi
    return abr, abi, (nr * ar + ni * ai) / den, (ni * ar - nr * ai) / den


def _disc_kernel(ar_ref, ai_ref, ldt_ref, ar16_ref, ai16_ref, ldt16_ref, br_ref, bi_ref,
                 pwr_ref, pwi_ref, bbr_ref, bbi_ref):
    abr, abi, _, _ = _zoh(ar_ref[...], ai_ref[...], ldt_ref[...])
    pr, pi = abr, abi
    for j in range(8):
        pwr_ref[j:j + 1, :] = pr
        pwi_ref[j:j + 1, :] = pi
        pr, pi = pr * abr - pi * abi, pr * abi + pi * abr
    _, _, fr, fi = _zoh(ar16_ref[...], ai16_ref[...], ldt16_ref[...])
    br = br_ref[...]
    bi = bi_ref[...]
    bbr_ref[...] = fr * br - fi * bi
    bbi_ref[...] = fr * bi + fi * br


def _discretize(a_re, a_im, log_dt, b_re, b_im):
    flat = lambda a: a.reshape(1, -1)
    ldt = jnp.broadcast_to(log_dt[:, None], (N_GROUPS, STATE_DIM))
    rep = lambda a: flat(jnp.broadcast_to(a[:, :, None], (N_GROUPS, STATE_DIM, GROUP_CH)))
    n16 = N_STATE * GROUP_CH
    pwr, pwi, bbr, bbi = pl.pallas_call(
        _disc_kernel,
        out_shape=[jax.ShapeDtypeStruct((8, N_STATE), F32), jax.ShapeDtypeStruct((8, N_STATE), F32),
                   jax.ShapeDtypeStruct((1, n16), F32), jax.ShapeDtypeStruct((1, n16), F32)],
        name="s5_discretize",
    )(flat(a_re), flat(a_im), flat(ldt), rep(a_re), rep(a_im), rep(ldt), flat(b_re), flat(b_im))
    return pwr, pwi, bbr.reshape(N_GROUPS, STATE_DIM, GROUP_CH), bbi.reshape(N_GROUPS, STATE_DIM, GROUP_CH)


def _block_diag_in(bb):
    eye = jnp.eye(N_GROUPS, dtype=bb.dtype)
    return jnp.einsum('gpc,gh->gchp', bb, eye).reshape(SSM_WIDTH, N_STATE)


def _block_diag_out(c):
    eye = jnp.eye(N_GROUPS, dtype=c.dtype)
    return jnp.einsum('gcp,gh->gphc', c, eye).reshape(N_STATE, SSM_WIDTH)


_LANE_CHUNK = 512


def _ssm_kernel(u_ref, bblk_ref, cblk_ref, pw_ref, d_ref, y_ref, xf_ref, x_ref, carry_ref, *, tt):
    ts = pl.program_id(1)

    @pl.when(ts == 0)
    def _():
        carry_ref[...] = jnp.zeros_like(carry_ref)

    u = u_ref[0]
    ub = u.astype(BF16)
    n_chunks = 2 * N_STATE // _LANE_CHUNK
    for c in range(n_chunks):
        cols = slice(c * _LANE_CHUNK, (c + 1) * _LANE_CHUNK)
        x_ref[:, cols] = jnp.dot(ub, bblk_ref[:, cols], preferred_element_type=F32)
    rid = lax.broadcasted_iota(I32, (8, _LANE_CHUNK), 0)

    def group(r, carry):
        r0 = pl.multiple_of(r * 8, 8)
        for c in range(N_STATE // _LANE_CHUNK):
            re = pl.ds(c * _LANE_CHUNK, _LANE_CHUNK)
            im = pl.ds(N_STATE + c * _LANE_CHUNK, _LANE_CHUNK)
            xr = x_ref[pl.ds(r0, 8), re]
            xi = x_ref[pl.ds(r0, 8), im]
            for d in (1, 2, 4):
                ar = pw_ref[d - 1:d, re]
                ai = pw_ref[d - 1:d, im]
                sr = jnp.where(rid >= d, pltpu.roll(xr, d, 0), 0.0)
                si = jnp.where(rid >= d, pltpu.roll(xi, d, 0), 0.0)
                xr, xi = xr + (ar * sr - ai * si), xi + (ar * si + ai * sr)
            cr = carry_ref[:, re]
            ci = carry_ref[:, im]
            pr = pw_ref[:, re]
            pi = pw_ref[:, im]
            xr, xi = xr + (pr * cr - pi * ci), xi + (pr * ci + pi * cr)
            x_ref[pl.ds(r0, 8), re] = xr
            x_ref[pl.ds(r0, 8), im] = xi
            carry_ref[:, re] = xr[7:8, :]
            carry_ref[:, im] = xi[7:8, :]
        return carry

    lax.fori_loop(0, tt // 8, group, 0)
    y = d_ref[...] * u
    for c in range(n_chunks):
        cols = slice(c * _LANE_CHUNK, (c + 1) * _LANE_CHUNK)
        y = y + jnp.dot(x_ref[:, cols].astype(BF16), cblk_ref[cols, :], preferred_element_type=F32)
    y_ref[0] = y
    xf_ref[0] = carry_ref[...]


def _ssm(u, bblk, cblk, pw, d_skip, *, tt):
    b, t, _ = u.shape
    return pl.pallas_call(
        functools.partial(_ssm_kernel, tt=tt),
        grid=(b, t // tt),
        in_specs=[
            pl.BlockSpec((1, tt, SSM_WIDTH), lambda bb, s: (bb, s, 0)),
            pl.BlockSpec((SSM_WIDTH, 2 * N_STATE), lambda bb, s: (0, 0)),
            pl.BlockSpec((2 * N_STATE, SSM_WIDTH), lambda bb, s: (0, 0)),
            pl.BlockSpec((8, 2 * N_STATE), lambda bb, s: (0, 0)),
            pl.BlockSpec((1, SSM_WIDTH), lambda bb, s: (0, 0)),
        ],
        out_specs=[pl.BlockSpec((1, tt, SSM_WIDTH), lambda bb, s: (bb, s, 0)),
                   pl.BlockSpec((1, 1, 2 * N_STATE), lambda bb, s: (bb, 0, 0))],
        out_shape=[jax.ShapeDtypeStruct((b, t, SSM_WIDTH), F32),
                   jax.ShapeDtypeStruct((b, 1, 2 * N_STATE), F32)],
        scratch_shapes=[pltpu.VMEM((tt, 2 * N_STATE), F32), pltpu.VMEM((1, 2 * N_STATE), F32)],
        compiler_params=_cparams("parallel", "arbitrary"),
        name="s5_scan",
    )(u, bblk, cblk, pw, d_skip)


def _ssm_step_kernel(u_ref, x0_ref, bblk_ref, cblk_ref, pw_ref, d_ref, y_ref, x_ref):
    u = u_ref[...]
    bu = jnp.dot(u, bblk_ref[...], preferred_element_type=F32, precision=lax.Precision.HIGHEST)
    ar = pw_ref[0:1, :N_STATE]
    ai = pw_ref[0:1, N_STATE:]
    x0r = x0_ref[:, :N_STATE]
    x0i = x0_ref[:, N_STATE:]
    xr = ar * x0r - ai * x0i + bu[:, :N_STATE]
    xi = ar * x0i + ai * x0r + bu[:, N_STATE:]
    x_ref[:, :N_STATE] = xr
    x_ref[:, N_STATE:] = xi
    y = jnp.dot(x_ref[...], cblk_ref[...], preferred_element_type=F32, precision=lax.Precision.HIGHEST)
    y_ref[...] = y + d_ref[...] * u


def _ssm_step(u, x0, bblk, cblk, pw, d_skip):
    n = u.shape[0]
    return pl.pallas_call(
        _ssm_step_kernel,
        out_shape=[jax.ShapeDtypeStruct((n, SSM_WIDTH), F32), jax.ShapeDtypeStruct((n, 2 * N_STATE), F32)],
        compiler_params=pltpu.CompilerParams(vmem_limit_bytes=VMEM_LIMIT_BYTES),
        name="s5_step",
    )(u, x0, bblk, cblk, pw, d_skip)


def _gelu_tanh(x):
    c = np.float32(np.sqrt(2.0 / np.pi))
    return 0.5 * x * (1.0 + jnp.tanh(c * (x + 0.044715 * (x * x * x))))


def _merge_kernel(x_ref, oa_ref, ys_ref, g_ref, wgt_ref, wglu_ref, bglu_ref, wpa_ref, wpb_ref, wout_ref, o_ref):
    x = x_ref[...]
    h = _rmsnorm(x, g_ref[...]).astype(BF16)
    gates = _sigmoid(jnp.dot(h, wgt_ref[...], preferred_element_type=F32))
    ys = _gelu_tanh(ys_ref[...])
    glu = jnp.dot(ys.astype(BF16), wglu_ref[...], preferred_element_type=F32) + bglu_ref[...]
    ob = ys * _sigmoid(glu)
    pa = jnp.dot(oa_ref[...], wpa_ref[...], preferred_element_type=F32)
    pb = jnp.dot(ob.astype(BF16), wpb_ref[...], preferred_element_type=F32)
    merged = gates[:, :D_MODEL] * pa + gates[:, D_MODEL:] * pb
    o_ref[...] = x + jnp.dot(merged.astype(BF16), wout_ref[...], preferred_element_type=F32)


def _merge(x, oa, ys, g, wgt, wglu, bglu, wpa, wpb, wout, *, tm):
    m, d = x.shape
    row = lambda n: pl.BlockSpec((tm, n), lambda i: (i, 0))
    full = lambda a: pl.BlockSpec(a.shape, lambda i: (0, 0))
    return pl.pallas_call(
        _merge_kernel,
        grid=(m // tm,),
        in_specs=[row(d), row(D_ATTN), row(SSM_WIDTH), full(g), full(wgt), full(wglu), full(bglu),
                  full(wpa), full(wpb), full(wout)],
        out_specs=row(d),
        out_shape=jax.ShapeDtypeStruct((m, d), F32),
        compiler_params=_cparams("parallel"),
        name="merge",
    )(x, oa, ys, g, wgt, wglu, bglu, wpa, wpb, wout)


_PAGES_PER_STEP = 8


def _page_specs(block, n_pages):
    def spec(r):
        def index_map(b, p, pt):
            return (pt[b, jnp.minimum(p * _PAGES_PER_STEP + r, n_pages - 1)],) + (0,) * (len(block) - 1)
        return pl.BlockSpec(block, index_map)
    return [spec(r) for r in range(_PAGES_PER_STEP)]


def _per_seq(shape):
    return pl.BlockSpec((1,) + shape, lambda b, p, pt: (b,) + (0,) * len(shape))


def _sample_scores_kernel(pt_ref, qi_ref, w_ref, knew_ref, *refs, page_size, n_steps):
    pages = refs[:_PAGES_PER_STEP]
    o_ref = refs[_PAGES_PER_STEP]
    p = pl.program_id(1)
    qi = qi_ref[0]
    w = w_ref[0]

    def score(kt):
        d = jnp.dot(qi, kt, preferred_element_type=F32)
        return jnp.sum(jnp.maximum(d, 0.0) * w, axis=0, keepdims=True)

    @pl.when(p < n_steps)
    def _():
        for r in range(_PAGES_PER_STEP):
            o_ref[0, :, r * page_size:(r + 1) * page_size] = score(pages[r][0].astype(BF16))

    @pl.when(p == n_steps)
    def _():
        o_ref[0] = jnp.full(o_ref.shape[1:], -jnp.inf, F32)
        lane = lax.broadcasted_iota(I32, (1, page_size), 1)
        o_ref[0, :, 0:page_size] = jnp.where(lane == 0, score(knew_ref[0]), -jnp.inf)


def _sample_scores(page_table, qi_rows, w_col, ki_new_t, kidx_t):
    db, n_pages = page_table.shape
    _, _, page_size = kidx_t.shape
    assert n_pages % _PAGES_PER_STEP == 0
    steps = n_pages // _PAGES_PER_STEP
    block = _PAGES_PER_STEP * page_size
    return pl.pallas_call(
        functools.partial(_sample_scores_kernel, page_size=page_size, n_steps=steps),
        grid_spec=pltpu.PrefetchScalarGridSpec(
            num_scalar_prefetch=1,
            grid=(db, steps + 1),
            in_specs=[_per_seq((8, IDX_DIM)), _per_seq((8, 1)), _per_seq((IDX_DIM, page_size))]
                     + _page_specs((1, IDX_DIM, page_size), n_pages),
            out_specs=pl.BlockSpec((1, 1, block), lambda b, p, pt: (b, 0, p)),
        ),
        out_shape=jax.ShapeDtypeStruct((db, 1, (steps + 1) * block), F32),
        compiler_params=_cparams("parallel", "arbitrary"),
        name="sample_scores",
    )(page_table, qi_rows, w_col, ki_new_t, *([kidx_t] * _PAGES_PER_STEP))


def _sample_select_kernel(st_ref, thr_ref, cut_ref, s_ref, fst_ref, ist_ref, *, tk, tq, n_keys, k_top):
    n_chunks = st_ref.shape[0] // tk
    row_iota = lax.broadcasted_iota(I32, (tk, tq), 0)

    def build(c, carry):
        mx, mn, n_hi = carry
        k0 = pl.multiple_of(c * tk, tk)
        s = st_ref[pl.ds(k0, tk), :]
        real = (k0 + row_iota) < n_keys
        sm = jnp.where(real, s, -jnp.inf)
        s_ref[pl.ds(k0, tk), :] = sm
        return (jnp.maximum(mx, _fold8(sm, jnp.max)),
                jnp.minimum(mn, _fold8(jnp.where(real, s, jnp.inf), jnp.min)),
                n_hi + _fold8(jnp.where(sm > NEG_BIG, 1, 0), jnp.sum))

    mx, mn, n_hi = lax.fori_loop(
        0, n_chunks, build,
        (jnp.full((_FOLD_ROWS, tq), -jnp.inf, F32), jnp.full((_FOLD_ROWS, tq), jnp.inf, F32),
         jnp.zeros((_FOLD_ROWS, tq), I32)))
    _topk_threshold(s_ref, n_chunks, tk, tq, jnp.max(mx, axis=0, keepdims=True),
                    jnp.min(mn, axis=0, keepdims=True), jnp.sum(n_hi, axis=0, keepdims=True),
                    jnp.zeros((1, tq), I32), k_top, thr_ref, cut_ref, fst_ref, ist_ref)


def _sample_select(scores_t, *, n_keys, k_top, tk):
    n_rows, tq = scores_t.shape
    kern = functools.partial(_sample_select_kernel, tk=tk, tq=tq, n_keys=n_keys, k_top=k_top)
    return pl.pallas_call(
        kern,
        out_shape=[jax.ShapeDtypeStruct((1, 1, tq), F32), jax.ShapeDtypeStruct((1, 1, tq), I32)],
        scratch_shapes=[pltpu.VMEM((n_rows, tq), F32), pltpu.VMEM((8, tq), F32), pltpu.VMEM((8, tq), I32)],
        compiler_params=pltpu.CompilerParams(vmem_limit_bytes=VMEM_LIMIT_BYTES),
        name="sample_select",
    )(scores_t)


def _sample_attn_kernel(pt_ref, qb_ref, s_ref, thr_ref, cut_ref, kself_ref, vself_ref, *refs,
                        page_size, n_steps):
    kp = refs[:_PAGES_PER_STEP]
    vp = refs[_PAGES_PER_STEP:2 * _PAGES_PER_STEP]
    o_ref, m_ref, l_ref, acc_ref = refs[2 * _PAGES_PER_STEP:]
    step = pl.program_id(1)
    thr = thr_ref[0]
    cut = cut_ref[0]
    lane = lax.broadcasted_iota(I32, (1, page_size), 1)
    is_self = step == n_steps

    @pl.when(step == 0)
    def _():
        m_ref[...] = jnp.full_like(m_ref, -jnp.inf)
        l_ref[...] = jnp.zeros_like(l_ref)
        acc_ref[...] = jnp.zeros_like(acc_ref)

    logits = []
    for r in range(_PAGES_PER_STEP):
        kidx = (step * _PAGES_PER_STEP + r) * page_size + lane
        bias = _select_bias(s_ref[0, :, r * page_size:(r + 1) * page_size], thr, kidx, cut)
        rows = []
        for h in range(N_HEADS):
            kt = kp[r][0, h]
            if r == 0:
                kt = jnp.where(is_self, kself_ref[0, h], kt)
            rows.append(jnp.sum(kt * qb_ref[0, h], axis=0, keepdims=True))
        logits.append(jnp.concatenate(rows, axis=0) + bias)

    m_old = m_ref[...]
    m_new = m_old
    for lg in logits:
        m_new = jnp.maximum(m_new, jnp.max(lg, axis=1, keepdims=True))
    m_safe = jnp.where(m_new == -jnp.inf, 0.0, m_new)
    alpha = jnp.exp(m_old - m_safe)
    probs = [jnp.exp(lg - m_safe) for lg in logits]
    l_new = alpha * l_ref[...]
    for p in probs:
        l_new = l_new + jnp.sum(p, axis=1, keepdims=True)
    l_ref[...] = l_new
    m_ref[...] = m_new
    for h in range(N_HEADS):
        acc = acc_ref[h] * alpha[h:h + 1, :]
        for r in range(_PAGES_PER_STEP):
            vt = vp[r][0, h]
            if r == 0:
                vt = jnp.where(is_self, vself_ref[0, h], vt)
            acc = acc + vt * probs[r][h:h + 1, :]
        acc_ref[h] = acc

    @pl.when(is_self)
    def _():
        for h in range(N_HEADS):
            o_ref[0, h] = jnp.sum(acc_ref[h], axis=1, keepdims=True) / l_ref[h:h + 1, :]


def _sample_attn(page_table, qb, scores, thr, cut, k_self, v_self, ck_t, cv_t):
    db, n_pages = page_table.shape
    _, nh, hd, page_size = ck_t.shape
    steps = n_pages // _PAGES_PER_STEP
    block = _PAGES_PER_STEP * page_size
    page = (1, nh, hd, page_size)
    return pl.pallas_call(
        functools.partial(_sample_attn_kernel, page_size=page_size, n_steps=steps),
        grid_spec=pltpu.PrefetchScalarGridSpec(
            num_scalar_prefetch=1,
            grid=(db, steps + 1),
            in_specs=[_per_seq(page[1:]),
                      pl.BlockSpec((1, 1, block), lambda b, p, pt: (b, 0, p)),
                      _per_seq((1, 1)), _per_seq((1, 1)), _per_seq(page[1:]), _per_seq(page[1:])]
                     + _page_specs(page, n_pages) + _page_specs(page, n_pages),
            out_specs=_per_seq((nh, hd, 1)),
            scratch_shapes=[pltpu.VMEM((nh, 1), F32), pltpu.VMEM((nh, 1), F32),
                            pltpu.VMEM((nh, hd, page_size), F32)],
        ),
        out_shape=jax.ShapeDtypeStruct((db, nh, hd, 1), F32),
        compiler_params=_cparams("parallel", "arbitrary"),
        name="sample_attn",
    )(page_table, qb, scores, thr, cut, k_self, v_self,
      *([ck_t] * _PAGES_PER_STEP), *([cv_t] * _PAGES_PER_STEP))


def _pick(n, pref):
    for t in pref:
        if n % t == 0:
            return t
    return n


def _pack_w_in(w_in):
    o = np.cumsum([0, D_ATTN, D_ATTN, D_ATTN, IDX_HEADS * IDX_DIM, IDX_DIM, IDX_HEADS, SSM_WIDTH, 2 * D_MODEL])
    z = lambda n: jnp.zeros((D_MODEL, n), w_in.dtype)
    w_feat = jnp.concatenate([w_in[:, o[0]:o[6]], z(_TEND - _TWI - IDX_HEADS)], axis=1)
    w_tok = jnp.concatenate([w_in[:, o[1]:o[2]], w_in[:, o[4]:o[5]], z(_RU - _RKI - IDX_DIM),
                             w_in[:, o[6]:o[7]]], axis=1)
    return w_feat.T.astype(BF16), w_tok.astype(BF16), w_in[:, o[7]:o[8]].astype(BF16)


def _prompt_layer(x, lw, ssm, final_norm):
    bsz, seq, d = x.shape
    m = bsz * seq
    tm = _pick(m, (512, 256, 128, 64, 32, 16, 8))
    x1 = _ffn(x.reshape(m, d), *lw["ffn1"], lw["g_final"], final_norm=False, tm=tm, tf=lw["tf"])
    qt, kt_f, vt_f, vt, kh, qit, kit_f, kib, wit, u_f = _proj(
        x1.reshape(bsz, seq, d), lw["g_mix"], lw["w_feat"], lw["w_tok"], tm=_pick(seq, (512, 256, 128)))
    heads = lambda a, nh: a.reshape(bsz, nh, -1, seq)
    qt, vt, qit = heads(qt, N_HEADS), heads(vt, N_HEADS), heads(qit, IDX_HEADS)

    k_top = min(TOPK_MAX, seq // 4)
    vf, cut = _select(qit, kib, wit, tq=_pick(seq, (512, 256, 128)), tk=_pick(seq, (256, 128)), k_top=k_top)
    o_t = _attn(qt, kh, vt, qit, kib, wit, vf, cut,
                tq=_pick(seq, (512, 256, 128)), tk=_pick(seq, (512, 256, 128)))
    oa = o_t.transpose(0, 3, 1, 2).reshape(m, D_ATTN)

    tt = _pick(seq, (256, 128, 64, 32, 16, 8))
    ys, xfin = _ssm(u_f, ssm["bblk"].astype(BF16), ssm["cblk"].astype(BF16), ssm["pw"], ssm["d_skip"], tt=tt)
    x2 = _merge(x1, oa, ys.reshape(m, SSM_WIDTH), *lw["merge"], tm=tm)
    y = _ffn(x2, *lw["ffn2"], lw["g_final"], final_norm=final_norm, tm=tm, tf=lw["tf"])
    new = (heads(kt_f, N_HEADS).transpose(0, 3, 1, 2), heads(vt_f, N_HEADS).transpose(0, 3, 1, 2),
           kit_f.transpose(0, 2, 1),
           xfin[:, 0, :N_STATE].reshape(bsz, N_GROUPS, STATE_DIM),
           xfin[:, 0, N_STATE:].reshape(bsz, N_GROUPS, STATE_DIM))
    return y.reshape(bsz, seq, d), new


def _sample_layer(x, lw, ssm, final_norm, cache_k, cache_v, cache_kidx, st_re, st_im, page_table):
    db, ds, d = x.shape
    assert ds == 1, "one new token per sample sequence"
    n_pool, page_size = cache_k.shape[0], cache_k.shape[1]
    past_len = page_table.shape[1] * page_size
    s1 = _ffn(x.reshape(db, d), *lw["ffn1"], lw["g_final"], final_norm=False, tm=db, tf=lw["tf"])
    qt, kt_f, vt_f, _, _, qit, kit_f, kib, wit, u_f = _proj(
        s1.reshape(1, db, d), lw["g_mix"], lw["w_feat"], lw["w_tok"], tm=db)
    q_b, k_f, v_f, qi_b, ki_f, wi_f = (a[0].T for a in (qt, kt_f, vt_f, qit, kit_f, wit))
    ki_b, u_f = kib[0], u_f[0]

    kidx_t = cache_kidx.transpose(0, 2, 1)
    ck_t = cache_k.transpose(0, 2, 3, 1)
    cv_t = cache_v.transpose(0, 2, 3, 1)
    lane0 = lambda a: jnp.zeros(a.shape + (page_size,), a.dtype).at[..., 0].set(a)

    qi_rows = jnp.zeros((db, 8, IDX_DIM), BF16).at[:, :IDX_HEADS].set(qi_b.reshape(db, IDX_HEADS, IDX_DIM))
    scores = _sample_scores(page_table, qi_rows, wi_f[:, :, None], lane0(ki_b), kidx_t)

    n_keys = past_len + 1
    k_top = min(TOPK_MAX, n_keys // 4)
    n = scores.shape[2]
    scores_t = jnp.zeros((n, -(-db // 128) * 128), F32).at[:, :db].set(scores[:, 0, :].T)
    thr, cut = _sample_select(scores_t, n_keys=n_keys, k_top=k_top, tk=256)

    qb = jnp.broadcast_to(q_b.astype(F32).reshape(db, N_HEADS, HEAD_DIM, 1), (db, N_HEADS, HEAD_DIM, page_size))
    o_s = _sample_attn(page_table, qb, scores, thr[0, 0, :db].reshape(db, 1, 1), cut[0, 0, :db].reshape(db, 1, 1),
                       lane0(k_f.reshape(db, N_HEADS, HEAD_DIM)), lane0(v_f.reshape(db, N_HEADS, HEAD_DIM)),
                       ck_t, cv_t)
    oa = o_s.reshape(db, D_ATTN).astype(BF16)

    x0 = jnp.concatenate([st_re.reshape(db, N_STATE), st_im.reshape(db, N_STATE)], axis=1)
    ys, x_new = _ssm_step(u_f, x0, ssm["bblk"], ssm["cblk"], ssm["pw"], ssm["d_skip"])
    s2 = _merge(s1, oa, ys, *lw["merge"], tm=db)
    y = _ffn(s2, *lw["ffn2"], lw["g_final"], final_norm=final_norm, tm=db, tf=lw["tf"])
    new = (k_f.reshape(db, ds, N_HEADS, HEAD_DIM), v_f.reshape(db, ds, N_HEADS, HEAD_DIM),
           ki_f.reshape(db, ds, IDX_DIM),
           x_new[:, :N_STATE].reshape(db, N_GROUPS, STATE_DIM),
           x_new[:, N_STATE:].reshape(db, N_GROUPS, STATE_DIM))
    return y.reshape(db, ds, d), new


def kernel(x_prompt, x_sample, cache_k, cache_v, cache_kidx, state_ssm_re, state_ssm_im, page_table,
           g_ffn1, w1_gate, w1_up, w1_down, g_mix, w_in, a_re, a_im, log_dt, b_re, b_im, c_re, c_im,
           d_skip, w_glu, b_glu, w_pa, w_pb, w_out, g_ffn2, w2_gate, w2_up, w2_down, g_final):
    depth = w_in.shape[0]
    row = lambda a: a.reshape(1, -1)
    bf = lambda a: a.astype(BF16)
    xp, xs = x_prompt, x_sample
    new_p, new_s = [], []
    for l in range(depth):
        w_feat, w_tok, w_gates = _pack_w_in(w_in[l])
        lw = dict(
            ffn1=(row(g_ffn1[l]), bf(w1_gate[l]), bf(w1_up[l]), bf(w1_down[l])),
            ffn2=(row(g_ffn2[l]), bf(w2_gate[l]), bf(w2_up[l]), bf(w2_down[l])),
            merge=(row(g_mix[l]), w_gates, bf(w_glu[l]), row(b_glu[l]), bf(w_pa[l]), bf(w_pb[l]), bf(w_out[l])),
            g_mix=row(g_mix[l]), w_feat=w_feat, w_tok=w_tok, g_final=row(g_final),
            tf=_pick(w1_gate.shape[2], (1408, 1024, 512, 256, 128)),
        )
        pwr, pwi, bbr, bbi = _discretize(a_re[l], a_im[l], log_dt[l], b_re[l], b_im[l])
        ssm = dict(
            pw=jnp.concatenate([pwr, pwi], axis=1),
            bblk=jnp.concatenate([_block_diag_in(bbr), _block_diag_in(bbi)], axis=1),
            cblk=jnp.concatenate([_block_diag_out(c_re[l]), -_block_diag_out(c_im[l])], axis=0),
            d_skip=row(d_skip[l]),
        )
        last = l == depth - 1
        xp, st_p = _prompt_layer(xp, lw, ssm, last)
        xs, st_s = _sample_layer(xs, lw, ssm, last, cache_k[l], cache_v[l], cache_kidx[l],
                                 state_ssm_re[l], state_ssm_im[l], page_table)
        new_p.append(st_p)
        new_s.append(st_s)
    stack = lambda states, i: jnp.stack([s[i] for s in states])
    return (xp, xs) + tuple(stack(new_p, i) for i in range(5)) + tuple(stack(new_s, i) for i in range(5))
```

```python
import functools

import numpy as np
import jax
import jax.numpy as jnp
from jax import lax
from jax.experimental import pallas as pl
from jax.experimental.pallas import tpu as pltpu

F32 = jnp.float32
BF16 = jnp.bfloat16
I32 = jnp.int32

D_MODEL = 1024
N_HEADS = 8
HEAD_DIM = 64
D_ATTN = N_HEADS * HEAD_DIM
IDX_HEADS = 4
IDX_DIM = 64
TOPK_MAX = 256
SSM_WIDTH = 512
GROUP_CH = 16
N_GROUPS = SSM_WIDTH // GROUP_CH
STATE_DIM = 64
N_STATE = N_GROUPS * STATE_DIM
NORM_EPS = 1e-6
NEG_BIG = -1e30

INT_MAX = 2 ** 31 - 1

VMEM_LIMIT_BYTES = 56 * 1024 * 1024


def _cparams(*sem):
    return pltpu.CompilerParams(dimension_semantics=sem, vmem_limit_bytes=VMEM_LIMIT_BYTES)


def _rmsnorm(x, g):
    y = x * lax.rsqrt(jnp.mean(x * x, axis=-1, keepdims=True) + NORM_EPS)
    return y * g


def _sigmoid(x):
    return 1.0 / (1.0 + jnp.exp(-x))


def _ffn_kernel(x_ref, g_ref, wg_ref, wu_ref, wd_ref, gf_ref, o_ref, *, final_norm, tf):
    x = x_ref[...]
    xn = _rmsnorm(x, g_ref[...]).astype(BF16)
    acc = None
    for f0 in range(0, wg_ref.shape[1], tf):
        gate = jnp.dot(xn, wg_ref[:, f0:f0 + tf], preferred_element_type=F32)
        up = jnp.dot(xn, wu_ref[:, f0:f0 + tf], preferred_element_type=F32)
        act = (gate * _sigmoid(gate)) * up
        part = jnp.dot(act.astype(BF16), wd_ref[f0:f0 + tf, :], preferred_element_type=F32)
        acc = part if acc is None else acc + part
    y = x + 0.5 * acc
    if final_norm:
        y = _rmsnorm(y, gf_ref[...])
    o_ref[...] = y


def _ffn(x, g, wg, wu, wd, g_final, *, final_norm, tm, tf):
    m, d = x.shape
    resident = lambda a: pl.BlockSpec(a.shape, lambda i: (0, 0), pipeline_mode=pl.Buffered(1))
    return pl.pallas_call(
        functools.partial(_ffn_kernel, final_norm=final_norm, tf=tf),
        grid=(m // tm,),
        in_specs=[pl.BlockSpec((tm, d), lambda i: (i, 0)), resident(g), resident(wg), resident(wu),
                  resident(wd), resident(g_final)],
        out_specs=pl.BlockSpec((tm, d), lambda i: (i, 0)),
        out_shape=jax.ShapeDtypeStruct((m, d), F32),
        compiler_params=_cparams("parallel"),
        name="ffn",
    )(x, g, wg, wu, wd, g_final)


_TQ, _TK, _TV, _TQI, _TKI, _TWI, _TEND = 0, 512, 1024, 1536, 1792, 1856, 1872
_RK, _RKI, _RU, _REND = 0, 512, 640, 1152


def _proj_kernel(x_ref, g_ref, wt_ref, wr_ref, qt_ref, ktf_ref, vtf_ref, vt_ref, kh_ref,
                 qit_ref, kitf_ref, kib_ref, wit_ref, u_ref):
    h = _rmsnorm(x_ref[0], g_ref[...]).astype(BF16)
    pt = lax.dot_general(wt_ref[...], h, (((1,), (1,)), ((), ())), preferred_element_type=F32)
    pr = jnp.dot(h, wr_ref[...], preferred_element_type=F32)
    qt_ref[0] = (pt[_TQ:_TK] * (HEAD_DIM ** -0.5)).astype(BF16)
    ktf_ref[0] = pt[_TK:_TV]
    v = pt[_TV:_TQI]
    vtf_ref[0] = v
    vt_ref[0] = v.astype(BF16)
    qit_ref[0] = pt[_TQI:_TKI].astype(BF16)
    kitf_ref[0] = pt[_TKI:_TWI]
    wit_ref[0] = pt[_TWI:_TWI + 8] * ((IDX_HEADS * IDX_DIM) ** -0.5)
    k = pr[:, _RK:_RKI].astype(BF16)
    for hd in range(N_HEADS):
        kh_ref[0, hd] = k[:, hd * HEAD_DIM:(hd + 1) * HEAD_DIM]
    kib_ref[0] = pr[:, _RKI:_RKI + IDX_DIM].astype(BF16)
    u_ref[0] = pr[:, _RU:_REND]


def _proj(x, g, wt, wr, *, tm):
    b, t, d = x.shape
    feat = lambda n: pl.BlockSpec((1, n, tm), lambda bb, i: (bb, 0, i))
    tok = lambda n: pl.BlockSpec((1, tm, n), lambda bb, i: (bb, i, 0))
    outs = [
        (feat(D_ATTN), (b, D_ATTN, t), BF16), (feat(D_ATTN), (b, D_ATTN, t), F32),
        (feat(D_ATTN), (b, D_ATTN, t), F32), (feat(D_ATTN), (b, D_ATTN, t), BF16),
        (pl.BlockSpec((1, N_HEADS, tm, HEAD_DIM), lambda bb, i: (bb, 0, i, 0)), (b, N_HEADS, t, HEAD_DIM), BF16),
        (feat(IDX_HEADS * IDX_DIM), (b, IDX_HEADS * IDX_DIM, t), BF16), (feat(IDX_DIM), (b, IDX_DIM, t), F32),
        (tok(IDX_DIM), (b, t, IDX_DIM), BF16), (feat(8), (b, 8, t), F32), (tok(SSM_WIDTH), (b, t, SSM_WIDTH), F32),
    ]
    return pl.pallas_call(
        _proj_kernel,
        grid=(b, t // tm),
        in_specs=[tok(d), pl.BlockSpec((1, d), lambda bb, i: (0, 0)),
                  pl.BlockSpec((_TEND, d), lambda bb, i: (0, 0)), pl.BlockSpec((d, _REND), lambda bb, i: (0, 0))],
        out_specs=[spec for spec, _, _ in outs],
        out_shape=[jax.ShapeDtypeStruct(shape, dt) for _, shape, dt in outs],
        compiler_params=_cparams("parallel", "parallel"),
        name="proj",
    )(x, g, wt, wr)


def _scores_t(kc, qit, w):
    s = None
    for h in range(IDX_HEADS):
        d = jnp.dot(kc, qit[h], preferred_element_type=F32)
        term = jnp.maximum(d, 0.0) * w[h:h + 1, :]
        s = term if s is None else s + term
    return s


def _select_bias(s, vf, kidx, cut):
    tie = jnp.where(kidx < cut, 0.0, -jnp.inf)
    return jnp.where(s > vf, 0.0, jnp.where(s == vf, tie, -jnp.inf))


_FOLD_ROWS = 16


def _fold8(x, op):
    return op(x.reshape(x.shape[0] // _FOLD_ROWS, _FOLD_ROWS, x.shape[1]), axis=0)


_PROBES_PER_ROUND = 4
_SETTLE_FIRST, _SETTLE_EVERY, _BISECT_CAP = 4, 2, 80


def _topk_threshold(s_ref, n_chunks, tk, tq, mx, mn, n_hi, n_nc, k_top, thr_ref, cut_ref, fst_ref, ist_ref):
    row_iota = lax.broadcasted_iota(I32, (tk, tq), 0)
    zeros_i = jnp.zeros((_FOLD_ROWS, tq), I32)

    def chunk(c):
        k0 = pl.multiple_of(c * tk, tk)
        return s_ref[pl.ds(k0, tk), :], k0

    def count_gt(v):
        def body(c, acc):
            blk, _ = chunk(c)
            return acc + _fold8(jnp.where(blk > v, 1, 0), jnp.sum)
        real = jnp.sum(lax.fori_loop(0, n_chunks, body, zeros_i), axis=0, keepdims=True)
        return real + jnp.where(NEG_BIG > v, n_nc, 0)

    def count_gt_eq(v):
        def body(c, acc):
            blk, _ = chunk(c)
            return (acc[0] + _fold8(jnp.where(blk > v, 1, 0), jnp.sum),
                    acc[1] + _fold8(jnp.where(blk == v, 1, 0), jnp.sum))
        g, e = lax.fori_loop(0, n_chunks, body, (zeros_i, zeros_i))
        g = jnp.sum(g, axis=0, keepdims=True) + jnp.where(NEG_BIG > v, n_nc, 0)
        return g, jnp.sum(e, axis=0, keepdims=True)

    def max_le(v):
        def body(c, acc):
            blk, _ = chunk(c)
            return jnp.maximum(acc, _fold8(jnp.where(blk <= v, blk, -jnp.inf), jnp.max))
        m = jnp.max(lax.fori_loop(0, n_chunks, body, jnp.full((_FOLD_ROWS, tq), -jnp.inf, F32)),
                    axis=0, keepdims=True)
        return jnp.where((n_nc > 0) & (NEG_BIG <= v), jnp.maximum(m, NEG_BIG), m)

    def settle(m):
        gt, eq_real = count_gt_eq(m)
        eq = eq_real + jnp.where(m == NEG_BIG, n_nc, 0)
        found = (ist_ref[0:1, :] == 0) & (gt + eq >= k_top)
        need = k_top - gt
        fst_ref[2:3, :] = jnp.where(found, m, fst_ref[2:3, :])
        ist_ref[1:2, :] = jnp.where(found, need, ist_ref[1:2, :])
        ist_ref[2:3, :] = jnp.where(found, jnp.where(eq_real > need, 1, 0), ist_ref[2:3, :])
        ist_ref[3:4, :] = jnp.where(found, INT_MAX, ist_ref[3:4, :])
        ist_ref[0:1, :] = jnp.where(found, 1, ist_ref[0:1, :])

    low = n_hi < k_top
    hi0 = jnp.where(low, NEG_BIG, mx)
    fst_ref[0:1, :] = jnp.where(low, jnp.minimum(mn, NEG_BIG), jnp.maximum(mn, NEG_BIG))
    fst_ref[1:2, :] = hi0
    fst_ref[2:3, :] = hi0
    ist_ref[...] = jnp.zeros_like(ist_ref)
    settle(hi0)

    def probe(lo, hi, thr, done):
        mid = 0.5 * lo + 0.5 * hi
        c = count_gt(mid)
        live = done == 0
        hit = live & (c == k_top)
        stuck = jnp.where(live & ((mid <= lo) | (mid >= hi)), 2, 0)
        return (jnp.where(live & (c > k_top), mid, lo), jnp.where(live & (c < k_top), mid, hi),
                jnp.where(hit, mid, thr), jnp.where(hit, 1, done), stuck)

    def step(carry):
        it, _ = carry
        lo, hi, thr, done = fst_ref[0:1, :], fst_ref[1:2, :], fst_ref[2:3, :], ist_ref[0:1, :]
        stuck = jnp.zeros_like(done)
        for _ in range(_PROBES_PER_ROUND):
            lo, hi, thr, done, s = probe(lo, hi, thr, done)
            stuck = jnp.maximum(stuck, s)
        fst_ref[0:1, :] = lo
        fst_ref[1:2, :] = hi
        fst_ref[2:3, :] = thr
        ist_ref[0:1, :] = done
        code = jnp.max(jnp.maximum(stuck, 1 - done))
        due = (it >= _SETTLE_FIRST) & ((it - _SETTLE_FIRST) % _SETTLE_EVERY == 0)

        check = (code > 0) & (due | (code >= 2))

        @pl.when(check)
        def _():
            settle(max_le(hi))

        return it + 1, lax.cond(check, lambda: jnp.max(1 - ist_ref[0:1, :]), lambda: code)

    lax.while_loop(lambda carry: (carry[1] > 0) & (carry[0] < _BISECT_CAP), step,
                   (jnp.int32(0), jnp.max(1 - ist_ref[0:1, :])))
    thr_ref[0] = fst_ref[2:3, :]
    cut_ref[0] = ist_ref[3:4, :]

    @pl.when(jnp.max(ist_ref[2:3, :]) > 0)
    def _():
        thr = fst_ref[2:3, :]
        need = ist_ref[1:2, :].astype(F32)
        tri = jnp.where(lax.broadcasted_iota(I32, (tk, tk), 1) <= lax.broadcasted_iota(I32, (tk, tk), 0),
                        1.0, 0.0).astype(BF16)

        def body(c, carry):
            seen, cut = carry
            blk, k0 = chunk(c)
            tie = blk == thr
            rank = jnp.dot(tri, jnp.where(tie, 1.0, 0.0).astype(BF16), preferred_element_type=F32) + seen
            bound = jnp.where(tie, jnp.where(rank <= need, k0 + row_iota + 1, 0), 0)
            return rank[tk - 1:tk, :], jnp.maximum(cut, jnp.max(bound, axis=0, keepdims=True))

        _, cut = lax.fori_loop(0, n_chunks, body, (jnp.zeros((1, tq), F32), jnp.zeros((1, tq), I32)))
        cut_ref[0] = jnp.where(ist_ref[2:3, :] > 0, cut, ist_ref[3:4, :])


def _select_kernel(qit_ref, ki_ref, wit_ref, thr_ref, cut_ref, s_ref, fst_ref, ist_ref, *, tq, tk, n_keys, k_top):
    i = pl.program_id(1)
    q0 = i * tq
    n_chunks = (q0 + tq + tk - 1) // tk
    t_row = q0 + lax.broadcasted_iota(I32, (1, tq), 1)
    row_iota = lax.broadcasted_iota(I32, (tk, tq), 0)
    qit = qit_ref[0]
    w = wit_ref[0]

    def build(c, carry):
        mx, mn, n_hi = carry
        k0 = pl.multiple_of(c * tk, tk)
        s = _scores_t(ki_ref[0, pl.ds(k0, tk), :], qit, w)
        causal = (k0 + row_iota) <= t_row
        sm = jnp.where(causal, s, -jnp.inf)
        s_ref[pl.ds(k0, tk), :] = sm
        return (jnp.maximum(mx, _fold8(sm, jnp.max)),
                jnp.minimum(mn, _fold8(jnp.where(causal, s, jnp.inf), jnp.min)),
                n_hi + _fold8(jnp.where(sm > NEG_BIG, 1, 0), jnp.sum))

    mx, mn, n_hi = lax.fori_loop(
        0, n_chunks, build,
        (jnp.full((_FOLD_ROWS, tq), -jnp.inf, F32), jnp.full((_FOLD_ROWS, tq), jnp.inf, F32),
         jnp.zeros((_FOLD_ROWS, tq), I32)))
    n_nc = (n_keys - 1) - t_row
    _topk_threshold(s_ref, n_chunks, tk, tq, jnp.max(mx, axis=0, keepdims=True),
                    jnp.min(mn, axis=0, keepdims=True), jnp.sum(n_hi, axis=0, keepdims=True),
                    n_nc, k_top, thr_ref, cut_ref, fst_ref, ist_ref)


def _select(qit, ki, wit, *, tq, tk, k_top):
    b, _, _, t = qit.shape
    kern = functools.partial(_select_kernel, tq=tq, tk=tk, n_keys=t, k_top=k_top)
    return pl.pallas_call(
        kern,
        grid=(b, t // tq),
        in_specs=[
            pl.BlockSpec((1, IDX_HEADS, IDX_DIM, tq), lambda bb, i: (bb, 0, 0, i)),
            pl.BlockSpec((1, t, IDX_DIM), lambda bb, i: (bb, 0, 0)),
            pl.BlockSpec((1, 8, tq), lambda bb, i: (bb, 0, i)),
        ],
        out_specs=[pl.BlockSpec((1, 1, tq), lambda bb, i: (bb, 0, i)),
                   pl.BlockSpec((1, 1, tq), lambda bb, i: (bb, 0, i))],
        out_shape=[jax.ShapeDtypeStruct((b, 1, t), F32), jax.ShapeDtypeStruct((b, 1, t), I32)],
        scratch_shapes=[pltpu.VMEM((t, tq), F32), pltpu.VMEM((8, tq), F32), pltpu.VMEM((8, tq), I32)],
        compiler_params=_cparams("parallel", "parallel"),
        name="select",
    )(qit, ki, wit)


_ATTN_SUB_ROWS = 128


def _n_key_tiles(i, tq, tk):
    return (i * tq + tq + tk - 1) // tk


def _attn_kernel(qt_ref, k_ref, vt_ref, qit_ref, ki_ref, wit_ref, vf_ref, cut_ref, o_ref,
                 m_ref, l_ref, acc_ref, bias_ref, lga_ref, lgb_ref, pa_ref, pb_ref, *, tq, tk):
    i = pl.program_id(1)
    j = pl.program_id(2)
    nkt = _n_key_tiles(i, tq, tk)
    sub = _ATTN_SUB_ROWS
    chunks = [slice(c * sub, (c + 1) * sub) for c in range(tk // sub)]

    @pl.when(j == 0)
    def _():
        m_ref[...] = jnp.full_like(m_ref, -jnp.inf)
        l_ref[...] = jnp.zeros_like(l_ref)
        acc_ref[...] = jnp.zeros_like(acc_ref)

    @pl.when(j < nkt)
    def _():
        qit = qit_ref[0]
        w = wit_ref[0]
        vf = vf_ref[0]
        cut = cut_ref[0]
        t_row = i * tq + lax.broadcasted_iota(I32, (1, tq), 1)
        row = lax.broadcasted_iota(I32, (sub, tq), 0)
        for c, rows in enumerate(chunks):
            kidx = j * tk + c * sub + row
            bias = _select_bias(_scores_t(ki_ref[0, rows, :], qit, w), vf, kidx, cut)
            bias_ref[rows, :] = jnp.where(kidx <= t_row, bias, -jnp.inf)

        def logits(h, lg_ref):
            qt = qt_ref[0, h]
            part = jnp.full((8, tq), -jnp.inf, F32)
            for rows in chunks:
                lg = jnp.dot(k_ref[0, h, rows, :], qt, preferred_element_type=F32) + bias_ref[rows, :]
                lg_ref[rows, :] = lg
                part = jnp.maximum(part, jnp.max(lg.reshape(sub // 8, 8, tq), axis=0))
            return jnp.max(part, axis=0, keepdims=True)

        def absorb(h, lg_ref, p_ref, tile_max):
            hrow = pl.ds(h, 1)
            m_old = m_ref[hrow, :]
            m_new = jnp.maximum(m_old, tile_max)
            m_safe = jnp.where(m_new == -jnp.inf, 0.0, m_new)
            alpha = jnp.exp(m_old - m_safe)
            psum = jnp.zeros((8, tq), F32)
            for rows in chunks:
                p = jnp.exp(lg_ref[rows, :] - m_safe)
                psum = psum + jnp.sum(p.reshape(sub // 8, 8, tq), axis=0)
                p_ref[rows, :] = p.astype(BF16)
            l_ref[hrow, :] = alpha * l_ref[hrow, :] + jnp.sum(psum, axis=0, keepdims=True)
            acc_ref[h] = alpha * acc_ref[h] + jnp.dot(vt_ref[0, h], p_ref[...], preferred_element_type=F32)
            m_ref[hrow, :] = m_new

        def two_heads(t, max_a):
            h = 2 * t
            max_b = logits(h + 1, lgb_ref)
            absorb(h, lga_ref, pa_ref, max_a)
            max_a = logits(h + 2, lga_ref)
            absorb(h + 1, lgb_ref, pb_ref, max_b)
            return max_a

        max_a = lax.fori_loop(0, N_HEADS // 2 - 1, two_heads, logits(0, lga_ref))
        max_b = logits(N_HEADS - 1, lgb_ref)
        absorb(N_HEADS - 2, lga_ref, pa_ref, max_a)
        absorb(N_HEADS - 1, lgb_ref, pb_ref, max_b)

    @pl.when(j == nkt - 1)
    def _():
        for h in range(N_HEADS):
            o_ref[0, h] = (acc_ref[h] / l_ref[h:h + 1, :]).astype(o_ref.dtype)


def _attn(qt, k, vt, qit, ki, wit, vf, cut, *, tq, tk):
    b, _, _, t = qt.shape
    kj = lambda i, j: jnp.minimum(j, _n_key_tiles(i, tq, tk) - 1)
    return pl.pallas_call(
        functools.partial(_attn_kernel, tq=tq, tk=tk),
        grid=(b, t // tq, t // tk),
        in_specs=[
            pl.BlockSpec((1, N_HEADS, HEAD_DIM, tq), lambda bb, i, j: (bb, 0, 0, i)),
            pl.BlockSpec((1, N_HEADS, tk, HEAD_DIM), lambda bb, i, j: (bb, 0, kj(i, j), 0)),
            pl.BlockSpec((1, N_HEADS, HEAD_DIM, tk), lambda bb, i, j: (bb, 0, 0, kj(i, j))),
            pl.BlockSpec((1, IDX_HEADS, IDX_DIM, tq), lambda bb, i, j: (bb, 0, 0, i)),
            pl.BlockSpec((1, tk, IDX_DIM), lambda bb, i, j: (bb, kj(i, j), 0)),
            pl.BlockSpec((1, 8, tq), lambda bb, i, j: (bb, 0, i)),
            pl.BlockSpec((1, 1, tq), lambda bb, i, j: (bb, 0, i)),
            pl.BlockSpec((1, 1, tq), lambda bb, i, j: (bb, 0, i)),
        ],
        out_specs=pl.BlockSpec((1, N_HEADS, HEAD_DIM, tq), lambda bb, i, j: (bb, 0, 0, i)),
        out_shape=jax.ShapeDtypeStruct((b, N_HEADS, HEAD_DIM, t), BF16),
        scratch_shapes=[pltpu.VMEM((N_HEADS, tq), F32), pltpu.VMEM((N_HEADS, tq), F32),
                        pltpu.VMEM((N_HEADS, HEAD_DIM, tq), F32),
                        pltpu.VMEM((tk, tq), F32), pltpu.VMEM((tk, tq), F32), pltpu.VMEM((tk, tq), F32),
                        pltpu.VMEM((tk, tq), BF16), pltpu.VMEM((tk, tq), BF16)],
        compiler_params=_cparams("parallel", "parallel", "arbitrary"),
        name="attn",
    )(qt, k, vt, qit, ki, wit, vf, cut)


def _zoh(ar, ai, ldt):
    dt = jnp.exp(ldt)
    mag = jnp.exp(dt * ar)
    abr = mag * jnp.cos(dt * ai)
    abi = mag * jnp.sin(dt * ai)
    den = ar * ar + ai * ai
    nr = abr - 1.0
    ni = abi
    return abr, abi, (nr * ar + ni * ai) / den, (ni * ar - nr * ai) / den


def _disc_kernel(ar_ref, ai_ref, ldt_ref, ar16_ref, ai16_ref, ldt16_ref, br_ref, bi_ref,
                 pwr_ref, pwi_ref, bbr_ref, bbi_ref):
    abr, abi, _, _ = _zoh(ar_ref[...], ai_ref[...], ldt_ref[...])
    pr, pi = abr, abi
    for j in range(8):
        pwr_ref[j:j + 1, :] = pr
        pwi_ref[j:j + 1, :] = pi
        pr, pi = pr * abr - pi * abi, pr * abi + pi * abr
    _, _, fr, fi = _zoh(ar16_ref[...], ai16_ref[...], ldt16_ref[...])
    br = br_ref[...]
    bi = bi_ref[...]
    bbr_ref[...] = fr * br - fi * bi
    bbi_ref[...] = fr * bi + fi * br


def _discretize(a_re, a_im, log_dt, b_re, b_im):
    flat = lambda a: a.reshape(1, -1)
    ldt = jnp.broadcast_to(log_dt[:, None], (N_GROUPS, STATE_DIM))
    rep = lambda a: flat(jnp.broadcast_to(a[:, :, None], (N_GROUPS, STATE_DIM, GROUP_CH)))
    n16 = N_STATE * GROUP_CH
    pwr, pwi, bbr, bbi = pl.pallas_call(
        _disc_kernel,
        out_shape=[jax.ShapeDtypeStruct((8, N_STATE), F32), jax.ShapeDtypeStruct((8, N_STATE), F32),
                   jax.ShapeDtypeStruct((1, n16), F32), jax.ShapeDtypeStruct((1, n16), F32)],
        name="s5_discretize",
    )(flat(a_re), flat(a_im), flat(ldt), rep(a_re), rep(a_im), rep(ldt), flat(b_re), flat(b_im))
    return pwr, pwi, bbr.reshape(N_GROUPS, STATE_DIM, GROUP_CH), bbi.reshape(N_GROUPS, STATE_DIM, GROUP_CH)


def _block_diag_in(bb):
    eye = jnp.eye(N_GROUPS, dtype=bb.dtype)
    return jnp.einsum('gpc,gh->gchp', bb, eye).reshape(SSM_WIDTH, N_STATE)


def _block_diag_out(c):
    eye = jnp.eye(N_GROUPS, dtype=c.dtype)
    return jnp.einsum('gcp,gh->gphc', c, eye).reshape(N_STATE, SSM_WIDTH)


_LANE_CHUNK = 512


def _ssm_kernel(u_ref, bblk_ref, cblk_ref, pw_ref, d_ref, y_ref, xf_ref, x_ref, carry_ref, *, tt):
    ts = pl.program_id(1)

    @pl.when(ts == 0)
    def _():
        carry_ref[...] = jnp.zeros_like(carry_ref)

    u = u_ref[0]
    ub = u.astype(BF16)
    n_chunks = 2 * N_STATE // _LANE_CHUNK
    for c in range(n_chunks):
        cols = slice(c * _LANE_CHUNK, (c + 1) * _LANE_CHUNK)
        x_ref[:, cols] = jnp.dot(ub, bblk_ref[:, cols], preferred_element_type=F32)
    rid = lax.broadcasted_iota(I32, (8, _LANE_CHUNK), 0)

    def group(r, carry):
        r0 = pl.multiple_of(r * 8, 8)
        for c in range(N_STATE // _LANE_CHUNK):
            re = pl.ds(c * _LANE_CHUNK, _LANE_CHUNK)
            im = pl.ds(N_STATE + c * _LANE_CHUNK, _LANE_CHUNK)
            xr = x_ref[pl.ds(r0, 8), re]
            xi = x_ref[pl.ds(r0, 8), im]
            for d in (1, 2, 4):
                ar = pw_ref[d - 1:d, re]
                ai = pw_ref[d - 1:d, im]
                sr = jnp.where(rid >= d, pltpu.roll(xr, d, 0), 0.0)
                si = jnp.where(rid >= d, pltpu.roll(xi, d, 0), 0.0)
                xr, xi = xr + (ar * sr - ai * si), xi + (ar * si + ai * sr)
            cr = carry_ref[:, re]
            ci = carry_ref[:, im]
            pr = pw_ref[:, re]
            pi = pw_ref[:, im]
            xr, xi = xr + (pr * cr - pi * ci), xi + (pr * ci + pi * cr)
            x_ref[pl.ds(r0, 8), re] = xr
            x_ref[pl.ds(r0, 8), im] = xi
            carry_ref[:, re] = xr[7:8, :]
            carry_ref[:, im] = xi[7:8, :]
        return carry

    lax.fori_loop(0, tt // 8, group, 0)
    y = d_ref[...] * u
    for c in range(n_chunks):
        cols = slice(c * _LANE_CHUNK, (c + 1) * _LANE_CHUNK)
        y = y + jnp.dot(x_ref[:, cols].astype(BF16), cblk_ref[cols, :], preferred_element_type=F32)
    y_ref[0] = y
    xf_ref[0] = carry_ref[...]


def _ssm(u, bblk, cblk, pw, d_skip, *, tt):
    b, t, _ = u.shape
    return pl.pallas_call(
        functools.partial(_ssm_kernel, tt=tt),
        grid=(b, t // tt),
        in_specs=[
            pl.BlockSpec((1, tt, SSM_WIDTH), lambda bb, s: (bb, s, 0)),
            pl.BlockSpec((SSM_WIDTH, 2 * N_STATE), lambda bb, s: (0, 0)),
            pl.BlockSpec((2 * N_STATE, SSM_WIDTH), lambda bb, s: (0, 0)),
            pl.BlockSpec((8, 2 * N_STATE), lambda bb, s: (0, 0)),
            pl.BlockSpec((1, SSM_WIDTH), lambda bb, s: (0, 0)),
        ],
        out_specs=[pl.BlockSpec((1, tt, SSM_WIDTH), lambda bb, s: (bb, s, 0)),
                   pl.BlockSpec((1, 1, 2 * N_STATE), lambda bb, s: (bb, 0, 0))],
        out_shape=[jax.ShapeDtypeStruct((b, t, SSM_WIDTH), F32),
                   jax.ShapeDtypeStruct((b, 1, 2 * N_STATE), F32)],
        scratch_shapes=[pltpu.VMEM((tt, 2 * N_STATE), F32), pltpu.VMEM((1, 2 * N_STATE), F32)],
        compiler_params=_cparams("parallel", "arbitrary"),
        name="s5_scan",
    )(u, bblk, cblk, pw, d_skip)


def _ssm_step_kernel(u_ref, x0_ref, bblk_ref, cblk_ref, pw_ref, d_ref, y_ref, x_ref):
    u = u_ref[...]
    bu = jnp.dot(u, bblk_ref[...], preferred_element_type=F32, precision=lax.Precision.HIGHEST)
    ar = pw_ref[0:1, :N_STATE]
    ai = pw_ref[0:1, N_STATE:]
    x0r = x0_ref[:, :N_STATE]
    x0i = x0_ref[:, N_STATE:]
    xr = ar * x0r - ai * x0i + bu[:, :N_STATE]
    xi = ar * x0i + ai * x0r + bu[:, N_STATE:]
    x_ref[:, :N_STATE] = xr
    x_ref[:, N_STATE:] = xi
    y = jnp.dot(x_ref[...], cblk_ref[...], preferred_element_type=F32, precision=lax.Precision.HIGHEST)
    y_ref[...] = y + d_ref[...] * u


def _ssm_step(u, x0, bblk, cblk, pw, d_skip):
    n = u.shape[0]
    return pl.pallas_call(
        _ssm_step_kernel,
        out_shape=[jax.ShapeDtypeStruct((n, SSM_WIDTH), F32), jax.ShapeDtypeStruct((n, 2 * N_STATE), F32)],
        compiler_params=pltpu.CompilerParams(vmem_limit_bytes=VMEM_LIMIT_BYTES),
        name="s5_step",
    )(u, x0, bblk, cblk, pw, d_skip)


def _gelu_tanh(x):
    c = np.float32(np.sqrt(2.0 / np.pi))
    return 0.5 * x * (1.0 + jnp.tanh(c * (x + 0.044715 * (x * x * x))))


def _merge_kernel(x_ref, oa_ref, ys_ref, g_ref, wgt_ref, wglu_ref, bglu_ref, wpa_ref, wpb_ref, wout_ref, o_ref):
    x = x_ref[...]
    h = _rmsnorm(x, g_ref[...]).astype(BF16)
    gates = _sigmoid(jnp.dot(h, wgt_ref[...], preferred_element_type=F32))
    ys = _gelu_tanh(ys_ref[...])
    glu = jnp.dot(ys.astype(BF16), wglu_ref[...], preferred_element_type=F32) + bglu_ref[...]
    ob = ys * _sigmoid(glu)
    pa = jnp.dot(oa_ref[...], wpa_ref[...], preferred_element_type=F32)
    pb = jnp.dot(ob.astype(BF16), wpb_ref[...], preferred_element_type=F32)
    merged = gates[:, :D_MODEL] * pa + gates[:, D_MODEL:] * pb
    o_ref[...] = x + jnp.dot(merged.astype(BF16), wout_ref[...], preferred_element_type=F32)


def _merge(x, oa, ys, g, wgt, wglu, bglu, wpa, wpb, wout, *, tm):
    m, d = x.shape
    row = lambda n: pl.BlockSpec((tm, n), lambda i: (i, 0))
    full = lambda a: pl.BlockSpec(a.shape, lambda i: (0, 0))
    return pl.pallas_call(
        _merge_kernel,
        grid=(m // tm,),
        in_specs=[row(d), row(D_ATTN), row(SSM_WIDTH), full(g), full(wgt), full(wglu), full(bglu),
                  full(wpa), full(wpb), full(wout)],
        out_specs=row(d),
        out_shape=jax.ShapeDtypeStruct((m, d), F32),
        compiler_params=_cparams("parallel"),
        name="merge",
    )(x, oa, ys, g, wgt, wglu, bglu, wpa, wpb, wout)


_PAGES_PER_STEP = 8


def _page_specs(block, n_pages):
    last = n_pages // _PAGES_PER_STEP - 1

    def spec(r):
        def index_map(b, p, pt):
            return (pt[b, jnp.minimum(p, last) * _PAGES_PER_STEP + r],) + (0,) * (len(block) - 1)
        return pl.BlockSpec(block, index_map)
    return [spec(r) for r in range(_PAGES_PER_STEP)]


def _per_seq(shape):
    return pl.BlockSpec((1,) + shape, lambda b, p, pt: (b,) + (0,) * len(shape))


def _sample_scores_kernel(pt_ref, qi_ref, w_ref, knew_ref, *refs, page_size, n_steps):
    pages = refs[:_PAGES_PER_STEP]
    o_ref = refs[_PAGES_PER_STEP]
    p = pl.program_id(1)
    qi = qi_ref[0]
    w = w_ref[0]

    def score(kt):
        d = jnp.dot(qi, kt, preferred_element_type=F32)
        return jnp.sum(jnp.maximum(d, 0.0) * w, axis=0, keepdims=True)

    @pl.when(p < n_steps)
    def _():
        for r in range(_PAGES_PER_STEP):
            o_ref[0, :, r * page_size:(r + 1) * page_size] = score(pages[r][0].astype(BF16))

    @pl.when(p == n_steps)
    def _():
        o_ref[0] = jnp.full(o_ref.shape[1:], -jnp.inf, F32)
        lane = lax.broadcasted_iota(I32, (1, page_size), 1)
        o_ref[0, :, 0:page_size] = jnp.where(lane == 0, score(knew_ref[0]), -jnp.inf)


def _sample_scores(page_table, qi_rows, w_col, ki_new_t, kidx_t):
    db, n_pages = page_table.shape
    _, _, page_size = kidx_t.shape
    assert n_pages % _PAGES_PER_STEP == 0
    steps = n_pages // _PAGES_PER_STEP
    block = _PAGES_PER_STEP * page_size
    return pl.pallas_call(
        functools.partial(_sample_scores_kernel, page_size=page_size, n_steps=steps),
        grid_spec=pltpu.PrefetchScalarGridSpec(
            num_scalar_prefetch=1,
            grid=(db, steps + 1),
            in_specs=[_per_seq((8, IDX_DIM)), _per_seq((8, 1)), _per_seq((IDX_DIM, page_size))]
                     + _page_specs((1, IDX_DIM, page_size), n_pages),
            out_specs=pl.BlockSpec((1, 1, block), lambda b, p, pt: (b, 0, p)),
        ),
        out_shape=jax.ShapeDtypeStruct((db, 1, (steps + 1) * block), F32),
        compiler_params=_cparams("parallel", "arbitrary"),
        name="sample_scores",
    )(page_table, qi_rows, w_col, ki_new_t, *([kidx_t] * _PAGES_PER_STEP))


def _sample_select_kernel(st_ref, thr_ref, cut_ref, s_ref, fst_ref, ist_ref, *, tk, tq, n_keys, k_top):
    n_chunks = st_ref.shape[0] // tk
    row_iota = lax.broadcasted_iota(I32, (tk, tq), 0)

    def build(c, carry):
        mx, mn, n_hi = carry
        k0 = pl.multiple_of(c * tk, tk)
        s = st_ref[pl.ds(k0, tk), :]
        real = (k0 + row_iota) < n_keys
        sm = jnp.where(real, s, -jnp.inf)
        s_ref[pl.ds(k0, tk), :] = sm
        return (jnp.maximum(mx, _fold8(sm, jnp.max)),
                jnp.minimum(mn, _fold8(jnp.where(real, s, jnp.inf), jnp.min)),
                n_hi + _fold8(jnp.where(sm > NEG_BIG, 1, 0), jnp.sum))

    mx, mn, n_hi = lax.fori_loop(
        0, n_chunks, build,
        (jnp.full((_FOLD_ROWS, tq), -jnp.inf, F32), jnp.full((_FOLD_ROWS, tq), jnp.inf, F32),
         jnp.zeros((_FOLD_ROWS, tq), I32)))
    _topk_threshold(s_ref, n_chunks, tk, tq, jnp.max(mx, axis=0, keepdims=True),
                    jnp.min(mn, axis=0, keepdims=True), jnp.sum(n_hi, axis=0, keepdims=True),
                    jnp.zeros((1, tq), I32), k_top, thr_ref, cut_ref, fst_ref, ist_ref)


def _sample_select(scores_t, *, n_keys, k_top, tk):
    n_rows, tq = scores_t.shape
    kern = functools.partial(_sample_select_kernel, tk=tk, tq=tq, n_keys=n_keys, k_top=k_top)
    return pl.pallas_call(
        kern,
        out_shape=[jax.ShapeDtypeStruct((1, 1, tq), F32), jax.ShapeDtypeStruct((1, 1, tq), I32)],
        scratch_shapes=[pltpu.VMEM((n_rows, tq), F32), pltpu.VMEM((8, tq), F32), pltpu.VMEM((8, tq), I32)],
        compiler_params=pltpu.CompilerParams(vmem_limit_bytes=VMEM_LIMIT_BYTES),
        name="sample_select",
    )(scores_t)


def _sample_attn_kernel(pt_ref, qb_ref, s_ref, thr_ref, cut_ref, kself_ref, vself_ref, *refs,
                        page_size, n_steps):
    kp = refs[:_PAGES_PER_STEP]
    vp = refs[_PAGES_PER_STEP:2 * _PAGES_PER_STEP]
    o_ref, m_ref, l_ref, acc_ref = refs[2 * _PAGES_PER_STEP:]
    step = pl.program_id(1)
    thr = thr_ref[0]
    cut = cut_ref[0]
    lane = lax.broadcasted_iota(I32, (1, page_size), 1)
    is_self = step == n_steps

    @pl.when(step == 0)
    def _():
        m_ref[...] = jnp.full_like(m_ref, -jnp.inf)
        l_ref[...] = jnp.zeros_like(l_ref)
        acc_ref[...] = jnp.zeros_like(acc_ref)

    logits = []
    for r in range(_PAGES_PER_STEP):
        kidx = (step * _PAGES_PER_STEP + r) * page_size + lane
        bias = _select_bias(s_ref[0, :, r * page_size:(r + 1) * page_size], thr, kidx, cut)
        rows = []
        for h in range(N_HEADS):
            kt = kp[r][0, h]
            if r == 0:
                kt = jnp.where(is_self, kself_ref[0, h], kt)
            rows.append(jnp.sum(kt * qb_ref[0, h], axis=0, keepdims=True))
        logits.append(jnp.concatenate(rows, axis=0) + bias)

    m_old = m_ref[...]
    m_new = m_old
    for lg in logits:
        m_new = jnp.maximum(m_new, jnp.max(lg, axis=1, keepdims=True))
    m_safe = jnp.where(m_new == -jnp.inf, 0.0, m_new)
    alpha = jnp.exp(m_old - m_safe)
    probs = [jnp.exp(lg - m_safe) for lg in logits]
    l_new = alpha * l_ref[...]
    for p in probs:
        l_new = l_new + jnp.sum(p, axis=1, keepdims=True)
    l_ref[...] = l_new
    m_ref[...] = m_new
    for h in range(N_HEADS):
        acc = acc_ref[h] * alpha[h:h + 1, :]
        for r in range(_PAGES_PER_STEP):
            vt = vp[r][0, h]
            if r == 0:
                vt = jnp.where(is_self, vself_ref[0, h], vt)
            acc = acc + vt * probs[r][h:h + 1, :]
        acc_ref[h] = acc

    @pl.when(is_self)
    def _():
        for h in range(N_HEADS):
            o_ref[0, h] = jnp.sum(acc_ref[h], axis=1, keepdims=True) / l_ref[h:h + 1, :]


def _sample_attn(page_table, qb, scores, thr, cut, k_self, v_self, ck_t, cv_t):
    db, n_pages = page_table.shape
    _, nh, hd, page_size = ck_t.shape
    steps = n_pages // _PAGES_PER_STEP
    block = _PAGES_PER_STEP * page_size
    page = (1, nh, hd, page_size)
    return pl.pallas_call(
        functools.partial(_sample_attn_kernel, page_size=page_size, n_steps=steps),
        grid_spec=pltpu.PrefetchScalarGridSpec(
            num_scalar_prefetch=1,
            grid=(db, steps + 1),
            in_specs=[_per_seq(page[1:]),
                      pl.BlockSpec((1, 1, block), lambda b, p, pt: (b, 0, p)),
                      _per_seq((1, 1)), _per_seq((1, 1)), _per_seq(page[1:]), _per_seq(page[1:])]
                     + _page_specs(page, n_pages) + _page_specs(page, n_pages),
            out_specs=_per_seq((nh, hd, 1)),
            scratch_shapes=[pltpu.VMEM((nh, 1), F32), pltpu.VMEM((nh, 1), F32),
                            pltpu.VMEM((nh, hd, page_size), F32)],
        ),
        out_shape=jax.ShapeDtypeStruct((db, nh, hd, 1), F32),
        compiler_params=_cparams("parallel", "arbitrary"),
        name="sample_attn",
    )(page_table, qb, scores, thr, cut, k_self, v_self,
      *([ck_t] * _PAGES_PER_STEP), *([cv_t] * _PAGES_PER_STEP))


def _pick(n, pref):
    for t in pref:
        if n % t == 0:
            return t
    return n


def _pack_w_in(w_in):
    o = np.cumsum([0, D_ATTN, D_ATTN, D_ATTN, IDX_HEADS * IDX_DIM, IDX_DIM, IDX_HEADS, SSM_WIDTH, 2 * D_MODEL])
    z = lambda n: jnp.zeros((D_MODEL, n), w_in.dtype)
    w_feat = jnp.concatenate([w_in[:, o[0]:o[6]], z(_TEND - _TWI - IDX_HEADS)], axis=1)
    w_tok = jnp.concatenate([w_in[:, o[1]:o[2]], w_in[:, o[4]:o[5]], z(_RU - _RKI - IDX_DIM),
                             w_in[:, o[6]:o[7]]], axis=1)
    return w_feat.T.astype(BF16), w_tok.astype(BF16), w_in[:, o[7]:o[8]].astype(BF16)


def _prompt_layer(x, lw, ssm, final_norm):
    bsz, seq, d = x.shape
    m = bsz * seq
    tm = _pick(m, (512, 256, 128, 64, 32, 16, 8))
    x1 = _ffn(x.reshape(m, d), *lw["ffn1"], lw["g_final"], final_norm=False, tm=tm, tf=lw["tf"])
    qt, kt_f, vt_f, vt, kh, qit, kit_f, kib, wit, u_f = _proj(
        x1.reshape(bsz, seq, d), lw["g_mix"], lw["w_feat"], lw["w_tok"], tm=_pick(seq, (512, 256, 128)))
    heads = lambda a, nh: a.reshape(bsz, nh, -1, seq)
    qt, vt, qit = heads(qt, N_HEADS), heads(vt, N_HEADS), heads(qit, IDX_HEADS)

    k_top = min(TOPK_MAX, seq // 4)
    vf, cut = _select(qit, kib, wit, tq=_pick(seq, (512, 256, 128)), tk=_pick(seq, (256, 128)), k_top=k_top)
    o_t = _attn(qt, kh, vt, qit, kib, wit, vf, cut,
                tq=_pick(seq, (512, 256, 128)), tk=_pick(seq, (512, 256, 128)))
    oa = o_t.transpose(0, 3, 1, 2).reshape(m, D_ATTN)

    tt = _pick(seq, (256, 128, 64, 32, 16, 8))
    ys, xfin = _ssm(u_f, ssm["bblk"].astype(BF16), ssm["cblk"].astype(BF16), ssm["pw"], ssm["d_skip"], tt=tt)
    x2 = _merge(x1, oa, ys.reshape(m, SSM_WIDTH), *lw["merge"], tm=tm)
    y = _ffn(x2, *lw["ffn2"], lw["g_final"], final_norm=final_norm, tm=tm, tf=lw["tf"])
    new = (heads(kt_f, N_HEADS).transpose(0, 3, 1, 2), heads(vt_f, N_HEADS).transpose(0, 3, 1, 2),
           kit_f.transpose(0, 2, 1),
           xfin[:, 0, :N_STATE].reshape(bsz, N_GROUPS, STATE_DIM),
           xfin[:, 0, N_STATE:].reshape(bsz, N_GROUPS, STATE_DIM))
    return y.reshape(bsz, seq, d), new


def _sample_layer(x, lw, ssm, final_norm, cache_k, cache_v, cache_kidx, st_re, st_im, page_table):
    db, ds, d = x.shape
    assert ds == 1, "one new token per sample sequence"
    n_pool, page_size = cache_k.shape[0], cache_k.shape[1]
    past_len = page_table.shape[1] * page_size
    s1 = _ffn(x.reshape(db, d), *lw["ffn1"], lw["g_final"], final_norm=False, tm=db, tf=lw["tf"])
    qt, kt_f, vt_f, _, _, qit, kit_f, kib, wit, u_f = _proj(
        s1.reshape(1, db, d), lw["g_mix"], lw["w_feat"], lw["w_tok"], tm=db)
    q_b, k_f, v_f, qi_b, ki_f, wi_f = (a[0].T for a in (qt, kt_f, vt_f, qit, kit_f, wit))
    ki_b, u_f = kib[0], u_f[0]

    kidx_t = cache_kidx.transpose(0, 2, 1)
    ck_t = cache_k.transpose(0, 2, 3, 1)
    cv_t = cache_v.transpose(0, 2, 3, 1)
    lane0 = lambda a: jnp.zeros(a.shape + (page_size,), a.dtype).at[..., 0].set(a)

    qi_rows = jnp.zeros((db, 8, IDX_DIM), BF16).at[:, :IDX_HEADS].set(qi_b.reshape(db, IDX_HEADS, IDX_DIM))
    scores = _sample_scores(page_table, qi_rows, wi_f[:, :, None], lane0(ki_b), kidx_t)

    n_keys = past_len + 1
    k_top = min(TOPK_MAX, n_keys // 4)
    n = scores.shape[2]
    scores_t = jnp.zeros((n, -(-db // 128) * 128), F32).at[:, :db].set(scores[:, 0, :].T)
    thr, cut = _sample_select(scores_t, n_keys=n_keys, k_top=k_top, tk=256)

    qb = jnp.broadcast_to(q_b.astype(F32).reshape(db, N_HEADS, HEAD_DIM, 1), (db, N_HEADS, HEAD_DIM, page_size))
    o_s = _sample_attn(page_table, qb, scores, thr[0, 0, :db].reshape(db, 1, 1), cut[0, 0, :db].reshape(db, 1, 1),
                       lane0(k_f.reshape(db, N_HEADS, HEAD_DIM)), lane0(v_f.reshape(db, N_HEADS, HEAD_DIM)),
                       ck_t, cv_t)
    oa = o_s.reshape(db, D_ATTN).astype(BF16)

    x0 = jnp.concatenate([st_re.reshape(db, N_STATE), st_im.reshape(db, N_STATE)], axis=1)
    ys, x_new = _ssm_step(u_f, x0, ssm["bblk"], ssm["cblk"], ssm["pw"], ssm["d_skip"])
    s2 = _merge(s1, oa, ys, *lw["merge"], tm=db)
    y = _ffn(s2, *lw["ffn2"], lw["g_final"], final_norm=final_norm, tm=db, tf=lw["tf"])
    new = (k_f.reshape(db, ds, N_HEADS, HEAD_DIM), v_f.reshape(db, ds, N_HEADS, HEAD_DIM),
           ki_f.reshape(db, ds, IDX_DIM),
           x_new[:, :N_STATE].reshape(db, N_GROUPS, STATE_DIM),
           x_new[:, N_STATE:].reshape(db, N_GROUPS, STATE_DIM))
    return y.reshape(db, ds, d), new


def kernel(x_prompt, x_sample, cache_k, cache_v, cache_kidx, state_ssm_re, state_ssm_im, page_table,
           g_ffn1, w1_gate, w1_up, w1_down, g_mix, w_in, a_re, a_im, log_dt, b_re, b_im, c_re, c_im,
           d_skip, w_glu, b_glu, w_pa, w_pb, w_out, g_ffn2, w2_gate, w2_up, w2_down, g_final):
    depth = w_in.shape[0]
    row = lambda a: a.reshape(1, -1)
    bf = lambda a: a.astype(BF16)
    xp, xs = x_prompt, x_sample
    new_p, new_s = [], []
    for l in range(depth):
        w_feat, w_tok, w_gates = _pack_w_in(w_in[l])
        lw = dict(
            ffn1=(row(g_ffn1[l]), bf(w1_gate[l]), bf(w1_up[l]), bf(w1_down[l])),
            ffn2=(row(g_ffn2[l]), bf(w2_gate[l]), bf(w2_up[l]), bf(w2_down[l])),
            merge=(row(g_mix[l]), w_gates, bf(w_glu[l]), row(b_glu[l]), bf(w_pa[l]), bf(w_pb[l]), bf(w_out[l])),
            g_mix=row(g_mix[l]), w_feat=w_feat, w_tok=w_tok, g_final=row(g_final),
            tf=_pick(w1_gate.shape[2], (1408, 1024, 512, 256, 128)),
        )
        pwr, pwi, bbr, bbi = _discretize(a_re[l], a_im[l], log_dt[l], b_re[l], b_im[l])
        ssm = dict(
            pw=jnp.concatenate([pwr, pwi], axis=1),
            bblk=jnp.concatenate([_block_diag_in(bbr), _block_diag_in(bbi)], axis=1),
            cblk=jnp.concatenate([_block_diag_out(c_re[l]), -_block_diag_out(c_im[l])], axis=0),
            d_skip=row(d_skip[l]),
        )
        last = l == depth - 1
        xp, st_p = _prompt_layer(xp, lw, ssm, last)
        xs, st_s = _sample_layer(xs, lw, ssm, last, cache_k[l], cache_v[l], cache_kidx[l],
                                 state_ssm_re[l], state_ssm_im[l], page_table)
        new_p.append(st_p)
        new_s.append(st_s)
    stack = lambda states, i: jnp.stack([s[i] for s in states])
    return (xp, xs) + tuple(stack(new_p, i) for i in range(5)) + tuple(stack(new_s, i) for i in range(5))
```

```python
import functools

import numpy as np
import jax
import jax.numpy as jnp
from jax import lax
from jax.experimental import pallas as pl
from jax.experimental.pallas import tpu as pltpu

F32 = jnp.float32
BF16 = jnp.bfloat16
I32 = jnp.int32

D_MODEL = 1024
N_HEADS = 8
HEAD_DIM = 64
D_ATTN = N_HEADS * HEAD_DIM
IDX_HEADS = 4
IDX_DIM = 64
TOPK_MAX = 256
SSM_WIDTH = 512
GROUP_CH = 16
N_GROUPS = SSM_WIDTH // GROUP_CH
STATE_DIM = 64
N_STATE = N_GROUPS * STATE_DIM
NORM_EPS = 1e-6
NEG_BIG = -1e30

INT_MAX = 2 ** 31 - 1

VMEM_LIMIT_BYTES = 56 * 1024 * 1024


def _cparams(*sem):
    return pltpu.CompilerParams(dimension_semantics=sem, vmem_limit_bytes=VMEM_LIMIT_BYTES)


def _rmsnorm(x, g):
    y = x * lax.rsqrt(jnp.mean(x * x, axis=-1, keepdims=True) + NORM_EPS)
    return y * g


def _sigmoid(x):
    return 1.0 / (1.0 + jnp.exp(-x))


def _ffn_kernel(x_ref, g_ref, wg_ref, wu_ref, wd_ref, gf_ref, o_ref, *, final_norm, tf):
    x = x_ref[...]
    xn = _rmsnorm(x, g_ref[...]).astype(BF16)
    acc = None
    for f0 in range(0, wg_ref.shape[1], tf):
        gate = jnp.dot(xn, wg_ref[:, f0:f0 + tf], preferred_element_type=F32)
        up = jnp.dot(xn, wu_ref[:, f0:f0 + tf], preferred_element_type=F32)
        act = (gate * _sigmoid(gate)) * up
        part = jnp.dot(act.astype(BF16), wd_ref[f0:f0 + tf, :], preferred_element_type=F32)
        acc = part if acc is None else acc + part
    y = x + 0.5 * acc
    if final_norm:
        y = _rmsnorm(y, gf_ref[...])
    o_ref[...] = y


def _ffn(x, g, wg, wu, wd, g_final, *, final_norm, tm, tf):
    m, d = x.shape
    resident = lambda a: pl.BlockSpec(a.shape, lambda i: (0, 0), pipeline_mode=pl.Buffered(1))
    return pl.pallas_call(
        functools.partial(_ffn_kernel, final_norm=final_norm, tf=tf),
        grid=(m // tm,),
        in_specs=[pl.BlockSpec((tm, d), lambda i: (i, 0)), resident(g), resident(wg), resident(wu),
                  resident(wd), resident(g_final)],
        out_specs=pl.BlockSpec((tm, d), lambda i: (i, 0)),
        out_shape=jax.ShapeDtypeStruct((m, d), F32),
        compiler_params=_cparams("parallel"),
        name="ffn",
    )(x, g, wg, wu, wd, g_final)


_TQ, _TK, _TV, _TQI, _TKI, _TWI, _TEND = 0, 512, 1024, 1536, 1792, 1856, 1872
_RK, _RKI, _RU, _REND = 0, 512, 640, 1152


def _proj_kernel(x_ref, g_ref, wt_ref, wr_ref, qt_ref, ktf_ref, vtf_ref, vt_ref, kh_ref,
                 qit_ref, kitf_ref, kib_ref, wit_ref, u_ref):
    h = _rmsnorm(x_ref[0], g_ref[...]).astype(BF16)
    pt = lax.dot_general(wt_ref[...], h, (((1,), (1,)), ((), ())), preferred_element_type=F32)
    pr = jnp.dot(h, wr_ref[...], preferred_element_type=F32)
    qt_ref[0] = (pt[_TQ:_TK] * (HEAD_DIM ** -0.5)).astype(BF16)
    ktf_ref[0] = pt[_TK:_TV]
    v = pt[_TV:_TQI]
    vtf_ref[0] = v
    vt_ref[0] = v.astype(BF16)
    qit_ref[0] = pt[_TQI:_TKI].astype(BF16)
    kitf_ref[0] = pt[_TKI:_TWI]
    wit_ref[0] = pt[_TWI:_TWI + 8] * ((IDX_HEADS * IDX_DIM) ** -0.5)
    k = pr[:, _RK:_RKI].astype(BF16)
    for hd in range(N_HEADS):
        kh_ref[0, hd] = k[:, hd * HEAD_DIM:(hd + 1) * HEAD_DIM]
    kib_ref[0] = pr[:, _RKI:_RKI + IDX_DIM].astype(BF16)
    u_ref[0] = pr[:, _RU:_REND]


def _proj(x, g, wt, wr, *, tm):
    b, t, d = x.shape
    feat = lambda n: pl.BlockSpec((1, n, tm), lambda bb, i: (bb, 0, i))
    tok = lambda n: pl.BlockSpec((1, tm, n), lambda bb, i: (bb, i, 0))
    outs = [
        (feat(D_ATTN), (b, D_ATTN, t), BF16), (feat(D_ATTN), (b, D_ATTN, t), F32),
        (feat(D_ATTN), (b, D_ATTN, t), F32), (feat(D_ATTN), (b, D_ATTN, t), BF16),
        (pl.BlockSpec((1, N_HEADS, tm, HEAD_DIM), lambda bb, i: (bb, 0, i, 0)), (b, N_HEADS, t, HEAD_DIM), BF16),
        (feat(IDX_HEADS * IDX_DIM), (b, IDX_HEADS * IDX_DIM, t), BF16), (feat(IDX_DIM), (b, IDX_DIM, t), F32),
        (tok(IDX_DIM), (b, t, IDX_DIM), BF16), (feat(8), (b, 8, t), F32), (tok(SSM_WIDTH), (b, t, SSM_WIDTH), F32),
    ]
    return pl.pallas_call(
        _proj_kernel,
        grid=(b, t // tm),
        in_specs=[tok(d), pl.BlockSpec((1, d), lambda bb, i: (0, 0)),
                  pl.BlockSpec((_TEND, d), lambda bb, i: (0, 0)), pl.BlockSpec((d, _REND), lambda bb, i: (0, 0))],
        out_specs=[spec for spec, _, _ in outs],
        out_shape=[jax.ShapeDtypeStruct(shape, dt) for _, shape, dt in outs],
        compiler_params=_cparams("parallel", "parallel"),
        name="proj",
    )(x, g, wt, wr)


def _scores_t(kc, qit, w):
    s = None
    for h in range(IDX_HEADS):
        d = jnp.dot(kc, qit[h], preferred_element_type=F32)
        term = jnp.maximum(d, 0.0) * w[h:h + 1, :]
        s = term if s is None else s + term
    return s


def _select_bias(s, vf, kidx, cut):
    tie = jnp.where(kidx < cut, 0.0, -jnp.inf)
    return jnp.where(s > vf, 0.0, jnp.where(s == vf, tie, -jnp.inf))


_FOLD_ROWS = 16


def _fold8(x, op):
    return op(x.reshape(x.shape[0] // _FOLD_ROWS, _FOLD_ROWS, x.shape[1]), axis=0)


_PROBES_PER_ROUND = 4
_SETTLE_FIRST, _SETTLE_EVERY, _BISECT_CAP = 4, 2, 80


def _topk_threshold(s_ref, n_chunks, tk, tq, mx, mn, n_hi, n_nc, k_top, thr_ref, cut_ref, fst_ref, ist_ref):
    row_iota = lax.broadcasted_iota(I32, (tk, tq), 0)
    zeros_i = jnp.zeros((_FOLD_ROWS, tq), I32)

    def chunk(c):
        k0 = pl.multiple_of(c * tk, tk)
        return s_ref[pl.ds(k0, tk), :], k0

    def count_gt(v):
        def body(c, acc):
            blk, _ = chunk(c)
            return acc + _fold8(jnp.where(blk > v, 1, 0), jnp.sum)
        real = jnp.sum(lax.fori_loop(0, n_chunks, body, zeros_i), axis=0, keepdims=True)
        return real + jnp.where(NEG_BIG > v, n_nc, 0)

    def count_gt_eq(v):
        def body(c, acc):
            blk, _ = chunk(c)
            return (acc[0] + _fold8(jnp.where(blk > v, 1, 0), jnp.sum),
                    acc[1] + _fold8(jnp.where(blk == v, 1, 0), jnp.sum))
        g, e = lax.fori_loop(0, n_chunks, body, (zeros_i, zeros_i))
        g = jnp.sum(g, axis=0, keepdims=True) + jnp.where(NEG_BIG > v, n_nc, 0)
        return g, jnp.sum(e, axis=0, keepdims=True)

    def max_le(v):
        def body(c, acc):
            blk, _ = chunk(c)
            return jnp.maximum(acc, _fold8(jnp.where(blk <= v, blk, -jnp.inf), jnp.max))
        m = jnp.max(lax.fori_loop(0, n_chunks, body, jnp.full((_FOLD_ROWS, tq), -jnp.inf, F32)),
                    axis=0, keepdims=True)
        return jnp.where((n_nc > 0) & (NEG_BIG <= v), jnp.maximum(m, NEG_BIG), m)

    def settle(m):
        gt, eq_real = count_gt_eq(m)
        eq = eq_real + jnp.where(m == NEG_BIG, n_nc, 0)
        found = (ist_ref[0:1, :] == 0) & (gt + eq >= k_top)
        need = k_top - gt
        fst_ref[2:3, :] = jnp.where(found, m, fst_ref[2:3, :])
        ist_ref[1:2, :] = jnp.where(found, need, ist_ref[1:2, :])
        ist_ref[2:3, :] = jnp.where(found, jnp.where(eq_real > need, 1, 0), ist_ref[2:3, :])
        ist_ref[3:4, :] = jnp.where(found, INT_MAX, ist_ref[3:4, :])
        ist_ref[0:1, :] = jnp.where(found, 1, ist_ref[0:1, :])

    low = n_hi < k_top
    hi0 = jnp.where(low, NEG_BIG, mx)
    fst_ref[0:1, :] = jnp.where(low, jnp.minimum(mn, NEG_BIG), jnp.maximum(mn, NEG_BIG))
    fst_ref[1:2, :] = hi0
    fst_ref[2:3, :] = hi0
    ist_ref[...] = jnp.zeros_like(ist_ref)
    settle(hi0)

    def probe(lo, hi, thr, done):
        mid = 0.5 * lo + 0.5 * hi
        c = count_gt(mid)
        live = done == 0
        hit = live & (c == k_top)
        stuck = jnp.where(live & ((mid <= lo) | (mid >= hi)), 2, 0)
        return (jnp.where(live & (c > k_top), mid, lo), jnp.where(live & (c < k_top), mid, hi),
                jnp.where(hit, mid, thr), jnp.where(hit, 1, done), stuck)

    def step(carry):
        it, _ = carry
        lo, hi, thr, done = fst_ref[0:1, :], fst_ref[1:2, :], fst_ref[2:3, :], ist_ref[0:1, :]
        stuck = jnp.zeros_like(done)
        for _ in range(_PROBES_PER_ROUND):
            lo, hi, thr, done, s = probe(lo, hi, thr, done)
            stuck = jnp.maximum(stuck, s)
        fst_ref[0:1, :] = lo
        fst_ref[1:2, :] = hi
        fst_ref[2:3, :] = thr
        ist_ref[0:1, :] = done
        code = jnp.max(jnp.maximum(stuck, 1 - done))
        due = (it >= _SETTLE_FIRST) & ((it - _SETTLE_FIRST) % _SETTLE_EVERY == 0)

        check = (code > 0) & (due | (code >= 2))

        @pl.when(check)
        def _():
            settle(max_le(hi))

        return it + 1, lax.cond(check, lambda: jnp.max(1 - ist_ref[0:1, :]), lambda: code)

    lax.while_loop(lambda carry: (carry[1] > 0) & (carry[0] < _BISECT_CAP), step,
                   (jnp.int32(0), jnp.max(1 - ist_ref[0:1, :])))
    thr_ref[0] = fst_ref[2:3, :]
    cut_ref[0] = ist_ref[3:4, :]

    @pl.when(jnp.max(ist_ref[2:3, :]) > 0)
    def _():
        thr = fst_ref[2:3, :]
        need = ist_ref[1:2, :].astype(F32)
        tri = jnp.where(lax.broadcasted_iota(I32, (tk, tk), 1) <= lax.broadcasted_iota(I32, (tk, tk), 0),
                        1.0, 0.0).astype(BF16)

        def body(c, carry):
            seen, cut = carry
            blk, k0 = chunk(c)
            tie = blk == thr
            rank = jnp.dot(tri, jnp.where(tie, 1.0, 0.0).astype(BF16), preferred_element_type=F32) + seen
            bound = jnp.where(tie, jnp.where(rank <= need, k0 + row_iota + 1, 0), 0)
            return rank[tk - 1:tk, :], jnp.maximum(cut, jnp.max(bound, axis=0, keepdims=True))

        _, cut = lax.fori_loop(0, n_chunks, body, (jnp.zeros((1, tq), F32), jnp.zeros((1, tq), I32)))
        cut_ref[0] = jnp.where(ist_ref[2:3, :] > 0, cut, ist_ref[3:4, :])


def _select_kernel(qit_ref, ki_ref, wit_ref, thr_ref, cut_ref, s_ref, fst_ref, ist_ref, *, tq, tk, n_keys, k_top):
    i = pl.program_id(1)
    q0 = i * tq
    n_chunks = (q0 + tq + tk - 1) // tk
    t_row = q0 + lax.broadcasted_iota(I32, (1, tq), 1)
    row_iota = lax.broadcasted_iota(I32, (tk, tq), 0)
    qit = qit_ref[0]
    w = wit_ref[0]

    def build(c, carry):
        mx, mn, n_hi = carry
        k0 = pl.multiple_of(c * tk, tk)
        s = _scores_t(ki_ref[0, pl.ds(k0, tk), :], qit, w)
        causal = (k0 + row_iota) <= t_row
        sm = jnp.where(causal, s, -jnp.inf)
        s_ref[pl.ds(k0, tk), :] = sm
        return (jnp.maximum(mx, _fold8(sm, jnp.max)),
                jnp.minimum(mn, _fold8(jnp.where(causal, s, jnp.inf), jnp.min)),
                n_hi + _fold8(jnp.where(sm > NEG_BIG, 1, 0), jnp.sum))

    mx, mn, n_hi = lax.fori_loop(
        0, n_chunks, build,
        (jnp.full((_FOLD_ROWS, tq), -jnp.inf, F32), jnp.full((_FOLD_ROWS, tq), jnp.inf, F32),
         jnp.zeros((_FOLD_ROWS, tq), I32)))
    n_nc = (n_keys - 1) - t_row
    _topk_threshold(s_ref, n_chunks, tk, tq, jnp.max(mx, axis=0, keepdims=True),
                    jnp.min(mn, axis=0, keepdims=True), jnp.sum(n_hi, axis=0, keepdims=True),
                    n_nc, k_top, thr_ref, cut_ref, fst_ref, ist_ref)


def _select(qit, ki, wit, *, tq, tk, k_top):
    b, _, _, t = qit.shape
    kern = functools.partial(_select_kernel, tq=tq, tk=tk, n_keys=t, k_top=k_top)
    return pl.pallas_call(
        kern,
        grid=(b, t // tq),
        in_specs=[
            pl.BlockSpec((1, IDX_HEADS, IDX_DIM, tq), lambda bb, i: (bb, 0, 0, i)),
            pl.BlockSpec((1, t, IDX_DIM), lambda bb, i: (bb, 0, 0)),
            pl.BlockSpec((1, 8, tq), lambda bb, i: (bb, 0, i)),
        ],
        out_specs=[pl.BlockSpec((1, 1, tq), lambda bb, i: (bb, 0, i)),
                   pl.BlockSpec((1, 1, tq), lambda bb, i: (bb, 0, i))],
        out_shape=[jax.ShapeDtypeStruct((b, 1, t), F32), jax.ShapeDtypeStruct((b, 1, t), I32)],
        scratch_shapes=[pltpu.VMEM((t, tq), F32), pltpu.VMEM((8, tq), F32), pltpu.VMEM((8, tq), I32)],
        compiler_params=_cparams("parallel", "parallel"),
        name="select",
    )(qit, ki, wit)


_ATTN_SUB_ROWS = 128


def _n_key_tiles(i, tq, tk):
    return (i * tq + tq + tk - 1) // tk


def _attn_kernel(qt_ref, k_ref, vt_ref, qit_ref, ki_ref, wit_ref, vf_ref, cut_ref, o_ref,
                 m_ref, l_ref, acc_ref, bias_ref, lga_ref, lgb_ref, pa_ref, pb_ref, *, tq, tk):
    i = pl.program_id(1)
    j = pl.program_id(2)
    nkt = _n_key_tiles(i, tq, tk)
    sub = _ATTN_SUB_ROWS
    chunks = [slice(c * sub, (c + 1) * sub) for c in range(tk // sub)]

    @pl.when(j == 0)
    def _():
        m_ref[...] = jnp.full_like(m_ref, -jnp.inf)
        l_ref[...] = jnp.zeros_like(l_ref)
        acc_ref[...] = jnp.zeros_like(acc_ref)

    @pl.when(j < nkt)
    def _():
        qit = qit_ref[0]
        w = wit_ref[0]
        vf = vf_ref[0]
        cut = cut_ref[0]
        t_row = i * tq + lax.broadcasted_iota(I32, (1, tq), 1)
        row = lax.broadcasted_iota(I32, (sub, tq), 0)
        for c, rows in enumerate(chunks):
            kidx = j * tk + c * sub + row
            bias = _select_bias(_scores_t(ki_ref[0, rows, :], qit, w), vf, kidx, cut)
            bias_ref[rows, :] = jnp.where(kidx <= t_row, bias, -jnp.inf)

        def logits(h, lg_ref):
            qt = qt_ref[0, h]
            part = jnp.full((8, tq), -jnp.inf, F32)
            for rows in chunks:
                lg = jnp.dot(k_ref[0, h, rows, :], qt, preferred_element_type=F32) + bias_ref[rows, :]
                lg_ref[rows, :] = lg
                part = jnp.maximum(part, jnp.max(lg.reshape(sub // 8, 8, tq), axis=0))
            return jnp.max(part, axis=0, keepdims=True)

        def absorb(h, lg_ref, p_ref, tile_max):
            hrow = pl.ds(h, 1)
            m_old = m_ref[hrow, :]
            m_new = jnp.maximum(m_old, tile_max)
            m_safe = jnp.where(m_new == -jnp.inf, 0.0, m_new)
            alpha = jnp.exp(m_old - m_safe)
            psum = jnp.zeros((8, tq), F32)
            for rows in chunks:
                p = jnp.exp(lg_ref[rows, :] - m_safe)
                psum = psum + jnp.sum(p.reshape(sub // 8, 8, tq), axis=0)
                p_ref[rows, :] = p.astype(BF16)
            l_ref[hrow, :] = alpha * l_ref[hrow, :] + jnp.sum(psum, axis=0, keepdims=True)
            acc_ref[h] = alpha * acc_ref[h] + jnp.dot(vt_ref[0, h], p_ref[...], preferred_element_type=F32)
            m_ref[hrow, :] = m_new

        def two_heads(t, max_a):
            h = 2 * t
            max_b = logits(h + 1, lgb_ref)
            absorb(h, lga_ref, pa_ref, max_a)
            max_a = logits(h + 2, lga_ref)
            absorb(h + 1, lgb_ref, pb_ref, max_b)
            return max_a

        max_a = lax.fori_loop(0, N_HEADS // 2 - 1, two_heads, logits(0, lga_ref))
        max_b = logits(N_HEADS - 1, lgb_ref)
        absorb(N_HEADS - 2, lga_ref, pa_ref, max_a)
        absorb(N_HEADS - 1, lgb_ref, pb_ref, max_b)

    @pl.when(j == nkt - 1)
    def _():
        for h in range(N_HEADS):
            o_ref[0, h] = (acc_ref[h] / l_ref[h:h + 1, :]).astype(o_ref.dtype)


def _attn(qt, k, vt, qit, ki, wit, vf, cut, *, tq, tk):
    b, _, _, t = qt.shape
    kj = lambda i, j: jnp.minimum(j, _n_key_tiles(i, tq, tk) - 1)
    return pl.pallas_call(
        functools.partial(_attn_kernel, tq=tq, tk=tk),
        grid=(b, t // tq, t // tk),
        in_specs=[
            pl.BlockSpec((1, N_HEADS, HEAD_DIM, tq), lambda bb, i, j: (bb, 0, 0, i)),
            pl.BlockSpec((1, N_HEADS, tk, HEAD_DIM), lambda bb, i, j: (bb, 0, kj(i, j), 0)),
            pl.BlockSpec((1, N_HEADS, HEAD_DIM, tk), lambda bb, i, j: (bb, 0, 0, kj(i, j))),
            pl.BlockSpec((1, IDX_HEADS, IDX_DIM, tq), lambda bb, i, j: (bb, 0, 0, i)),
            pl.BlockSpec((1, tk, IDX_DIM), lambda bb, i, j: (bb, kj(i, j), 0)),
            pl.BlockSpec((1, 8, tq), lambda bb, i, j: (bb, 0, i)),
            pl.BlockSpec((1, 1, tq), lambda bb, i, j: (bb, 0, i)),
            pl.BlockSpec((1, 1, tq), lambda bb, i, j: (bb, 0, i)),
        ],
        out_specs=pl.BlockSpec((1, N_HEADS, HEAD_DIM, tq), lambda bb, i, j: (bb, 0, 0, i)),
        out_shape=jax.ShapeDtypeStruct((b, N_HEADS, HEAD_DIM, t), BF16),
        scratch_shapes=[pltpu.VMEM((N_HEADS, tq), F32), pltpu.VMEM((N_HEADS, tq), F32),
                        pltpu.VMEM((N_HEADS, HEAD_DIM, tq), F32),
                        pltpu.VMEM((tk, tq), F32), pltpu.VMEM((tk, tq), F32), pltpu.VMEM((tk, tq), F32),
                        pltpu.VMEM((tk, tq), BF16), pltpu.VMEM((tk, tq), BF16)],
        compiler_params=_cparams("parallel", "parallel", "arbitrary"),
        name="attn",
    )(qt, k, vt, qit, ki, wit, vf, cut)


def _zoh(ar, ai, ldt):
    dt = jnp.exp(ldt)
    mag = jnp.exp(dt * ar)
    abr = mag * jnp.cos(dt * ai)
    abi = mag * jnp.sin(dt * ai)
    den = ar * ar + ai * ai
    nr = abr - 1.0
    ni = abi
    return abr, abi, (nr * ar + ni * ai) / den, (ni * ar - nr * ai) / den


def _disc_kernel(ar_ref, ai_ref, ldt_ref, ar16_ref, ai16_ref, ldt16_ref, br_ref, bi_ref,
                 pwr_ref, pwi_ref, bbr_ref, bbi_ref):
    abr, abi, _, _ = _zoh(ar_ref[...], ai_ref[...], ldt_ref[...])
    pr, pi = abr, abi
    for j in range(8):
        pwr_ref[j:j + 1, :] = pr
        pwi_ref[j:j + 1, :] = pi
        pr, pi = pr * abr - pi * abi, pr * abi + pi * abr
    _, _, fr, fi = _zoh(ar16_ref[...], ai16_ref[...], ldt16_ref[...])
    br = br_ref[...]
    bi = bi_ref[...]
    bbr_ref[...] = fr * br - fi * bi
    bbi_ref[...] = fr * bi + fi * br


def _discretize(a_re, a_im, log_dt, b_re, b_im):
    flat = lambda a: a.reshape(1, -1)
    ldt = jnp.broadcast_to(log_dt[:, None], (N_GROUPS, STATE_DIM))
    rep = lambda a: flat(jnp.broadcast_to(a[:, :, None], (N_GROUPS, STATE_DIM, GROUP_CH)))
    n16 = N_STATE * GROUP_CH
    pwr, pwi, bbr, bbi = pl.pallas_call(
        _disc_kernel,
        out_shape=[jax.ShapeDtypeStruct((8, N_STATE), F32), jax.ShapeDtypeStruct((8, N_STATE), F32),
                   jax.ShapeDtypeStruct((1, n16), F32), jax.ShapeDtypeStruct((1, n16), F32)],
        name="s5_discretize",
    )(flat(a_re), flat(a_im), flat(ldt), rep(a_re), rep(a_im), rep(ldt), flat(b_re), flat(b_im))
    return pwr, pwi, bbr.reshape(N_GROUPS, STATE_DIM, GROUP_CH), bbi.reshape(N_GROUPS, STATE_DIM, GROUP_CH)


def _block_diag_in(bb):
    eye = jnp.eye(N_GROUPS, dtype=bb.dtype)
    return jnp.einsum('gpc,gh->gchp', bb, eye).reshape(SSM_WIDTH, N_STATE)


def _block_diag_out(c):
    eye = jnp.eye(N_GROUPS, dtype=c.dtype)
    return jnp.einsum('gcp,gh->gphc', c, eye).reshape(N_STATE, SSM_WIDTH)


_LANE_CHUNK = 512


def _ssm_kernel(u_ref, bblk_ref, cblk_ref, pw_ref, step_ref, d_ref, y_ref, xf_ref, x_ref, carry_ref, *, tt):
    ts = pl.program_id(1)

    @pl.when(ts == 0)
    def _():
        carry_ref[...] = jnp.zeros_like(carry_ref)

    u = u_ref[0]
    ub = u.astype(BF16)
    n_chunks = 2 * N_STATE // _LANE_CHUNK
    ch = _LANE_CHUNK // STATE_DIM * GROUP_CH
    chans = [slice((c % (N_STATE // _LANE_CHUNK)) * ch, (c % (N_STATE // _LANE_CHUNK) + 1) * ch)
             for c in range(n_chunks)]
    for c in range(n_chunks):
        cols = slice(c * _LANE_CHUNK, (c + 1) * _LANE_CHUNK)
        x_ref[:, cols] = jnp.dot(ub[:, chans[c]], bblk_ref[chans[c], cols], preferred_element_type=F32)

    def group(r, carry):
        r0 = pl.multiple_of(r * 8, 8)
        for c in range(N_STATE // _LANE_CHUNK):
            re = pl.ds(c * _LANE_CHUNK, _LANE_CHUNK)
            im = pl.ds(N_STATE + c * _LANE_CHUNK, _LANE_CHUNK)
            xr = x_ref[pl.ds(r0, 8), re]
            xi = x_ref[pl.ds(r0, 8), im]
            for k, d in enumerate((1, 2, 4)):
                ar = step_ref[k, :, re]
                ai = step_ref[k, :, im]
                sr = pltpu.roll(xr, d, 0)
                si = pltpu.roll(xi, d, 0)
                xr, xi = xr + (ar * sr - ai * si), xi + (ar * si + ai * sr)
            cr = carry_ref[:, re]
            ci = carry_ref[:, im]
            pr = pw_ref[:, re]
            pi = pw_ref[:, im]
            xr, xi = xr + (pr * cr - pi * ci), xi + (pr * ci + pi * cr)
            x_ref[pl.ds(r0, 8), re] = xr
            x_ref[pl.ds(r0, 8), im] = xi
            carry_ref[:, re] = xr[7:8, :]
            carry_ref[:, im] = xi[7:8, :]
        return carry

    lax.fori_loop(0, tt // 8, group, 0)
    y_ref[0] = d_ref[...] * u
    for c in range(n_chunks):
        cols = slice(c * _LANE_CHUNK, (c + 1) * _LANE_CHUNK)
        y_ref[0, :, chans[c]] += jnp.dot(x_ref[:, cols].astype(BF16), cblk_ref[cols, chans[c]],
                                         preferred_element_type=F32)
    xf_ref[0] = carry_ref[...]


def _ssm(u, bblk, cblk, pw, d_skip, *, tt):
    b, t, _ = u.shape
    rows = jnp.arange(8)[None, :, None]
    steps = jnp.stack([jnp.where(rows >= dd, pw[dd - 1][None, None, :], 0.0)[0] for dd in (1, 2, 4)])
    return pl.pallas_call(
        functools.partial(_ssm_kernel, tt=tt),
        grid=(b, t // tt),
        in_specs=[
            pl.BlockSpec((1, tt, SSM_WIDTH), lambda bb, s: (bb, s, 0)),
            pl.BlockSpec((SSM_WIDTH, 2 * N_STATE), lambda bb, s: (0, 0)),
            pl.BlockSpec((2 * N_STATE, SSM_WIDTH), lambda bb, s: (0, 0)),
            pl.BlockSpec((8, 2 * N_STATE), lambda bb, s: (0, 0)),
            pl.BlockSpec((3, 8, 2 * N_STATE), lambda bb, s: (0, 0, 0)),
            pl.BlockSpec((1, SSM_WIDTH), lambda bb, s: (0, 0)),
        ],
        out_specs=[pl.BlockSpec((1, tt, SSM_WIDTH), lambda bb, s: (bb, s, 0)),
                   pl.BlockSpec((1, 1, 2 * N_STATE), lambda bb, s: (bb, 0, 0))],
        out_shape=[jax.ShapeDtypeStruct((b, t, SSM_WIDTH), F32),
                   jax.ShapeDtypeStruct((b, 1, 2 * N_STATE), F32)],
        scratch_shapes=[pltpu.VMEM((tt, 2 * N_STATE), F32), pltpu.VMEM((1, 2 * N_STATE), F32)],
        compiler_params=_cparams("parallel", "arbitrary"),
        name="s5_scan",
    )(u, bblk, cblk, pw, steps, d_skip)


def _ssm_step_kernel(u_ref, x0_ref, bblk_ref, cblk_ref, pw_ref, d_ref, y_ref, x_ref):
    u = u_ref[...]
    bu = jnp.dot(u, bblk_ref[...], preferred_element_type=F32, precision=lax.Precision.HIGHEST)
    ar = pw_ref[0:1, :N_STATE]
    ai = pw_ref[0:1, N_STATE:]
    x0r = x0_ref[:, :N_STATE]
    x0i = x0_ref[:, N_STATE:]
    xr = ar * x0r - ai * x0i + bu[:, :N_STATE]
    xi = ar * x0i + ai * x0r + bu[:, N_STATE:]
    x_ref[:, :N_STATE] = xr
    x_ref[:, N_STATE:] = xi
    y = jnp.dot(x_ref[...], cblk_ref[...], preferred_element_type=F32, precision=lax.Precision.HIGHEST)
    y_ref[...] = y + d_ref[...] * u


def _ssm_step(u, x0, bblk, cblk, pw, d_skip):
    n = u.shape[0]
    return pl.pallas_call(
        _ssm_step_kernel,
        out_shape=[jax.ShapeDtypeStruct((n, SSM_WIDTH), F32), jax.ShapeDtypeStruct((n, 2 * N_STATE), F32)],
        compiler_params=pltpu.CompilerParams(vmem_limit_bytes=VMEM_LIMIT_BYTES),
        name="s5_step",
    )(u, x0, bblk, cblk, pw, d_skip)


def _gelu_tanh(x):
    c = np.float32(np.sqrt(2.0 / np.pi))
    return 0.5 * x * (1.0 + jnp.tanh(c * (x + 0.044715 * (x * x * x))))


def _merge_kernel(x_ref, oa_ref, ys_ref, g_ref, wgt_ref, wglu_ref, bglu_ref, wpa_ref, wpb_ref, wout_ref, o_ref):
    x = x_ref[...]
    h = _rmsnorm(x, g_ref[...]).astype(BF16)
    gates = _sigmoid(jnp.dot(h, wgt_ref[...], preferred_element_type=F32))
    ys = _gelu_tanh(ys_ref[...])
    glu = jnp.dot(ys.astype(BF16), wglu_ref[...], preferred_element_type=F32) + bglu_ref[...]
    ob = ys * _sigmoid(glu)
    pa = jnp.dot(oa_ref[...], wpa_ref[...], preferred_element_type=F32)
    pb = jnp.dot(ob.astype(BF16), wpb_ref[...], preferred_element_type=F32)
    merged = gates[:, :D_MODEL] * pa + gates[:, D_MODEL:] * pb
    o_ref[...] = x + jnp.dot(merged.astype(BF16), wout_ref[...], preferred_element_type=F32)


def _merge(x, oa, ys, g, wgt, wglu, bglu, wpa, wpb, wout, *, tm):
    m, d = x.shape
    row = lambda n: pl.BlockSpec((tm, n), lambda i: (i, 0))
    full = lambda a: pl.BlockSpec(a.shape, lambda i: (0, 0))
    return pl.pallas_call(
        _merge_kernel,
        grid=(m // tm,),
        in_specs=[row(d), row(D_ATTN), row(SSM_WIDTH), full(g), full(wgt), full(wglu), full(bglu),
                  full(wpa), full(wpb), full(wout)],
        out_specs=row(d),
        out_shape=jax.ShapeDtypeStruct((m, d), F32),
        compiler_params=_cparams("parallel"),
        name="merge",
    )(x, oa, ys, g, wgt, wglu, bglu, wpa, wpb, wout)


_PAGES_PER_STEP = 8


def _page_specs(block, n_pages):
    last = n_pages // _PAGES_PER_STEP - 1

    def spec(r):
        def index_map(b, p, pt):
            return (pt[b, jnp.minimum(p, last) * _PAGES_PER_STEP + r],) + (0,) * (len(block) - 1)
        return pl.BlockSpec(block, index_map)
    return [spec(r) for r in range(_PAGES_PER_STEP)]


def _per_seq(shape):
    return pl.BlockSpec((1,) + shape, lambda b, p, pt: (b,) + (0,) * len(shape))


def _sample_scores_kernel(pt_ref, qi_ref, w_ref, knew_ref, *refs, page_size, n_steps):
    pages = refs[:_PAGES_PER_STEP]
    o_ref = refs[_PAGES_PER_STEP]
    p = pl.program_id(1)
    qi = qi_ref[0]
    w = w_ref[0]

    def score(kt):
        d = jnp.dot(qi, kt, preferred_element_type=F32)
        return jnp.sum(jnp.maximum(d, 0.0) * w, axis=0, keepdims=True)

    @pl.when(p < n_steps)
    def _():
        for r in range(_PAGES_PER_STEP):
            o_ref[0, :, r * page_size:(r + 1) * page_size] = score(pages[r][0].astype(BF16))

    @pl.when(p == n_steps)
    def _():
        o_ref[0] = jnp.full(o_ref.shape[1:], -jnp.inf, F32)
        lane = lax.broadcasted_iota(I32, (1, page_size), 1)
        o_ref[0, :, 0:page_size] = jnp.where(lane == 0, score(knew_ref[0]), -jnp.inf)


def _sample_scores(page_table, qi_rows, w_col, ki_new_t, kidx_t):
    db, n_pages = page_table.shape
    _, _, page_size = kidx_t.shape
    assert n_pages % _PAGES_PER_STEP == 0
    steps = n_pages // _PAGES_PER_STEP
    block = _PAGES_PER_STEP * page_size
    return pl.pallas_call(
        functools.partial(_sample_scores_kernel, page_size=page_size, n_steps=steps),
        grid_spec=pltpu.PrefetchScalarGridSpec(
            num_scalar_prefetch=1,
            grid=(db, steps + 1),
            in_specs=[_per_seq((8, IDX_DIM)), _per_seq((8, 1)), _per_seq((IDX_DIM, page_size))]
                     + _page_specs((1, IDX_DIM, page_size), n_pages),
            out_specs=pl.BlockSpec((1, 1, block), lambda b, p, pt: (b, 0, p)),
        ),
        out_shape=jax.ShapeDtypeStruct((db, 1, (steps + 1) * block), F32),
        compiler_params=_cparams("parallel", "arbitrary"),
        name="sample_scores",
    )(page_table, qi_rows, w_col, ki_new_t, *([kidx_t] * _PAGES_PER_STEP))


def _sample_select_kernel(st_ref, thr_ref, cut_ref, s_ref, fst_ref, ist_ref, *, tk, tq, n_keys, k_top):
    n_chunks = st_ref.shape[0] // tk
    row_iota = lax.broadcasted_iota(I32, (tk, tq), 0)

    def build(c, carry):
        mx, mn, n_hi = carry
        k0 = pl.multiple_of(c * tk, tk)
        s = st_ref[pl.ds(k0, tk), :]
        real = (k0 + row_iota) < n_keys
        sm = jnp.where(real, s, -jnp.inf)
        s_ref[pl.ds(k0, tk), :] = sm
        return (jnp.maximum(mx, _fold8(sm, jnp.max)),
                jnp.minimum(mn, _fold8(jnp.where(real, s, jnp.inf), jnp.min)),
                n_hi + _fold8(jnp.where(sm > NEG_BIG, 1, 0), jnp.sum))

    mx, mn, n_hi = lax.fori_loop(
        0, n_chunks, build,
        (jnp.full((_FOLD_ROWS, tq), -jnp.inf, F32), jnp.full((_FOLD_ROWS, tq), jnp.inf, F32),
         jnp.zeros((_FOLD_ROWS, tq), I32)))
    _topk_threshold(s_ref, n_chunks, tk, tq, jnp.max(mx, axis=0, keepdims=True),
                    jnp.min(mn, axis=0, keepdims=True), jnp.sum(n_hi, axis=0, keepdims=True),
                    jnp.zeros((1, tq), I32), k_top, thr_ref, cut_ref, fst_ref, ist_ref)


def _sample_select(scores_t, *, n_keys, k_top, tk):
    n_rows, tq = scores_t.shape
    kern = functools.partial(_sample_select_kernel, tk=tk, tq=tq, n_keys=n_keys, k_top=k_top)
    return pl.pallas_call(
        kern,
        out_shape=[jax.ShapeDtypeStruct((1, 1, tq), F32), jax.ShapeDtypeStruct((1, 1, tq), I32)],
        scratch_shapes=[pltpu.VMEM((n_rows, tq), F32), pltpu.VMEM((8, tq), F32), pltpu.VMEM((8, tq), I32)],
        compiler_params=pltpu.CompilerParams(vmem_limit_bytes=VMEM_LIMIT_BYTES),
        name="sample_select",
    )(scores_t)


def _sample_attn_kernel(pt_ref, qb_ref, s_ref, thr_ref, cut_ref, kself_ref, vself_ref, *refs,
                        page_size, n_steps):
    kp = refs[:_PAGES_PER_STEP]
    vp = refs[_PAGES_PER_STEP:2 * _PAGES_PER_STEP]
    o_ref, m_ref, l_ref, acc_ref = refs[2 * _PAGES_PER_STEP:]
    step = pl.program_id(1)
    thr = thr_ref[0]
    cut = cut_ref[0]
    lane = lax.broadcasted_iota(I32, (1, page_size), 1)
    is_self = step == n_steps

    @pl.when(step == 0)
    def _():
        m_ref[...] = jnp.full_like(m_ref, -jnp.inf)
        l_ref[...] = jnp.zeros_like(l_ref)
        acc_ref[...] = jnp.zeros_like(acc_ref)

    logits = []
    for r in range(_PAGES_PER_STEP):
        kidx = (step * _PAGES_PER_STEP + r) * page_size + lane
        bias = _select_bias(s_ref[0, :, r * page_size:(r + 1) * page_size], thr, kidx, cut)
        rows = []
        for h in range(N_HEADS):
            kt = kp[r][0, h]
            if r == 0:
                kt = jnp.where(is_self, kself_ref[0, h], kt)
            rows.append(jnp.sum(kt * qb_ref[0, h], axis=0, keepdims=True))
        logits.append(jnp.concatenate(rows, axis=0) + bias)

    m_old = m_ref[...]
    m_new = m_old
    for lg in logits:
        m_new = jnp.maximum(m_new, jnp.max(lg, axis=1, keepdims=True))
    m_safe = jnp.where(m_new == -jnp.inf, 0.0, m_new)
    alpha = jnp.exp(m_old - m_safe)
    probs = [jnp.exp(lg - m_safe) for lg in logits]
    l_new = alpha * l_ref[...]
    for p in probs:
        l_new = l_new + jnp.sum(p, axis=1, keepdims=True)
    l_ref[...] = l_new
    m_ref[...] = m_new
    for h in range(N_HEADS):
        acc = acc_ref[h] * alpha[h:h + 1, :]
        for r in range(_PAGES_PER_STEP):
            vt = vp[r][0, h]
            if r == 0:
                vt = jnp.where(is_self, vself_ref[0, h], vt)
            acc = acc + vt * probs[r][h:h + 1, :]
        acc_ref[h] = acc

    @pl.when(is_self)
    def _():
        for h in range(N_HEADS):
            o_ref[0, h] = jnp.sum(acc_ref[h], axis=1, keepdims=True) / l_ref[h:h + 1, :]


def _sample_attn(page_table, qb, scores, thr, cut, k_self, v_self, ck_t, cv_t):
    db, n_pages = page_table.shape
    _, nh, hd, page_size = ck_t.shape
    steps = n_pages // _PAGES_PER_STEP
    block = _PAGES_PER_STEP * page_size
    page = (1, nh, hd, page_size)
    return pl.pallas_call(
        functools.partial(_sample_attn_kernel, page_size=page_size, n_steps=steps),
        grid_spec=pltpu.PrefetchScalarGridSpec(
            num_scalar_prefetch=1,
            grid=(db, steps + 1),
            in_specs=[_per_seq(page[1:]),
                      pl.BlockSpec((1, 1, block), lambda b, p, pt: (b, 0, p)),
                      _per_seq((1, 1)), _per_seq((1, 1)), _per_seq(page[1:]), _per_seq(page[1:])]
                     + _page_specs(page, n_pages) + _page_specs(page, n_pages),
            out_specs=_per_seq((nh, hd, 1)),
            scratch_shapes=[pltpu.VMEM((nh, 1), F32), pltpu.VMEM((nh, 1), F32),
                            pltpu.VMEM((nh, hd, page_size), F32)],
        ),
        out_shape=jax.ShapeDtypeStruct((db, nh, hd, 1), F32),
        compiler_params=_cparams("parallel", "arbitrary"),
        name="sample_attn",
    )(page_table, qb, scores, thr, cut, k_self, v_self,
      *([ck_t] * _PAGES_PER_STEP), *([cv_t] * _PAGES_PER_STEP))


def _pick(n, pref):
    for t in pref:
        if n % t == 0:
            return t
    return n


def _pack_w_in(w_in):
    o = np.cumsum([0, D_ATTN, D_ATTN, D_ATTN, IDX_HEADS * IDX_DIM, IDX_DIM, IDX_HEADS, SSM_WIDTH, 2 * D_MODEL])
    z = lambda n: jnp.zeros((D_MODEL, n), w_in.dtype)
    w_feat = jnp.concatenate([w_in[:, o[0]:o[6]], z(_TEND - _TWI - IDX_HEADS)], axis=1)
    w_tok = jnp.concatenate([w_in[:, o[1]:o[2]], w_in[:, o[4]:o[5]], z(_RU - _RKI - IDX_DIM),
                             w_in[:, o[6]:o[7]]], axis=1)
    return w_feat.T.astype(BF16), w_tok.astype(BF16), w_in[:, o[7]:o[8]].astype(BF16)


def _prompt_layer(x, lw, ssm, final_norm):
    bsz, seq, d = x.shape
    m = bsz * seq
    tm = _pick(m, (512, 256, 128, 64, 32, 16, 8))
    x1 = _ffn(x.reshape(m, d), *lw["ffn1"], lw["g_final"], final_norm=False, tm=tm, tf=lw["tf"])
    qt, kt_f, vt_f, vt, kh, qit, kit_f, kib, wit, u_f = _proj(
        x1.reshape(bsz, seq, d), lw["g_mix"], lw["w_feat"], lw["w_tok"], tm=_pick(seq, (512, 256, 128)))
    heads = lambda a, nh: a.reshape(bsz, nh, -1, seq)
    qt, vt, qit = heads(qt, N_HEADS), heads(vt, N_HEADS), heads(qit, IDX_HEADS)

    k_top = min(TOPK_MAX, seq // 4)
    vf, cut = _select(qit, kib, wit, tq=_pick(seq, (512, 256, 128)), tk=_pick(seq, (256, 128)), k_top=k_top)
    o_t = _attn(qt, kh, vt, qit, kib, wit, vf, cut,
                tq=_pick(seq, (512, 256, 128)), tk=_pick(seq, (512, 256, 128)))
    oa = o_t.transpose(0, 3, 1, 2).reshape(m, D_ATTN)

    tt = _pick(seq, (256, 128, 64, 32, 16, 8))
    ys, xfin = _ssm(u_f, ssm["bblk"].astype(BF16), ssm["cblk"].astype(BF16), ssm["pw"], ssm["d_skip"], tt=tt)
    x2 = _merge(x1, oa, ys.reshape(m, SSM_WIDTH), *lw["merge"], tm=tm)
    y = _ffn(x2, *lw["ffn2"], lw["g_final"], final_norm=final_norm, tm=tm, tf=lw["tf"])
    new = (heads(kt_f, N_HEADS).transpose(0, 3, 1, 2), heads(vt_f, N_HEADS).transpose(0, 3, 1, 2),
           kit_f.transpose(0, 2, 1),
           xfin[:, 0, :N_STATE].reshape(bsz, N_GROUPS, STATE_DIM),
           xfin[:, 0, N_STATE:].reshape(bsz, N_GROUPS, STATE_DIM))
    return y.reshape(bsz, seq, d), new


def _sample_layer(x, lw, ssm, final_norm, cache_k, cache_v, cache_kidx, st_re, st_im, page_table):
    db, ds, d = x.shape
    assert ds == 1, "one new token per sample sequence"
    n_pool, page_size = cache_k.shape[0], cache_k.shape[1]
    past_len = page_table.shape[1] * page_size
    s1 = _ffn(x.reshape(db, d), *lw["ffn1"], lw["g_final"], final_norm=False, tm=db, tf=lw["tf"])
    qt, kt_f, vt_f, _, _, qit, kit_f, kib, wit, u_f = _proj(
        s1.reshape(1, db, d), lw["g_mix"], lw["w_feat"], lw["w_tok"], tm=db)
    q_b, k_f, v_f, qi_b, ki_f, wi_f = (a[0].T for a in (qt, kt_f, vt_f, qit, kit_f, wit))
    ki_b, u_f = kib[0], u_f[0]

    kidx_t = cache_kidx.transpose(0, 2, 1)
    ck_t = cache_k.transpose(0, 2, 3, 1)
    cv_t = cache_v.transpose(0, 2, 3, 1)
    lane0 = lambda a: jnp.zeros(a.shape + (page_size,), a.dtype).at[..., 0].set(a)

    qi_rows = jnp.zeros((db, 8, IDX_DIM), BF16).at[:, :IDX_HEADS].set(qi_b.reshape(db, IDX_HEADS, IDX_DIM))
    scores = _sample_scores(page_table, qi_rows, wi_f[:, :, None], lane0(ki_b), kidx_t)

    n_keys = past_len + 1
    k_top = min(TOPK_MAX, n_keys // 4)
    n = scores.shape[2]
    scores_t = jnp.zeros((n, -(-db // 128) * 128), F32).at[:, :db].set(scores[:, 0, :].T)
    thr, cut = _sample_select(scores_t, n_keys=n_keys, k_top=k_top, tk=256)

    qb = jnp.broadcast_to(q_b.astype(F32).reshape(db, N_HEADS, HEAD_DIM, 1), (db, N_HEADS, HEAD_DIM, page_size))
    o_s = _sample_attn(page_table, qb, scores, thr[0, 0, :db].reshape(db, 1, 1), cut[0, 0, :db].reshape(db, 1, 1),
                       lane0(k_f.reshape(db, N_HEADS, HEAD_DIM)), lane0(v_f.reshape(db, N_HEADS, HEAD_DIM)),
                       ck_t, cv_t)
    oa = o_s.reshape(db, D_ATTN).astype(BF16)

    x0 = jnp.concatenate([st_re.reshape(db, N_STATE), st_im.reshape(db, N_STATE)], axis=1)
    ys, x_new = _ssm_step(u_f, x0, ssm["bblk"], ssm["cblk"], ssm["pw"], ssm["d_skip"])
    s2 = _merge(s1, oa, ys, *lw["merge"], tm=db)
    y = _ffn(s2, *lw["ffn2"], lw["g_final"], final_norm=final_norm, tm=db, tf=lw["tf"])
    new = (k_f.reshape(db, ds, N_HEADS, HEAD_DIM), v_f.reshape(db, ds, N_HEADS, HEAD_DIM),
           ki_f.reshape(db, ds, IDX_DIM),
           x_new[:, :N_STATE].reshape(db, N_GROUPS, STATE_DIM),
           x_new[:, N_STATE:].reshape(db, N_GROUPS, STATE_DIM))
    return y.reshape(db, ds, d), new


def kernel(x_prompt, x_sample, cache_k, cache_v, cache_kidx, state_ssm_re, state_ssm_im, page_table,
           g_ffn1, w1_gate, w1_up, w1_down, g_mix, w_in, a_re, a_im, log_dt, b_re, b_im, c_re, c_im,
           d_skip, w_glu, b_glu, w_pa, w_pb, w_out, g_ffn2, w2_gate, w2_up, w2_down, g_final):
    depth = w_in.shape[0]
    row = lambda a: a.reshape(1, -1)
    bf = lambda a: a.astype(BF16)
    xp, xs = x_prompt, x_sample
    new_p, new_s = [], []
    for l in range(depth):
        w_feat, w_tok, w_gates = _pack_w_in(w_in[l])
        lw = dict(
            ffn1=(row(g_ffn1[l]), bf(w1_gate[l]), bf(w1_up[l]), bf(w1_down[l])),
            ffn2=(row(g_ffn2[l]), bf(w2_gate[l]), bf(w2_up[l]), bf(w2_down[l])),
            merge=(row(g_mix[l]), w_gates, bf(w_glu[l]), row(b_glu[l]), bf(w_pa[l]), bf(w_pb[l]), bf(w_out[l])),
            g_mix=row(g_mix[l]), w_feat=w_feat, w_tok=w_tok, g_final=row(g_final),
            tf=_pick(w1_gate.shape[2], (1408, 1024, 512, 256, 128)),
        )
        pwr, pwi, bbr, bbi = _discretize(a_re[l], a_im[l], log_dt[l], b_re[l], b_im[l])
        ssm = dict(
            pw=jnp.concatenate([pwr, pwi], axis=1),
            bblk=jnp.concatenate([_block_diag_in(bbr), _block_diag_in(bbi)], axis=1),
            cblk=jnp.concatenate([_block_diag_out(c_re[l]), -_block_diag_out(c_im[l])], axis=0),
            d_skip=row(d_skip[l]),
        )
        last = l == depth - 1
        xp, st_p = _prompt_layer(xp, lw, ssm, last)
        xs, st_s = _sample_layer(xs, lw, ssm, last, cache_k[l], cache_v[l], cache_kidx[l],
                                 state_ssm_re[l], state_ssm_im[l], page_table)
        new_p.append(st_p)
        new_s.append(st_s)
    stack = lambda states, i: jnp.stack([s[i] for s in states])
    return (xp, xs) + tuple(stack(new_p, i) for i in range(5)) + tuple(stack(new_s, i) for i in range(5))
```

```python
import functools

import numpy as np
import jax
import jax.numpy as jnp
from jax import lax
from jax.experimental import pallas as pl
from jax.experimental.pallas import tpu as pltpu

F32 = jnp.float32
BF16 = jnp.bfloat16
I32 = jnp.int32

D_MODEL = 1024
N_HEADS = 8
HEAD_DIM = 64
D_ATTN = N_HEADS * HEAD_DIM
IDX_HEADS = 4
IDX_DIM = 64
TOPK_MAX = 256
SSM_WIDTH = 512
GROUP_CH = 16
N_GROUPS = SSM_WIDTH // GROUP_CH
STATE_DIM = 64
N_STATE = N_GROUPS * STATE_DIM
NORM_EPS = 1e-6
NEG_BIG = -1e30

INT_MAX = 2 ** 31 - 1

VMEM_LIMIT_BYTES = 56 * 1024 * 1024


def _cparams(*sem):
    return pltpu.CompilerParams(dimension_semantics=sem, vmem_limit_bytes=VMEM_LIMIT_BYTES)


def _rmsnorm(x, g):
    y = x * lax.rsqrt(jnp.mean(x * x, axis=-1, keepdims=True) + NORM_EPS)
    return y * g


def _sigmoid(x):
    return 1.0 / (1.0 + jnp.exp(-x))


def _ffn_kernel(x_ref, g_ref, wg_ref, wu_ref, wd_ref, gf_ref, o_ref, *, final_norm, tf):
    x = x_ref[...]
    xn = _rmsnorm(x, g_ref[...]).astype(BF16)
    acc = None
    for f0 in range(0, wg_ref.shape[1], tf):
        gate = jnp.dot(xn, wg_ref[:, f0:f0 + tf], preferred_element_type=F32)
        up = jnp.dot(xn, wu_ref[:, f0:f0 + tf], preferred_element_type=F32)
        act = (gate * _sigmoid(gate)) * up
        part = jnp.dot(act.astype(BF16), wd_ref[f0:f0 + tf, :], preferred_element_type=F32)
        acc = part if acc is None else acc + part
    y = x + 0.5 * acc
    if final_norm:
        y = _rmsnorm(y, gf_ref[...])
    o_ref[...] = y


def _ffn(x, g, wg, wu, wd, g_final, *, final_norm, tm, tf):
    m, d = x.shape
    resident = lambda a: pl.BlockSpec(a.shape, lambda i: (0, 0), pipeline_mode=pl.Buffered(1))
    return pl.pallas_call(
        functools.partial(_ffn_kernel, final_norm=final_norm, tf=tf),
        grid=(m // tm,),
        in_specs=[pl.BlockSpec((tm, d), lambda i: (i, 0)), resident(g), resident(wg), resident(wu),
                  resident(wd), resident(g_final)],
        out_specs=pl.BlockSpec((tm, d), lambda i: (i, 0)),
        out_shape=jax.ShapeDtypeStruct((m, d), F32),
        compiler_params=_cparams("parallel"),
        name="ffn",
    )(x, g, wg, wu, wd, g_final)


_TQ, _TK, _TV, _TQI, _TKI, _TWI, _TEND = 0, 512, 1024, 1536, 1792, 1856, 1872
_RK, _RKI, _RU, _REND = 0, 512, 640, 1152


def _proj_kernel(x_ref, g_ref, wt_ref, wr_ref, qt_ref, ktf_ref, vtf_ref, vt_ref, kh_ref,
                 qit_ref, kitf_ref, kib_ref, wit_ref, u_ref):
    h = _rmsnorm(x_ref[0], g_ref[...]).astype(BF16)
    pt = lax.dot_general(wt_ref[...], h, (((1,), (1,)), ((), ())), preferred_element_type=F32)
    pr = jnp.dot(h, wr_ref[...], preferred_element_type=F32)
    qt_ref[0] = (pt[_TQ:_TK] * (HEAD_DIM ** -0.5)).astype(BF16)
    ktf_ref[0] = pt[_TK:_TV]
    v = pt[_TV:_TQI]
    vtf_ref[0] = v
    vt_ref[0] = v.astype(BF16)
    qit_ref[0] = pt[_TQI:_TKI].astype(BF16)
    kitf_ref[0] = pt[_TKI:_TWI]
    wit_ref[0] = pt[_TWI:_TWI + 8] * ((IDX_HEADS * IDX_DIM) ** -0.5)
    k = pr[:, _RK:_RKI].astype(BF16)
    for hd in range(N_HEADS):
        kh_ref[0, hd] = k[:, hd * HEAD_DIM:(hd + 1) * HEAD_DIM]
    kib_ref[0] = pr[:, _RKI:_RKI + IDX_DIM].astype(BF16)
    u_ref[0] = pr[:, _RU:_REND]


def _proj(x, g, wt, wr, *, tm):
    b, t, d = x.shape
    feat = lambda n: pl.BlockSpec((1, n, tm), lambda bb, i: (bb, 0, i))
    tok = lambda n: pl.BlockSpec((1, tm, n), lambda bb, i: (bb, i, 0))
    outs = [
        (feat(D_ATTN), (b, D_ATTN, t), BF16), (feat(D_ATTN), (b, D_ATTN, t), F32),
        (feat(D_ATTN), (b, D_ATTN, t), F32), (feat(D_ATTN), (b, D_ATTN, t), BF16),
        (pl.BlockSpec((1, N_HEADS, tm, HEAD_DIM), lambda bb, i: (bb, 0, i, 0)), (b, N_HEADS, t, HEAD_DIM), BF16),
        (feat(IDX_HEADS * IDX_DIM), (b, IDX_HEADS * IDX_DIM, t), BF16), (feat(IDX_DIM), (b, IDX_DIM, t), F32),
        (tok(IDX_DIM), (b, t, IDX_DIM), BF16), (feat(8), (b, 8, t), F32), (tok(SSM_WIDTH), (b, t, SSM_WIDTH), F32),
    ]
    return pl.pallas_call(
        _proj_kernel,
        grid=(b, t // tm),
        in_specs=[tok(d), pl.BlockSpec((1, d), lambda bb, i: (0, 0)),
                  pl.BlockSpec((_TEND, d), lambda bb, i: (0, 0)), pl.BlockSpec((d, _REND), lambda bb, i: (0, 0))],
        out_specs=[spec for spec, _, _ in outs],
        out_shape=[jax.ShapeDtypeStruct(shape, dt) for _, shape, dt in outs],
        compiler_params=_cparams("parallel", "parallel"),
        name="proj",
    )(x, g, wt, wr)


def _scores_t(kc, qit, w):
    s = None
    for h in range(IDX_HEADS):
        d = jnp.dot(kc, qit[h], preferred_element_type=F32)
        term = jnp.maximum(d, 0.0) * w[h:h + 1, :]
        s = term if s is None else s + term
    return s


def _select_bias(s, vf, kidx, cut):
    tie = jnp.where(kidx < cut, 0.0, -jnp.inf)
    return jnp.where(s > vf, 0.0, jnp.where(s == vf, tie, -jnp.inf))


_FOLD_ROWS = 16


def _fold8(x, op):
    return op(x.reshape(x.shape[0] // _FOLD_ROWS, _FOLD_ROWS, x.shape[1]), axis=0)


_PROBES_PER_ROUND = 4
_SETTLE_FIRST, _SETTLE_EVERY, _BISECT_CAP = 4, 2, 80


def _topk_threshold(s_ref, n_chunks, tk, tq, mx, mn, n_hi, n_nc, k_top, thr_ref, cut_ref, fst_ref, ist_ref):
    row_iota = lax.broadcasted_iota(I32, (tk, tq), 0)
    zeros_i = jnp.zeros((_FOLD_ROWS, tq), I32)

    def chunk(c):
        k0 = pl.multiple_of(c * tk, tk)
        return s_ref[pl.ds(k0, tk), :], k0

    def count_gt(v):
        def body(c, acc):
            blk, _ = chunk(c)
            return acc + _fold8(jnp.where(blk > v, 1, 0), jnp.sum)
        real = jnp.sum(lax.fori_loop(0, n_chunks, body, zeros_i), axis=0, keepdims=True)
        return real + jnp.where(NEG_BIG > v, n_nc, 0)

    def count_gt_eq(v):
        def body(c, acc):
            blk, _ = chunk(c)
            return (acc[0] + _fold8(jnp.where(blk > v, 1, 0), jnp.sum),
                    acc[1] + _fold8(jnp.where(blk == v, 1, 0), jnp.sum))
        g, e = lax.fori_loop(0, n_chunks, body, (zeros_i, zeros_i))
        g = jnp.sum(g, axis=0, keepdims=True) + jnp.where(NEG_BIG > v, n_nc, 0)
        return g, jnp.sum(e, axis=0, keepdims=True)

    def max_le(v):
        def body(c, acc):
            blk, _ = chunk(c)
            return jnp.maximum(acc, _fold8(jnp.where(blk <= v, blk, -jnp.inf), jnp.max))
        m = jnp.max(lax.fori_loop(0, n_chunks, body, jnp.full((_FOLD_ROWS, tq), -jnp.inf, F32)),
                    axis=0, keepdims=True)
        return jnp.where((n_nc > 0) & (NEG_BIG <= v), jnp.maximum(m, NEG_BIG), m)

    def settle(m):
        gt, eq_real = count_gt_eq(m)
        eq = eq_real + jnp.where(m == NEG_BIG, n_nc, 0)
        found = (ist_ref[0:1, :] == 0) & (gt + eq >= k_top)
        need = k_top - gt
        fst_ref[2:3, :] = jnp.where(found, m, fst_ref[2:3, :])
        ist_ref[1:2, :] = jnp.where(found, need, ist_ref[1:2, :])
        ist_ref[2:3, :] = jnp.where(found, jnp.where(eq_real > need, 1, 0), ist_ref[2:3, :])
        ist_ref[3:4, :] = jnp.where(found, INT_MAX, ist_ref[3:4, :])
        ist_ref[0:1, :] = jnp.where(found, 1, ist_ref[0:1, :])

    low = n_hi < k_top
    hi0 = jnp.where(low, NEG_BIG, mx)
    fst_ref[0:1, :] = jnp.where(low, jnp.minimum(mn, NEG_BIG), jnp.maximum(mn, NEG_BIG))
    fst_ref[1:2, :] = hi0
    fst_ref[2:3, :] = hi0
    ist_ref[...] = jnp.zeros_like(ist_ref)
    settle(hi0)

    def probe(lo, hi, thr, done):
        mid = 0.5 * lo + 0.5 * hi
        c = count_gt(mid)
        live = done == 0
        hit = live & (c == k_top)
        stuck = jnp.where(live & ((mid <= lo) | (mid >= hi)), 2, 0)
        return (jnp.where(live & (c > k_top), mid, lo), jnp.where(live & (c < k_top), mid, hi),
                jnp.where(hit, mid, thr), jnp.where(hit, 1, done), stuck)

    def step(carry):
        it, _ = carry
        lo, hi, thr, done = fst_ref[0:1, :], fst_ref[1:2, :], fst_ref[2:3, :], ist_ref[0:1, :]
        stuck = jnp.zeros_like(done)
        for _ in range(_PROBES_PER_ROUND):
            lo, hi, thr, done, s = probe(lo, hi, thr, done)
            stuck = jnp.maximum(stuck, s)
        fst_ref[0:1, :] = lo
        fst_ref[1:2, :] = hi
        fst_ref[2:3, :] = thr
        ist_ref[0:1, :] = done
        code = jnp.max(jnp.maximum(stuck, 1 - done))
        due = (it >= _SETTLE_FIRST) & ((it - _SETTLE_FIRST) % _SETTLE_EVERY == 0)

        check = (code > 0) & (due | (code >= 2))

        @pl.when(check)
        def _():
            settle(max_le(hi))

        return it + 1, lax.cond(check, lambda: jnp.max(1 - ist_ref[0:1, :]), lambda: code)

    lax.while_loop(lambda carry: (carry[1] > 0) & (carry[0] < _BISECT_CAP), step,
                   (jnp.int32(0), jnp.max(1 - ist_ref[0:1, :])))
    thr_ref[0] = fst_ref[2:3, :]
    cut_ref[0] = ist_ref[3:4, :]

    @pl.when(jnp.max(ist_ref[2:3, :]) > 0)
    def _():
        thr = fst_ref[2:3, :]
        need = ist_ref[1:2, :].astype(F32)
        tri = jnp.where(lax.broadcasted_iota(I32, (tk, tk), 1) <= lax.broadcasted_iota(I32, (tk, tk), 0),
                        1.0, 0.0).astype(BF16)

        def body(c, carry):
            seen, cut = carry
            blk, k0 = chunk(c)
            tie = blk == thr
            rank = jnp.dot(tri, jnp.where(tie, 1.0, 0.0).astype(BF16), preferred_element_type=F32) + seen
            bound = jnp.where(tie, jnp.where(rank <= need, k0 + row_iota + 1, 0), 0)
            return rank[tk - 1:tk, :], jnp.maximum(cut, jnp.max(bound, axis=0, keepdims=True))

        _, cut = lax.fori_loop(0, n_chunks, body, (jnp.zeros((1, tq), F32), jnp.zeros((1, tq), I32)))
        cut_ref[0] = jnp.where(ist_ref[2:3, :] > 0, cut, ist_ref[3:4, :])


def _select_kernel(qit_ref, ki_ref, wit_ref, thr_ref, cut_ref, s_ref, fst_ref, ist_ref, *, tq, tk, n_keys, k_top):
    i = pl.program_id(1)
    q0 = i * tq
    n_chunks = (q0 + tq + tk - 1) // tk
    t_row = q0 + lax.broadcasted_iota(I32, (1, tq), 1)
    row_iota = lax.broadcasted_iota(I32, (tk, tq), 0)
    qit = qit_ref[0]
    w = wit_ref[0]

    def build(c, carry):
        mx, mn, n_hi = carry
        k0 = pl.multiple_of(c * tk, tk)
        s = _scores_t(ki_ref[0, pl.ds(k0, tk), :], qit, w)
        causal = (k0 + row_iota) <= t_row
        sm = jnp.where(causal, s, -jnp.inf)
        s_ref[pl.ds(k0, tk), :] = sm
        return (jnp.maximum(mx, _fold8(sm, jnp.max)),
                jnp.minimum(mn, _fold8(jnp.where(causal, s, jnp.inf), jnp.min)),
                n_hi + _fold8(jnp.where(sm > NEG_BIG, 1, 0), jnp.sum))

    mx, mn, n_hi = lax.fori_loop(
        0, n_chunks, build,
        (jnp.full((_FOLD_ROWS, tq), -jnp.inf, F32), jnp.full((_FOLD_ROWS, tq), jnp.inf, F32),
         jnp.zeros((_FOLD_ROWS, tq), I32)))
    n_nc = (n_keys - 1) - t_row
    _topk_threshold(s_ref, n_chunks, tk, tq, jnp.max(mx, axis=0, keepdims=True),
                    jnp.min(mn, axis=0, keepdims=True), jnp.sum(n_hi, axis=0, keepdims=True),
                    n_nc, k_top, thr_ref, cut_ref, fst_ref, ist_ref)


def _select(qit, ki, wit, *, tq, tk, k_top):
    b, _, _, t = qit.shape
    kern = functools.partial(_select_kernel, tq=tq, tk=tk, n_keys=t, k_top=k_top)
    return pl.pallas_call(
        kern,
        grid=(b, t // tq),
        in_specs=[
            pl.BlockSpec((1, IDX_HEADS, IDX_DIM, tq), lambda bb, i: (bb, 0, 0, i)),
            pl.BlockSpec((1, t, IDX_DIM), lambda bb, i: (bb, 0, 0)),
            pl.BlockSpec((1, 8, tq), lambda bb, i: (bb, 0, i)),
        ],
        out_specs=[pl.BlockSpec((1, 1, tq), lambda bb, i: (bb, 0, i)),
                   pl.BlockSpec((1, 1, tq), lambda bb, i: (bb, 0, i))],
        out_shape=[jax.ShapeDtypeStruct((b, 1, t), F32), jax.ShapeDtypeStruct((b, 1, t), I32)],
        scratch_shapes=[pltpu.VMEM((t, tq), F32), pltpu.VMEM((8, tq), F32), pltpu.VMEM((8, tq), I32)],
        compiler_params=_cparams("parallel", "parallel"),
        name="select",
    )(qit, ki, wit)


_ATTN_SUB_ROWS = 128


def _n_key_tiles(i, tq, tk):
    return (i * tq + tq + tk - 1) // tk


def _attn_kernel(it_ref, jt_ref, qt_ref, k_ref, vt_ref, qit_ref, ki_ref, wit_ref, vf_ref, cut_ref, o_ref,
                 m_ref, l_ref, acc_ref, bias_ref, lga_ref, lgb_ref, pa_ref, pb_ref, *, tq, tk):
    i = it_ref[pl.program_id(1)]
    j = jt_ref[pl.program_id(1)]
    nkt = _n_key_tiles(i, tq, tk)
    sub = _ATTN_SUB_ROWS
    chunks = [slice(c * sub, (c + 1) * sub) for c in range(tk // sub)]

    @pl.when(j == 0)
    def _():
        m_ref[...] = jnp.full_like(m_ref, -jnp.inf)
        l_ref[...] = jnp.zeros_like(l_ref)
        acc_ref[...] = jnp.zeros_like(acc_ref)

    @pl.when(j < nkt)
    def _():
        qit = qit_ref[0]
        w = wit_ref[0]
        vf = vf_ref[0]
        cut = cut_ref[0]
        t_row = i * tq + lax.broadcasted_iota(I32, (1, tq), 1)
        row = lax.broadcasted_iota(I32, (sub, tq), 0)
        for c, rows in enumerate(chunks):
            kidx = j * tk + c * sub + row
            bias = _select_bias(_scores_t(ki_ref[0, rows, :], qit, w), vf, kidx, cut)
            bias_ref[rows, :] = jnp.where(kidx <= t_row, bias, -jnp.inf)

        def logits(h, lg_ref):
            qt = qt_ref[0, h]
            part = jnp.full((8, tq), -jnp.inf, F32)
            for rows in chunks:
                lg = jnp.dot(k_ref[0, h, rows, :], qt, preferred_element_type=F32) + bias_ref[rows, :]
                lg_ref[rows, :] = lg
                part = jnp.maximum(part, jnp.max(lg.reshape(sub // 8, 8, tq), axis=0))
            return jnp.max(part, axis=0, keepdims=True)

        def absorb(h, lg_ref, p_ref, tile_max):
            hrow = pl.ds(h, 1)
            m_old = m_ref[hrow, :]
            m_new = jnp.maximum(m_old, tile_max)
            m_safe = jnp.where(m_new == -jnp.inf, 0.0, m_new)
            alpha = jnp.exp(m_old - m_safe)
            psum = jnp.zeros((8, tq), F32)
            for rows in chunks:
                p = jnp.exp(lg_ref[rows, :] - m_safe)
                psum = psum + jnp.sum(p.reshape(sub // 8, 8, tq), axis=0)
                p_ref[rows, :] = p.astype(BF16)
            l_ref[hrow, :] = alpha * l_ref[hrow, :] + jnp.sum(psum, axis=0, keepdims=True)
            acc_ref[h] = alpha * acc_ref[h] + jnp.dot(vt_ref[0, h], p_ref[...], preferred_element_type=F32)
            m_ref[hrow, :] = m_new

        def two_heads(t, max_a):
            h = 2 * t
            max_b = logits(h + 1, lgb_ref)
            absorb(h, lga_ref, pa_ref, max_a)
            max_a = logits(h + 2, lga_ref)
            absorb(h + 1, lgb_ref, pb_ref, max_b)
            return max_a

        max_a = lax.fori_loop(0, N_HEADS // 2 - 1, two_heads, logits(0, lga_ref))
        max_b = logits(N_HEADS - 1, lgb_ref)
        absorb(N_HEADS - 2, lga_ref, pa_ref, max_a)
        absorb(N_HEADS - 1, lgb_ref, pb_ref, max_b)

    @pl.when(j == nkt - 1)
    def _():
        for h in range(N_HEADS):
            o_ref[0, h] = (acc_ref[h] / l_ref[h:h + 1, :]).astype(o_ref.dtype)


def _attn(qt, k, vt, qit, ki, wit, vf, cut, *, tq, tk):
    b, _, _, t = qt.shape
    pairs = [(i, j) for i in range(t // tq) for j in range(_n_key_tiles(i, tq, tk))]
    i_tab = jnp.asarray([p[0] for p in pairs], I32)
    j_tab = jnp.asarray([p[1] for p in pairs], I32)
    q_tile = lambda *lead: (lambda bb, s, it, jt: (bb,) + lead + (it[s],))
    return pl.pallas_call(
        functools.partial(_attn_kernel, tq=tq, tk=tk),
        grid_spec=pltpu.PrefetchScalarGridSpec(
            num_scalar_prefetch=2,
            grid=(b, len(pairs)),
            in_specs=[
                pl.BlockSpec((1, N_HEADS, HEAD_DIM, tq), q_tile(0, 0)),
                pl.BlockSpec((1, N_HEADS, tk, HEAD_DIM), lambda bb, s, it, jt: (bb, 0, jt[s], 0)),
                pl.BlockSpec((1, N_HEADS, HEAD_DIM, tk), lambda bb, s, it, jt: (bb, 0, 0, jt[s])),
                pl.BlockSpec((1, IDX_HEADS, IDX_DIM, tq), q_tile(0, 0)),
                pl.BlockSpec((1, tk, IDX_DIM), lambda bb, s, it, jt: (bb, jt[s], 0)),
                pl.BlockSpec((1, 8, tq), q_tile(0)),
                pl.BlockSpec((1, 1, tq), q_tile(0)),
                pl.BlockSpec((1, 1, tq), q_tile(0)),
            ],
            out_specs=pl.BlockSpec((1, N_HEADS, HEAD_DIM, tq), q_tile(0, 0)),
            scratch_shapes=[pltpu.VMEM((N_HEADS, tq), F32), pltpu.VMEM((N_HEADS, tq), F32),
                            pltpu.VMEM((N_HEADS, HEAD_DIM, tq), F32),
                            pltpu.VMEM((tk, tq), F32), pltpu.VMEM((tk, tq), F32), pltpu.VMEM((tk, tq), F32),
                            pltpu.VMEM((tk, tq), BF16), pltpu.VMEM((tk, tq), BF16)],
        ),
        out_shape=jax.ShapeDtypeStruct((b, N_HEADS, HEAD_DIM, t), BF16),
        compiler_params=_cparams("parallel", "arbitrary"),
        name="attn",
    )(i_tab, j_tab, qt, k, vt, qit, ki, wit, vf, cut)


def _zoh(ar, ai, ldt):
    dt = jnp.exp(ldt)
    mag = jnp.exp(dt * ar)
    abr = mag * jnp.cos(dt * ai)
    abi = mag * jnp.sin(dt * ai)
    den = ar * ar + ai * ai
    nr = abr - 1.0
    ni = abi
    return abr, abi, (nr * ar + ni * ai) / den, (ni * ar - nr * ai) / den


def _disc_kernel(ar_ref, ai_ref, ldt_ref, ar16_ref, ai16_ref, ldt16_ref, br_ref, bi_ref,
                 pwr_ref, pwi_ref, bbr_ref, bbi_ref):
    abr, abi, _, _ = _zoh(ar_ref[...], ai_ref[...], ldt_ref[...])
    pr, pi = abr, abi
    for j in range(8):
        pwr_ref[j:j + 1, :] = pr
        pwi_ref[j:j + 1, :] = pi
        pr, pi = pr * abr - pi * abi, pr * abi + pi * abr
    _, _, fr, fi = _zoh(ar16_ref[...], ai16_ref[...], ldt16_ref[...])
    br = br_ref[...]
    bi = bi_ref[...]
    bbr_ref[...] = fr * br - fi * bi
    bbi_ref[...] = fr * bi + fi * br


def _discretize(a_re, a_im, log_dt, b_re, b_im):
    flat = lambda a: a.reshape(1, -1)
    ldt = jnp.broadcast_to(log_dt[:, None], (N_GROUPS, STATE_DIM))
    rep = lambda a: flat(jnp.broadcast_to(a[:, :, None], (N_GROUPS, STATE_DIM, GROUP_CH)))
    n16 = N_STATE * GROUP_CH
    pwr, pwi, bbr, bbi = pl.pallas_call(
        _disc_kernel,
        out_shape=[jax.ShapeDtypeStruct((8, N_STATE), F32), jax.ShapeDtypeStruct((8, N_STATE), F32),
                   jax.ShapeDtypeStruct((1, n16), F32), jax.ShapeDtypeStruct((1, n16), F32)],
        name="s5_discretize",
    )(flat(a_re), flat(a_im), flat(ldt), rep(a_re), rep(a_im), rep(ldt), flat(b_re), flat(b_im))
    return pwr, pwi, bbr.reshape(N_GROUPS, STATE_DIM, GROUP_CH), bbi.reshape(N_GROUPS, STATE_DIM, GROUP_CH)


def _block_diag_in(bb):
    eye = jnp.eye(N_GROUPS, dtype=bb.dtype)
    return jnp.einsum('gpc,gh->gchp', bb, eye).reshape(SSM_WIDTH, N_STATE)


def _block_diag_out(c):
    eye = jnp.eye(N_GROUPS, dtype=c.dtype)
    return jnp.einsum('gcp,gh->gphc', c, eye).reshape(N_STATE, SSM_WIDTH)


_LANE_CHUNK = 512


def _ssm_kernel(u_ref, bblk_ref, cblk_ref, pw_ref, step_ref, d_ref, y_ref, xf_ref, x_ref, carry_ref, *, tt):
    ts = pl.program_id(1)

    @pl.when(ts == 0)
    def _():
        carry_ref[...] = jnp.zeros_like(carry_ref)

    u = u_ref[0]
    ub = u.astype(BF16)
    n_chunks = 2 * N_STATE // _LANE_CHUNK
    ch = _LANE_CHUNK // STATE_DIM * GROUP_CH
    chans = [slice((c % (N_STATE // _LANE_CHUNK)) * ch, (c % (N_STATE // _LANE_CHUNK) + 1) * ch)
             for c in range(n_chunks)]
    for c in range(n_chunks):
        cols = slice(c * _LANE_CHUNK, (c + 1) * _LANE_CHUNK)
        x_ref[:, cols] = jnp.dot(ub[:, chans[c]], bblk_ref[chans[c], cols], preferred_element_type=F32)

    def group(r, carry):
        r0 = pl.multiple_of(r * 8, 8)
        for c in range(N_STATE // _LANE_CHUNK):
            re = pl.ds(c * _LANE_CHUNK, _LANE_CHUNK)
            im = pl.ds(N_STATE + c * _LANE_CHUNK, _LANE_CHUNK)
            xr = x_ref[pl.ds(r0, 8), re]
            xi = x_ref[pl.ds(r0, 8), im]
            for k, d in enumerate((1, 2, 4)):
                ar = step_ref[k, :, re]
                ai = step_ref[k, :, im]
                sr = pltpu.roll(xr, d, 0)
                si = pltpu.roll(xi, d, 0)
                xr, xi = xr + (ar * sr - ai * si), xi + (ar * si + ai * sr)
            cr = carry_ref[:, re]
            ci = carry_ref[:, im]
            pr = pw_ref[:, re]
            pi = pw_ref[:, im]
            xr, xi = xr + (pr * cr - pi * ci), xi + (pr * ci + pi * cr)
            x_ref[pl.ds(r0, 8), re] = xr
            x_ref[pl.ds(r0, 8), im] = xi
            carry_ref[:, re] = xr[7:8, :]
            carry_ref[:, im] = xi[7:8, :]
        return carry

    lax.fori_loop(0, tt // 8, group, 0)
    y_ref[0] = d_ref[...] * u
    for c in range(n_chunks):
        cols = slice(c * _LANE_CHUNK, (c + 1) * _LANE_CHUNK)
        y_ref[0, :, chans[c]] += jnp.dot(x_ref[:, cols].astype(BF16), cblk_ref[cols, chans[c]],
                                         preferred_element_type=F32)
    xf_ref[0] = carry_ref[...]


def _ssm(u, bblk, cblk, pw, d_skip, *, tt):
    b, t, _ = u.shape
    rows = jnp.arange(8)[None, :, None]
    steps = jnp.stack([jnp.where(rows >= dd, pw[dd - 1][None, None, :], 0.0)[0] for dd in (1, 2, 4)])
    return pl.pallas_call(
        functools.partial(_ssm_kernel, tt=tt),
        grid=(b, t // tt),
        in_specs=[
            pl.BlockSpec((1, tt, SSM_WIDTH), lambda bb, s: (bb, s, 0)),
            pl.BlockSpec((SSM_WIDTH, 2 * N_STATE), lambda bb, s: (0, 0)),
            pl.BlockSpec((2 * N_STATE, SSM_WIDTH), lambda bb, s: (0, 0)),
            pl.BlockSpec((8, 2 * N_STATE), lambda bb, s: (0, 0)),
            pl.BlockSpec((3, 8, 2 * N_STATE), lambda bb, s: (0, 0, 0)),
            pl.BlockSpec((1, SSM_WIDTH), lambda bb, s: (0, 0)),
        ],
        out_specs=[pl.BlockSpec((1, tt, SSM_WIDTH), lambda bb, s: (bb, s, 0)),
                   pl.BlockSpec((1, 1, 2 * N_STATE), lambda bb, s: (bb, 0, 0))],
        out_shape=[jax.ShapeDtypeStruct((b, t, SSM_WIDTH), F32),
                   jax.ShapeDtypeStruct((b, 1, 2 * N_STATE), F32)],
        scratch_shapes=[pltpu.VMEM((tt, 2 * N_STATE), F32), pltpu.VMEM((1, 2 * N_STATE), F32)],
        compiler_params=_cparams("parallel", "arbitrary"),
        name="s5_scan",
    )(u, bblk, cblk, pw, steps, d_skip)


def _ssm_step_kernel(u_ref, x0_ref, bblk_ref, cblk_ref, pw_ref, d_ref, y_ref, x_ref):
    u = u_ref[...]
    bu = jnp.dot(u, bblk_ref[...], preferred_element_type=F32, precision=lax.Precision.HIGHEST)
    ar = pw_ref[0:1, :N_STATE]
    ai = pw_ref[0:1, N_STATE:]
    x0r = x0_ref[:, :N_STATE]
    x0i = x0_ref[:, N_STATE:]
    xr = ar * x0r - ai * x0i + bu[:, :N_STATE]
    xi = ar * x0i + ai * x0r + bu[:, N_STATE:]
    x_ref[:, :N_STATE] = xr
    x_ref[:, N_STATE:] = xi
    y = jnp.dot(x_ref[...], cblk_ref[...], preferred_element_type=F32, precision=lax.Precision.HIGHEST)
    y_ref[...] = y + d_ref[...] * u


def _ssm_step(u, x0, bblk, cblk, pw, d_skip):
    n = u.shape[0]
    return pl.pallas_call(
        _ssm_step_kernel,
        out_shape=[jax.ShapeDtypeStruct((n, SSM_WIDTH), F32), jax.ShapeDtypeStruct((n, 2 * N_STATE), F32)],
        compiler_params=pltpu.CompilerParams(vmem_limit_bytes=VMEM_LIMIT_BYTES),
        name="s5_step",
    )(u, x0, bblk, cblk, pw, d_skip)


def _gelu_tanh(x):
    c = np.float32(np.sqrt(2.0 / np.pi))
    return 0.5 * x * (1.0 + jnp.tanh(c * (x + 0.044715 * (x * x * x))))


def _merge_kernel(x_ref, oa_ref, ys_ref, g_ref, wgt_ref, wglu_ref, bglu_ref, wpa_ref, wpb_ref, wout_ref, o_ref):
    x = x_ref[...]
    h = _rmsnorm(x, g_ref[...]).astype(BF16)
    gates = _sigmoid(jnp.dot(h, wgt_ref[...], preferred_element_type=F32))
    ys = _gelu_tanh(ys_ref[...])
    glu = jnp.dot(ys.astype(BF16), wglu_ref[...], preferred_element_type=F32) + bglu_ref[...]
    ob = ys * _sigmoid(glu)
    pa = jnp.dot(oa_ref[...], wpa_ref[...], preferred_element_type=F32)
    pb = jnp.dot(ob.astype(BF16), wpb_ref[...], preferred_element_type=F32)
    merged = gates[:, :D_MODEL] * pa + gates[:, D_MODEL:] * pb
    o_ref[...] = x + jnp.dot(merged.astype(BF16), wout_ref[...], preferred_element_type=F32)


def _merge(x, oa, ys, g, wgt, wglu, bglu, wpa, wpb, wout, *, tm):
    m, d = x.shape
    row = lambda n: pl.BlockSpec((tm, n), lambda i: (i, 0))
    full = lambda a: pl.BlockSpec(a.shape, lambda i: (0, 0))
    return pl.pallas_call(
        _merge_kernel,
        grid=(m // tm,),
        in_specs=[row(d), row(D_ATTN), row(SSM_WIDTH), full(g), full(wgt), full(wglu), full(bglu),
                  full(wpa), full(wpb), full(wout)],
        out_specs=row(d),
        out_shape=jax.ShapeDtypeStruct((m, d), F32),
        compiler_params=_cparams("parallel"),
        name="merge",
    )(x, oa, ys, g, wgt, wglu, bglu, wpa, wpb, wout)


_PAGES_PER_STEP = 8
_SCORE_PAGES_PER_STEP = 16


def _page_specs(block, n_pages, per_step):
    last = n_pages // per_step - 1

    def spec(r):
        def index_map(b, p, pt):
            return (pt[b, jnp.minimum(p, last) * per_step + r],) + (0,) * (len(block) - 1)
        return pl.BlockSpec(block, index_map)
    return [spec(r) for r in range(per_step)]


def _per_seq(shape):
    return pl.BlockSpec((1,) + shape, lambda b, p, pt: (b,) + (0,) * len(shape))


def _sample_scores_kernel(pt_ref, qi_ref, w_ref, knew_ref, *refs, page_size, n_steps):
    pages = refs[:-1]
    o_ref = refs[-1]
    p = pl.program_id(1)
    qi = qi_ref[0]
    w = w_ref[0]

    def score(kt):
        d = jnp.dot(qi, kt, preferred_element_type=F32)
        return jnp.sum(jnp.maximum(d, 0.0) * w, axis=0, keepdims=True)

    @pl.when(p < n_steps)
    def _():
        for r, page in enumerate(pages):
            o_ref[0, :, r * page_size:(r + 1) * page_size] = score(page[0].astype(BF16))

    @pl.when(p == n_steps)
    def _():
        o_ref[0] = jnp.full(o_ref.shape[1:], -jnp.inf, F32)
        lane = lax.broadcasted_iota(I32, (1, page_size), 1)
        o_ref[0, :, 0:page_size] = jnp.where(lane == 0, score(knew_ref[0]), -jnp.inf)


def _sample_scores(page_table, qi_rows, w_col, ki_new_t, kidx_t):
    db, n_pages = page_table.shape
    _, _, page_size = kidx_t.shape
    per_step = _pick(n_pages, (_SCORE_PAGES_PER_STEP, _PAGES_PER_STEP))
    assert n_pages % per_step == 0 and per_step % _PAGES_PER_STEP == 0
    steps = n_pages // per_step
    block = per_step * page_size
    return pl.pallas_call(
        functools.partial(_sample_scores_kernel, page_size=page_size, n_steps=steps),
        grid_spec=pltpu.PrefetchScalarGridSpec(
            num_scalar_prefetch=1,
            grid=(db, steps + 1),
            in_specs=[_per_seq((8, IDX_DIM)), _per_seq((8, 1)), _per_seq((IDX_DIM, page_size))]
                     + _page_specs((1, IDX_DIM, page_size), n_pages, per_step),
            out_specs=pl.BlockSpec((1, 1, block), lambda b, p, pt: (b, 0, p)),
        ),
        out_shape=jax.ShapeDtypeStruct((db, 1, (steps + 1) * block), F32),
        compiler_params=_cparams("parallel", "arbitrary"),
        name="sample_scores",
    )(page_table, qi_rows, w_col, ki_new_t, *([kidx_t] * per_step))


def _sample_select_kernel(st_ref, thr_ref, cut_ref, s_ref, fst_ref, ist_ref, *, tk, tq, n_keys, k_top):
    n_chunks = st_ref.shape[0] // tk
    row_iota = lax.broadcasted_iota(I32, (tk, tq), 0)

    def build(c, carry):
        mx, mn, n_hi = carry
        k0 = pl.multiple_of(c * tk, tk)
        s = st_ref[pl.ds(k0, tk), :]
        real = (k0 + row_iota) < n_keys
        sm = jnp.where(real, s, -jnp.inf)
        s_ref[pl.ds(k0, tk), :] = sm
        return (jnp.maximum(mx, _fold8(sm, jnp.max)),
                jnp.minimum(mn, _fold8(jnp.where(real, s, jnp.inf), jnp.min)),
                n_hi + _fold8(jnp.where(sm > NEG_BIG, 1, 0), jnp.sum))

    mx, mn, n_hi = lax.fori_loop(
        0, n_chunks, build,
        (jnp.full((_FOLD_ROWS, tq), -jnp.inf, F32), jnp.full((_FOLD_ROWS, tq), jnp.inf, F32),
         jnp.zeros((_FOLD_ROWS, tq), I32)))
    _topk_threshold(s_ref, n_chunks, tk, tq, jnp.max(mx, axis=0, keepdims=True),
                    jnp.min(mn, axis=0, keepdims=True), jnp.sum(n_hi, axis=0, keepdims=True),
                    jnp.zeros((1, tq), I32), k_top, thr_ref, cut_ref, fst_ref, ist_ref)


def _sample_select(scores_t, *, n_keys, k_top, tk):
    n_rows, tq = scores_t.shape
    kern = functools.partial(_sample_select_kernel, tk=tk, tq=tq, n_keys=n_keys, k_top=k_top)
    return pl.pallas_call(
        kern,
        out_shape=[jax.ShapeDtypeStruct((1, 1, tq), F32), jax.ShapeDtypeStruct((1, 1, tq), I32)],
        scratch_shapes=[pltpu.VMEM((n_rows, tq), F32), pltpu.VMEM((8, tq), F32), pltpu.VMEM((8, tq), I32)],
        compiler_params=pltpu.CompilerParams(vmem_limit_bytes=VMEM_LIMIT_BYTES),
        name="sample_select",
    )(scores_t)


def _sample_attn_kernel(pt_ref, qb_ref, s_ref, thr_ref, cut_ref, kself_ref, vself_ref, *refs,
                        page_size, n_steps):
    kp = refs[:_PAGES_PER_STEP]
    vp = refs[_PAGES_PER_STEP:2 * _PAGES_PER_STEP]
    o_ref, m_ref, l_ref, acc_ref = refs[2 * _PAGES_PER_STEP:]
    step = pl.program_id(1)
    thr = thr_ref[0]
    cut = cut_ref[0]
    lane = lax.broadcasted_iota(I32, (1, page_size), 1)
    is_self = step == n_steps

    @pl.when(step == 0)
    def _():
        m_ref[...] = jnp.full_like(m_ref, -jnp.inf)
        l_ref[...] = jnp.zeros_like(l_ref)
        acc_ref[...] = jnp.zeros_like(acc_ref)

    logits = []
    for r in range(_PAGES_PER_STEP):
        kidx = (step * _PAGES_PER_STEP + r) * page_size + lane
        bias = _select_bias(s_ref[0, :, r * page_size:(r + 1) * page_size], thr, kidx, cut)
        rows = []
        for h in range(N_HEADS):
            kt = kp[r][0, h]
            if r == 0:
                kt = jnp.where(is_self, kself_ref[0, h], kt)
            rows.append(jnp.sum(kt * qb_ref[0, h], axis=0, keepdims=True))
        logits.append(jnp.concatenate(rows, axis=0) + bias)

    m_old = m_ref[...]
    m_new = m_old
    for lg in logits:
        m_new = jnp.maximum(m_new, jnp.max(lg, axis=1, keepdims=True))
    m_safe = jnp.where(m_new == -jnp.inf, 0.0, m_new)
    alpha = jnp.exp(m_old - m_safe)
    probs = [jnp.exp(lg - m_safe) for lg in logits]
    l_new = alpha * l_ref[...]
    for p in probs:
        l_new = l_new + jnp.sum(p, axis=1, keepdims=True)
    l_ref[...] = l_new
    m_ref[...] = m_new
    for h in range(N_HEADS):
        acc = acc_ref[h] * alpha[h:h + 1, :]
        for r in range(_PAGES_PER_STEP):
            vt = vp[r][0, h]
            if r == 0:
                vt = jnp.where(is_self, vself_ref[0, h], vt)
            acc = acc + vt * probs[r][h:h + 1, :]
        acc_ref[h] = acc

    @pl.when(is_self)
    def _():
        for h in range(N_HEADS):
            o_ref[0, h] = jnp.sum(acc_ref[h], axis=1, keepdims=True) / l_ref[h:h + 1, :]


def _sample_attn(page_table, qb, scores, thr, cut, k_self, v_self, ck_t, cv_t):
    db, n_pages = page_table.shape
    _, nh, hd, page_size = ck_t.shape
    steps = n_pages // _PAGES_PER_STEP
    block = _PAGES_PER_STEP * page_size
    page = (1, nh, hd, page_size)
    return pl.pallas_call(
        functools.partial(_sample_attn_kernel, page_size=page_size, n_steps=steps),
        grid_spec=pltpu.PrefetchScalarGridSpec(
            num_scalar_prefetch=1,
            grid=(db, steps + 1),
            in_specs=[_per_seq(page[1:]),
                      pl.BlockSpec((1, 1, block), lambda b, p, pt: (b, 0, p)),
                      _per_seq((1, 1)), _per_seq((1, 1)), _per_seq(page[1:]), _per_seq(page[1:])]
                     + _page_specs(page, n_pages, _PAGES_PER_STEP) + _page_specs(page, n_pages, _PAGES_PER_STEP),
            out_specs=_per_seq((nh, hd, 1)),
            scratch_shapes=[pltpu.VMEM((nh, 1), F32), pltpu.VMEM((nh, 1), F32),
                            pltpu.VMEM((nh, hd, page_size), F32)],
        ),
        out_shape=jax.ShapeDtypeStruct((db, nh, hd, 1), F32),
        compiler_params=_cparams("parallel", "arbitrary"),
        name="sample_attn",
    )(page_table, qb, scores, thr, cut, k_self, v_self,
      *([ck_t] * _PAGES_PER_STEP), *([cv_t] * _PAGES_PER_STEP))


def _pick(n, pref):
    for t in pref:
        if n % t == 0:
            return t
    return n


def _pack_w_in(w_in):
    o = np.cumsum([0, D_ATTN, D_ATTN, D_ATTN, IDX_HEADS * IDX_DIM, IDX_DIM, IDX_HEADS, SSM_WIDTH, 2 * D_MODEL])
    z = lambda n: jnp.zeros((D_MODEL, n), w_in.dtype)
    w_feat = jnp.concatenate([w_in[:, o[0]:o[6]], z(_TEND - _TWI - IDX_HEADS)], axis=1)
    w_tok = jnp.concatenate([w_in[:, o[1]:o[2]], w_in[:, o[4]:o[5]], z(_RU - _RKI - IDX_DIM),
                             w_in[:, o[6]:o[7]]], axis=1)
    return w_feat.T.astype(BF16), w_tok.astype(BF16), w_in[:, o[7]:o[8]].astype(BF16)


def _prompt_layer(x, lw, ssm, final_norm):
    bsz, seq, d = x.shape
    m = bsz * seq
    tm = _pick(m, (512, 256, 128, 64, 32, 16, 8))
    x1 = _ffn(x.reshape(m, d), *lw["ffn1"], lw["g_final"], final_norm=False, tm=tm, tf=lw["tf"])
    qt, kt_f, vt_f, vt, kh, qit, kit_f, kib, wit, u_f = _proj(
        x1.reshape(bsz, seq, d), lw["g_mix"], lw["w_feat"], lw["w_tok"], tm=_pick(seq, (512, 256, 128)))
    heads = lambda a, nh: a.reshape(bsz, nh, -1, seq)
    qt, vt, qit = heads(qt, N_HEADS), heads(vt, N_HEADS), heads(qit, IDX_HEADS)

    k_top = min(TOPK_MAX, seq // 4)
    vf, cut = _select(qit, kib, wit, tq=_pick(seq, (512, 256, 128)), tk=_pick(seq, (256, 128)), k_top=k_top)
    o_t = _attn(qt, kh, vt, qit, kib, wit, vf, cut,
                tq=_pick(seq, (512, 256, 128)), tk=_pick(seq, (512, 256, 128)))
    oa = o_t.transpose(0, 3, 1, 2).reshape(m, D_ATTN)

    tt = _pick(seq, (256, 128, 64, 32, 16, 8))
    ys, xfin = _ssm(u_f, ssm["bblk"].astype(BF16), ssm["cblk"].astype(BF16), ssm["pw"], ssm["d_skip"], tt=tt)
    x2 = _merge(x1, oa, ys.reshape(m, SSM_WIDTH), *lw["merge"], tm=tm)
    y = _ffn(x2, *lw["ffn2"], lw["g_final"], final_norm=final_norm, tm=tm, tf=lw["tf"])
    new = (heads(kt_f, N_HEADS).transpose(0, 3, 1, 2), heads(vt_f, N_HEADS).transpose(0, 3, 1, 2),
           kit_f.transpose(0, 2, 1),
           xfin[:, 0, :N_STATE].reshape(bsz, N_GROUPS, STATE_DIM),
           xfin[:, 0, N_STATE:].reshape(bsz, N_GROUPS, STATE_DIM))
    return y.reshape(bsz, seq, d), new


def _sample_layer(x, lw, ssm, final_norm, cache_k, cache_v, cache_kidx, st_re, st_im, page_table):
    db, ds, d = x.shape
    assert ds == 1, "one new token per sample sequence"
    n_pool, page_size = cache_k.shape[0], cache_k.shape[1]
    past_len = page_table.shape[1] * page_size
    s1 = _ffn(x.reshape(db, d), *lw["ffn1"], lw["g_final"], final_norm=False, tm=db, tf=lw["tf"])
    qt, kt_f, vt_f, _, _, qit, kit_f, kib, wit, u_f = _proj(
        s1.reshape(1, db, d), lw["g_mix"], lw["w_feat"], lw["w_tok"], tm=db)
    q_b, k_f, v_f, qi_b, ki_f, wi_f = (a[0].T for a in (qt, kt_f, vt_f, qit, kit_f, wit))
    ki_b, u_f = kib[0], u_f[0]

    kidx_t = cache_kidx.transpose(0, 2, 1)
    ck_t = cache_k.transpose(0, 2, 3, 1)
    cv_t = cache_v.transpose(0, 2, 3, 1)
    lane0 = lambda a: jnp.zeros(a.shape + (page_size,), a.dtype).at[..., 0].set(a)

    qi_rows = jnp.zeros((db, 8, IDX_DIM), BF16).at[:, :IDX_HEADS].set(qi_b.reshape(db, IDX_HEADS, IDX_DIM))
    scores = _sample_scores(page_table, qi_rows, wi_f[:, :, None], lane0(ki_b), kidx_t)

    n_keys = past_len + 1
    k_top = min(TOPK_MAX, n_keys // 4)
    n = scores.shape[2]
    scores_t = jnp.zeros((n, -(-db // 128) * 128), F32).at[:, :db].set(scores[:, 0, :].T)
    thr, cut = _sample_select(scores_t, n_keys=n_keys, k_top=k_top, tk=256)

    qb = jnp.broadcast_to(q_b.astype(F32).reshape(db, N_HEADS, HEAD_DIM, 1), (db, N_HEADS, HEAD_DIM, page_size))
    o_s = _sample_attn(page_table, qb, scores, thr[0, 0, :db].reshape(db, 1, 1), cut[0, 0, :db].reshape(db, 1, 1),
                       lane0(k_f.reshape(db, N_HEADS, HEAD_DIM)), lane0(v_f.reshape(db, N_HEADS, HEAD_DIM)),
                       ck_t, cv_t)
    oa = o_s.reshape(db, D_ATTN).astype(BF16)

    x0 = jnp.concatenate([st_re.reshape(db, N_STATE), st_im.reshape(db, N_STATE)], axis=1)
    ys, x_new = _ssm_step(u_f, x0, ssm["bblk"], ssm["cblk"], ssm["pw"], ssm["d_skip"])
    s2 = _merge(s1, oa, ys, *lw["merge"], tm=db)
    y = _ffn(s2, *lw["ffn2"], lw["g_final"], final_norm=final_norm, tm=db, tf=lw["tf"])
    new = (k_f.reshape(db, ds, N_HEADS, HEAD_DIM), v_f.reshape(db, ds, N_HEADS, HEAD_DIM),
           ki_f.reshape(db, ds, IDX_DIM),
           x_new[:, :N_STATE].reshape(db, N_GROUPS, STATE_DIM),
           x_new[:, N_STATE:].reshape(db, N_GROUPS, STATE_DIM))
    return y.reshape(db, ds, d), new


def kernel(x_prompt, x_sample, cache_k, cache_v, cache_kidx, state_ssm_re, state_ssm_im, page_table,
           g_ffn1, w1_gate, w1_up, w1_down, g_mix, w_in, a_re, a_im, log_dt, b_re, b_im, c_re, c_im,
           d_skip, w_glu, b_glu, w_pa, w_pb, w_out, g_ffn2, w2_gate, w2_up, w2_down, g_final):
    depth = w_in.shape[0]
    row = lambda a: a.reshape(1, -1)
    bf = lambda a: a.astype(BF16)
    xp, xs = x_prompt, x_sample
    new_p, new_s = [], []
    for l in range(depth):
        w_feat, w_tok, w_gates = _pack_w_in(w_in[l])
        lw = dict(
            ffn1=(row(g_ffn1[l]), bf(w1_gate[l]), bf(w1_up[l]), bf(w1_down[l])),
            ffn2=(row(g_ffn2[l]), bf(w2_gate[l]), bf(w2_up[l]), bf(w2_down[l])),
            merge=(row(g_mix[l]), w_gates, bf(w_glu[l]), row(b_glu[l]), bf(w_pa[l]), bf(w_pb[l]), bf(w_out[l])),
            g_mix=row(g_mix[l]), w_feat=w_feat, w_tok=w_tok, g_final=row(g_final),
            tf=_pick(w1_gate.shape[2], (1408, 1024, 512, 256, 128)),
        )
        pwr, pwi, bbr, bbi = _discretize(a_re[l], a_im[l], log_dt[l], b_re[l], b_im[l])
        ssm = dict(
            pw=jnp.concatenate([pwr, pwi], axis=1),
            bblk=jnp.concatenate([_block_diag_in(bbr), _block_diag_in(bbi)], axis=1),
            cblk=jnp.concatenate([_block_diag_out(c_re[l]), -_block_diag_out(c_im[l])], axis=0),
            d_skip=row(d_skip[l]),
        )
        last = l == depth - 1
        xp, st_p = _prompt_layer(xp, lw, ssm, last)
        xs, st_s = _sample_layer(xs, lw, ssm, last, cache_k[l], cache_v[l], cache_kidx[l],
                                 state_ssm_re[l], state_ssm_im[l], page_table)
        new_p.append(st_p)
        new_s.append(st_s)
    stack = lambda states, i: jnp.stack([s[i] for s in states])
    return (xp, xs) + tuple(stack(new_p, i) for i in range(5)) + tuple(stack(new_s, i) for i in range(5))
```

```python
import functools

import numpy as np
import jax
import jax.numpy as jnp
from jax import lax
from jax.experimental import pallas as pl
from jax.experimental.pallas import tpu as pltpu

F32 = jnp.float32
BF16 = jnp.bfloat16
I32 = jnp.int32

D_MODEL = 1024
N_HEADS = 8
HEAD_DIM = 64
D_ATTN = N_HEADS * HEAD_DIM
IDX_HEADS = 4
IDX_DIM = 64
TOPK_MAX = 256
SSM_WIDTH = 512
GROUP_CH = 16
N_GROUPS = SSM_WIDTH // GROUP_CH
STATE_DIM = 64
N_STATE = N_GROUPS * STATE_DIM
NORM_EPS = 1e-6
NEG_BIG = -1e30

INT_MAX = 2 ** 31 - 1

VMEM_LIMIT_BYTES = 56 * 1024 * 1024


def _cparams(*sem):
    return pltpu.CompilerParams(dimension_semantics=sem, vmem_limit_bytes=VMEM_LIMIT_BYTES)


def _rmsnorm(x, g):
    y = x * lax.rsqrt(jnp.mean(x * x, axis=-1, keepdims=True) + NORM_EPS)
    return y * g


def _sigmoid(x):
    return 1.0 / (1.0 + jnp.exp(-x))


def _ffn_kernel(x_ref, g_ref, wg_ref, wu_ref, wd_ref, gf_ref, o_ref, *, final_norm, tf):
    x = x_ref[...]
    xn = _rmsnorm(x, g_ref[...]).astype(BF16)
    acc = None
    for f0 in range(0, wg_ref.shape[1], tf):
        gate = jnp.dot(xn, wg_ref[:, f0:f0 + tf], preferred_element_type=F32)
        up = jnp.dot(xn, wu_ref[:, f0:f0 + tf], preferred_element_type=F32)
        act = (gate * _sigmoid(gate)) * up
        part = jnp.dot(act.astype(BF16), wd_ref[f0:f0 + tf, :], preferred_element_type=F32)
        acc = part if acc is None else acc + part
    y = x + 0.5 * acc
    if final_norm:
        y = _rmsnorm(y, gf_ref[...])
    o_ref[...] = y


def _ffn(x, g, wg, wu, wd, g_final, *, final_norm, tm, tf):
    m, d = x.shape
    resident = lambda a: pl.BlockSpec(a.shape, lambda i: (0, 0), pipeline_mode=pl.Buffered(1))
    return pl.pallas_call(
        functools.partial(_ffn_kernel, final_norm=final_norm, tf=tf),
        grid=(m // tm,),
        in_specs=[pl.BlockSpec((tm, d), lambda i: (i, 0)), resident(g), resident(wg), resident(wu),
                  resident(wd), resident(g_final)],
        out_specs=pl.BlockSpec((tm, d), lambda i: (i, 0)),
        out_shape=jax.ShapeDtypeStruct((m, d), F32),
        compiler_params=_cparams("parallel"),
        name="ffn",
    )(x, g, wg, wu, wd, g_final)


_TQ, _TK, _TV, _TQI, _TKI, _TWI, _TEND = 0, 512, 1024, 1536, 1792, 1856, 1872
_RK, _RKI, _RU, _REND = 0, 512, 640, 1152


def _proj_kernel(x_ref, g_ref, wt_ref, wr_ref, qt_ref, ktf_ref, vtf_ref, vt_ref, kh_ref,
                 qit_ref, kitf_ref, kib_ref, wit_ref, u_ref):
    h = _rmsnorm(x_ref[0], g_ref[...]).astype(BF16)
    pt = lax.dot_general(wt_ref[...], h, (((1,), (1,)), ((), ())), preferred_element_type=F32)
    pr = jnp.dot(h, wr_ref[...], preferred_element_type=F32)
    qt_ref[0] = (pt[_TQ:_TK] * (HEAD_DIM ** -0.5)).astype(BF16)
    ktf_ref[0] = pt[_TK:_TV]
    v = pt[_TV:_TQI]
    vtf_ref[0] = v
    vt_ref[0] = v.astype(BF16)
    qit_ref[0] = pt[_TQI:_TKI].astype(BF16)
    kitf_ref[0] = pt[_TKI:_TWI]
    wit_ref[0] = pt[_TWI:_TWI + 8] * ((IDX_HEADS * IDX_DIM) ** -0.5)
    k = pr[:, _RK:_RKI].astype(BF16)
    for hd in range(N_HEADS):
        kh_ref[0, hd] = k[:, hd * HEAD_DIM:(hd + 1) * HEAD_DIM]
    kib_ref[0] = pr[:, _RKI:_RKI + IDX_DIM].astype(BF16)
    u_ref[0] = pr[:, _RU:_REND]


def _proj(x, g, wt, wr, *, tm):
    b, t, d = x.shape
    feat = lambda n: pl.BlockSpec((1, n, tm), lambda bb, i: (bb, 0, i))
    tok = lambda n: pl.BlockSpec((1, tm, n), lambda bb, i: (bb, i, 0))
    outs = [
        (feat(D_ATTN), (b, D_ATTN, t), BF16), (feat(D_ATTN), (b, D_ATTN, t), F32),
        (feat(D_ATTN), (b, D_ATTN, t), F32), (feat(D_ATTN), (b, D_ATTN, t), BF16),
        (pl.BlockSpec((1, N_HEADS, tm, HEAD_DIM), lambda bb, i: (bb, 0, i, 0)), (b, N_HEADS, t, HEAD_DIM), BF16),
        (feat(IDX_HEADS * IDX_DIM), (b, IDX_HEADS * IDX_DIM, t), BF16), (feat(IDX_DIM), (b, IDX_DIM, t), F32),
        (tok(IDX_DIM), (b, t, IDX_DIM), BF16), (feat(8), (b, 8, t), F32), (tok(SSM_WIDTH), (b, t, SSM_WIDTH), F32),
    ]
    return pl.pallas_call(
        _proj_kernel,
        grid=(b, t // tm),
        in_specs=[tok(d), pl.BlockSpec((1, d), lambda bb, i: (0, 0)),
                  pl.BlockSpec((_TEND, d), lambda bb, i: (0, 0)), pl.BlockSpec((d, _REND), lambda bb, i: (0, 0))],
        out_specs=[spec for spec, _, _ in outs],
        out_shape=[jax.ShapeDtypeStruct(shape, dt) for _, shape, dt in outs],
        compiler_params=_cparams("parallel", "parallel"),
        name="proj",
    )(x, g, wt, wr)


def _scores_t(kc, qit, w):
    s = None
    for h in range(IDX_HEADS):
        d = jnp.dot(kc, qit[h], preferred_element_type=F32)
        term = jnp.maximum(d, 0.0) * w[h:h + 1, :]
        s = term if s is None else s + term
    return s


def _select_bias(s, vf, kidx, cut):
    tie = jnp.where(kidx < cut, 0.0, -jnp.inf)
    return jnp.where(s > vf, 0.0, jnp.where(s == vf, tie, -jnp.inf))


_FOLD_ROWS = 16


def _fold8(x, op):
    return op(x.reshape(x.shape[0] // _FOLD_ROWS, _FOLD_ROWS, x.shape[1]), axis=0)


_PROBES_PER_ROUND = 4
_SETTLE_FIRST, _SETTLE_EVERY, _BISECT_CAP = 4, 2, 80
_SETTLE_FEW_FROM, _SETTLE_FEW_LANES = 2, 16
_STUCK_FLAG = 1 << 16


def _topk_threshold(s_ref, n_chunks, tk, tq, mx, mn, n_hi, n_nc, k_top, thr_ref, cut_ref, fst_ref, ist_ref):
    row_iota = lax.broadcasted_iota(I32, (tk, tq), 0)
    zeros_i = jnp.zeros((_FOLD_ROWS, tq), I32)

    def chunk(c):
        k0 = pl.multiple_of(c * tk, tk)
        return s_ref[pl.ds(k0, tk), :], k0

    def count_gt(v):
        def body(c, acc):
            blk, _ = chunk(c)
            return acc + _fold8(jnp.where(blk > v, 1, 0), jnp.sum)
        real = jnp.sum(lax.fori_loop(0, n_chunks, body, zeros_i), axis=0, keepdims=True)
        return real + jnp.where(NEG_BIG > v, n_nc, 0)

    def count_gt_eq(v):
        def body(c, acc):
            blk, _ = chunk(c)
            return (acc[0] + _fold8(jnp.where(blk > v, 1, 0), jnp.sum),
                    acc[1] + _fold8(jnp.where(blk == v, 1, 0), jnp.sum))
        g, e = lax.fori_loop(0, n_chunks, body, (zeros_i, zeros_i))
        g = jnp.sum(g, axis=0, keepdims=True) + jnp.where(NEG_BIG > v, n_nc, 0)
        return g, jnp.sum(e, axis=0, keepdims=True)

    def count_eq(v):
        def body(c, acc):
            blk, _ = chunk(c)
            return acc + _fold8(jnp.where(blk == v, 1, 0), jnp.sum)
        return jnp.sum(lax.fori_loop(0, n_chunks, body, zeros_i), axis=0, keepdims=True)

    def max_le(v):
        def body(c, acc):
            blk, _ = chunk(c)
            return jnp.maximum(acc, _fold8(jnp.where(blk <= v, blk, -jnp.inf), jnp.max))
        m = jnp.max(lax.fori_loop(0, n_chunks, body, jnp.full((_FOLD_ROWS, tq), -jnp.inf, F32)),
                    axis=0, keepdims=True)
        return jnp.where((n_nc > 0) & (NEG_BIG <= v), jnp.maximum(m, NEG_BIG), m)

    def settle(m, counts=None):
        gt, eq_real = count_gt_eq(m) if counts is None else counts
        eq = eq_real + jnp.where(m == NEG_BIG, n_nc, 0)
        found = (ist_ref[0:1, :] == 0) & (gt + eq >= k_top)
        need = k_top - gt
        fst_ref[2:3, :] = jnp.where(found, m, fst_ref[2:3, :])
        ist_ref[1:2, :] = jnp.where(found, need, ist_ref[1:2, :])
        ist_ref[2:3, :] = jnp.where(found, jnp.where(eq_real > need, 1, 0), ist_ref[2:3, :])
        ist_ref[3:4, :] = jnp.where(found, INT_MAX, ist_ref[3:4, :])
        ist_ref[0:1, :] = jnp.where(found, 1, ist_ref[0:1, :])

    low = n_hi < k_top
    hi0 = jnp.where(low, NEG_BIG, mx)
    fst_ref[0:1, :] = jnp.where(low, jnp.minimum(mn, NEG_BIG), jnp.maximum(mn, NEG_BIG))
    fst_ref[1:2, :] = hi0
    fst_ref[2:3, :] = hi0
    ist_ref[...] = jnp.zeros_like(ist_ref)
    settle(hi0, (jnp.where(low, n_hi, 0), count_eq(hi0)))

    def probe(lo, hi, thr, done):
        mid = 0.5 * lo + 0.5 * hi
        c = count_gt(mid)
        live = done == 0
        hit = live & (c == k_top)
        stuck = jnp.where(live & ((mid <= lo) | (mid >= hi)), 2, 0)
        return (jnp.where(live & (c > k_top), mid, lo), jnp.where(live & (c < k_top), mid, hi),
                jnp.where(hit, mid, thr), jnp.where(hit, 1, done), stuck)

    def step(carry):
        it, _ = carry
        lo, hi, thr, done = fst_ref[0:1, :], fst_ref[1:2, :], fst_ref[2:3, :], ist_ref[0:1, :]
        stuck = jnp.zeros_like(done)
        for _ in range(_PROBES_PER_ROUND):
            lo, hi, thr, done, s = probe(lo, hi, thr, done)
            stuck = jnp.maximum(stuck, s)
        fst_ref[0:1, :] = lo
        fst_ref[1:2, :] = hi
        fst_ref[2:3, :] = thr
        ist_ref[0:1, :] = done
        tally = jnp.sum((1 - done) + jnp.where(stuck > 0, _STUCK_FLAG, 0))
        active = tally % _STUCK_FLAG
        due = (it >= _SETTLE_FIRST) & ((it - _SETTLE_FIRST) % _SETTLE_EVERY == 0)
        few = (it >= _SETTLE_FEW_FROM) & (active <= _SETTLE_FEW_LANES)
        check = (active > 0) & (due | few | (tally >= _STUCK_FLAG))

        @pl.when(check)
        def _():
            settle(max_le(hi))

        return it + 1, lax.cond(check, lambda: jnp.sum(1 - ist_ref[0:1, :]), lambda: active)

    lax.while_loop(lambda carry: (carry[1] > 0) & (carry[0] < _BISECT_CAP), step,
                   (jnp.int32(0), jnp.sum(1 - ist_ref[0:1, :])))
    thr_ref[0] = fst_ref[2:3, :]
    cut_ref[0] = ist_ref[3:4, :]

    @pl.when(jnp.max(ist_ref[2:3, :]) > 0)
    def _():
        thr = fst_ref[2:3, :]
        need = ist_ref[1:2, :].astype(F32)
        tri = jnp.where(lax.broadcasted_iota(I32, (tk, tk), 1) <= lax.broadcasted_iota(I32, (tk, tk), 0),
                        1.0, 0.0).astype(BF16)

        def body(c, carry):
            seen, cut = carry
            blk, k0 = chunk(c)
            tie = blk == thr
            rank = jnp.dot(tri, jnp.where(tie, 1.0, 0.0).astype(BF16), preferred_element_type=F32) + seen
            bound = jnp.where(tie, jnp.where(rank <= need, k0 + row_iota + 1, 0), 0)
            return rank[tk - 1:tk, :], jnp.maximum(cut, jnp.max(bound, axis=0, keepdims=True))

        _, cut = lax.fori_loop(0, n_chunks, body, (jnp.zeros((1, tq), F32), jnp.zeros((1, tq), I32)))
        cut_ref[0] = jnp.where(ist_ref[2:3, :] > 0, cut, ist_ref[3:4, :])


def _select_kernel(qit_ref, ki_ref, wit_ref, thr_ref, cut_ref, s_ref, fst_ref, ist_ref, *, tq, tk, n_keys, k_top):
    i = pl.program_id(1)
    q0 = i * tq
    n_chunks = (q0 + tq + tk - 1) // tk
    t_row = q0 + lax.broadcasted_iota(I32, (1, tq), 1)
    row_iota = lax.broadcasted_iota(I32, (tk, tq), 0)
    qit = qit_ref[0]
    w = wit_ref[0]

    def build(c, carry):
        mx, mn, n_hi = carry
        k0 = pl.multiple_of(c * tk, tk)
        s = _scores_t(ki_ref[0, pl.ds(k0, tk), :], qit, w)
        causal = (k0 + row_iota) <= t_row
        sm = jnp.where(causal, s, -jnp.inf)
        s_ref[pl.ds(k0, tk), :] = sm
        return (jnp.maximum(mx, _fold8(sm, jnp.max)),
                jnp.minimum(mn, _fold8(jnp.where(causal, s, jnp.inf), jnp.min)),
                n_hi + _fold8(jnp.where(sm > NEG_BIG, 1, 0), jnp.sum))

    mx, mn, n_hi = lax.fori_loop(
        0, n_chunks, build,
        (jnp.full((_FOLD_ROWS, tq), -jnp.inf, F32), jnp.full((_FOLD_ROWS, tq), jnp.inf, F32),
         jnp.zeros((_FOLD_ROWS, tq), I32)))
    n_nc = (n_keys - 1) - t_row
    _topk_threshold(s_ref, n_chunks, tk, tq, jnp.max(mx, axis=0, keepdims=True),
                    jnp.min(mn, axis=0, keepdims=True), jnp.sum(n_hi, axis=0, keepdims=True),
                    n_nc, k_top, thr_ref, cut_ref, fst_ref, ist_ref)


def _select(qit, ki, wit, *, tq, tk, k_top):
    b, _, _, t = qit.shape
    kern = functools.partial(_select_kernel, tq=tq, tk=tk, n_keys=t, k_top=k_top)
    return pl.pallas_call(
        kern,
        grid=(b, t // tq),
        in_specs=[
            pl.BlockSpec((1, IDX_HEADS, IDX_DIM, tq), lambda bb, i: (bb, 0, 0, i)),
            pl.BlockSpec((1, t, IDX_DIM), lambda bb, i: (bb, 0, 0)),
            pl.BlockSpec((1, 8, tq), lambda bb, i: (bb, 0, i)),
        ],
        out_specs=[pl.BlockSpec((1, 1, tq), lambda bb, i: (bb, 0, i)),
                   pl.BlockSpec((1, 1, tq), lambda bb, i: (bb, 0, i))],
        out_shape=[jax.ShapeDtypeStruct((b, 1, t), F32), jax.ShapeDtypeStruct((b, 1, t), I32)],
        scratch_shapes=[pltpu.VMEM((t, tq), F32), pltpu.VMEM((8, tq), F32), pltpu.VMEM((8, tq), I32)],
        compiler_params=_cparams("parallel", "parallel"),
        name="select",
    )(qit, ki, wit)


_ATTN_SUB_ROWS = 128


def _n_key_tiles(i, tq, tk):
    return (i * tq + tq + tk - 1) // tk


def _attn_kernel(it_ref, jt_ref, qt_ref, k_ref, vt_ref, qit_ref, ki_ref, wit_ref, vf_ref, cut_ref, o_ref,
                 m_ref, l_ref, acc_ref, bias_ref, lga_ref, lgb_ref, pa_ref, pb_ref, *, tq, tk):
    i = it_ref[pl.program_id(1)]
    j = jt_ref[pl.program_id(1)]
    nkt = _n_key_tiles(i, tq, tk)
    sub = _ATTN_SUB_ROWS
    chunks = [slice(c * sub, (c + 1) * sub) for c in range(tk // sub)]

    @pl.when(j == 0)
    def _():
        m_ref[...] = jnp.full_like(m_ref, -jnp.inf)
        l_ref[...] = jnp.zeros_like(l_ref)
        acc_ref[...] = jnp.zeros_like(acc_ref)

    @pl.when(j < nkt)
    def _():
        qit = qit_ref[0]
        w = wit_ref[0]
        vf = vf_ref[0]
        cut = cut_ref[0]
        t_row = i * tq + lax.broadcasted_iota(I32, (1, tq), 1)
        row = lax.broadcasted_iota(I32, (sub, tq), 0)
        for c, rows in enumerate(chunks):
            kidx = j * tk + c * sub + row
            bias = _select_bias(_scores_t(ki_ref[0, rows, :], qit, w), vf, kidx, cut)
            bias_ref[rows, :] = jnp.where(kidx <= t_row, bias, -jnp.inf)

        def logits(h, lg_ref):
            qt = qt_ref[0, h]
            part = jnp.full((8, tq), -jnp.inf, F32)
            for rows in chunks:
                lg = jnp.dot(k_ref[0, h, rows, :], qt, preferred_element_type=F32) + bias_ref[rows, :]
                lg_ref[rows, :] = lg
                part = jnp.maximum(part, jnp.max(lg.reshape(sub // 8, 8, tq), axis=0))
            return jnp.max(part, axis=0, keepdims=True)

        def absorb(h, lg_ref, p_ref, tile_max):
            hrow = pl.ds(h, 1)
            m_old = m_ref[hrow, :]
            m_new = jnp.maximum(m_old, tile_max)
            m_safe = jnp.where(m_new == -jnp.inf, 0.0, m_new)
            alpha = jnp.exp(m_old - m_safe)
            psum = jnp.zeros((8, tq), F32)
            for rows in chunks:
                p = jnp.exp(lg_ref[rows, :] - m_safe)
                psum = psum + jnp.sum(p.reshape(sub // 8, 8, tq), axis=0)
                p_ref[rows, :] = p.astype(BF16)
            l_ref[hrow, :] = alpha * l_ref[hrow, :] + jnp.sum(psum, axis=0, keepdims=True)
            acc_ref[h] = alpha * acc_ref[h] + jnp.dot(vt_ref[0, h], p_ref[...], preferred_element_type=F32)
            m_ref[hrow, :] = m_new

        def two_heads(t, max_a):
            h = 2 * t
            max_b = logits(h + 1, lgb_ref)
            absorb(h, lga_ref, pa_ref, max_a)
            max_a = logits(h + 2, lga_ref)
            absorb(h + 1, lgb_ref, pb_ref, max_b)
            return max_a

        max_a = lax.fori_loop(0, N_HEADS // 2 - 1, two_heads, logits(0, lga_ref))
        max_b = logits(N_HEADS - 1, lgb_ref)
        absorb(N_HEADS - 2, lga_ref, pa_ref, max_a)
        absorb(N_HEADS - 1, lgb_ref, pb_ref, max_b)

    @pl.when(j == nkt - 1)
    def _():
        for h in range(N_HEADS):
            o_ref[0, h] = (acc_ref[h] / l_ref[h:h + 1, :]).astype(o_ref.dtype)


def _attn(qt, k, vt, qit, ki, wit, vf, cut, *, tq, tk):
    b, _, _, t = qt.shape
    pairs = [(i, j) for i in range(t // tq) for j in range(_n_key_tiles(i, tq, tk))]
    i_tab = jnp.asarray([p[0] for p in pairs], I32)
    j_tab = jnp.asarray([p[1] for p in pairs], I32)
    q_tile = lambda *lead: (lambda bb, s, it, jt: (bb,) + lead + (it[s],))
    return pl.pallas_call(
        functools.partial(_attn_kernel, tq=tq, tk=tk),
        grid_spec=pltpu.PrefetchScalarGridSpec(
            num_scalar_prefetch=2,
            grid=(b, len(pairs)),
            in_specs=[
                pl.BlockSpec((1, N_HEADS, HEAD_DIM, tq), q_tile(0, 0)),
                pl.BlockSpec((1, N_HEADS, tk, HEAD_DIM), lambda bb, s, it, jt: (bb, 0, jt[s], 0)),
                pl.BlockSpec((1, N_HEADS, HEAD_DIM, tk), lambda bb, s, it, jt: (bb, 0, 0, jt[s])),
                pl.BlockSpec((1, IDX_HEADS, IDX_DIM, tq), q_tile(0, 0)),
                pl.BlockSpec((1, tk, IDX_DIM), lambda bb, s, it, jt: (bb, jt[s], 0)),
                pl.BlockSpec((1, 8, tq), q_tile(0)),
                pl.BlockSpec((1, 1, tq), q_tile(0)),
                pl.BlockSpec((1, 1, tq), q_tile(0)),
            ],
            out_specs=pl.BlockSpec((1, N_HEADS, HEAD_DIM, tq), q_tile(0, 0)),
            scratch_shapes=[pltpu.VMEM((N_HEADS, tq), F32), pltpu.VMEM((N_HEADS, tq), F32),
                            pltpu.VMEM((N_HEADS, HEAD_DIM, tq), F32),
                            pltpu.VMEM((tk, tq), F32), pltpu.VMEM((tk, tq), F32), pltpu.VMEM((tk, tq), F32),
                            pltpu.VMEM((tk, tq), BF16), pltpu.VMEM((tk, tq), BF16)],
        ),
        out_shape=jax.ShapeDtypeStruct((b, N_HEADS, HEAD_DIM, t), BF16),
        compiler_params=_cparams("parallel", "arbitrary"),
        name="attn",
    )(i_tab, j_tab, qt, k, vt, qit, ki, wit, vf, cut)


def _zoh(ar, ai, ldt):
    dt = jnp.exp(ldt)
    mag = jnp.exp(dt * ar)
    abr = mag * jnp.cos(dt * ai)
    abi = mag * jnp.sin(dt * ai)
    den = ar * ar + ai * ai
    nr = abr - 1.0
    ni = abi
    return abr, abi, (nr * ar + ni * ai) / den, (ni * ar - nr * ai) / den


def _disc_kernel(ar_ref, ai_ref, ldt_ref, ar16_ref, ai16_ref, ldt16_ref, br_ref, bi_ref,
                 pwr_ref, pwi_ref, bbr_ref, bbi_ref):
    abr, abi, _, _ = _zoh(ar_ref[...], ai_ref[...], ldt_ref[...])
    pr, pi = abr, abi
    for j in range(8):
        pwr_ref[j:j + 1, :] = pr
        pwi_ref[j:j + 1, :] = pi
        pr, pi = pr * abr - pi * abi, pr * abi + pi * abr
    _, _, fr, fi = _zoh(ar16_ref[...], ai16_ref[...], ldt16_ref[...])
    br = br_ref[...]
    bi = bi_ref[...]
    bbr_ref[...] = fr * br - fi * bi
    bbi_ref[...] = fr * bi + fi * br


def _discretize(a_re, a_im, log_dt, b_re, b_im):
    flat = lambda a: a.reshape(1, -1)
    ldt = jnp.broadcast_to(log_dt[:, None], (N_GROUPS, STATE_DIM))
    rep = lambda a: flat(jnp.broadcast_to(a[:, :, None], (N_GROUPS, STATE_DIM, GROUP_CH)))
    n16 = N_STATE * GROUP_CH
    pwr, pwi, bbr, bbi = pl.pallas_call(
        _disc_kernel,
        out_shape=[jax.ShapeDtypeStruct((8, N_STATE), F32), jax.ShapeDtypeStruct((8, N_STATE), F32),
                   jax.ShapeDtypeStruct((1, n16), F32), jax.ShapeDtypeStruct((1, n16), F32)],
        name="s5_discretize",
    )(flat(a_re), flat(a_im), flat(ldt), rep(a_re), rep(a_im), rep(ldt), flat(b_re), flat(b_im))
    return pwr, pwi, bbr.reshape(N_GROUPS, STATE_DIM, GROUP_CH), bbi.reshape(N_GROUPS, STATE_DIM, GROUP_CH)


def _block_diag_in(bb):
    eye = jnp.eye(N_GROUPS, dtype=bb.dtype)
    return jnp.einsum('gpc,gh->gchp', bb, eye).reshape(SSM_WIDTH, N_STATE)


def _block_diag_out(c):
    eye = jnp.eye(N_GROUPS, dtype=c.dtype)
    return jnp.einsum('gcp,gh->gphc', c, eye).reshape(N_STATE, SSM_WIDTH)


_LANE_CHUNK = 512


def _ssm_kernel(u_ref, bblk_ref, cblk_ref, pw_ref, step_ref, d_ref, y_ref, xf_ref, x_ref, carry_ref, *, tt):
    ts = pl.program_id(1)

    @pl.when(ts == 0)
    def _():
        carry_ref[...] = jnp.zeros_like(carry_ref)

    u = u_ref[0]
    ub = u.astype(BF16)
    n_chunks = 2 * N_STATE // _LANE_CHUNK
    ch = _LANE_CHUNK // STATE_DIM * GROUP_CH
    chans = [slice((c % (N_STATE // _LANE_CHUNK)) * ch, (c % (N_STATE // _LANE_CHUNK) + 1) * ch)
             for c in range(n_chunks)]
    for c in range(n_chunks):
        cols = slice(c * _LANE_CHUNK, (c + 1) * _LANE_CHUNK)
        x_ref[:, cols] = jnp.dot(ub[:, chans[c]], bblk_ref[chans[c], cols], preferred_element_type=F32)

    def group(r, carry):
        r0 = pl.multiple_of(r * 8, 8)
        for c in range(N_STATE // _LANE_CHUNK):
            re = pl.ds(c * _LANE_CHUNK, _LANE_CHUNK)
            im = pl.ds(N_STATE + c * _LANE_CHUNK, _LANE_CHUNK)
            xr = x_ref[pl.ds(r0, 8), re]
            xi = x_ref[pl.ds(r0, 8), im]
            for k, d in enumerate((1, 2, 4)):
                ar = step_ref[k, :, re]
                ai = step_ref[k, :, im]
                sr = pltpu.roll(xr, d, 0)
                si = pltpu.roll(xi, d, 0)
                xr, xi = xr + (ar * sr - ai * si), xi + (ar * si + ai * sr)
            cr = carry_ref[:, re]
            ci = carry_ref[:, im]
            pr = pw_ref[:, re]
            pi = pw_ref[:, im]
            xr, xi = xr + (pr * cr - pi * ci), xi + (pr * ci + pi * cr)
            x_ref[pl.ds(r0, 8), re] = xr
            x_ref[pl.ds(r0, 8), im] = xi
            carry_ref[:, re] = xr[7:8, :]
            carry_ref[:, im] = xi[7:8, :]
        return carry

    lax.fori_loop(0, tt // 8, group, 0)
    y_ref[0] = d_ref[...] * u
    for c in range(n_chunks):
        cols = slice(c * _LANE_CHUNK, (c + 1) * _LANE_CHUNK)
        y_ref[0, :, chans[c]] += jnp.dot(x_ref[:, cols].astype(BF16), cblk_ref[cols, chans[c]],
                                         preferred_element_type=F32)
    xf_ref[0] = carry_ref[...]


def _ssm(u, bblk, cblk, pw, d_skip, *, tt):
    b, t, _ = u.shape
    rows = jnp.arange(8)[None, :, None]
    steps = jnp.stack([jnp.where(rows >= dd, pw[dd - 1][None, None, :], 0.0)[0] for dd in (1, 2, 4)])
    return pl.pallas_call(
        functools.partial(_ssm_kernel, tt=tt),
        grid=(b, t // tt),
        in_specs=[
            pl.BlockSpec((1, tt, SSM_WIDTH), lambda bb, s: (bb, s, 0)),
            pl.BlockSpec((SSM_WIDTH, 2 * N_STATE), lambda bb, s: (0, 0)),
            pl.BlockSpec((2 * N_STATE, SSM_WIDTH), lambda bb, s: (0, 0)),
            pl.BlockSpec((8, 2 * N_STATE), lambda bb, s: (0, 0)),
            pl.BlockSpec((3, 8, 2 * N_STATE), lambda bb, s: (0, 0, 0)),
            pl.BlockSpec((1, SSM_WIDTH), lambda bb, s: (0, 0)),
        ],
        out_specs=[pl.BlockSpec((1, tt, SSM_WIDTH), lambda bb, s: (bb, s, 0)),
                   pl.BlockSpec((1, 1, 2 * N_STATE), lambda bb, s: (bb, 0, 0))],
        out_shape=[jax.ShapeDtypeStruct((b, t, SSM_WIDTH), F32),
                   jax.ShapeDtypeStruct((b, 1, 2 * N_STATE), F32)],
        scratch_shapes=[pltpu.VMEM((tt, 2 * N_STATE), F32), pltpu.VMEM((1, 2 * N_STATE), F32)],
        compiler_params=_cparams("parallel", "arbitrary"),
        name="s5_scan",
    )(u, bblk, cblk, pw, steps, d_skip)


def _ssm_step_kernel(u_ref, x0_ref, bblk_ref, cblk_ref, pw_ref, d_ref, y_ref, x_ref):
    u = u_ref[...]
    bu = jnp.dot(u, bblk_ref[...], preferred_element_type=F32, precision=lax.Precision.HIGHEST)
    ar = pw_ref[0:1, :N_STATE]
    ai = pw_ref[0:1, N_STATE:]
    x0r = x0_ref[:, :N_STATE]
    x0i = x0_ref[:, N_STATE:]
    xr = ar * x0r - ai * x0i + bu[:, :N_STATE]
    xi = ar * x0i + ai * x0r + bu[:, N_STATE:]
    x_ref[:, :N_STATE] = xr
    x_ref[:, N_STATE:] = xi
    y = jnp.dot(x_ref[...], cblk_ref[...], preferred_element_type=F32, precision=lax.Precision.HIGHEST)
    y_ref[...] = y + d_ref[...] * u


def _ssm_step(u, x0, bblk, cblk, pw, d_skip):
    n = u.shape[0]
    return pl.pallas_call(
        _ssm_step_kernel,
        out_shape=[jax.ShapeDtypeStruct((n, SSM_WIDTH), F32), jax.ShapeDtypeStruct((n, 2 * N_STATE), F32)],
        compiler_params=pltpu.CompilerParams(vmem_limit_bytes=VMEM_LIMIT_BYTES),
        name="s5_step",
    )(u, x0, bblk, cblk, pw, d_skip)


def _gelu_tanh(x):
    c = np.float32(np.sqrt(2.0 / np.pi))
    return 0.5 * x * (1.0 + jnp.tanh(c * (x + 0.044715 * (x * x * x))))


def _merge_kernel(x_ref, oa_ref, ys_ref, g_ref, wgt_ref, wglu_ref, bglu_ref, wpa_ref, wpb_ref, wout_ref, o_ref):
    x = x_ref[...]
    h = _rmsnorm(x, g_ref[...]).astype(BF16)
    gates = _sigmoid(jnp.dot(h, wgt_ref[...], preferred_element_type=F32))
    ys = _gelu_tanh(ys_ref[...])
    glu = jnp.dot(ys.astype(BF16), wglu_ref[...], preferred_element_type=F32) + bglu_ref[...]
    ob = ys * _sigmoid(glu)
    pa = jnp.dot(oa_ref[...], wpa_ref[...], preferred_element_type=F32)
    pb = jnp.dot(ob.astype(BF16), wpb_ref[...], preferred_element_type=F32)
    merged = gates[:, :D_MODEL] * pa + gates[:, D_MODEL:] * pb
    o_ref[...] = x + jnp.dot(merged.astype(BF16), wout_ref[...], preferred_element_type=F32)


def _merge(x, oa, ys, g, wgt, wglu, bglu, wpa, wpb, wout, *, tm):
    m, d = x.shape
    row = lambda n: pl.BlockSpec((tm, n), lambda i: (i, 0))
    full = lambda a: pl.BlockSpec(a.shape, lambda i: (0, 0))
    return pl.pallas_call(
        _merge_kernel,
        grid=(m // tm,),
        in_specs=[row(d), row(D_ATTN), row(SSM_WIDTH), full(g), full(wgt), full(wglu), full(bglu),
                  full(wpa), full(wpb), full(wout)],
        out_specs=row(d),
        out_shape=jax.ShapeDtypeStruct((m, d), F32),
        compiler_params=_cparams("parallel"),
        name="merge",
    )(x, oa, ys, g, wgt, wglu, bglu, wpa, wpb, wout)


_PAGES_PER_STEP = 8
_SCORE_PAGES_PER_STEP = 16


def _page_specs(block, n_pages, per_step):
    last = n_pages // per_step - 1

    def spec(r):
        def index_map(b, p, pt):
            return (pt[b, jnp.minimum(p, last) * per_step + r],) + (0,) * (len(block) - 1)
        return pl.BlockSpec(block, index_map)
    return [spec(r) for r in range(per_step)]


def _per_seq(shape):
    return pl.BlockSpec((1,) + shape, lambda b, p, pt: (b,) + (0,) * len(shape))


def _sample_scores_kernel(pt_ref, qi_ref, w_ref, knew_ref, *refs, page_size, n_steps):
    pages = refs[:-1]
    o_ref = refs[-1]
    p = pl.program_id(1)
    qi = qi_ref[0]
    w = w_ref[0]

    def score(kt):
        d = jnp.dot(qi, kt, preferred_element_type=F32)
        return jnp.sum(jnp.maximum(d, 0.0) * w, axis=0, keepdims=True)

    @pl.when(p < n_steps)
    def _():
        for r, page in enumerate(pages):
            o_ref[0, :, r * page_size:(r + 1) * page_size] = score(page[0].astype(BF16))

    @pl.when(p == n_steps)
    def _():
        o_ref[0] = jnp.full(o_ref.shape[1:], -jnp.inf, F32)
        lane = lax.broadcasted_iota(I32, (1, page_size), 1)
        o_ref[0, :, 0:page_size] = jnp.where(lane == 0, score(knew_ref[0]), -jnp.inf)


def _sample_scores(page_table, qi_rows, w_col, ki_new_t, kidx_t):
    db, n_pages = page_table.shape
    _, _, page_size = kidx_t.shape
    per_step = _pick(n_pages, (_SCORE_PAGES_PER_STEP, _PAGES_PER_STEP))
    assert n_pages % per_step == 0 and per_step % _PAGES_PER_STEP == 0
    steps = n_pages // per_step
    block = per_step * page_size
    return pl.pallas_call(
        functools.partial(_sample_scores_kernel, page_size=page_size, n_steps=steps),
        grid_spec=pltpu.PrefetchScalarGridSpec(
            num_scalar_prefetch=1,
            grid=(db, steps + 1),
            in_specs=[_per_seq((8, IDX_DIM)), _per_seq((8, 1)), _per_seq((IDX_DIM, page_size))]
                     + _page_specs((1, IDX_DIM, page_size), n_pages, per_step),
            out_specs=pl.BlockSpec((1, 1, block), lambda b, p, pt: (b, 0, p)),
        ),
        out_shape=jax.ShapeDtypeStruct((db, 1, (steps + 1) * block), F32),
        compiler_params=_cparams("parallel", "arbitrary"),
        name="sample_scores",
    )(page_table, qi_rows, w_col, ki_new_t, *([kidx_t] * per_step))


def _sample_select_kernel(st_ref, thr_ref, cut_ref, s_ref, fst_ref, ist_ref, *, tk, tq, n_keys, k_top):
    n_chunks = st_ref.shape[0] // tk
    row_iota = lax.broadcasted_iota(I32, (tk, tq), 0)

    def build(c, carry):
        mx, mn, n_hi = carry
        k0 = pl.multiple_of(c * tk, tk)
        s = st_ref[pl.ds(k0, tk), :]
        real = (k0 + row_iota) < n_keys
        sm = jnp.where(real, s, -jnp.inf)
        s_ref[pl.ds(k0, tk), :] = sm
        return (jnp.maximum(mx, _fold8(sm, jnp.max)),
                jnp.minimum(mn, _fold8(jnp.where(real, s, jnp.inf), jnp.min)),
                n_hi + _fold8(jnp.where(sm > NEG_BIG, 1, 0), jnp.sum))

    mx, mn, n_hi = lax.fori_loop(
        0, n_chunks, build,
        (jnp.full((_FOLD_ROWS, tq), -jnp.inf, F32), jnp.full((_FOLD_ROWS, tq), jnp.inf, F32),
         jnp.zeros((_FOLD_ROWS, tq), I32)))
    _topk_threshold(s_ref, n_chunks, tk, tq, jnp.max(mx, axis=0, keepdims=True),
                    jnp.min(mn, axis=0, keepdims=True), jnp.sum(n_hi, axis=0, keepdims=True),
                    jnp.zeros((1, tq), I32), k_top, thr_ref, cut_ref, fst_ref, ist_ref)


def _sample_select(scores_t, *, n_keys, k_top, tk):
    n_rows, tq = scores_t.shape
    kern = functools.partial(_sample_select_kernel, tk=tk, tq=tq, n_keys=n_keys, k_top=k_top)
    return pl.pallas_call(
        kern,
        out_shape=[jax.ShapeDtypeStruct((1, 1, tq), F32), jax.ShapeDtypeStruct((1, 1, tq), I32)],
        scratch_shapes=[pltpu.VMEM((n_rows, tq), F32), pltpu.VMEM((8, tq), F32), pltpu.VMEM((8, tq), I32)],
        compiler_params=pltpu.CompilerParams(vmem_limit_bytes=VMEM_LIMIT_BYTES),
        name="sample_select",
    )(scores_t)


def _sample_attn_kernel(pt_ref, qb_ref, s_ref, thr_ref, cut_ref, kself_ref, vself_ref, *refs,
                        page_size, n_steps):
    kp = refs[:_PAGES_PER_STEP]
    vp = refs[_PAGES_PER_STEP:2 * _PAGES_PER_STEP]
    o_ref, m_ref, l_ref, acc_ref = refs[2 * _PAGES_PER_STEP:]
    step = pl.program_id(1)
    thr = thr_ref[0]
    cut = cut_ref[0]
    lane = lax.broadcasted_iota(I32, (1, page_size), 1)
    is_self = step == n_steps

    @pl.when(step == 0)
    def _():
        m_ref[...] = jnp.full_like(m_ref, -jnp.inf)
        l_ref[...] = jnp.zeros_like(l_ref)
        acc_ref[...] = jnp.zeros_like(acc_ref)

    logits = []
    for r in range(_PAGES_PER_STEP):
        kidx = (step * _PAGES_PER_STEP + r) * page_size + lane
        bias = _select_bias(s_ref[0, :, r * page_size:(r + 1) * page_size], thr, kidx, cut)
        rows = []
        for h in range(N_HEADS):
            kt = kp[r][0, h]
            if r == 0:
                kt = jnp.where(is_self, kself_ref[0, h], kt)
            rows.append(jnp.sum(kt * qb_ref[0, h], axis=0, keepdims=True))
        logits.append(jnp.concatenate(rows, axis=0) + bias)

    m_old = m_ref[...]
    m_new = m_old
    for lg in logits:
        m_new = jnp.maximum(m_new, jnp.max(lg, axis=1, keepdims=True))
    m_safe = jnp.where(m_new == -jnp.inf, 0.0, m_new)
    alpha = jnp.exp(m_old - m_safe)
    probs = [jnp.exp(lg - m_safe) for lg in logits]
    l_new = alpha * l_ref[...]
    for p in probs:
        l_new = l_new + jnp.sum(p, axis=1, keepdims=True)
    l_ref[...] = l_new
    m_ref[...] = m_new
    for h in range(N_HEADS):
        acc = acc_ref[h] * alpha[h:h + 1, :]
        for r in range(_PAGES_PER_STEP):
            vt = vp[r][0, h]
            if r == 0:
                vt = jnp.where(is_self, vself_ref[0, h], vt)
            acc = acc + vt * probs[r][h:h + 1, :]
        acc_ref[h] = acc

    @pl.when(is_self)
    def _():
        for h in range(N_HEADS):
            o_ref[0, h] = jnp.sum(acc_ref[h], axis=1, keepdims=True) / l_ref[h:h + 1, :]


def _sample_attn(page_table, qb, scores, thr, cut, k_self, v_self, ck_t, cv_t):
    db, n_pages = page_table.shape
    _, nh, hd, page_size = ck_t.shape
    steps = n_pages // _PAGES_PER_STEP
    block = _PAGES_PER_STEP * page_size
    page = (1, nh, hd, page_size)
    return pl.pallas_call(
        functools.partial(_sample_attn_kernel, page_size=page_size, n_steps=steps),
        grid_spec=pltpu.PrefetchScalarGridSpec(
            num_scalar_prefetch=1,
            grid=(db, steps + 1),
            in_specs=[_per_seq(page[1:]),
                      pl.BlockSpec((1, 1, block), lambda b, p, pt: (b, 0, p)),
                      _per_seq((1, 1)), _per_seq((1, 1)), _per_seq(page[1:]), _per_seq(page[1:])]
                     + _page_specs(page, n_pages, _PAGES_PER_STEP) + _page_specs(page, n_pages, _PAGES_PER_STEP),
            out_specs=_per_seq((nh, hd, 1)),
            scratch_shapes=[pltpu.VMEM((nh, 1), F32), pltpu.VMEM((nh, 1), F32),
                            pltpu.VMEM((nh, hd, page_size), F32)],
        ),
        out_shape=jax.ShapeDtypeStruct((db, nh, hd, 1), F32),
        compiler_params=_cparams("parallel", "arbitrary"),
        name="sample_attn",
    )(page_table, qb, scores, thr, cut, k_self, v_self,
      *([ck_t] * _PAGES_PER_STEP), *([cv_t] * _PAGES_PER_STEP))


def _pick(n, pref):
    for t in pref:
        if n % t == 0:
            return t
    return n


def _pack_w_in(w_in):
    o = np.cumsum([0, D_ATTN, D_ATTN, D_ATTN, IDX_HEADS * IDX_DIM, IDX_DIM, IDX_HEADS, SSM_WIDTH, 2 * D_MODEL])
    z = lambda n: jnp.zeros((D_MODEL, n), w_in.dtype)
    w_feat = jnp.concatenate([w_in[:, o[0]:o[6]], z(_TEND - _TWI - IDX_HEADS)], axis=1)
    w_tok = jnp.concatenate([w_in[:, o[1]:o[2]], w_in[:, o[4]:o[5]], z(_RU - _RKI - IDX_DIM),
                             w_in[:, o[6]:o[7]]], axis=1)
    return w_feat.T.astype(BF16), w_tok.astype(BF16), w_in[:, o[7]:o[8]].astype(BF16)


def _prompt_layer(x, lw, ssm, final_norm):
    bsz, seq, d = x.shape
    m = bsz * seq
    tm = _pick(m, (512, 256, 128, 64, 32, 16, 8))
    x1 = _ffn(x.reshape(m, d), *lw["ffn1"], lw["g_final"], final_norm=False, tm=tm, tf=lw["tf"])
    qt, kt_f, vt_f, vt, kh, qit, kit_f, kib, wit, u_f = _proj(
        x1.reshape(bsz, seq, d), lw["g_mix"], lw["w_feat"], lw["w_tok"], tm=_pick(seq, (512, 256, 128)))
    heads = lambda a, nh: a.reshape(bsz, nh, -1, seq)
    qt, vt, qit = heads(qt, N_HEADS), heads(vt, N_HEADS), heads(qit, IDX_HEADS)

    k_top = min(TOPK_MAX, seq // 4)
    vf, cut = _select(qit, kib, wit, tq=_pick(seq, (512, 256, 128)), tk=_pick(seq, (256, 128)), k_top=k_top)
    o_t = _attn(qt, kh, vt, qit, kib, wit, vf, cut,
                tq=_pick(seq, (512, 256, 128)), tk=_pick(seq, (512, 256, 128)))
    oa = o_t.transpose(0, 3, 1, 2).reshape(m, D_ATTN)

    tt = _pick(seq, (256, 128, 64, 32, 16, 8))
    ys, xfin = _ssm(u_f, ssm["bblk"].astype(BF16), ssm["cblk"].astype(BF16), ssm["pw"], ssm["d_skip"], tt=tt)
    x2 = _merge(x1, oa, ys.reshape(m, SSM_WIDTH), *lw["merge"], tm=tm)
    y = _ffn(x2, *lw["ffn2"], lw["g_final"], final_norm=final_norm, tm=tm, tf=lw["tf"])
    new = (heads(kt_f, N_HEADS).transpose(0, 3, 1, 2), heads(vt_f, N_HEADS).transpose(0, 3, 1, 2),
           kit_f.transpose(0, 2, 1),
           xfin[:, 0, :N_STATE].reshape(bsz, N_GROUPS, STATE_DIM),
           xfin[:, 0, N_STATE:].reshape(bsz, N_GROUPS, STATE_DIM))
    return y.reshape(bsz, seq, d), new


def _sample_layer(x, lw, ssm, final_norm, cache_k, cache_v, cache_kidx, st_re, st_im, page_table):
    db, ds, d = x.shape
    assert ds == 1, "one new token per sample sequence"
    n_pool, page_size = cache_k.shape[0], cache_k.shape[1]
    past_len = page_table.shape[1] * page_size
    s1 = _ffn(x.reshape(db, d), *lw["ffn1"], lw["g_final"], final_norm=False, tm=db, tf=lw["tf"])
    qt, kt_f, vt_f, _, _, qit, kit_f, kib, wit, u_f = _proj(
        s1.reshape(1, db, d), lw["g_mix"], lw["w_feat"], lw["w_tok"], tm=db)
    q_b, k_f, v_f, qi_b, ki_f, wi_f = (a[0].T for a in (qt, kt_f, vt_f, qit, kit_f, wit))
    ki_b, u_f = kib[0], u_f[0]

    kidx_t = cache_kidx.transpose(0, 2, 1)
    ck_t = cache_k.transpose(0, 2, 3, 1)
    cv_t = cache_v.transpose(0, 2, 3, 1)
    lane0 = lambda a: jnp.zeros(a.shape + (page_size,), a.dtype).at[..., 0].set(a)

    qi_rows = jnp.zeros((db, 8, IDX_DIM), BF16).at[:, :IDX_HEADS].set(qi_b.reshape(db, IDX_HEADS, IDX_DIM))
    scores = _sample_scores(page_table, qi_rows, wi_f[:, :, None], lane0(ki_b), kidx_t)

    n_keys = past_len + 1
    k_top = min(TOPK_MAX, n_keys // 4)
    n = scores.shape[2]
    scores_t = jnp.zeros((n, -(-db // 128) * 128), F32).at[:, :db].set(scores[:, 0, :].T)
    thr, cut = _sample_select(scores_t, n_keys=n_keys, k_top=k_top, tk=256)

    qb = jnp.broadcast_to(q_b.astype(F32).reshape(db, N_HEADS, HEAD_DIM, 1), (db, N_HEADS, HEAD_DIM, page_size))
    o_s = _sample_attn(page_table, qb, scores, thr[0, 0, :db].reshape(db, 1, 1), cut[0, 0, :db].reshape(db, 1, 1),
                       lane0(k_f.reshape(db, N_HEADS, HEAD_DIM)), lane0(v_f.reshape(db, N_HEADS, HEAD_DIM)),
                       ck_t, cv_t)
    oa = o_s.reshape(db, D_ATTN).astype(BF16)

    x0 = jnp.concatenate([st_re.reshape(db, N_STATE), st_im.reshape(db, N_STATE)], axis=1)
    ys, x_new = _ssm_step(u_f, x0, ssm["bblk"], ssm["cblk"], ssm["pw"], ssm["d_skip"])
    s2 = _merge(s1, oa, ys, *lw["merge"], tm=db)
    y = _ffn(s2, *lw["ffn2"], lw["g_final"], final_norm=final_norm, tm=db, tf=lw["tf"])
    new = (k_f.reshape(db, ds, N_HEADS, HEAD_DIM), v_f.reshape(db, ds, N_HEADS, HEAD_DIM),
           ki_f.reshape(db, ds, IDX_DIM),
           x_new[:, :N_STATE].reshape(db, N_GROUPS, STATE_DIM),
           x_new[:, N_STATE:].reshape(db, N_GROUPS, STATE_DIM))
    return y.reshape(db, ds, d), new


def kernel(x_prompt, x_sample, cache_k, cache_v, cache_kidx, state_ssm_re, state_ssm_im, page_table,
           g_ffn1, w1_gate, w1_up, w1_down, g_mix, w_in, a_re, a_im, log_dt, b_re, b_im, c_re, c_im,
           d_skip, w_glu, b_glu, w_pa, w_pb, w_out, g_ffn2, w2_gate, w2_up, w2_down, g_final):
    depth = w_in.shape[0]
    row = lambda a: a.reshape(1, -1)
    bf = lambda a: a.astype(BF16)
    xp, xs = x_prompt, x_sample
    new_p, new_s = [], []
    for l in range(depth):
        w_feat, w_tok, w_gates = _pack_w_in(w_in[l])
        lw = dict(
            ffn1=(row(g_ffn1[l]), bf(w1_gate[l]), bf(w1_up[l]), bf(w1_down[l])),
            ffn2=(row(g_ffn2[l]), bf(w2_gate[l]), bf(w2_up[l]), bf(w2_down[l])),
            merge=(row(g_mix[l]), w_gates, bf(w_glu[l]), row(b_glu[l]), bf(w_pa[l]), bf(w_pb[l]), bf(w_out[l])),
            g_mix=row(g_mix[l]), w_feat=w_feat, w_tok=w_tok, g_final=row(g_final),
            tf=_pick(w1_gate.shape[2], (1408, 1024, 512, 256, 128)),
        )
        pwr, pwi, bbr, bbi = _discretize(a_re[l], a_im[l], log_dt[l], b_re[l], b_im[l])
        ssm = dict(
            pw=jnp.concatenate([pwr, pwi], axis=1),
            bblk=jnp.concatenate([_block_diag_in(bbr), _block_diag_in(bbi)], axis=1),
            cblk=jnp.concatenate([_block_diag_out(c_re[l]), -_block_diag_out(c_im[l])], axis=0),
            d_skip=row(d_skip[l]),
        )
        last = l == depth - 1
        xp, st_p = _prompt_layer(xp, lw, ssm, last)
        xs, st_s = _sample_layer(xs, lw, ssm, last, cache_k[l], cache_v[l], cache_kidx[l],
                                 state_ssm_re[l], state_ssm_im[l], page_table)
        new_p.append(st_p)
        new_s.append(st_s)
    stack = lambda states, i: jnp.stack([s[i] for s in states])
    return (xp, xs) + tuple(stack(new_p, i) for i in range(5)) + tuple(stack(new_s, i) for i in range(5))
```

```python
import functools

import numpy as np
import jax
import jax.numpy as jnp
from jax import lax
from jax.experimental import pallas as pl
from jax.experimental.pallas import tpu as pltpu

F32 = jnp.float32
BF16 = jnp.bfloat16
I32 = jnp.int32

D_MODEL = 1024
N_HEADS = 8
HEAD_DIM = 64
D_ATTN = N_HEADS * HEAD_DIM
IDX_HEADS = 4
IDX_DIM = 64
TOPK_MAX = 256
SSM_WIDTH = 512
GROUP_CH = 16
N_GROUPS = SSM_WIDTH // GROUP_CH
STATE_DIM = 64
N_STATE = N_GROUPS * STATE_DIM
NORM_EPS = 1e-6
NEG_BIG = -1e30

INT_MAX = 2 ** 31 - 1

VMEM_LIMIT_BYTES = 56 * 1024 * 1024


def _cparams(*sem):
    return pltpu.CompilerParams(dimension_semantics=sem, vmem_limit_bytes=VMEM_LIMIT_BYTES)


def _rmsnorm(x, g):
    y = x * lax.rsqrt(jnp.mean(x * x, axis=-1, keepdims=True) + NORM_EPS)
    return y * g


def _sigmoid(x):
    return 1.0 / (1.0 + jnp.exp(-x))


def _ffn_kernel(x_ref, g_ref, wg_ref, wu_ref, wd_ref, gf_ref, o_ref, *, final_norm, tf):
    x = x_ref[...]
    xn = _rmsnorm(x, g_ref[...]).astype(BF16)
    acc = None
    for f0 in range(0, wg_ref.shape[1], tf):
        gate = jnp.dot(xn, wg_ref[:, f0:f0 + tf], preferred_element_type=F32)
        up = jnp.dot(xn, wu_ref[:, f0:f0 + tf], preferred_element_type=F32)
        act = (gate * _sigmoid(gate)) * up
        part = jnp.dot(act.astype(BF16), wd_ref[f0:f0 + tf, :], preferred_element_type=F32)
        acc = part if acc is None else acc + part
    y = x + 0.5 * acc
    if final_norm:
        y = _rmsnorm(y, gf_ref[...])
    o_ref[...] = y


def _ffn(x, g, wg, wu, wd, g_final, *, final_norm, tm, tf):
    m, d = x.shape
    resident = lambda a: pl.BlockSpec(a.shape, lambda i: (0, 0), pipeline_mode=pl.Buffered(1))
    return pl.pallas_call(
        functools.partial(_ffn_kernel, final_norm=final_norm, tf=tf),
        grid=(m // tm,),
        in_specs=[pl.BlockSpec((tm, d), lambda i: (i, 0)), resident(g), resident(wg), resident(wu),
                  resident(wd), resident(g_final)],
        out_specs=pl.BlockSpec((tm, d), lambda i: (i, 0)),
        out_shape=jax.ShapeDtypeStruct((m, d), F32),
        compiler_params=_cparams("parallel"),
        name="ffn",
    )(x, g, wg, wu, wd, g_final)


_TQ, _TK, _TV, _TQI, _TKI, _TWI, _TEND = 0, 512, 1024, 1536, 1792, 1856, 1872
_RK, _RKI, _RU, _REND = 0, 512, 640, 1152


def _proj_kernel(x_ref, g_ref, wt_ref, wr_ref, qt_ref, ktf_ref, vtf_ref, vt_ref, kh_ref,
                 qit_ref, kitf_ref, kib_ref, wit_ref, u_ref):
    h = _rmsnorm(x_ref[0], g_ref[...]).astype(BF16)
    pt = lax.dot_general(wt_ref[...], h, (((1,), (1,)), ((), ())), preferred_element_type=F32)
    pr = jnp.dot(h, wr_ref[...], preferred_element_type=F32)
    qt_ref[0] = (pt[_TQ:_TK] * (HEAD_DIM ** -0.5)).astype(BF16)
    ktf_ref[0] = pt[_TK:_TV]
    v = pt[_TV:_TQI]
    vtf_ref[0] = v
    vt_ref[0] = v.astype(BF16)
    qit_ref[0] = pt[_TQI:_TKI].astype(BF16)
    kitf_ref[0] = pt[_TKI:_TWI]
    wit_ref[0] = pt[_TWI:_TWI + 8] * ((IDX_HEADS * IDX_DIM) ** -0.5)
    k = pr[:, _RK:_RKI].astype(BF16)
    for hd in range(N_HEADS):
        kh_ref[0, hd] = k[:, hd * HEAD_DIM:(hd + 1) * HEAD_DIM]
    kib_ref[0] = pr[:, _RKI:_RKI + IDX_DIM].astype(BF16)
    u_ref[0] = pr[:, _RU:_REND]


def _proj(x, g, wt, wr, *, tm):
    b, t, d = x.shape
    feat = lambda n: pl.BlockSpec((1, n, tm), lambda bb, i: (bb, 0, i))
    tok = lambda n: pl.BlockSpec((1, tm, n), lambda bb, i: (bb, i, 0))
    outs = [
        (feat(D_ATTN), (b, D_ATTN, t), BF16), (feat(D_ATTN), (b, D_ATTN, t), F32),
        (feat(D_ATTN), (b, D_ATTN, t), F32), (feat(D_ATTN), (b, D_ATTN, t), BF16),
        (pl.BlockSpec((1, N_HEADS, tm, HEAD_DIM), lambda bb, i: (bb, 0, i, 0)), (b, N_HEADS, t, HEAD_DIM), BF16),
        (feat(IDX_HEADS * IDX_DIM), (b, IDX_HEADS * IDX_DIM, t), BF16), (feat(IDX_DIM), (b, IDX_DIM, t), F32),
        (tok(IDX_DIM), (b, t, IDX_DIM), BF16), (feat(8), (b, 8, t), F32), (tok(SSM_WIDTH), (b, t, SSM_WIDTH), F32),
    ]
    return pl.pallas_call(
        _proj_kernel,
        grid=(b, t // tm),
        in_specs=[tok(d), pl.BlockSpec((1, d), lambda bb, i: (0, 0)),
                  pl.BlockSpec((_TEND, d), lambda bb, i: (0, 0)), pl.BlockSpec((d, _REND), lambda bb, i: (0, 0))],
        out_specs=[spec for spec, _, _ in outs],
        out_shape=[jax.ShapeDtypeStruct(shape, dt) for _, shape, dt in outs],
        compiler_params=_cparams("parallel", "parallel"),
        name="proj",
    )(x, g, wt, wr)


def _scores_t(kc, qit, w):
    s = None
    for h in range(IDX_HEADS):
        d = jnp.dot(kc, qit[h], preferred_element_type=F32)
        term = jnp.maximum(d, 0.0) * w[h:h + 1, :]
        s = term if s is None else s + term
    return s


def _select_bias(s, vf, kidx, cut):
    tie = jnp.where(kidx < cut, 0.0, -jnp.inf)
    return jnp.where(s > vf, 0.0, jnp.where(s == vf, tie, -jnp.inf))


_FOLD_ROWS = 16


def _fold8(x, op):
    return op(x.reshape(x.shape[0] // _FOLD_ROWS, _FOLD_ROWS, x.shape[1]), axis=0)


_PROBES_PER_ROUND = 4
_SETTLE_FIRST, _SETTLE_EVERY, _BISECT_CAP = 4, 2, 80
_SETTLE_FEW_FROM, _SETTLE_FEW_LANES = 2, 16
_STUCK_FLAG = 1 << 16


def _topk_threshold(s_ref, n_chunks, tk, tq, mx, mn, n_hi, n_nc, k_top, thr_ref, cut_ref, fst_ref, ist_ref):
    row_iota = lax.broadcasted_iota(I32, (tk, tq), 0)
    zeros_i = jnp.zeros((_FOLD_ROWS, tq), I32)

    def chunk(c):
        k0 = pl.multiple_of(c * tk, tk)
        return s_ref[pl.ds(k0, tk), :], k0

    def count_gt(v):
        def body(c, acc):
            blk, _ = chunk(c)
            return acc + _fold8(jnp.where(blk > v, 1, 0), jnp.sum)
        real = jnp.sum(lax.fori_loop(0, n_chunks, body, zeros_i), axis=0, keepdims=True)
        return real + jnp.where(NEG_BIG > v, n_nc, 0)

    def count_gt_eq(v):
        def body(c, acc):
            blk, _ = chunk(c)
            return (acc[0] + _fold8(jnp.where(blk > v, 1, 0), jnp.sum),
                    acc[1] + _fold8(jnp.where(blk == v, 1, 0), jnp.sum))
        g, e = lax.fori_loop(0, n_chunks, body, (zeros_i, zeros_i))
        g = jnp.sum(g, axis=0, keepdims=True) + jnp.where(NEG_BIG > v, n_nc, 0)
        return g, jnp.sum(e, axis=0, keepdims=True)

    def count_eq(v):
        def body(c, acc):
            blk, _ = chunk(c)
            return acc + _fold8(jnp.where(blk == v, 1, 0), jnp.sum)
        return jnp.sum(lax.fori_loop(0, n_chunks, body, zeros_i), axis=0, keepdims=True)

    def max_le(v):
        def body(c, acc):
            blk, _ = chunk(c)
            return jnp.maximum(acc, _fold8(jnp.where(blk <= v, blk, -jnp.inf), jnp.max))
        m = jnp.max(lax.fori_loop(0, n_chunks, body, jnp.full((_FOLD_ROWS, tq), -jnp.inf, F32)),
                    axis=0, keepdims=True)
        return jnp.where((n_nc > 0) & (NEG_BIG <= v), jnp.maximum(m, NEG_BIG), m)

    def settle(m, counts=None):
        gt, eq_real = count_gt_eq(m) if counts is None else counts
        eq = eq_real + jnp.where(m == NEG_BIG, n_nc, 0)
        found = (ist_ref[0:1, :] == 0) & (gt + eq >= k_top)
        need = k_top - gt
        fst_ref[2:3, :] = jnp.where(found, m, fst_ref[2:3, :])
        ist_ref[1:2, :] = jnp.where(found, need, ist_ref[1:2, :])
        ist_ref[2:3, :] = jnp.where(found, jnp.where(eq_real > need, 1, 0), ist_ref[2:3, :])
        ist_ref[3:4, :] = jnp.where(found, INT_MAX, ist_ref[3:4, :])
        ist_ref[0:1, :] = jnp.where(found, 1, ist_ref[0:1, :])

    low = n_hi < k_top
    hi0 = jnp.where(low, NEG_BIG, mx)
    fst_ref[0:1, :] = jnp.where(low, jnp.minimum(mn, NEG_BIG), jnp.maximum(mn, NEG_BIG))
    fst_ref[1:2, :] = hi0
    fst_ref[2:3, :] = hi0
    ist_ref[...] = jnp.zeros_like(ist_ref)
    settle(hi0, (jnp.where(low, n_hi, 0), count_eq(hi0)))

    def probe(lo, hi, thr, done):
        mid = 0.5 * lo + 0.5 * hi
        c = count_gt(mid)
        live = done == 0
        hit = live & (c == k_top)
        stuck = jnp.where(live & ((mid <= lo) | (mid >= hi)), 2, 0)
        return (jnp.where(live & (c > k_top), mid, lo), jnp.where(live & (c < k_top), mid, hi),
                jnp.where(hit, mid, thr), jnp.where(hit, 1, done), stuck)

    def step(carry):
        it, _ = carry
        lo, hi, thr, done = fst_ref[0:1, :], fst_ref[1:2, :], fst_ref[2:3, :], ist_ref[0:1, :]
        stuck = jnp.zeros_like(done)
        for _ in range(_PROBES_PER_ROUND):
            lo, hi, thr, done, s = probe(lo, hi, thr, done)
            stuck = jnp.maximum(stuck, s)
        fst_ref[0:1, :] = lo
        fst_ref[1:2, :] = hi
        fst_ref[2:3, :] = thr
        ist_ref[0:1, :] = done
        tally = jnp.sum((1 - done) + jnp.where(stuck > 0, _STUCK_FLAG, 0))
        active = tally % _STUCK_FLAG
        due = (it >= _SETTLE_FIRST) & ((it - _SETTLE_FIRST) % _SETTLE_EVERY == 0)
        few = (it >= _SETTLE_FEW_FROM) & (active <= _SETTLE_FEW_LANES)
        check = (active > 0) & (due | few | (tally >= _STUCK_FLAG))

        @pl.when(check)
        def _():
            settle(max_le(hi))

        return it + 1, lax.cond(check, lambda: jnp.sum(1 - ist_ref[0:1, :]), lambda: active)

    lax.while_loop(lambda carry: (carry[1] > 0) & (carry[0] < _BISECT_CAP), step,
                   (jnp.int32(0), jnp.sum(1 - ist_ref[0:1, :])))
    thr_ref[0] = fst_ref[2:3, :]
    cut_ref[0] = ist_ref[3:4, :]

    @pl.when(jnp.max(ist_ref[2:3, :]) > 0)
    def _():
        thr = fst_ref[2:3, :]
        need = ist_ref[1:2, :].astype(F32)
        tri = jnp.where(lax.broadcasted_iota(I32, (tk, tk), 1) <= lax.broadcasted_iota(I32, (tk, tk), 0),
                        1.0, 0.0).astype(BF16)

        def body(c, carry):
            seen, cut = carry
            blk, k0 = chunk(c)
            tie = blk == thr
            rank = jnp.dot(tri, jnp.where(tie, 1.0, 0.0).astype(BF16), preferred_element_type=F32) + seen
            bound = jnp.where(tie, jnp.where(rank <= need, k0 + row_iota + 1, 0), 0)
            return rank[tk - 1:tk, :], jnp.maximum(cut, jnp.max(bound, axis=0, keepdims=True))

        _, cut = lax.fori_loop(0, n_chunks, body, (jnp.zeros((1, tq), F32), jnp.zeros((1, tq), I32)))
        cut_ref[0] = jnp.where(ist_ref[2:3, :] > 0, cut, ist_ref[3:4, :])


def _select_kernel(qit_ref, ki_ref, wit_ref, thr_ref, cut_ref, s_ref, fst_ref, ist_ref, *, tq, tk, n_keys, k_top):
    i = pl.program_id(1)
    q0 = i * tq
    n_chunks = (q0 + tq + tk - 1) // tk
    t_row = q0 + lax.broadcasted_iota(I32, (1, tq), 1)
    row_iota = lax.broadcasted_iota(I32, (tk, tq), 0)
    qit = qit_ref[0]
    w = wit_ref[0]

    def build(c, carry):
        mx, mn, n_hi = carry
        k0 = pl.multiple_of(c * tk, tk)
        s = _scores_t(ki_ref[0, pl.ds(k0, tk), :], qit, w)
        causal = (k0 + row_iota) <= t_row
        sm = jnp.where(causal, s, -jnp.inf)
        s_ref[pl.ds(k0, tk), :] = sm
        return (jnp.maximum(mx, _fold8(sm, jnp.max)),
                jnp.minimum(mn, _fold8(jnp.where(causal, s, jnp.inf), jnp.min)),
                n_hi + _fold8(jnp.where(sm > NEG_BIG, 1, 0), jnp.sum))

    mx, mn, n_hi = lax.fori_loop(
        0, n_chunks, build,
        (jnp.full((_FOLD_ROWS, tq), -jnp.inf, F32), jnp.full((_FOLD_ROWS, tq), jnp.inf, F32),
         jnp.zeros((_FOLD_ROWS, tq), I32)))
    n_nc = (n_keys - 1) - t_row
    _topk_threshold(s_ref, n_chunks, tk, tq, jnp.max(mx, axis=0, keepdims=True),
                    jnp.min(mn, axis=0, keepdims=True), jnp.sum(n_hi, axis=0, keepdims=True),
                    n_nc, k_top, thr_ref, cut_ref, fst_ref, ist_ref)


def _select(qit, ki, wit, *, tq, tk, k_top):
    b, _, _, t = qit.shape
    kern = functools.partial(_select_kernel, tq=tq, tk=tk, n_keys=t, k_top=k_top)
    return pl.pallas_call(
        kern,
        grid=(b, t // tq),
        in_specs=[
            pl.BlockSpec((1, IDX_HEADS, IDX_DIM, tq), lambda bb, i: (bb, 0, 0, i)),
            pl.BlockSpec((1, t, IDX_DIM), lambda bb, i: (bb, 0, 0)),
            pl.BlockSpec((1, 8, tq), lambda bb, i: (bb, 0, i)),
        ],
        out_specs=[pl.BlockSpec((1, 1, tq), lambda bb, i: (bb, 0, i)),
                   pl.BlockSpec((1, 1, tq), lambda bb, i: (bb, 0, i))],
        out_shape=[jax.ShapeDtypeStruct((b, 1, t), F32), jax.ShapeDtypeStruct((b, 1, t), I32)],
        scratch_shapes=[pltpu.VMEM((t, tq), F32), pltpu.VMEM((8, tq), F32), pltpu.VMEM((8, tq), I32)],
        compiler_params=_cparams("parallel", "parallel"),
        name="select",
    )(qit, ki, wit)


_ATTN_SUB_ROWS = 128


def _n_key_tiles(i, tq, tk):
    return (i * tq + tq + tk - 1) // tk


def _attn_kernel(it_ref, jt_ref, qt_ref, k_ref, vt_ref, qit_ref, ki_ref, wit_ref, vf_ref, cut_ref, o_ref,
                 m_ref, l_ref, acc_ref, bias_ref, lga_ref, lgb_ref, pa_ref, pb_ref, *, tq, tk):
    i = it_ref[pl.program_id(1)]
    j = jt_ref[pl.program_id(1)]
    nkt = _n_key_tiles(i, tq, tk)
    sub = _ATTN_SUB_ROWS
    chunks = [slice(c * sub, (c + 1) * sub) for c in range(tk // sub)]

    @pl.when(j == 0)
    def _():
        m_ref[...] = jnp.full_like(m_ref, -jnp.inf)
        l_ref[...] = jnp.zeros_like(l_ref)
        acc_ref[...] = jnp.zeros_like(acc_ref)

    @pl.when(j < nkt)
    def _():
        qit = qit_ref[0]
        w = wit_ref[0]
        vf = vf_ref[0]
        cut = cut_ref[0]
        t_row = i * tq + lax.broadcasted_iota(I32, (1, tq), 1)
        row = lax.broadcasted_iota(I32, (sub, tq), 0)
        for c, rows in enumerate(chunks):
            kidx = j * tk + c * sub + row
            bias = _select_bias(_scores_t(ki_ref[0, rows, :], qit, w), vf, kidx, cut)
            bias_ref[rows, :] = jnp.where(kidx <= t_row, bias, -jnp.inf)

        def logits(h, lg_ref):
            qt = qt_ref[0, h]
            part = jnp.full((8, tq), -jnp.inf, F32)
            for rows in chunks:
                lg = jnp.dot(k_ref[0, h, rows, :], qt, preferred_element_type=F32) + bias_ref[rows, :]
                lg_ref[rows, :] = lg
                part = jnp.maximum(part, jnp.max(lg.reshape(sub // 8, 8, tq), axis=0))
            return jnp.max(part, axis=0, keepdims=True)

        def absorb(h, lg_ref, p_ref, tile_max):
            hrow = pl.ds(h, 1)
            m_old = m_ref[hrow, :]
            m_new = jnp.maximum(m_old, tile_max)
            m_safe = jnp.where(m_new == -jnp.inf, 0.0, m_new)
            alpha = jnp.exp(m_old - m_safe)
            psum = jnp.zeros((8, tq), F32)
            for rows in chunks:
                p = jnp.exp(lg_ref[rows, :] - m_safe)
                psum = psum + jnp.sum(p.reshape(sub // 8, 8, tq), axis=0)
                p_ref[rows, :] = p.astype(BF16)
            l_ref[hrow, :] = alpha * l_ref[hrow, :] + jnp.sum(psum, axis=0, keepdims=True)
            acc_ref[h] = alpha * acc_ref[h] + jnp.dot(vt_ref[0, h], p_ref[...], preferred_element_type=F32)
            m_ref[hrow, :] = m_new

        def two_heads(t, max_a):
            h = 2 * t
            max_b = logits(h + 1, lgb_ref)
            absorb(h, lga_ref, pa_ref, max_a)
            max_a = logits(h + 2, lga_ref)
            absorb(h + 1, lgb_ref, pb_ref, max_b)
            return max_a

        max_a = lax.fori_loop(0, N_HEADS // 2 - 1, two_heads, logits(0, lga_ref))
        max_b = logits(N_HEADS - 1, lgb_ref)
        absorb(N_HEADS - 2, lga_ref, pa_ref, max_a)
        absorb(N_HEADS - 1, lgb_ref, pb_ref, max_b)

    @pl.when(j == nkt - 1)
    def _():
        for h in range(0, N_HEADS, 2):
            pair = jnp.concatenate([acc_ref[h] / l_ref[h:h + 1, :], acc_ref[h + 1] / l_ref[h + 1:h + 2, :]], axis=0)
            o_ref[0, :, h * HEAD_DIM:(h + 2) * HEAD_DIM] = jnp.transpose(pair).astype(o_ref.dtype)


def _attn(qt, k, vt, qit, ki, wit, vf, cut, *, tq, tk):
    b, _, _, t = qt.shape
    pairs = [(i, j) for i in range(t // tq) for j in range(_n_key_tiles(i, tq, tk))]
    i_tab = jnp.asarray([p[0] for p in pairs], I32)
    j_tab = jnp.asarray([p[1] for p in pairs], I32)
    q_tile = lambda *lead: (lambda bb, s, it, jt: (bb,) + lead + (it[s],))
    return pl.pallas_call(
        functools.partial(_attn_kernel, tq=tq, tk=tk),
        grid_spec=pltpu.PrefetchScalarGridSpec(
            num_scalar_prefetch=2,
            grid=(b, len(pairs)),
            in_specs=[
                pl.BlockSpec((1, N_HEADS, HEAD_DIM, tq), q_tile(0, 0)),
                pl.BlockSpec((1, N_HEADS, tk, HEAD_DIM), lambda bb, s, it, jt: (bb, 0, jt[s], 0)),
                pl.BlockSpec((1, N_HEADS, HEAD_DIM, tk), lambda bb, s, it, jt: (bb, 0, 0, jt[s])),
                pl.BlockSpec((1, IDX_HEADS, IDX_DIM, tq), q_tile(0, 0)),
                pl.BlockSpec((1, tk, IDX_DIM), lambda bb, s, it, jt: (bb, jt[s], 0)),
                pl.BlockSpec((1, 8, tq), q_tile(0)),
                pl.BlockSpec((1, 1, tq), q_tile(0)),
                pl.BlockSpec((1, 1, tq), q_tile(0)),
            ],
            out_specs=pl.BlockSpec((1, tq, D_ATTN), lambda bb, s, it, jt: (bb, it[s], 0)),
            scratch_shapes=[pltpu.VMEM((N_HEADS, tq), F32), pltpu.VMEM((N_HEADS, tq), F32),
                            pltpu.VMEM((N_HEADS, HEAD_DIM, tq), F32),
                            pltpu.VMEM((tk, tq), F32), pltpu.VMEM((tk, tq), F32), pltpu.VMEM((tk, tq), F32),
                            pltpu.VMEM((tk, tq), BF16), pltpu.VMEM((tk, tq), BF16)],
        ),
        out_shape=jax.ShapeDtypeStruct((b, t, D_ATTN), BF16),
        compiler_params=_cparams("parallel", "arbitrary"),
        name="attn",
    )(i_tab, j_tab, qt, k, vt, qit, ki, wit, vf, cut)


def _zoh(ar, ai, ldt):
    dt = jnp.exp(ldt)
    mag = jnp.exp(dt * ar)
    abr = mag * jnp.cos(dt * ai)
    abi = mag * jnp.sin(dt * ai)
    den = ar * ar + ai * ai
    nr = abr - 1.0
    ni = abi
    return abr, abi, (nr * ar + ni * ai) / den, (ni * ar - nr * ai) / den


def _disc_kernel(ar_ref, ai_ref, ldt_ref, ar16_ref, ai16_ref, ldt16_ref, br_ref, bi_ref,
                 pwr_ref, pwi_ref, bbr_ref, bbi_ref):
    abr, abi, _, _ = _zoh(ar_ref[...], ai_ref[...], ldt_ref[...])
    pr, pi = abr, abi
    for j in range(8):
        pwr_ref[j:j + 1, :] = pr
        pwi_ref[j:j + 1, :] = pi
        pr, pi = pr * abr - pi * abi, pr * abi + pi * abr
    _, _, fr, fi = _zoh(ar16_ref[...], ai16_ref[...], ldt16_ref[...])
    br = br_ref[...]
    bi = bi_ref[...]
    bbr_ref[...] = fr * br - fi * bi
    bbi_ref[...] = fr * bi + fi * br


def _discretize(a_re, a_im, log_dt, b_re, b_im):
    flat = lambda a: a.reshape(1, -1)
    ldt = jnp.broadcast_to(log_dt[:, None], (N_GROUPS, STATE_DIM))
    rep = lambda a: flat(jnp.broadcast_to(a[:, :, None], (N_GROUPS, STATE_DIM, GROUP_CH)))
    n16 = N_STATE * GROUP_CH
    pwr, pwi, bbr, bbi = pl.pallas_call(
        _disc_kernel,
        out_shape=[jax.ShapeDtypeStruct((8, N_STATE), F32), jax.ShapeDtypeStruct((8, N_STATE), F32),
                   jax.ShapeDtypeStruct((1, n16), F32), jax.ShapeDtypeStruct((1, n16), F32)],
        name="s5_discretize",
    )(flat(a_re), flat(a_im), flat(ldt), rep(a_re), rep(a_im), rep(ldt), flat(b_re), flat(b_im))
    return pwr, pwi, bbr.reshape(N_GROUPS, STATE_DIM, GROUP_CH), bbi.reshape(N_GROUPS, STATE_DIM, GROUP_CH)


def _block_diag_in(bb):
    eye = jnp.eye(N_GROUPS, dtype=bb.dtype)
    return jnp.einsum('gpc,gh->gchp', bb, eye).reshape(SSM_WIDTH, N_STATE)


def _block_diag_out(c):
    eye = jnp.eye(N_GROUPS, dtype=c.dtype)
    return jnp.einsum('gcp,gh->gphc', c, eye).reshape(N_STATE, SSM_WIDTH)


_LANE_CHUNK = 512


def _ssm_kernel(u_ref, bblk_ref, cblk_ref, pw_ref, step_ref, d_ref, y_ref, xf_ref, x_ref, carry_ref, *, tt):
    ts = pl.program_id(1)

    @pl.when(ts == 0)
    def _():
        carry_ref[...] = jnp.zeros_like(carry_ref)

    u = u_ref[0]
    ub = u.astype(BF16)
    n_chunks = 2 * N_STATE // _LANE_CHUNK
    ch = _LANE_CHUNK // STATE_DIM * GROUP_CH
    chans = [slice((c % (N_STATE // _LANE_CHUNK)) * ch, (c % (N_STATE // _LANE_CHUNK) + 1) * ch)
             for c in range(n_chunks)]
    for c in range(n_chunks):
        cols = slice(c * _LANE_CHUNK, (c + 1) * _LANE_CHUNK)
        x_ref[:, cols] = jnp.dot(ub[:, chans[c]], bblk_ref[chans[c], cols], preferred_element_type=F32)

    def group(r, carry):
        r0 = pl.multiple_of(r * 8, 8)
        for c in range(N_STATE // _LANE_CHUNK):
            re = pl.ds(c * _LANE_CHUNK, _LANE_CHUNK)
            im = pl.ds(N_STATE + c * _LANE_CHUNK, _LANE_CHUNK)
            xr = x_ref[pl.ds(r0, 8), re]
            xi = x_ref[pl.ds(r0, 8), im]
            for k, d in enumerate((1, 2, 4)):
                ar = step_ref[k, :, re]
                ai = step_ref[k, :, im]
                sr = pltpu.roll(xr, d, 0)
                si = pltpu.roll(xi, d, 0)
                xr, xi = xr + (ar * sr - ai * si), xi + (ar * si + ai * sr)
            cr = carry_ref[:, re]
            ci = carry_ref[:, im]
            pr = pw_ref[:, re]
            pi = pw_ref[:, im]
            xr, xi = xr + (pr * cr - pi * ci), xi + (pr * ci + pi * cr)
            x_ref[pl.ds(r0, 8), re] = xr
            x_ref[pl.ds(r0, 8), im] = xi
            carry_ref[:, re] = xr[7:8, :]
            carry_ref[:, im] = xi[7:8, :]
        return carry

    lax.fori_loop(0, tt // 8, group, 0)
    y_ref[0] = d_ref[...] * u
    for c in range(n_chunks):
        cols = slice(c * _LANE_CHUNK, (c + 1) * _LANE_CHUNK)
        y_ref[0, :, chans[c]] += jnp.dot(x_ref[:, cols].astype(BF16), cblk_ref[cols, chans[c]],
                                         preferred_element_type=F32)
    xf_ref[0] = carry_ref[...]


def _ssm(u, bblk, cblk, pw, d_skip, *, tt):
    b, t, _ = u.shape
    rows = jnp.arange(8)[None, :, None]
    steps = jnp.stack([jnp.where(rows >= dd, pw[dd - 1][None, None, :], 0.0)[0] for dd in (1, 2, 4)])
    return pl.pallas_call(
        functools.partial(_ssm_kernel, tt=tt),
        grid=(b, t // tt),
        in_specs=[
            pl.BlockSpec((1, tt, SSM_WIDTH), lambda bb, s: (bb, s, 0)),
            pl.BlockSpec((SSM_WIDTH, 2 * N_STATE), lambda bb, s: (0, 0)),
            pl.BlockSpec((2 * N_STATE, SSM_WIDTH), lambda bb, s: (0, 0)),
            pl.BlockSpec((8, 2 * N_STATE), lambda bb, s: (0, 0)),
            pl.BlockSpec((3, 8, 2 * N_STATE), lambda bb, s: (0, 0, 0)),
            pl.BlockSpec((1, SSM_WIDTH), lambda bb, s: (0, 0)),
        ],
        out_specs=[pl.BlockSpec((1, tt, SSM_WIDTH), lambda bb, s: (bb, s, 0)),
                   pl.BlockSpec((1, 1, 2 * N_STATE), lambda bb, s: (bb, 0, 0))],
        out_shape=[jax.ShapeDtypeStruct((b, t, SSM_WIDTH), F32),
                   jax.ShapeDtypeStruct((b, 1, 2 * N_STATE), F32)],
        scratch_shapes=[pltpu.VMEM((tt, 2 * N_STATE), F32), pltpu.VMEM((1, 2 * N_STATE), F32)],
        compiler_params=_cparams("parallel", "arbitrary"),
        name="s5_scan",
    )(u, bblk, cblk, pw, steps, d_skip)


def _ssm_step_kernel(u_ref, x0_ref, bblk_ref, cblk_ref, pw_ref, d_ref, y_ref, x_ref):
    u = u_ref[...]
    bu = jnp.dot(u, bblk_ref[...], preferred_element_type=F32, precision=lax.Precision.HIGHEST)
    ar = pw_ref[0:1, :N_STATE]
    ai = pw_ref[0:1, N_STATE:]
    x0r = x0_ref[:, :N_STATE]
    x0i = x0_ref[:, N_STATE:]
    xr = ar * x0r - ai * x0i + bu[:, :N_STATE]
    xi = ar * x0i + ai * x0r + bu[:, N_STATE:]
    x_ref[:, :N_STATE] = xr
    x_ref[:, N_STATE:] = xi
    y = jnp.dot(x_ref[...], cblk_ref[...], preferred_element_type=F32, precision=lax.Precision.HIGHEST)
    y_ref[...] = y + d_ref[...] * u


def _ssm_step(u, x0, bblk, cblk, pw, d_skip):
    n = u.shape[0]
    return pl.pallas_call(
        _ssm_step_kernel,
        out_shape=[jax.ShapeDtypeStruct((n, SSM_WIDTH), F32), jax.ShapeDtypeStruct((n, 2 * N_STATE), F32)],
        compiler_params=pltpu.CompilerParams(vmem_limit_bytes=VMEM_LIMIT_BYTES),
        name="s5_step",
    )(u, x0, bblk, cblk, pw, d_skip)


def _gelu_tanh(x):
    c = np.float32(np.sqrt(2.0 / np.pi))
    return 0.5 * x * (1.0 + jnp.tanh(c * (x + 0.044715 * (x * x * x))))


def _merge_kernel(x_ref, oa_ref, ys_ref, g_ref, wgt_ref, wglu_ref, bglu_ref, wpa_ref, wpb_ref, wout_ref, o_ref):
    x = x_ref[...]
    h = _rmsnorm(x, g_ref[...]).astype(BF16)
    gates = _sigmoid(jnp.dot(h, wgt_ref[...], preferred_element_type=F32))
    ys = _gelu_tanh(ys_ref[...])
    glu = jnp.dot(ys.astype(BF16), wglu_ref[...], preferred_element_type=F32) + bglu_ref[...]
    ob = ys * _sigmoid(glu)
    pa = jnp.dot(oa_ref[...], wpa_ref[...], preferred_element_type=F32)
    pb = jnp.dot(ob.astype(BF16), wpb_ref[...], preferred_element_type=F32)
    merged = gates[:, :D_MODEL] * pa + gates[:, D_MODEL:] * pb
    o_ref[...] = x + jnp.dot(merged.astype(BF16), wout_ref[...], preferred_element_type=F32)


def _merge(x, oa, ys, g, wgt, wglu, bglu, wpa, wpb, wout, *, tm):
    m, d = x.shape
    row = lambda n: pl.BlockSpec((tm, n), lambda i: (i, 0))
    full = lambda a: pl.BlockSpec(a.shape, lambda i: (0, 0))
    return pl.pallas_call(
        _merge_kernel,
        grid=(m // tm,),
        in_specs=[row(d), row(D_ATTN), row(SSM_WIDTH), full(g), full(wgt), full(wglu), full(bglu),
                  full(wpa), full(wpb), full(wout)],
        out_specs=row(d),
        out_shape=jax.ShapeDtypeStruct((m, d), F32),
        compiler_params=_cparams("parallel"),
        name="merge",
    )(x, oa, ys, g, wgt, wglu, bglu, wpa, wpb, wout)


_PAGES_PER_STEP = 8
_SCORE_PAGES_PER_STEP = 16


def _page_specs(block, n_pages, per_step):
    last = n_pages // per_step - 1

    def spec(r):
        def index_map(b, p, pt):
            return (pt[b, jnp.minimum(p, last) * per_step + r],) + (0,) * (len(block) - 1)
        return pl.BlockSpec(block, index_map)
    return [spec(r) for r in range(per_step)]


def _per_seq(shape):
    return pl.BlockSpec((1,) + shape, lambda b, p, pt: (b,) + (0,) * len(shape))


def _sample_scores_kernel(pt_ref, qi_ref, w_ref, knew_ref, *refs, page_size, n_steps):
    pages = refs[:-1]
    o_ref = refs[-1]
    p = pl.program_id(1)
    qi = qi_ref[0]
    w = w_ref[0]

    def score(kt):
        d = jnp.dot(qi, kt, preferred_element_type=F32)
        return jnp.sum(jnp.maximum(d, 0.0) * w, axis=0, keepdims=True)

    @pl.when(p < n_steps)
    def _():
        for r, page in enumerate(pages):
            o_ref[0, :, r * page_size:(r + 1) * page_size] = score(page[0].astype(BF16))

    @pl.when(p == n_steps)
    def _():
        o_ref[0] = jnp.full(o_ref.shape[1:], -jnp.inf, F32)
        lane = lax.broadcasted_iota(I32, (1, page_size), 1)
        o_ref[0, :, 0:page_size] = jnp.where(lane == 0, score(knew_ref[0]), -jnp.inf)


def _sample_scores(page_table, qi_rows, w_col, ki_new_t, kidx_t):
    db, n_pages = page_table.shape
    _, _, page_size = kidx_t.shape
    per_step = _pick(n_pages, (_SCORE_PAGES_PER_STEP, _PAGES_PER_STEP))
    assert n_pages % per_step == 0 and per_step % _PAGES_PER_STEP == 0
    steps = n_pages // per_step
    block = per_step * page_size
    return pl.pallas_call(
        functools.partial(_sample_scores_kernel, page_size=page_size, n_steps=steps),
        grid_spec=pltpu.PrefetchScalarGridSpec(
            num_scalar_prefetch=1,
            grid=(db, steps + 1),
            in_specs=[_per_seq((8, IDX_DIM)), _per_seq((8, 1)), _per_seq((IDX_DIM, page_size))]
                     + _page_specs((1, IDX_DIM, page_size), n_pages, per_step),
            out_specs=pl.BlockSpec((1, 1, block), lambda b, p, pt: (b, 0, p)),
        ),
        out_shape=jax.ShapeDtypeStruct((db, 1, (steps + 1) * block), F32),
        compiler_params=_cparams("parallel", "arbitrary"),
        name="sample_scores",
    )(page_table, qi_rows, w_col, ki_new_t, *([kidx_t] * per_step))


def _sample_select_kernel(st_ref, thr_ref, cut_ref, s_ref, fst_ref, ist_ref, *, tk, tq, n_keys, k_top):
    n_chunks = st_ref.shape[0] // tk
    row_iota = lax.broadcasted_iota(I32, (tk, tq), 0)

    def build(c, carry):
        mx, mn, n_hi = carry
        k0 = pl.multiple_of(c * tk, tk)
        s = st_ref[pl.ds(k0, tk), :]
        real = (k0 + row_iota) < n_keys
        sm = jnp.where(real, s, -jnp.inf)
        s_ref[pl.ds(k0, tk), :] = sm
        return (jnp.maximum(mx, _fold8(sm, jnp.max)),
                jnp.minimum(mn, _fold8(jnp.where(real, s, jnp.inf), jnp.min)),
                n_hi + _fold8(jnp.where(sm > NEG_BIG, 1, 0), jnp.sum))

    mx, mn, n_hi = lax.fori_loop(
        0, n_chunks, build,
        (jnp.full((_FOLD_ROWS, tq), -jnp.inf, F32), jnp.full((_FOLD_ROWS, tq), jnp.inf, F32),
         jnp.zeros((_FOLD_ROWS, tq), I32)))
    _topk_threshold(s_ref, n_chunks, tk, tq, jnp.max(mx, axis=0, keepdims=True),
                    jnp.min(mn, axis=0, keepdims=True), jnp.sum(n_hi, axis=0, keepdims=True),
                    jnp.zeros((1, tq), I32), k_top, thr_ref, cut_ref, fst_ref, ist_ref)


def _sample_select(scores_t, *, n_keys, k_top, tk):
    n_rows, tq = scores_t.shape
    kern = functools.partial(_sample_select_kernel, tk=tk, tq=tq, n_keys=n_keys, k_top=k_top)
    return pl.pallas_call(
        kern,
        out_shape=[jax.ShapeDtypeStruct((1, 1, tq), F32), jax.ShapeDtypeStruct((1, 1, tq), I32)],
        scratch_shapes=[pltpu.VMEM((n_rows, tq), F32), pltpu.VMEM((8, tq), F32), pltpu.VMEM((8, tq), I32)],
        compiler_params=pltpu.CompilerParams(vmem_limit_bytes=VMEM_LIMIT_BYTES),
        name="sample_select",
    )(scores_t)


def _sample_attn_kernel(pt_ref, qb_ref, s_ref, thr_ref, cut_ref, kself_ref, vself_ref, *refs,
                        page_size, n_steps):
    kp = refs[:_PAGES_PER_STEP]
    vp = refs[_PAGES_PER_STEP:2 * _PAGES_PER_STEP]
    o_ref, m_ref, l_ref, acc_ref = refs[2 * _PAGES_PER_STEP:]
    step = pl.program_id(1)
    thr = thr_ref[0]
    cut = cut_ref[0]
    lane = lax.broadcasted_iota(I32, (1, page_size), 1)
    is_self = step == n_steps

    @pl.when(step == 0)
    def _():
        m_ref[...] = jnp.full_like(m_ref, -jnp.inf)
        l_ref[...] = jnp.zeros_like(l_ref)
        acc_ref[...] = jnp.zeros_like(acc_ref)

    logits = []
    for r in range(_PAGES_PER_STEP):
        kidx = (step * _PAGES_PER_STEP + r) * page_size + lane
        bias = _select_bias(s_ref[0, :, r * page_size:(r + 1) * page_size], thr, kidx, cut)
        rows = []
        for h in range(N_HEADS):
            kt = kp[r][0, h]
            if r == 0:
                kt = jnp.where(is_self, kself_ref[0, h], kt)
            rows.append(jnp.sum(kt * qb_ref[0, h], axis=0, keepdims=True))
        logits.append(jnp.concatenate(rows, axis=0) + bias)

    m_old = m_ref[...]
    m_new = m_old
    for lg in logits:
        m_new = jnp.maximum(m_new, jnp.max(lg, axis=1, keepdims=True))
    m_safe = jnp.where(m_new == -jnp.inf, 0.0, m_new)
    alpha = jnp.exp(m_old - m_safe)
    probs = [jnp.exp(lg - m_safe) for lg in logits]
    l_new = alpha * l_ref[...]
    for p in probs:
        l_new = l_new + jnp.sum(p, axis=1, keepdims=True)
    l_ref[...] = l_new
    m_ref[...] = m_new
    for h in range(N_HEADS):
        acc = acc_ref[h] * alpha[h:h + 1, :]
        for r in range(_PAGES_PER_STEP):
            vt = vp[r][0, h]
            if r == 0:
                vt = jnp.where(is_self, vself_ref[0, h], vt)
            acc = acc + vt * probs[r][h:h + 1, :]
        acc_ref[h] = acc

    @pl.when(is_self)
    def _():
        for h in range(N_HEADS):
            o_ref[0, h] = jnp.sum(acc_ref[h], axis=1, keepdims=True) / l_ref[h:h + 1, :]


def _sample_attn(page_table, qb, scores, thr, cut, k_self, v_self, ck_t, cv_t):
    db, n_pages = page_table.shape
    _, nh, hd, page_size = ck_t.shape
    steps = n_pages // _PAGES_PER_STEP
    block = _PAGES_PER_STEP * page_size
    page = (1, nh, hd, page_size)
    return pl.pallas_call(
        functools.partial(_sample_attn_kernel, page_size=page_size, n_steps=steps),
        grid_spec=pltpu.PrefetchScalarGridSpec(
            num_scalar_prefetch=1,
            grid=(db, steps + 1),
            in_specs=[_per_seq(page[1:]),
                      pl.BlockSpec((1, 1, block), lambda b, p, pt: (b, 0, p)),
                      _per_seq((1, 1)), _per_seq((1, 1)), _per_seq(page[1:]), _per_seq(page[1:])]
                     + _page_specs(page, n_pages, _PAGES_PER_STEP) + _page_specs(page, n_pages, _PAGES_PER_STEP),
            out_specs=_per_seq((nh, hd, 1)),
            scratch_shapes=[pltpu.VMEM((nh, 1), F32), pltpu.VMEM((nh, 1), F32),
                            pltpu.VMEM((nh, hd, page_size), F32)],
        ),
        out_shape=jax.ShapeDtypeStruct((db, nh, hd, 1), F32),
        compiler_params=_cparams("parallel", "arbitrary"),
        name="sample_attn",
    )(page_table, qb, scores, thr, cut, k_self, v_self,
      *([ck_t] * _PAGES_PER_STEP), *([cv_t] * _PAGES_PER_STEP))


def _pick(n, pref):
    for t in pref:
        if n % t == 0:
            return t
    return n


def _pack_w_in(w_in):
    o = np.cumsum([0, D_ATTN, D_ATTN, D_ATTN, IDX_HEADS * IDX_DIM, IDX_DIM, IDX_HEADS, SSM_WIDTH, 2 * D_MODEL])
    z = lambda n: jnp.zeros((D_MODEL, n), w_in.dtype)
    w_feat = jnp.concatenate([w_in[:, o[0]:o[6]], z(_TEND - _TWI - IDX_HEADS)], axis=1)
    w_tok = jnp.concatenate([w_in[:, o[1]:o[2]], w_in[:, o[4]:o[5]], z(_RU - _RKI - IDX_DIM),
                             w_in[:, o[6]:o[7]]], axis=1)
    return w_feat.T.astype(BF16), w_tok.astype(BF16), w_in[:, o[7]:o[8]].astype(BF16)


def _prompt_layer(x, lw, ssm, final_norm):
    bsz, seq, d = x.shape
    m = bsz * seq
    tm = _pick(m, (512, 256, 128, 64, 32, 16, 8))
    x1 = _ffn(x.reshape(m, d), *lw["ffn1"], lw["g_final"], final_norm=False, tm=tm, tf=lw["tf"])
    qt, kt_f, vt_f, vt, kh, qit, kit_f, kib, wit, u_f = _proj(
        x1.reshape(bsz, seq, d), lw["g_mix"], lw["w_feat"], lw["w_tok"], tm=_pick(seq, (512, 256, 128)))
    heads = lambda a, nh: a.reshape(bsz, nh, -1, seq)
    qt, vt, qit = heads(qt, N_HEADS), heads(vt, N_HEADS), heads(qit, IDX_HEADS)

    k_top = min(TOPK_MAX, seq // 4)
    vf, cut = _select(qit, kib, wit, tq=_pick(seq, (512, 256, 128)), tk=_pick(seq, (256, 128)), k_top=k_top)
    oa = _attn(qt, kh, vt, qit, kib, wit, vf, cut,
               tq=_pick(seq, (512, 256, 128)), tk=_pick(seq, (512, 256, 128))).reshape(m, D_ATTN)

    tt = _pick(seq, (256, 128, 64, 32, 16, 8))
    ys, xfin = _ssm(u_f, ssm["bblk"].astype(BF16), ssm["cblk"].astype(BF16), ssm["pw"], ssm["d_skip"], tt=tt)
    x2 = _merge(x1, oa, ys.reshape(m, SSM_WIDTH), *lw["merge"], tm=tm)
    y = _ffn(x2, *lw["ffn2"], lw["g_final"], final_norm=final_norm, tm=tm, tf=lw["tf"])
    new = (heads(kt_f, N_HEADS).transpose(0, 3, 1, 2), heads(vt_f, N_HEADS).transpose(0, 3, 1, 2),
           kit_f.transpose(0, 2, 1),
           xfin[:, 0, :N_STATE].reshape(bsz, N_GROUPS, STATE_DIM),
           xfin[:, 0, N_STATE:].reshape(bsz, N_GROUPS, STATE_DIM))
    return y.reshape(bsz, seq, d), new


def _sample_layer(x, lw, ssm, final_norm, cache_k, cache_v, cache_kidx, st_re, st_im, page_table):
    db, ds, d = x.shape
    assert ds == 1, "one new token per sample sequence"
    n_pool, page_size = cache_k.shape[0], cache_k.shape[1]
    past_len = page_table.shape[1] * page_size
    s1 = _ffn(x.reshape(db, d), *lw["ffn1"], lw["g_final"], final_norm=False, tm=db, tf=lw["tf"])
    qt, kt_f, vt_f, _, _, qit, kit_f, kib, wit, u_f = _proj(
        s1.reshape(1, db, d), lw["g_mix"], lw["w_feat"], lw["w_tok"], tm=db)
    q_b, k_f, v_f, qi_b, ki_f, wi_f = (a[0].T for a in (qt, kt_f, vt_f, qit, kit_f, wit))
    ki_b, u_f = kib[0], u_f[0]

    kidx_t = cache_kidx.transpose(0, 2, 1)
    ck_t = cache_k.transpose(0, 2, 3, 1)
    cv_t = cache_v.transpose(0, 2, 3, 1)
    lane0 = lambda a: jnp.zeros(a.shape + (page_size,), a.dtype).at[..., 0].set(a)

    qi_rows = jnp.zeros((db, 8, IDX_DIM), BF16).at[:, :IDX_HEADS].set(qi_b.reshape(db, IDX_HEADS, IDX_DIM))
    scores = _sample_scores(page_table, qi_rows, wi_f[:, :, None], lane0(ki_b), kidx_t)

    n_keys = past_len + 1
    k_top = min(TOPK_MAX, n_keys // 4)
    n = scores.shape[2]
    scores_t = jnp.zeros((n, -(-db // 128) * 128), F32).at[:, :db].set(scores[:, 0, :].T)
    thr, cut = _sample_select(scores_t, n_keys=n_keys, k_top=k_top, tk=256)

    qb = jnp.broadcast_to(q_b.astype(F32).reshape(db, N_HEADS, HEAD_DIM, 1), (db, N_HEADS, HEAD_DIM, page_size))
    o_s = _sample_attn(page_table, qb, scores, thr[0, 0, :db].reshape(db, 1, 1), cut[0, 0, :db].reshape(db, 1, 1),
                       lane0(k_f.reshape(db, N_HEADS, HEAD_DIM)), lane0(v_f.reshape(db, N_HEADS, HEAD_DIM)),
                       ck_t, cv_t)
    oa = o_s.reshape(db, D_ATTN).astype(BF16)

    x0 = jnp.concatenate([st_re.reshape(db, N_STATE), st_im.reshape(db, N_STATE)], axis=1)
    ys, x_new = _ssm_step(u_f, x0, ssm["bblk"], ssm["cblk"], ssm["pw"], ssm["d_skip"])
    s2 = _merge(s1, oa, ys, *lw["merge"], tm=db)
    y = _ffn(s2, *lw["ffn2"], lw["g_final"], final_norm=final_norm, tm=db, tf=lw["tf"])
    new = (k_f.reshape(db, ds, N_HEADS, HEAD_DIM), v_f.reshape(db, ds, N_HEADS, HEAD_DIM),
           ki_f.reshape(db, ds, IDX_DIM),
           x_new[:, :N_STATE].reshape(db, N_GROUPS, STATE_DIM),
           x_new[:, N_STATE:].reshape(db, N_GROUPS, STATE_DIM))
    return y.reshape(db, ds, d), new


def kernel(x_prompt, x_sample, cache_k, cache_v, cache_kidx, state_ssm_re, state_ssm_im, page_table,
           g_ffn1, w1_gate, w1_up, w1_down, g_mix, w_in, a_re, a_im, log_dt, b_re, b_im, c_re, c_im,
           d_skip, w_glu, b_glu, w_pa, w_pb, w_out, g_ffn2, w2_gate, w2_up, w2_down, g_final):
    depth = w_in.shape[0]
    row = lambda a: a.reshape(1, -1)
    bf = lambda a: a.astype(BF16)
    xp, xs = x_prompt, x_sample
    new_p, new_s = [], []
    for l in range(depth):
        w_feat, w_tok, w_gates = _pack_w_in(w_in[l])
        lw = dict(
            ffn1=(row(g_ffn1[l]), bf(w1_gate[l]), bf(w1_up[l]), bf(w1_down[l])),
            ffn2=(row(g_ffn2[l]), bf(w2_gate[l]), bf(w2_up[l]), bf(w2_down[l])),
            merge=(row(g_mix[l]), w_gates, bf(w_glu[l]), row(b_glu[l]), bf(w_pa[l]), bf(w_pb[l]), bf(w_out[l])),
            g_mix=row(g_mix[l]), w_feat=w_feat, w_tok=w_tok, g_final=row(g_final),
            tf=_pick(w1_gate.shape[2], (1408, 1024, 512, 256, 128)),
        )
        pwr, pwi, bbr, bbi = _discretize(a_re[l], a_im[l], log_dt[l], b_re[l], b_im[l])
        ssm = dict(
            pw=jnp.concatenate([pwr, pwi], axis=1),
            bblk=jnp.concatenate([_block_diag_in(bbr), _block_diag_in(bbi)], axis=1),
            cblk=jnp.concatenate([_block_diag_out(c_re[l]), -_block_diag_out(c_im[l])], axis=0),
            d_skip=row(d_skip[l]),
        )
        last = l == depth - 1
        xp, st_p = _prompt_layer(xp, lw, ssm, last)
        xs, st_s = _sample_layer(xs, lw, ssm, last, cache_k[l], cache_v[l], cache_kidx[l],
                                 state_ssm_re[l], state_ssm_im[l], page_table)
        new_p.append(st_p)
        new_s.append(st_s)
    stack = lambda states, i: jnp.stack([s[i] for s in states])
    return (xp, xs) + tuple(stack(new_p, i) for i in range(5)) + tuple(stack(new_s, i) for i in range(5))
```

```python
import functools

import numpy as np
import jax
import jax.numpy as jnp
from jax import lax
from jax.experimental import pallas as pl
from jax.experimental.pallas import tpu as pltpu

F32 = jnp.float32
BF16 = jnp.bfloat16
I32 = jnp.int32

D_MODEL = 1024
N_HEADS = 8
HEAD_DIM = 64
D_ATTN = N_HEADS * HEAD_DIM
IDX_HEADS = 4
IDX_DIM = 64
TOPK_MAX = 256
SSM_WIDTH = 512
GROUP_CH = 16
N_GROUPS = SSM_WIDTH // GROUP_CH
STATE_DIM = 64
N_STATE = N_GROUPS * STATE_DIM
NORM_EPS = 1e-6
NEG_BIG = -1e30

INT_MAX = 2 ** 31 - 1

VMEM_LIMIT_BYTES = 56 * 1024 * 1024


def _cparams(*sem):
    return pltpu.CompilerParams(dimension_semantics=sem, vmem_limit_bytes=VMEM_LIMIT_BYTES)


def _rmsnorm(x, g):
    y = x * lax.rsqrt(jnp.mean(x * x, axis=-1, keepdims=True) + NORM_EPS)
    return y * g


def _sigmoid(x):
    return 1.0 / (1.0 + jnp.exp(-x))


def _ffn_kernel(x_ref, g_ref, wg_ref, wu_ref, wd_ref, gf_ref, o_ref, *, final_norm, tf):
    x = x_ref[...]
    xn = _rmsnorm(x, g_ref[...]).astype(BF16)
    acc = None
    for f0 in range(0, wg_ref.shape[1], tf):
        gate = jnp.dot(xn, wg_ref[:, f0:f0 + tf], preferred_element_type=F32)
        up = jnp.dot(xn, wu_ref[:, f0:f0 + tf], preferred_element_type=F32)
        act = (gate * _sigmoid(gate)) * up
        part = jnp.dot(act.astype(BF16), wd_ref[f0:f0 + tf, :], preferred_element_type=F32)
        acc = part if acc is None else acc + part
    y = x + 0.5 * acc
    if final_norm:
        y = _rmsnorm(y, gf_ref[...])
    o_ref[...] = y


def _ffn(x, g, wg, wu, wd, g_final, *, final_norm, tm, tf):
    m, d = x.shape
    resident = lambda a: pl.BlockSpec(a.shape, lambda i: (0, 0), pipeline_mode=pl.Buffered(1))
    return pl.pallas_call(
        functools.partial(_ffn_kernel, final_norm=final_norm, tf=tf),
        grid=(m // tm,),
        in_specs=[pl.BlockSpec((tm, d), lambda i: (i, 0)), resident(g), resident(wg), resident(wu),
                  resident(wd), resident(g_final)],
        out_specs=pl.BlockSpec((tm, d), lambda i: (i, 0)),
        out_shape=jax.ShapeDtypeStruct((m, d), F32),
        compiler_params=_cparams("parallel"),
        name="ffn",
    )(x, g, wg, wu, wd, g_final)


_TQ, _TK, _TV, _TQI, _TKI, _TWI, _TEND = 0, 512, 1024, 1536, 1792, 1856, 1872
_RK, _RKI, _RU, _REND = 0, 512, 640, 1152


def _proj_kernel(x_ref, g_ref, wt_ref, wr_ref, qt_ref, ktf_ref, vtf_ref, vt_ref, kh_ref,
                 qit_ref, kitf_ref, kib_ref, wit_ref, u_ref):
    h = _rmsnorm(x_ref[0], g_ref[...]).astype(BF16)
    pt = lax.dot_general(wt_ref[...], h, (((1,), (1,)), ((), ())), preferred_element_type=F32)
    pr = jnp.dot(h, wr_ref[...], preferred_element_type=F32)
    qt_ref[0] = (pt[_TQ:_TK] * (HEAD_DIM ** -0.5)).astype(BF16)
    ktf_ref[0] = pt[_TK:_TV]
    v = pt[_TV:_TQI]
    vtf_ref[0] = v
    vt_ref[0] = v.astype(BF16)
    qit_ref[0] = pt[_TQI:_TKI].astype(BF16)
    kitf_ref[0] = pt[_TKI:_TWI]
    wit_ref[0] = pt[_TWI:_TWI + 8] * ((IDX_HEADS * IDX_DIM) ** -0.5)
    k = pr[:, _RK:_RKI].astype(BF16)
    for hd in range(N_HEADS):
        kh_ref[0, hd] = k[:, hd * HEAD_DIM:(hd + 1) * HEAD_DIM]
    kib_ref[0] = pr[:, _RKI:_RKI + IDX_DIM].astype(BF16)
    u_ref[0] = pr[:, _RU:_REND]


def _proj(x, g, wt, wr, *, tm):
    b, t, d = x.shape
    feat = lambda n: pl.BlockSpec((1, n, tm), lambda bb, i: (bb, 0, i))
    tok = lambda n: pl.BlockSpec((1, tm, n), lambda bb, i: (bb, i, 0))
    outs = [
        (feat(D_ATTN), (b, D_ATTN, t), BF16), (feat(D_ATTN), (b, D_ATTN, t), F32),
        (feat(D_ATTN), (b, D_ATTN, t), F32), (feat(D_ATTN), (b, D_ATTN, t), BF16),
        (pl.BlockSpec((1, N_HEADS, tm, HEAD_DIM), lambda bb, i: (bb, 0, i, 0)), (b, N_HEADS, t, HEAD_DIM), BF16),
        (feat(IDX_HEADS * IDX_DIM), (b, IDX_HEADS * IDX_DIM, t), BF16), (feat(IDX_DIM), (b, IDX_DIM, t), F32),
        (tok(IDX_DIM), (b, t, IDX_DIM), BF16), (feat(8), (b, 8, t), F32), (tok(SSM_WIDTH), (b, t, SSM_WIDTH), F32),
    ]
    return pl.pallas_call(
        _proj_kernel,
        grid=(b, t // tm),
        in_specs=[tok(d), pl.BlockSpec((1, d), lambda bb, i: (0, 0)),
                  pl.BlockSpec((_TEND, d), lambda bb, i: (0, 0)), pl.BlockSpec((d, _REND), lambda bb, i: (0, 0))],
        out_specs=[spec for spec, _, _ in outs],
        out_shape=[jax.ShapeDtypeStruct(shape, dt) for _, shape, dt in outs],
        compiler_params=_cparams("parallel", "parallel"),
        name="proj",
    )(x, g, wt, wr)


def _scores_t(kc, qit, w):
    s = None
    for h in range(IDX_HEADS):
        d = jnp.dot(kc, qit[h], preferred_element_type=F32)
        term = jnp.maximum(d, 0.0) * w[h:h + 1, :]
        s = term if s is None else s + term
    return s


def _select_bias(s, vf, kidx, cut):
    tie = jnp.where(kidx < cut, 0.0, -jnp.inf)
    return jnp.where(s > vf, 0.0, jnp.where(s == vf, tie, -jnp.inf))


_FOLD_ROWS = 16


def _fold8(x, op):
    return op(x.reshape(x.shape[0] // _FOLD_ROWS, _FOLD_ROWS, x.shape[1]), axis=0)


_PROBES_PER_ROUND = 4
_SETTLE_FIRST, _SETTLE_EVERY, _BISECT_CAP = 4, 2, 80
_SETTLE_FEW_FROM, _SETTLE_FEW_LANES = 2, 16
_STUCK_FLAG = 1 << 16


def _topk_threshold(s_ref, n_chunks, tk, tq, mx, mn, n_hi, n_nc, k_top, thr_ref, cut_ref, fst_ref, ist_ref):
    row_iota = lax.broadcasted_iota(I32, (tk, tq), 0)
    zeros_i = jnp.zeros((_FOLD_ROWS, tq), I32)

    def chunk(c):
        k0 = pl.multiple_of(c * tk, tk)
        return s_ref[pl.ds(k0, tk), :], k0

    def count_gt(v):
        def body(c, acc):
            blk, _ = chunk(c)
            return acc + _fold8(jnp.where(blk > v, 1, 0), jnp.sum)
        real = jnp.sum(lax.fori_loop(0, n_chunks, body, zeros_i), axis=0, keepdims=True)
        return real + jnp.where(NEG_BIG > v, n_nc, 0)

    def count_gt_eq(v):
        def body(c, acc):
            blk, _ = chunk(c)
            return (acc[0] + _fold8(jnp.where(blk > v, 1, 0), jnp.sum),
                    acc[1] + _fold8(jnp.where(blk == v, 1, 0), jnp.sum))
        g, e = lax.fori_loop(0, n_chunks, body, (zeros_i, zeros_i))
        g = jnp.sum(g, axis=0, keepdims=True) + jnp.where(NEG_BIG > v, n_nc, 0)
        return g, jnp.sum(e, axis=0, keepdims=True)

    def count_eq(v):
        def body(c, acc):
            blk, _ = chunk(c)
            return acc + _fold8(jnp.where(blk == v, 1, 0), jnp.sum)
        return jnp.sum(lax.fori_loop(0, n_chunks, body, zeros_i), axis=0, keepdims=True)

    def max_le(v):
        def body(c, acc):
            blk, _ = chunk(c)
            return jnp.maximum(acc, _fold8(jnp.where(blk <= v, blk, -jnp.inf), jnp.max))
        m = jnp.max(lax.fori_loop(0, n_chunks, body, jnp.full((_FOLD_ROWS, tq), -jnp.inf, F32)),
                    axis=0, keepdims=True)
        return jnp.where((n_nc > 0) & (NEG_BIG <= v), jnp.maximum(m, NEG_BIG), m)

    def settle(m, counts=None):
        gt, eq_real = count_gt_eq(m) if counts is None else counts
        eq = eq_real + jnp.where(m == NEG_BIG, n_nc, 0)
        found = (ist_ref[0:1, :] == 0) & (gt + eq >= k_top)
        need = k_top - gt
        fst_ref[2:3, :] = jnp.where(found, m, fst_ref[2:3, :])
        ist_ref[1:2, :] = jnp.where(found, need, ist_ref[1:2, :])
        ist_ref[2:3, :] = jnp.where(found, jnp.where(eq_real > need, 1, 0), ist_ref[2:3, :])
        ist_ref[3:4, :] = jnp.where(found, INT_MAX, ist_ref[3:4, :])
        ist_ref[0:1, :] = jnp.where(found, 1, ist_ref[0:1, :])

    low = n_hi < k_top
    hi0 = jnp.where(low, NEG_BIG, mx)
    fst_ref[0:1, :] = jnp.where(low, jnp.minimum(mn, NEG_BIG), jnp.maximum(mn, NEG_BIG))
    fst_ref[1:2, :] = hi0
    fst_ref[2:3, :] = hi0
    ist_ref[...] = jnp.zeros_like(ist_ref)
    settle(hi0, (jnp.where(low, n_hi, 0), count_eq(hi0)))

    def probe(lo, hi, thr, done):
        mid = 0.5 * lo + 0.5 * hi
        c = count_gt(mid)
        live = done == 0
        hit = live & (c == k_top)
        stuck = jnp.where(live & ((mid <= lo) | (mid >= hi)), 2, 0)
        return (jnp.where(live & (c > k_top), mid, lo), jnp.where(live & (c < k_top), mid, hi),
                jnp.where(hit, mid, thr), jnp.where(hit, 1, done), stuck)

    def step(carry):
        it, _ = carry
        lo, hi, thr, done = fst_ref[0:1, :], fst_ref[1:2, :], fst_ref[2:3, :], ist_ref[0:1, :]
        stuck = jnp.zeros_like(done)
        for _ in range(_PROBES_PER_ROUND):
            lo, hi, thr, done, s = probe(lo, hi, thr, done)
            stuck = jnp.maximum(stuck, s)
        fst_ref[0:1, :] = lo
        fst_ref[1:2, :] = hi
        fst_ref[2:3, :] = thr
        ist_ref[0:1, :] = done
        tally = jnp.sum((1 - done) + jnp.where(stuck > 0, _STUCK_FLAG, 0))
        active = tally % _STUCK_FLAG
        due = (it >= _SETTLE_FIRST) & ((it - _SETTLE_FIRST) % _SETTLE_EVERY == 0)
        few = (it >= _SETTLE_FEW_FROM) & (active <= _SETTLE_FEW_LANES)
        check = (active > 0) & (due | few | (tally >= _STUCK_FLAG))

        @pl.when(check)
        def _():
            settle(max_le(hi))

        return it + 1, lax.cond(check, lambda: jnp.sum(1 - ist_ref[0:1, :]), lambda: active)

    lax.while_loop(lambda carry: (carry[1] > 0) & (carry[0] < _BISECT_CAP), step,
                   (jnp.int32(0), jnp.sum(1 - ist_ref[0:1, :])))
    thr_ref[0] = fst_ref[2:3, :]
    cut_ref[0] = ist_ref[3:4, :]

    @pl.when(jnp.max(ist_ref[2:3, :]) > 0)
    def _():
        thr = fst_ref[2:3, :]
        need = ist_ref[1:2, :].astype(F32)
        tri = jnp.where(lax.broadcasted_iota(I32, (tk, tk), 1) <= lax.broadcasted_iota(I32, (tk, tk), 0),
                        1.0, 0.0).astype(BF16)

        def body(c, carry):
            seen, cut = carry
            blk, k0 = chunk(c)
            tie = blk == thr
            rank = jnp.dot(tri, jnp.where(tie, 1.0, 0.0).astype(BF16), preferred_element_type=F32) + seen
            bound = jnp.where(tie, jnp.where(rank <= need, k0 + row_iota + 1, 0), 0)
            return rank[tk - 1:tk, :], jnp.maximum(cut, jnp.max(bound, axis=0, keepdims=True))

        _, cut = lax.fori_loop(0, n_chunks, body, (jnp.zeros((1, tq), F32), jnp.zeros((1, tq), I32)))
        cut_ref[0] = jnp.where(ist_ref[2:3, :] > 0, cut, ist_ref[3:4, :])


def _select_kernel(qit_ref, ki_ref, wit_ref, thr_ref, cut_ref, s_ref, fst_ref, ist_ref, *, tq, tk, n_keys, k_top):
    i = pl.program_id(1)
    q0 = i * tq
    n_chunks = (q0 + tq + tk - 1) // tk
    t_row = q0 + lax.broadcasted_iota(I32, (1, tq), 1)
    row_iota = lax.broadcasted_iota(I32, (tk, tq), 0)
    qit = qit_ref[0]
    w = wit_ref[0]

    def build(c, carry):
        mx, mn, n_hi = carry
        k0 = pl.multiple_of(c * tk, tk)
        s = _scores_t(ki_ref[0, pl.ds(k0, tk), :], qit, w)
        causal = (k0 + row_iota) <= t_row
        sm = jnp.where(causal, s, -jnp.inf)
        s_ref[pl.ds(k0, tk), :] = sm
        return (jnp.maximum(mx, _fold8(sm, jnp.max)),
                jnp.minimum(mn, _fold8(jnp.where(causal, s, jnp.inf), jnp.min)),
                n_hi + _fold8(jnp.where(sm > NEG_BIG, 1, 0), jnp.sum))

    mx, mn, n_hi = lax.fori_loop(
        0, n_chunks, build,
        (jnp.full((_FOLD_ROWS, tq), -jnp.inf, F32), jnp.full((_FOLD_ROWS, tq), jnp.inf, F32),
         jnp.zeros((_FOLD_ROWS, tq), I32)))
    n_nc = (n_keys - 1) - t_row
    _topk_threshold(s_ref, n_chunks, tk, tq, jnp.max(mx, axis=0, keepdims=True),
                    jnp.min(mn, axis=0, keepdims=True), jnp.sum(n_hi, axis=0, keepdims=True),
                    n_nc, k_top, thr_ref, cut_ref, fst_ref, ist_ref)


def _select(qit, ki, wit, *, tq, tk, k_top):
    b, _, _, t = qit.shape
    kern = functools.partial(_select_kernel, tq=tq, tk=tk, n_keys=t, k_top=k_top)
    return pl.pallas_call(
        kern,
        grid=(b, t // tq),
        in_specs=[
            pl.BlockSpec((1, IDX_HEADS, IDX_DIM, tq), lambda bb, i: (bb, 0, 0, i)),
            pl.BlockSpec((1, t, IDX_DIM), lambda bb, i: (bb, 0, 0)),
            pl.BlockSpec((1, 8, tq), lambda bb, i: (bb, 0, i)),
        ],
        out_specs=[pl.BlockSpec((1, 1, tq), lambda bb, i: (bb, 0, i)),
                   pl.BlockSpec((1, 1, tq), lambda bb, i: (bb, 0, i))],
        out_shape=[jax.ShapeDtypeStruct((b, 1, t), F32), jax.ShapeDtypeStruct((b, 1, t), I32)],
        scratch_shapes=[pltpu.VMEM((t, tq), F32), pltpu.VMEM((8, tq), F32), pltpu.VMEM((8, tq), I32)],
        compiler_params=_cparams("parallel", "parallel"),
        name="select",
    )(qit, ki, wit)


_ATTN_SUB_ROWS = 128


def _n_key_tiles(i, tq, tk):
    return (i * tq + tq + tk - 1) // tk


def _attn_kernel(it_ref, jt_ref, qt_ref, k_ref, vt_ref, qit_ref, ki_ref, wit_ref, vf_ref, cut_ref, o_ref,
                 m_ref, l_ref, acc_ref, bias_ref, lga_ref, lgb_ref, pa_ref, pb_ref, *, tq, tk):
    i = it_ref[pl.program_id(1)]
    j = jt_ref[pl.program_id(1)]
    nkt = _n_key_tiles(i, tq, tk)
    sub = _ATTN_SUB_ROWS
    chunks = [slice(c * sub, (c + 1) * sub) for c in range(tk // sub)]

    @pl.when(j == 0)
    def _():
        m_ref[...] = jnp.full_like(m_ref, -jnp.inf)
        l_ref[...] = jnp.zeros_like(l_ref)
        acc_ref[...] = jnp.zeros_like(acc_ref)

    @pl.when(j < nkt)
    def _():
        qit = qit_ref[0]
        w = wit_ref[0]
        vf = vf_ref[0]
        cut = cut_ref[0]
        t_row = i * tq + lax.broadcasted_iota(I32, (1, tq), 1)
        row = lax.broadcasted_iota(I32, (sub, tq), 0)
        for c, rows in enumerate(chunks):
            kidx = j * tk + c * sub + row
            bias = _select_bias(_scores_t(ki_ref[0, rows, :], qit, w), vf, kidx, cut)
            bias_ref[rows, :] = jnp.where(kidx <= t_row, bias, -jnp.inf)

        def logits(h, lg_ref):
            qt = qt_ref[0, h]
            part = jnp.full((8, tq), -jnp.inf, F32)
            for rows in chunks:
                lg = jnp.dot(k_ref[0, h, rows, :], qt, preferred_element_type=F32) + bias_ref[rows, :]
                lg_ref[rows, :] = lg
                part = jnp.maximum(part, jnp.max(lg.reshape(sub // 8, 8, tq), axis=0))
            return jnp.max(part, axis=0, keepdims=True)

        def absorb(h, lg_ref, p_ref, tile_max):
            hrow = pl.ds(h, 1)
            m_old = m_ref[hrow, :]
            m_new = jnp.maximum(m_old, tile_max)
            m_safe = jnp.where(m_new == -jnp.inf, 0.0, m_new)
            alpha = jnp.exp(m_old - m_safe)
            psum = jnp.zeros((8, tq), F32)
            for rows in chunks:
                p = jnp.exp(lg_ref[rows, :] - m_safe)
                psum = psum + jnp.sum(p.reshape(sub // 8, 8, tq), axis=0)
                p_ref[rows, :] = p.astype(BF16)
            l_ref[hrow, :] = alpha * l_ref[hrow, :] + jnp.sum(psum, axis=0, keepdims=True)
            acc_ref[h] = alpha * acc_ref[h] + jnp.dot(vt_ref[0, h], p_ref[...], preferred_element_type=F32)
            m_ref[hrow, :] = m_new

        def two_heads(t, max_a):
            h = 2 * t
            max_b = logits(h + 1, lgb_ref)
            absorb(h, lga_ref, pa_ref, max_a)
            max_a = logits(h + 2, lga_ref)
            absorb(h + 1, lgb_ref, pb_ref, max_b)
            return max_a

        max_a = lax.fori_loop(0, N_HEADS // 2 - 1, two_heads, logits(0, lga_ref))
        max_b = logits(N_HEADS - 1, lgb_ref)
        absorb(N_HEADS - 2, lga_ref, pa_ref, max_a)
        absorb(N_HEADS - 1, lgb_ref, pb_ref, max_b)

    @pl.when(j == nkt - 1)
    def _():
        for h in range(0, N_HEADS, 2):
            pair = jnp.concatenate([acc_ref[h] / l_ref[h:h + 1, :], acc_ref[h + 1] / l_ref[h + 1:h + 2, :]], axis=0)
            o_ref[0, :, h * HEAD_DIM:(h + 2) * HEAD_DIM] = jnp.transpose(pair).astype(o_ref.dtype)


def _attn(qt, k, vt, qit, ki, wit, vf, cut, *, tq, tk):
    b, _, _, t = qt.shape
    pairs = [(i, j) for i in range(t // tq) for j in range(_n_key_tiles(i, tq, tk))]
    i_tab = jnp.asarray([p[0] for p in pairs], I32)
    j_tab = jnp.asarray([p[1] for p in pairs], I32)
    q_tile = lambda *lead: (lambda bb, s, it, jt: (bb,) + lead + (it[s],))
    return pl.pallas_call(
        functools.partial(_attn_kernel, tq=tq, tk=tk),
        grid_spec=pltpu.PrefetchScalarGridSpec(
            num_scalar_prefetch=2,
            grid=(b, len(pairs)),
            in_specs=[
                pl.BlockSpec((1, N_HEADS, HEAD_DIM, tq), q_tile(0, 0)),
                pl.BlockSpec((1, N_HEADS, tk, HEAD_DIM), lambda bb, s, it, jt: (bb, 0, jt[s], 0)),
                pl.BlockSpec((1, N_HEADS, HEAD_DIM, tk), lambda bb, s, it, jt: (bb, 0, 0, jt[s])),
                pl.BlockSpec((1, IDX_HEADS, IDX_DIM, tq), q_tile(0, 0)),
                pl.BlockSpec((1, tk, IDX_DIM), lambda bb, s, it, jt: (bb, jt[s], 0)),
                pl.BlockSpec((1, 8, tq), q_tile(0)),
                pl.BlockSpec((1, 1, tq), q_tile(0)),
                pl.BlockSpec((1, 1, tq), q_tile(0)),
            ],
            out_specs=pl.BlockSpec((1, tq, D_ATTN), lambda bb, s, it, jt: (bb, it[s], 0)),
            scratch_shapes=[pltpu.VMEM((N_HEADS, tq), F32), pltpu.VMEM((N_HEADS, tq), F32),
                            pltpu.VMEM((N_HEADS, HEAD_DIM, tq), F32),
                            pltpu.VMEM((tk, tq), F32), pltpu.VMEM((tk, tq), F32), pltpu.VMEM((tk, tq), F32),
                            pltpu.VMEM((tk, tq), BF16), pltpu.VMEM((tk, tq), BF16)],
        ),
        out_shape=jax.ShapeDtypeStruct((b, t, D_ATTN), BF16),
        compiler_params=_cparams("parallel", "arbitrary"),
        name="attn",
    )(i_tab, j_tab, qt, k, vt, qit, ki, wit, vf, cut)


def _zoh(ar, ai, ldt):
    dt = jnp.exp(ldt)
    mag = jnp.exp(dt * ar)
    abr = mag * jnp.cos(dt * ai)
    abi = mag * jnp.sin(dt * ai)
    den = ar * ar + ai * ai
    nr = abr - 1.0
    ni = abi
    return abr, abi, (nr * ar + ni * ai) / den, (ni * ar - nr * ai) / den


def _disc_kernel(ar_ref, ai_ref, ldt_ref, ar16_ref, ai16_ref, ldt16_ref, br_ref, bi_ref,
                 pwr_ref, pwi_ref, bbr_ref, bbi_ref):
    abr, abi, _, _ = _zoh(ar_ref[...], ai_ref[...], ldt_ref[...])
    pr, pi = abr, abi
    for j in range(8):
        pwr_ref[j:j + 1, :] = pr
        pwi_ref[j:j + 1, :] = pi
        pr, pi = pr * abr - pi * abi, pr * abi + pi * abr
    _, _, fr, fi = _zoh(ar16_ref[...], ai16_ref[...], ldt16_ref[...])
    br = br_ref[...]
    bi = bi_ref[...]
    bbr_ref[...] = fr * br - fi * bi
    bbi_ref[...] = fr * bi + fi * br


def _discretize(a_re, a_im, log_dt, b_re, b_im):
    flat = lambda a: a.reshape(1, -1)
    ldt = jnp.broadcast_to(log_dt[:, None], (N_GROUPS, STATE_DIM))
    rep = lambda a: flat(jnp.broadcast_to(a[:, :, None], (N_GROUPS, STATE_DIM, GROUP_CH)))
    n16 = N_STATE * GROUP_CH
    pwr, pwi, bbr, bbi = pl.pallas_call(
        _disc_kernel,
        out_shape=[jax.ShapeDtypeStruct((8, N_STATE), F32), jax.ShapeDtypeStruct((8, N_STATE), F32),
                   jax.ShapeDtypeStruct((1, n16), F32), jax.ShapeDtypeStruct((1, n16), F32)],
        name="s5_discretize",
    )(flat(a_re), flat(a_im), flat(ldt), rep(a_re), rep(a_im), rep(ldt), flat(b_re), flat(b_im))
    return pwr, pwi, bbr.reshape(N_GROUPS, STATE_DIM, GROUP_CH), bbi.reshape(N_GROUPS, STATE_DIM, GROUP_CH)


def _block_diag_in(bb):
    eye = jnp.eye(N_GROUPS, dtype=bb.dtype)
    return jnp.einsum('gpc,gh->gchp', bb, eye).reshape(SSM_WIDTH, N_STATE)


def _block_diag_out(c):
    eye = jnp.eye(N_GROUPS, dtype=c.dtype)
    return jnp.einsum('gcp,gh->gphc', c, eye).reshape(N_STATE, SSM_WIDTH)


_LANE_CHUNK = 512


def _ssm_kernel(u_ref, bblk_ref, cblk_ref, pw_ref, step_ref, d_ref, y_ref, xf_ref, x_ref, carry_ref, *, tt):
    ts = pl.program_id(1)

    @pl.when(ts == 0)
    def _():
        carry_ref[...] = jnp.zeros_like(carry_ref)

    u = u_ref[0]
    ub = u.astype(BF16)
    n_chunks = 2 * N_STATE // _LANE_CHUNK
    ch = _LANE_CHUNK // STATE_DIM * GROUP_CH
    chans = [slice((c % (N_STATE // _LANE_CHUNK)) * ch, (c % (N_STATE // _LANE_CHUNK) + 1) * ch)
             for c in range(n_chunks)]
    for c in range(n_chunks):
        cols = slice(c * _LANE_CHUNK, (c + 1) * _LANE_CHUNK)
        x_ref[:, cols] = jnp.dot(ub[:, chans[c]], bblk_ref[chans[c], cols], preferred_element_type=F32)

    def group(r, carry):
        r0 = pl.multiple_of(r * 8, 8)
        for c in range(N_STATE // _LANE_CHUNK):
            re = pl.ds(c * _LANE_CHUNK, _LANE_CHUNK)
            im = pl.ds(N_STATE + c * _LANE_CHUNK, _LANE_CHUNK)
            xr = x_ref[pl.ds(r0, 8), re]
            xi = x_ref[pl.ds(r0, 8), im]
            for k, d in enumerate((1, 2, 4)):
                ar = step_ref[k, :, re]
                ai = step_ref[k, :, im]
                sr = pltpu.roll(xr, d, 0)
                si = pltpu.roll(xi, d, 0)
                xr, xi = xr + (ar * sr - ai * si), xi + (ar * si + ai * sr)
            cr = carry_ref[:, re]
            ci = carry_ref[:, im]
            pr = pw_ref[:, re]
            pi = pw_ref[:, im]
            xr, xi = xr + (pr * cr - pi * ci), xi + (pr * ci + pi * cr)
            x_ref[pl.ds(r0, 8), re] = xr
            x_ref[pl.ds(r0, 8), im] = xi
            carry_ref[:, re] = xr[7:8, :]
            carry_ref[:, im] = xi[7:8, :]
        return carry

    lax.fori_loop(0, tt // 8, group, 0)
    y_ref[0] = d_ref[...] * u
    for c in range(n_chunks):
        cols = slice(c * _LANE_CHUNK, (c + 1) * _LANE_CHUNK)
        y_ref[0, :, chans[c]] += jnp.dot(x_ref[:, cols].astype(BF16), cblk_ref[cols, chans[c]],
                                         preferred_element_type=F32)
    xf_ref[0] = carry_ref[...]


def _ssm(u, bblk, cblk, pw, d_skip, *, tt):
    b, t, _ = u.shape
    rows = jnp.arange(8)[None, :, None]
    steps = jnp.stack([jnp.where(rows >= dd, pw[dd - 1][None, None, :], 0.0)[0] for dd in (1, 2, 4)])
    return pl.pallas_call(
        functools.partial(_ssm_kernel, tt=tt),
        grid=(b, t // tt),
        in_specs=[
            pl.BlockSpec((1, tt, SSM_WIDTH), lambda bb, s: (bb, s, 0)),
            pl.BlockSpec((SSM_WIDTH, 2 * N_STATE), lambda bb, s: (0, 0)),
            pl.BlockSpec((2 * N_STATE, SSM_WIDTH), lambda bb, s: (0, 0)),
            pl.BlockSpec((8, 2 * N_STATE), lambda bb, s: (0, 0)),
            pl.BlockSpec((3, 8, 2 * N_STATE), lambda bb, s: (0, 0, 0)),
            pl.BlockSpec((1, SSM_WIDTH), lambda bb, s: (0, 0)),
        ],
        out_specs=[pl.BlockSpec((1, tt, SSM_WIDTH), lambda bb, s: (bb, s, 0)),
                   pl.BlockSpec((1, 1, 2 * N_STATE), lambda bb, s: (bb, 0, 0))],
        out_shape=[jax.ShapeDtypeStruct((b, t, SSM_WIDTH), F32),
                   jax.ShapeDtypeStruct((b, 1, 2 * N_STATE), F32)],
        scratch_shapes=[pltpu.VMEM((tt, 2 * N_STATE), F32), pltpu.VMEM((1, 2 * N_STATE), F32)],
        compiler_params=_cparams("parallel", "arbitrary"),
        name="s5_scan",
    )(u, bblk, cblk, pw, steps, d_skip)


def _ssm_step_kernel(u_ref, x0_ref, bblk_ref, cblk_ref, pw_ref, d_ref, y_ref, x_ref):
    u = u_ref[...]
    bu = jnp.dot(u, bblk_ref[...], preferred_element_type=F32, precision=lax.Precision.HIGHEST)
    ar = pw_ref[0:1, :N_STATE]
    ai = pw_ref[0:1, N_STATE:]
    x0r = x0_ref[:, :N_STATE]
    x0i = x0_ref[:, N_STATE:]
    xr = ar * x0r - ai * x0i + bu[:, :N_STATE]
    xi = ar * x0i + ai * x0r + bu[:, N_STATE:]
    x_ref[:, :N_STATE] = xr
    x_ref[:, N_STATE:] = xi
    y = jnp.dot(x_ref[...], cblk_ref[...], preferred_element_type=F32, precision=lax.Precision.HIGHEST)
    y_ref[...] = y + d_ref[...] * u


def _ssm_step(u, x0, bblk, cblk, pw, d_skip):
    n = u.shape[0]
    return pl.pallas_call(
        _ssm_step_kernel,
        out_shape=[jax.ShapeDtypeStruct((n, SSM_WIDTH), F32), jax.ShapeDtypeStruct((n, 2 * N_STATE), F32)],
        compiler_params=pltpu.CompilerParams(vmem_limit_bytes=VMEM_LIMIT_BYTES),
        name="s5_step",
    )(u, x0, bblk, cblk, pw, d_skip)


def _gelu_tanh(x):
    c = np.float32(np.sqrt(2.0 / np.pi))
    return 0.5 * x * (1.0 + jnp.tanh(c * (x + 0.044715 * (x * x * x))))


def _merge_kernel(x_ref, oa_ref, ys_ref, g_ref, wgt_ref, wglu_ref, bglu_ref, wpa_ref, wpb_ref, wout_ref, o_ref):
    x = x_ref[...]
    h = _rmsnorm(x, g_ref[...]).astype(BF16)
    gates = _sigmoid(jnp.dot(h, wgt_ref[...], preferred_element_type=F32))
    ys = _gelu_tanh(ys_ref[...])
    glu = jnp.dot(ys.astype(BF16), wglu_ref[...], preferred_element_type=F32) + bglu_ref[...]
    ob = ys * _sigmoid(glu)
    pa = jnp.dot(oa_ref[...], wpa_ref[...], preferred_element_type=F32)
    pb = jnp.dot(ob.astype(BF16), wpb_ref[...], preferred_element_type=F32)
    merged = gates[:, :D_MODEL] * pa + gates[:, D_MODEL:] * pb
    o_ref[...] = x + jnp.dot(merged.astype(BF16), wout_ref[...], preferred_element_type=F32)


def _merge(x, oa, ys, g, wgt, wglu, bglu, wpa, wpb, wout, *, tm):
    m, d = x.shape
    row = lambda n: pl.BlockSpec((tm, n), lambda i: (i, 0))
    full = lambda a: pl.BlockSpec(a.shape, lambda i: (0, 0), pipeline_mode=pl.Buffered(1))
    return pl.pallas_call(
        _merge_kernel,
        grid=(m // tm,),
        in_specs=[row(d), row(D_ATTN), row(SSM_WIDTH), full(g), full(wgt), full(wglu), full(bglu),
                  full(wpa), full(wpb), full(wout)],
        out_specs=row(d),
        out_shape=jax.ShapeDtypeStruct((m, d), F32),
        compiler_params=_cparams("parallel"),
        name="merge",
    )(x, oa, ys, g, wgt, wglu, bglu, wpa, wpb, wout)


_PAGES_PER_STEP = 8
_SCORE_PAGES_PER_STEP = 32


def _page_specs(block, n_pages, per_step):
    last = n_pages // per_step - 1

    def spec(r):
        def index_map(b, p, pt):
            return (pt[b, jnp.minimum(p, last) * per_step + r],) + (0,) * (len(block) - 1)
        return pl.BlockSpec(block, index_map)
    return [spec(r) for r in range(per_step)]


def _per_seq(shape):
    return pl.BlockSpec((1,) + shape, lambda b, p, pt: (b,) + (0,) * len(shape))


def _sample_scores_kernel(pt_ref, qi_ref, w_ref, knew_ref, *refs, page_size, n_steps):
    pages = refs[:-1]
    o_ref = refs[-1]
    p = pl.program_id(1)
    qi = qi_ref[0]
    w = w_ref[0]

    def score(kt):
        d = jnp.dot(qi, kt, preferred_element_type=F32)
        return jnp.sum(jnp.maximum(d, 0.0) * w, axis=0, keepdims=True)

    @pl.when(p < n_steps)
    def _():
        for r, page in enumerate(pages):
            o_ref[0, :, r * page_size:(r + 1) * page_size] = score(page[0].astype(BF16))

    @pl.when(p == n_steps)
    def _():
        o_ref[0] = jnp.full(o_ref.shape[1:], -jnp.inf, F32)
        lane = lax.broadcasted_iota(I32, (1, page_size), 1)
        o_ref[0, :, 0:page_size] = jnp.where(lane == 0, score(knew_ref[0]), -jnp.inf)


def _sample_scores(page_table, qi_rows, w_col, ki_new_t, kidx_t):
    db, n_pages = page_table.shape
    _, _, page_size = kidx_t.shape
    per_step = _pick(n_pages, (_SCORE_PAGES_PER_STEP, _PAGES_PER_STEP))
    assert n_pages % per_step == 0 and per_step % _PAGES_PER_STEP == 0
    steps = n_pages // per_step
    block = per_step * page_size
    return pl.pallas_call(
        functools.partial(_sample_scores_kernel, page_size=page_size, n_steps=steps),
        grid_spec=pltpu.PrefetchScalarGridSpec(
            num_scalar_prefetch=1,
            grid=(db, steps + 1),
            in_specs=[_per_seq((8, IDX_DIM)), _per_seq((8, 1)), _per_seq((IDX_DIM, page_size))]
                     + _page_specs((1, IDX_DIM, page_size), n_pages, per_step),
            out_specs=pl.BlockSpec((1, 1, block), lambda b, p, pt: (b, 0, p)),
        ),
        out_shape=jax.ShapeDtypeStruct((db, 1, (steps + 1) * block), F32),
        compiler_params=_cparams("parallel", "arbitrary"),
        name="sample_scores",
    )(page_table, qi_rows, w_col, ki_new_t, *([kidx_t] * per_step))


def _sample_select_kernel(st_ref, thr_ref, cut_ref, s_ref, fst_ref, ist_ref, *, tk, tq, n_keys, k_top):
    n_chunks = st_ref.shape[0] // tk
    row_iota = lax.broadcasted_iota(I32, (tk, tq), 0)

    def build(c, carry):
        mx, mn, n_hi = carry
        k0 = pl.multiple_of(c * tk, tk)
        s = st_ref[pl.ds(k0, tk), :]
        real = (k0 + row_iota) < n_keys
        sm = jnp.where(real, s, -jnp.inf)
        s_ref[pl.ds(k0, tk), :] = sm
        return (jnp.maximum(mx, _fold8(sm, jnp.max)),
                jnp.minimum(mn, _fold8(jnp.where(real, s, jnp.inf), jnp.min)),
                n_hi + _fold8(jnp.where(sm > NEG_BIG, 1, 0), jnp.sum))

    mx, mn, n_hi = lax.fori_loop(
        0, n_chunks, build,
        (jnp.full((_FOLD_ROWS, tq), -jnp.inf, F32), jnp.full((_FOLD_ROWS, tq), jnp.inf, F32),
         jnp.zeros((_FOLD_ROWS, tq), I32)))
    _topk_threshold(s_ref, n_chunks, tk, tq, jnp.max(mx, axis=0, keepdims=True),
                    jnp.min(mn, axis=0, keepdims=True), jnp.sum(n_hi, axis=0, keepdims=True),
                    jnp.zeros((1, tq), I32), k_top, thr_ref, cut_ref, fst_ref, ist_ref)


def _sample_select(scores_t, *, n_keys, k_top, tk):
    n_rows, tq = scores_t.shape
    kern = functools.partial(_sample_select_kernel, tk=tk, tq=tq, n_keys=n_keys, k_top=k_top)
    return pl.pallas_call(
        kern,
        out_shape=[jax.ShapeDtypeStruct((1, 1, tq), F32), jax.ShapeDtypeStruct((1, 1, tq), I32)],
        scratch_shapes=[pltpu.VMEM((n_rows, tq), F32), pltpu.VMEM((8, tq), F32), pltpu.VMEM((8, tq), I32)],
        compiler_params=pltpu.CompilerParams(vmem_limit_bytes=VMEM_LIMIT_BYTES),
        name="sample_select",
    )(scores_t)


def _sample_attn_kernel(pt_ref, qb_ref, s_ref, thr_ref, cut_ref, kself_ref, vself_ref, *refs,
                        page_size, n_steps):
    kp = refs[:_PAGES_PER_STEP]
    vp = refs[_PAGES_PER_STEP:2 * _PAGES_PER_STEP]
    o_ref, m_ref, l_ref, acc_ref = refs[2 * _PAGES_PER_STEP:]
    step = pl.program_id(1)
    thr = thr_ref[0]
    cut = cut_ref[0]
    lane = lax.broadcasted_iota(I32, (1, page_size), 1)
    is_self = step == n_steps

    @pl.when(step == 0)
    def _():
        m_ref[...] = jnp.full_like(m_ref, -jnp.inf)
        l_ref[...] = jnp.zeros_like(l_ref)
        acc_ref[...] = jnp.zeros_like(acc_ref)

    logits = []
    for r in range(_PAGES_PER_STEP):
        kidx = (step * _PAGES_PER_STEP + r) * page_size + lane
        bias = _select_bias(s_ref[0, :, r * page_size:(r + 1) * page_size], thr, kidx, cut)
        rows = []
        for h in range(N_HEADS):
            kt = kp[r][0, h]
            if r == 0:
                kt = jnp.where(is_self, kself_ref[0, h], kt)
            rows.append(jnp.sum(kt * qb_ref[0, h], axis=0, keepdims=True))
        logits.append(jnp.concatenate(rows, axis=0) + bias)

    m_old = m_ref[...]
    m_new = m_old
    for lg in logits:
        m_new = jnp.maximum(m_new, jnp.max(lg, axis=1, keepdims=True))
    m_safe = jnp.where(m_new == -jnp.inf, 0.0, m_new)
    alpha = jnp.exp(m_old - m_safe)
    probs = [jnp.exp(lg - m_safe) for lg in logits]
    l_new = alpha * l_ref[...]
    for p in probs:
        l_new = l_new + jnp.sum(p, axis=1, keepdims=True)
    l_ref[...] = l_new
    m_ref[...] = m_new
    for h in range(N_HEADS):
        acc = acc_ref[h] * alpha[h:h + 1, :]
        for r in range(_PAGES_PER_STEP):
            vt = vp[r][0, h]
            if r == 0:
                vt = jnp.where(is_self, vself_ref[0, h], vt)
            acc = acc + vt * probs[r][h:h + 1, :]
        acc_ref[h] = acc

    @pl.when(is_self)
    def _():
        for h in range(N_HEADS):
            o_ref[0, h] = jnp.sum(acc_ref[h], axis=1, keepdims=True) / l_ref[h:h + 1, :]


def _sample_attn(page_table, qb, scores, thr, cut, k_self, v_self, ck_t, cv_t):
    db, n_pages = page_table.shape
    _, nh, hd, page_size = ck_t.shape
    steps = n_pages // _PAGES_PER_STEP
    block = _PAGES_PER_STEP * page_size
    page = (1, nh, hd, page_size)
    return pl.pallas_call(
        functools.partial(_sample_attn_kernel, page_size=page_size, n_steps=steps),
        grid_spec=pltpu.PrefetchScalarGridSpec(
            num_scalar_prefetch=1,
            grid=(db, steps + 1),
            in_specs=[_per_seq(page[1:]),
                      pl.BlockSpec((1, 1, block), lambda b, p, pt: (b, 0, p)),
                      _per_seq((1, 1)), _per_seq((1, 1)), _per_seq(page[1:]), _per_seq(page[1:])]
                     + _page_specs(page, n_pages, _PAGES_PER_STEP) + _page_specs(page, n_pages, _PAGES_PER_STEP),
            out_specs=_per_seq((nh, hd, 1)),
            scratch_shapes=[pltpu.VMEM((nh, 1), F32), pltpu.VMEM((nh, 1), F32),
                            pltpu.VMEM((nh, hd, page_size), F32)],
        ),
        out_shape=jax.ShapeDtypeStruct((db, nh, hd, 1), F32),
        compiler_params=_cparams("parallel", "arbitrary"),
        name="sample_attn",
    )(page_table, qb, scores, thr, cut, k_self, v_self,
      *([ck_t] * _PAGES_PER_STEP), *([cv_t] * _PAGES_PER_STEP))


def _pick(n, pref):
    for t in pref:
        if n % t == 0:
            return t
    return n


def _pack_w_in(w_in):
    o = np.cumsum([0, D_ATTN, D_ATTN, D_ATTN, IDX_HEADS * IDX_DIM, IDX_DIM, IDX_HEADS, SSM_WIDTH, 2 * D_MODEL])
    z = lambda n: jnp.zeros((D_MODEL, n), w_in.dtype)
    w_feat = jnp.concatenate([w_in[:, o[0]:o[6]], z(_TEND - _TWI - IDX_HEADS)], axis=1)
    w_tok = jnp.concatenate([w_in[:, o[1]:o[2]], w_in[:, o[4]:o[5]], z(_RU - _RKI - IDX_DIM),
                             w_in[:, o[6]:o[7]]], axis=1)
    return w_feat.T.astype(BF16), w_tok.astype(BF16), w_in[:, o[7]:o[8]].astype(BF16)


def _prompt_layer(x, lw, ssm, final_norm):
    bsz, seq, d = x.shape
    m = bsz * seq
    tm = _pick(m, (512, 256, 128, 64, 32, 16, 8))
    x1 = _ffn(x.reshape(m, d), *lw["ffn1"], lw["g_final"], final_norm=False, tm=tm, tf=lw["tf"])
    qt, kt_f, vt_f, vt, kh, qit, kit_f, kib, wit, u_f = _proj(
        x1.reshape(bsz, seq, d), lw["g_mix"], lw["w_feat"], lw["w_tok"], tm=_pick(seq, (512, 256, 128)))
    heads = lambda a, nh: a.reshape(bsz, nh, -1, seq)
    qt, vt, qit = heads(qt, N_HEADS), heads(vt, N_HEADS), heads(qit, IDX_HEADS)

    k_top = min(TOPK_MAX, seq // 4)
    vf, cut = _select(qit, kib, wit, tq=_pick(seq, (512, 256, 128)), tk=_pick(seq, (256, 128)), k_top=k_top)
    oa = _attn(qt, kh, vt, qit, kib, wit, vf, cut,
               tq=_pick(seq, (512, 256, 128)), tk=_pick(seq, (512, 256, 128))).reshape(m, D_ATTN)

    tt = _pick(seq, (512, 256, 128, 64, 32, 16, 8))
    ys, xfin = _ssm(u_f, ssm["bblk"].astype(BF16), ssm["cblk"].astype(BF16), ssm["pw"], ssm["d_skip"], tt=tt)
    x2 = _merge(x1, oa, ys.reshape(m, SSM_WIDTH), *lw["merge"], tm=tm)
    y = _ffn(x2, *lw["ffn2"], lw["g_final"], final_norm=final_norm, tm=tm, tf=lw["tf"])
    new = (heads(kt_f, N_HEADS).transpose(0, 3, 1, 2), heads(vt_f, N_HEADS).transpose(0, 3, 1, 2),
           kit_f.transpose(0, 2, 1),
           xfin[:, 0, :N_STATE].reshape(bsz, N_GROUPS, STATE_DIM),
           xfin[:, 0, N_STATE:].reshape(bsz, N_GROUPS, STATE_DIM))
    return y.reshape(bsz, seq, d), new


def _sample_layer(x, lw, ssm, final_norm, cache_k, cache_v, cache_kidx, st_re, st_im, page_table):
    db, ds, d = x.shape
    assert ds == 1, "one new token per sample sequence"
    n_pool, page_size = cache_k.shape[0], cache_k.shape[1]
    past_len = page_table.shape[1] * page_size
    s1 = _ffn(x.reshape(db, d), *lw["ffn1"], lw["g_final"], final_norm=False, tm=db, tf=lw["tf"])
    qt, kt_f, vt_f, _, _, qit, kit_f, kib, wit, u_f = _proj(
        s1.reshape(1, db, d), lw["g_mix"], lw["w_feat"], lw["w_tok"], tm=db)
    q_b, k_f, v_f, qi_b, ki_f, wi_f = (a[0].T for a in (qt, kt_f, vt_f, qit, kit_f, wit))
    ki_b, u_f = kib[0], u_f[0]

    kidx_t = cache_kidx.transpose(0, 2, 1)
    ck_t = cache_k.transpose(0, 2, 3, 1)
    cv_t = cache_v.transpose(0, 2, 3, 1)
    lane0 = lambda a: jnp.zeros(a.shape + (page_size,), a.dtype).at[..., 0].set(a)

    qi_rows = jnp.zeros((db, 8, IDX_DIM), BF16).at[:, :IDX_HEADS].set(qi_b.reshape(db, IDX_HEADS, IDX_DIM))
    scores = _sample_scores(page_table, qi_rows, wi_f[:, :, None], lane0(ki_b), kidx_t)

    n_keys = past_len + 1
    k_top = min(TOPK_MAX, n_keys // 4)
    n = scores.shape[2]
    scores_t = jnp.zeros((n, -(-db // 128) * 128), F32).at[:, :db].set(scores[:, 0, :].T)
    thr, cut = _sample_select(scores_t, n_keys=n_keys, k_top=k_top, tk=256)

    qb = jnp.broadcast_to(q_b.astype(F32).reshape(db, N_HEADS, HEAD_DIM, 1), (db, N_HEADS, HEAD_DIM, page_size))
    o_s = _sample_attn(page_table, qb, scores, thr[0, 0, :db].reshape(db, 1, 1), cut[0, 0, :db].reshape(db, 1, 1),
                       lane0(k_f.reshape(db, N_HEADS, HEAD_DIM)), lane0(v_f.reshape(db, N_HEADS, HEAD_DIM)),
                       ck_t, cv_t)
    oa = o_s.reshape(db, D_ATTN).astype(BF16)

    x0 = jnp.concatenate([st_re.reshape(db, N_STATE), st_im.reshape(db, N_STATE)], axis=1)
    ys, x_new = _ssm_step(u_f, x0, ssm["bblk"], ssm["cblk"], ssm["pw"], ssm["d_skip"])
    s2 = _merge(s1, oa, ys, *lw["merge"], tm=db)
    y = _ffn(s2, *lw["ffn2"], lw["g_final"], final_norm=final_norm, tm=db, tf=lw["tf"])
    new = (k_f.reshape(db, ds, N_HEADS, HEAD_DIM), v_f.reshape(db, ds, N_HEADS, HEAD_DIM),
           ki_f.reshape(db, ds, IDX_DIM),
           x_new[:, :N_STATE].reshape(db, N_GROUPS, STATE_DIM),
           x_new[:, N_STATE:].reshape(db, N_GROUPS, STATE_DIM))
    return y.reshape(db, ds, d), new


def kernel(x_prompt, x_sample, cache_k, cache_v, cache_kidx, state_ssm_re, state_ssm_im, page_table,
           g_ffn1, w1_gate, w1_up, w1_down, g_mix, w_in, a_re, a_im, log_dt, b_re, b_im, c_re, c_im,
           d_skip, w_glu, b_glu, w_pa, w_pb, w_out, g_ffn2, w2_gate, w2_up, w2_down, g_final):
    depth = w_in.shape[0]
    row = lambda a: a.reshape(1, -1)
    bf = lambda a: a.astype(BF16)
    xp, xs = x_prompt, x_sample
    new_p, new_s = [], []
    for l in range(depth):
        w_feat, w_tok, w_gates = _pack_w_in(w_in[l])
        lw = dict(
            ffn1=(row(g_ffn1[l]), bf(w1_gate[l]), bf(w1_up[l]), bf(w1_down[l])),
            ffn2=(row(g_ffn2[l]), bf(w2_gate[l]), bf(w2_up[l]), bf(w2_down[l])),
            merge=(row(g_mix[l]), w_gates, bf(w_glu[l]), row(b_glu[l]), bf(w_pa[l]), bf(w_pb[l]), bf(w_out[l])),
            g_mix=row(g_mix[l]), w_feat=w_feat, w_tok=w_tok, g_final=row(g_final),
            tf=_pick(w1_gate.shape[2], (1408, 1024, 512, 256, 128)),
        )
        pwr, pwi, bbr, bbi = _discretize(a_re[l], a_im[l], log_dt[l], b_re[l], b_im[l])
        ssm = dict(
            pw=jnp.concatenate([pwr, pwi], axis=1),
            bblk=jnp.concatenate([_block_diag_in(bbr), _block_diag_in(bbi)], axis=1),
            cblk=jnp.concatenate([_block_diag_out(c_re[l]), -_block_diag_out(c_im[l])], axis=0),
            d_skip=row(d_skip[l]),
        )
        last = l == depth - 1
        xp, st_p = _prompt_layer(xp, lw, ssm, last)
        xs, st_s = _sample_layer(xs, lw, ssm, last, cache_k[l], cache_v[l], cache_kidx[l],
                                 state_ssm_re[l], state_ssm_im[l], page_table)
        new_p.append(st_p)
        new_s.append(st_s)
    stack = lambda states, i: jnp.stack([s[i] for s in states])
    return (xp, xs) + tuple(stack(new_p, i) for i in range(5)) + tuple(stack(new_s, i) for i in range(5))
```
